```python
import jax
import jax.numpy as jnp
from jax import lax
import numpy as np


D_MODEL = 2048
BATCH = 2
SEQ = 4096
DEPTH = 1

CTX_LEN = 256
GRID_W = 64
D_MIX = D_MODEL
CONV_DIM = D_MIX // 2
N_HEADS = 4
V_HEAD = (D_MIX - CONV_DIM) // N_HEADS
QK_HEAD = V_HEAD // 2
CHUNK = 128
N_GATES = 4 * N_HEADS
PROJ_SIZES = [CONV_DIM, CONV_DIM, CONV_DIM, N_HEADS * QK_HEAD, N_HEADS * QK_HEAD,
              N_HEADS * V_HEAD, N_HEADS * V_HEAD, N_GATES]
D_PROJ = int(sum(PROJ_SIZES))
PROJ_SPLITS = [int(s) for s in np.cumsum(PROJ_SIZES)[:-1]]
N_EXPERTS = 64
TOP_K = 6
D_EXPERT = 1408
D_SHARED = 1408
ROUTED_SCALE = 2.446
MOE_BLOCK = 256
EPS = 1e-6

kernel_name = 'hybrid_conv_mlstm_moe_dit_block'


def rms_norm(x, g):
    x32 = x.astype(jnp.float32)
    y = x32 * lax.rsqrt(jnp.mean(x32 * x32, axis=-1, keepdims=True) + EPS) * g.astype(jnp.float32)
    return y.astype(x.dtype)


def norm_mod(x, g, shift, scale):
    x32 = x.astype(jnp.float32)
    y = x32 * lax.rsqrt(jnp.mean(x32 * x32, axis=-1, keepdims=True) + EPS) * g.astype(jnp.float32)
    y = y * (1.0 + scale.astype(jnp.float32)) + shift.astype(jnp.float32)
    return y.astype(x.dtype)


def conv3(u, w):
    up = jnp.pad(u, ((0, 0), (1, 1), (0, 0)))
    return up[:, :-2] * w[0] + up[:, 1:-1] * w[1] + up[:, 2:] * w[2]


def short_conv_mixer(bg, cg, hv, w, rows):
    u = cg * hv
    bsz, length, cd = u.shape
    if rows > 0:
        y = conv3(u.reshape(bsz * rows, GRID_W, cd), w).reshape(bsz, length, cd)
    else:
        y = conv3(u, w)
    return bg * y


def to_heads(a, dh):
    bsz, length, _ = a.shape
    return a.reshape(bsz, length, N_HEADS, dh).transpose(0, 2, 1, 3).astype(jnp.float32)


def gate_preacts(g, gate_b):
    bsz, length, _ = g.shape
    g = g.astype(jnp.float32).reshape(bsz, length, 2, 2, N_HEADS) + gate_b.astype(jnp.float32)
    g = jnp.transpose(g, (2, 3, 0, 4, 1))
    return g[:, 0], jax.nn.log_sigmoid(g[:, 1])


def flip_t(a):
    return jnp.flip(a, axis=2)


def empty_state(bsz):
    return (jnp.zeros((bsz, N_HEADS, V_HEAD, QK_HEAD), jnp.float32),
            jnp.zeros((bsz, N_HEADS, QK_HEAD), jnp.float32),
            jnp.full((bsz, N_HEADS), -jnp.inf, jnp.float32))


def mlstm_final_state(k, v, ig, lf):
    b = jnp.cumsum(lf, axis=-1)
    g = b[..., -1:] - b + ig
    m = jnp.max(g, axis=-1)
    w = jnp.exp(g - m[..., None])
    C = jnp.einsum('bhs,bhsv,bhsd->bhvd', w, v, k)
    n = jnp.einsum('bhs,bhsd->bhd', w, k)
    return (C, n, m)


def mlstm_chunkwise(q, k, v, ig, lf, state):
    bsz, nh, length, _ = q.shape
    n_chunks = length // CHUNK

    def to_chunks(a):
        return jnp.moveaxis(a.reshape(bsz, nh, n_chunks, CHUNK, *a.shape[3:]), 2, 0)

    lower = jnp.tril(jnp.ones((CHUNK, CHUNK), bool))

    def step(carry, inp):
        C, n, m = carry
        qc, kc, vc, igc, lfc = inp
        b = jnp.cumsum(lfc, axis=-1)
        inter = b + m[..., None]
        dmat = jnp.where(lower, b[..., :, None] - b[..., None, :] + igc[..., None, :], -jnp.inf)
        m_t = jnp.maximum(inter, jnp.max(dmat, axis=-1))
        w_inter = jnp.exp(inter - m_t)
        sqk = jnp.einsum('bhtd,bhsd->bhts', qc, kc) * jnp.exp(dmat - m_t[..., None])
        num = jnp.einsum('bhts,bhsv->bhtv', sqk, vc) + w_inter[..., None] * jnp.einsum('bhvd,bhtd->bhtv', C, qc)
        den = jnp.sum(sqk, axis=-1) + w_inter * jnp.einsum('bhd,bhtd->bht', n, qc)
        h = num / jnp.maximum(jnp.abs(den), jnp.exp(-m_t))[..., None]
        b_last = b[..., -1]
        g = b_last[..., None] - b + igc
        m_new = jnp.maximum(b_last + m, jnp.max(g, axis=-1))
        a = jnp.exp(b_last + m - m_new)
        ws = jnp.exp(g - m_new[..., None])
        C_new = a[..., None, None] * C + jnp.einsum('bhs,bhsv,bhsd->bhvd', ws, vc, kc)
        n_new = a[..., None] * n + jnp.einsum('bhs,bhsd->bhd', ws, kc)
        return (C_new, n_new, m_new), h

    state, hs = lax.scan(step, state, (to_chunks(q), to_chunks(k), to_chunks(v), to_chunks(ig), to_chunks(lf)))
    h = jnp.moveaxis(hs, 0, 2).reshape(bsz, nh, length, V_HEAD)
    return h, state


def mlstm_output(h, o, head_g):
    h = h * lax.rsqrt(jnp.mean(h * h, axis=-1, keepdims=True) + EPS)
    bsz, nh, length, dv = h.shape
    h = h.transpose(0, 2, 1, 3).reshape(bsz, length, nh * dv) * head_g.astype(jnp.float32)
    return (h * jax.nn.sigmoid(o.astype(jnp.float32))).astype(o.dtype)


def token_mixer(xn, cn, w_in, conv_w, gate_b, head_g, w_out, rows, last):
    px = xn @ w_in
    pc = cn @ w_in
    bx, cx, hx, qx, kx, vx, ox, gx = jnp.split(px, PROJ_SPLITS, axis=-1)
    bc, cc, hc, qc, kc, vc, oc, gc = jnp.split(pc, PROJ_SPLITS, axis=-1)
    qscale = QK_HEAD ** -0.5
    conv_x = short_conv_mixer(bx, cx, hx, conv_w, rows)
    qx_h, kx_h, vx_h = to_heads(qx, QK_HEAD) * qscale, to_heads(kx, QK_HEAD), to_heads(vx, V_HEAD)
    kc_h, vc_h = to_heads(kc, QK_HEAD), to_heads(vc, V_HEAD)
    igx, lfx = gate_preacts(gx, gate_b)
    igc, lfc = gate_preacts(gc, gate_b)
    bsz = xn.shape[0]
    if last:
        st_f = mlstm_final_state(kc_h, vc_h, igc[0], lfc[0])
        st_b = mlstm_final_state(flip_t(kc_h), flip_t(vc_h), flip_t(igc[1]), flip_t(lfc[1]))
    else:
        qc_h = to_heads(qc, QK_HEAD) * qscale
        hcf, st_f = mlstm_chunkwise(qc_h, kc_h, vc_h, igc[0], lfc[0], empty_state(bsz))
        hcb, st_b = mlstm_chunkwise(flip_t(qc_h), flip_t(kc_h), flip_t(vc_h), flip_t(igc[1]), flip_t(lfc[1]), empty_state(bsz))
    hxf, _ = mlstm_chunkwise(qx_h, kx_h, vx_h, igx[0], lfx[0], st_f)
    hxb, _ = mlstm_chunkwise(flip_t(qx_h), flip_t(kx_h), flip_t(vx_h), flip_t(igx[1]), flip_t(lfx[1]), st_b)
    mlstm_x = mlstm_output(hxf + flip_t(hxb), ox, head_g)
    y_lat = jnp.concatenate([conv_x, mlstm_x], axis=-1) @ w_out
    if last:
        return y_lat, None
    conv_c = short_conv_mixer(bc, cc, hc, conv_w, 0)
    mlstm_c = mlstm_output(hcf + flip_t(hcb), oc, head_g)
    y_ctx = jnp.concatenate([conv_c, mlstm_c], axis=-1) @ w_out
    return y_lat, y_ctx


def moe(h, w_router, b_router, we_gate, we_up, we_down, ws_gate, ws_up, ws_down):
    bsz, length, d = h.shape
    t = h.reshape(-1, d)
    n_tok = t.shape[0]
    scores = jax.nn.sigmoid(jnp.dot(t.astype(jnp.float32), w_router.astype(jnp.float32)))
    _, idx = lax.top_k(scores + b_router.astype(jnp.float32), TOP_K)
    sel = jnp.take_along_axis(scores, idx, axis=-1)
    gates = sel / jnp.sum(sel, axis=-1, keepdims=True) * ROUTED_SCALE
    n_assign = n_tok * TOP_K
    flat_e = idx.reshape(n_assign).astype(jnp.int32)
    order = jnp.argsort(flat_e)
    e_sorted = flat_e[order]
    tok_sorted = (order // TOP_K).astype(jnp.int32)
    w_sorted = gates.reshape(n_assign)[order]
    counts = jnp.bincount(flat_e, length=N_EXPERTS).astype(jnp.int32)
    padded = (counts + MOE_BLOCK - 1) // MOE_BLOCK * MOE_BLOCK
    pad_end = jnp.cumsum(padded)
    pad_start = pad_end - padded
    grp_start = jnp.cumsum(counts) - counts
    dest = pad_start[e_sorted] + jnp.arange(n_assign, dtype=jnp.int32) - grp_start[e_sorted]
    n_blocks = -(-n_assign // MOE_BLOCK) + N_EXPERTS
    n_slots = n_blocks * MOE_BLOCK
    slot_tok = jnp.full((n_slots,), n_tok, jnp.int32).at[dest].set(tok_sorted)
    slot_w = jnp.zeros((n_slots,), jnp.float32).at[dest].set(w_sorted)
    block_e = jnp.minimum(jnp.searchsorted(pad_end, jnp.arange(n_blocks, dtype=jnp.int32) * MOE_BLOCK, side='right'), N_EXPERTS - 1)
    t_pad = jnp.concatenate([t, jnp.zeros((1, d), t.dtype)], axis=0)

    def expert_block(args):
        tok, wt, e = args
        xb = t_pad[tok]
        hb = jax.nn.silu(xb @ we_gate[e]) * (xb @ we_up[e])
        return (hb @ we_down[e]) * wt[:, None].astype(xb.dtype)

    yb = lax.map(expert_block, (slot_tok.reshape(n_blocks, MOE_BLOCK), slot_w.reshape(n_blocks, MOE_BLOCK), block_e))
    routed = jax.ops.segment_sum(yb.reshape(n_slots, d), slot_tok, num_segments=n_tok + 1)[:n_tok]
    shared = (jax.nn.silu(t @ ws_gate) * (t @ ws_up)) @ ws_down
    return (routed + shared).reshape(bsz, length, d)


def setup_inputs(seed: int = 0) -> dict:
    key = jax.random.key(seed)
    ks = jax.random.split(key, 24)

    def nrm(k, shape, s):
        return jax.random.normal(k, shape, jnp.float32) * s

    i_bias = nrm(ks[9], (DEPTH, 2, 1, N_HEADS), 0.1)
    f_bias = 3.0 + 3.0 * jax.random.uniform(ks[10], (DEPTH, 2, 1, N_HEADS), jnp.float32)
    return {
        'x': nrm(ks[0], (BATCH, SEQ, D_MODEL), 1.0),
        'c': nrm(ks[1], (BATCH, D_MODEL), 1.0),
        'ctx': nrm(ks[2], (BATCH, CTX_LEN, D_MODEL), 1.0),
        'c_ctx': nrm(ks[3], (D_MODEL,), 1.0),
        'norm1_g': 1.0 + nrm(ks[4], (DEPTH, D_MODEL), 0.02),
        'norm2_g': 1.0 + nrm(ks[5], (DEPTH, D_MODEL), 0.02),
        'w_ada': nrm(ks[6], (DEPTH, D_MODEL, 6 * D_MODEL), 0.5 * D_MODEL ** -0.5),
        'b_ada': nrm(ks[7], (DEPTH, 6 * D_MODEL), 0.02),
        'w_in': nrm(ks[8], (DEPTH, D_MODEL, D_PROJ), D_MODEL ** -0.5),
        'conv_w': nrm(ks[11], (DEPTH, 3, CONV_DIM), 3.0 ** -0.5),
        'gate_b': jnp.concatenate([i_bias, f_bias], axis=2),
        'head_g': 1.0 + nrm(ks[12], (DEPTH, N_HEADS * V_HEAD), 0.02),
        'w_out': nrm(ks[13], (DEPTH, D_MIX, D_MODEL), D_MIX ** -0.5),
        'w_router': nrm(ks[14], (DEPTH, D_MODEL, N_EXPERTS), D_MODEL ** -0.5),
        'b_router': nrm(ks[15], (DEPTH, N_EXPERTS), 0.01),
        'we_gate': nrm(ks[16], (DEPTH, N_EXPERTS, D_MODEL, D_EXPERT), D_MODEL ** -0.5),
        'we_up': nrm(ks[17], (DEPTH, N_EXPERTS, D_MODEL, D_EXPERT), D_MODEL ** -0.5),
        'we_down': nrm(ks[18], (DEPTH, N_EXPERTS, D_EXPERT, D_MODEL), D_EXPERT ** -0.5),
        'ws_gate': nrm(ks[19], (DEPTH, D_MODEL, D_SHARED), D_MODEL ** -0.5),
        'ws_up': nrm(ks[20], (DEPTH, D_MODEL, D_SHARED), D_MODEL ** -0.5),
        'ws_down': nrm(ks[21], (DEPTH, D_SHARED, D_MODEL), D_SHARED ** -0.5),
        'final_g': 1.0 + nrm(ks[22], (D_MODEL,), 0.02),
    }


def reference(x, c, ctx, c_ctx, norm1_g, norm2_g, w_ada, b_ada, w_in, conv_w, gate_b, head_g, w_out,
              w_router, b_router, we_gate, we_up, we_down, ws_gate, ws_up, ws_down, final_g):
    rows = x.shape[1] // GRID_W
    for l in range(DEPTH):
        last = l == DEPTH - 1
        sh1, sc1, gt1, sh2, sc2, gt2 = jnp.split(jax.nn.silu(c) @ w_ada[l] + b_ada[l], 6, axis=-1)
        csh1, csc1, cgt1, csh2, csc2, cgt2 = jnp.split(jax.nn.silu(c_ctx) @ w_ada[l] + b_ada[l], 6, axis=-1)
        xn = norm_mod(x, norm1_g[l], sh1[:, None], sc1[:, None])
        cn = norm_mod(ctx, norm1_g[l], csh1, csc1)
        y_lat, y_ctx = token_mixer(xn, cn, w_in[l], conv_w[l], gate_b[l], head_g[l], w_out[l], rows, last)
        x = x + gt1[:, None] * y_lat
        x = x + gt2[:, None] * moe(norm_mod(x, norm2_g[l], sh2[:, None], sc2[:, None]), w_router[l], b_router[l],
                                   we_gate[l], we_up[l], we_down[l], ws_gate[l], ws_up[l], ws_down[l])
        if not last:
            ctx = ctx + cgt1 * y_ctx
            ctx = ctx + cgt2 * moe(norm_mod(ctx, norm2_g[l], csh2, csc2), w_router[l], b_router[l],
                                   we_gate[l], we_up[l], we_down[l], ws_gate[l], ws_up[l], ws_down[l])
    return rms_norm(x, final_g)
```

```python
import functools

import jax
import jax.numpy as jnp
from jax import lax
from jax.experimental import pallas as pl
from jax.experimental.pallas import tpu as pltpu

F32 = jnp.float32
BF16 = jnp.bfloat16
I32 = jnp.int32

N_HEADS = 4
GRID_W = 64
CHUNK = 128
TOP_K = 6
N_EXPERTS = 64
ROUTED_SCALE = 2.446
EPS = 1e-6
N_GATES = 4 * N_HEADS

LANES = 128
SUBLANES = 8
MOE_BLOCK = 256
ROW_TILE = 256
FINAL_TILE = 128
ADALN_TILE = 1024
WEIGHT_CHUNKS = 8
VMEM_LIMIT = 56 * 1024 * 1024

_HIGHEST = lax.Precision.HIGHEST
_NEG_INF = float("-inf")


def _resident(shape):
    nd = len(shape)
    return pl.BlockSpec(shape, lambda *_: (0,) * nd, pipeline_mode=pl.Buffered(1))


def _params(n_axes):
    return pltpu.CompilerParams(
        dimension_semantics=("arbitrary",) * n_axes, vmem_limit_bytes=VMEM_LIMIT)


def _log_sigmoid(x):
    return jnp.minimum(x, 0.0) - jnp.log1p(jnp.exp(-jnp.abs(x)))


def _silu(x):
    return x * jax.nn.sigmoid(x)


def _adaln_body(c_ref, w_ref, b_ref, o_ref):
    s = _silu(c_ref[...])
    o_ref[...] = jnp.dot(s.astype(BF16), w_ref[...].astype(BF16),
                         preferred_element_type=F32) + b_ref[...]


def _adaln(cc, w, b):
    d, n6 = w.shape
    return pl.pallas_call(
        _adaln_body,
        grid=(n6 // ADALN_TILE,),
        in_specs=[pl.BlockSpec((SUBLANES, d), lambda j: (0, 0)),
                  pl.BlockSpec((d, ADALN_TILE), lambda j: (0, j)),
                  pl.BlockSpec((1, ADALN_TILE), lambda j: (0, j))],
        out_specs=pl.BlockSpec((SUBLANES, ADALN_TILE), lambda j: (0, j)),
        out_shape=jax.ShapeDtypeStruct((SUBLANES, n6), F32),
        compiler_params=_params(1),
        name="adaln",
    )(cc, w, b)


def _norm_mod(x, g, shift, scale):
    y = x * lax.rsqrt(jnp.mean(x * x, axis=-1, keepdims=True) + EPS) * g
    return y * (1.0 + scale) + shift


def _gate_prep(xb, wg_ref, gb_ref, g_ref, gt_ref):
    tm = xb.shape[0]
    gg = jnp.dot(xb, wg_ref[...], preferred_element_type=F32) + gb_ref[...]
    lane = lax.broadcasted_iota(I32, (tm, LANES), 1)
    is_f = (lane & N_HEADS) != 0
    is_bwd = (lane & (2 * N_HEADS)) != 0
    lf = jnp.where(is_f, _log_sigmoid(gg), 0.0)
    r = lax.broadcasted_iota(I32, (tm, tm), 0)
    c = lax.broadcasted_iota(I32, (tm, tm), 1)
    same = (r // CHUNK) == (c // CHUNK)
    tri_l = jnp.where(same & (c <= r), 1.0, 0.0).astype(F32)
    tri_u = jnp.where(same & (c >= r), 1.0, 0.0).astype(F32)
    pre = jnp.dot(tri_l, lf, precision=_HIGHEST, preferred_element_type=F32)
    suf = jnp.dot(tri_u, lf, precision=_HIGHEST, preferred_element_type=F32)
    out = jnp.where(is_f, jnp.where(is_bwd, suf, pre), gg)
    g_ref[...] = out[:, :N_GATES]
    gt_ref[...] = out.T[:N_GATES, :]


def _inproj_body(x_ref, sh_ref, sc_ref, g1_ref, w_ref, wg_ref, gb_ref, cw_ref,
                 conv_ref, q_ref, k_ref, v_ref, o_ref, g_ref, gt_ref, *, conv_dim, qk_all, v_all):
    tm = x_ref.shape[0]
    xb = _norm_mod(x_ref[...], g1_ref[...], sh_ref[0], sc_ref[0]).astype(BF16)

    def proj(lo, width):
        return jnp.dot(xb, w_ref[:, lo:lo + width], preferred_element_type=F32)

    u = proj(conv_dim, conv_dim) * proj(2 * conv_dim, conv_dim)
    pos = lax.broadcasted_iota(I32, (tm, 1), 0) % GRID_W
    um = jnp.where(pos == 0, 0.0, pltpu.roll(u, 1, axis=0))
    up = jnp.where(pos == GRID_W - 1, 0.0, pltpu.roll(u, tm - 1, axis=0))
    y = um * cw_ref[0:1, :] + u * cw_ref[1:2, :] + up * cw_ref[2:3, :]
    conv_ref[...] = (proj(0, conv_dim) * y).astype(BF16)

    off = 3 * conv_dim
    qscale = (qk_all // N_HEADS) ** -0.5
    q_ref[...] = (proj(off, qk_all) * qscale).astype(BF16)
    k_ref[...] = proj(off + qk_all, qk_all).astype(BF16)
    v_ref[...] = proj(off + 2 * qk_all, v_all).astype(BF16)
    o_ref[...] = jax.nn.sigmoid(proj(off + 2 * qk_all + v_all, v_all)).astype(BF16)
    _gate_prep(xb, wg_ref, gb_ref, g_ref, gt_ref)


def _inproj_ctx_body(x_ref, sh_ref, sc_ref, g1_ref, w_ref, wg_ref, gb_ref,
                     k_ref, v_ref, g_ref, gt_ref, *, qk_all, v_all):
    xb = _norm_mod(x_ref[...], g1_ref[...], sh_ref[0], sc_ref[0]).astype(BF16)
    k_ref[...] = jnp.dot(xb, w_ref[:, 0:qk_all], preferred_element_type=F32).astype(BF16)
    v_ref[...] = jnp.dot(xb, w_ref[:, qk_all:qk_all + v_all], preferred_element_type=F32).astype(BF16)
    _gate_prep(xb, wg_ref, gb_ref, g_ref, gt_ref)


def _mod_spec(part, tiles_per_row, fixed_row=None):
    def index(i):
        row = fixed_row if fixed_row is not None else i // tiles_per_row
        return (row * 6 + part, 0, 0)

    return index


def _inproj(x2d, mod, g1, w_main, w_gate, gate_b, conv_w, rows_per_batch, conv_dim, qk_all, v_all):
    n, d = x2d.shape
    tm = ROW_TILE
    tiles_per_batch = rows_per_batch // tm
    row = lambda i: (i, 0)
    mod_block = (1, 1, d)
    out_shapes = (
        jax.ShapeDtypeStruct((n, conv_dim), BF16),
        jax.ShapeDtypeStruct((n, qk_all), BF16),
        jax.ShapeDtypeStruct((n, qk_all), BF16),
        jax.ShapeDtypeStruct((n, v_all), BF16),
        jax.ShapeDtypeStruct((n, v_all), BF16),
        jax.ShapeDtypeStruct((n, N_GATES), F32),
        jax.ShapeDtypeStruct((N_GATES, n), F32),
    )
    out_specs = (
        pl.BlockSpec((tm, conv_dim), row),
        pl.BlockSpec((tm, qk_all), row),
        pl.BlockSpec((tm, qk_all), row),
        pl.BlockSpec((tm, v_all), row),
        pl.BlockSpec((tm, v_all), row),
        pl.BlockSpec((tm, N_GATES), row),
        pl.BlockSpec((N_GATES, tm), lambda i: (0, i)),
    )
    return pl.pallas_call(
        functools.partial(_inproj_body, conv_dim=conv_dim, qk_all=qk_all, v_all=v_all),
        grid=(n // tm,),
        in_specs=[pl.BlockSpec((tm, d), row),
                  pl.BlockSpec(mod_block, _mod_spec(0, tiles_per_batch)),
                  pl.BlockSpec(mod_block, _mod_spec(1, tiles_per_batch)),
                  _resident(g1.shape), _resident(w_main.shape), _resident(w_gate.shape),
                  _resident(gate_b.shape), _resident(conv_w.shape)],
        out_specs=out_specs,
        out_shape=out_shapes,
        compiler_params=_params(1),
        name="inproj",
    )(x2d, mod, mod, g1, w_main, w_gate, gate_b, conv_w)


def _inproj_ctx(c2d, mod, g1, w_kv, w_gate, gate_b, ctx_mod_row, qk_all, v_all):
    n, d = c2d.shape
    tm = ROW_TILE
    row = lambda i: (i, 0)
    mod_block = (1, 1, d)
    return pl.pallas_call(
        functools.partial(_inproj_ctx_body, qk_all=qk_all, v_all=v_all),
        grid=(n // tm,),
        in_specs=[pl.BlockSpec((tm, d), row),
                  pl.BlockSpec(mod_block, _mod_spec(0, 1, ctx_mod_row)),
                  pl.BlockSpec(mod_block, _mod_spec(1, 1, ctx_mod_row)),
                  _resident(g1.shape), _resident(w_kv.shape), _resident(w_gate.shape),
                  _resident(gate_b.shape)],
        out_specs=(pl.BlockSpec((tm, qk_all), row), pl.BlockSpec((tm, v_all), row),
                   pl.BlockSpec((tm, N_GATES), row), pl.BlockSpec((N_GATES, tm), lambda i: (0, i))),
        out_shape=(jax.ShapeDtypeStruct((n, qk_all), BF16), jax.ShapeDtypeStruct((n, v_all), BF16),
                   jax.ShapeDtypeStruct((n, N_GATES), F32), jax.ShapeDtypeStruct((N_GATES, n), F32)),
        compiler_params=_params(1),
        name="inproj_ctx",
    )(c2d, mod, mod, g1, w_kv, w_gate, gate_b)


def _mlstm_state_update(h, direction, k_ref, v_ref, g_ref, gt_ref, c_ref, n_ref, m_ref, qk, vh):
    ci = direction * 2 * N_HEADS + h
    cb = ci + N_HEADS
    last = 0 if direction else CHUNK - 1
    k = k_ref[:, h * qk:(h + 1) * qk].astype(F32)
    v = v_ref[:, h * vh:(h + 1) * vh]
    ig_c = g_ref[:, ci:ci + 1]
    b_c = g_ref[:, cb:cb + 1]
    b_last = gt_ref[cb:cb + 1, last:last + 1]
    m_prev = m_ref[h][0:1, 0:1]
    g_c = b_last - b_c + ig_c
    m_new = jnp.maximum(b_last + m_prev, jnp.max(g_c, axis=0, keepdims=True))
    a = jnp.exp(b_last + m_prev - m_new)
    kw = k * jnp.exp(g_c - m_new)
    upd = jnp.dot(kw.T.astype(BF16), v, preferred_element_type=F32)
    c_ref[h] = a * c_ref[h] + upd
    n_new = a * n_ref[h][0:1, :] + jnp.sum(kw, axis=0, keepdims=True)
    n_ref[h] = jnp.broadcast_to(n_new, n_ref.shape[1:])
    m_ref[h] = jnp.broadcast_to(m_new, m_ref.shape[1:])


def _mlstm_head_output(h, direction, q_ref, k_ref, v_ref, g_ref, gt_ref, c_ref, n_ref, m_ref, qk, vh):
    ci = direction * 2 * N_HEADS + h
    cb = ci + N_HEADS
    q = q_ref[:, h * qk:(h + 1) * qk]
    k = k_ref[:, h * qk:(h + 1) * qk]
    v = v_ref[:, h * vh:(h + 1) * vh]
    ig_r = gt_ref[ci:ci + 1, :]
    b_r = gt_ref[cb:cb + 1, :]
    b_c = g_ref[:, cb:cb + 1]
    m_prev = m_ref[h][0:1, 0:1]
    row = lax.broadcasted_iota(I32, (CHUNK, CHUNK), 0)
    col = lax.broadcasted_iota(I32, (CHUNK, CHUNK), 1)
    mask = (col >= row) if direction else (col <= row)
    dm = jnp.where(mask, b_c + (ig_r - b_r), _NEG_INF)
    inter = b_c + m_prev
    m_t = jnp.maximum(inter, jnp.max(dm, axis=1, keepdims=True))
    w_inter = jnp.exp(inter - m_t)
    s = lax.dot_general(q, k, (((1,), (1,)), ((), ())), preferred_element_type=F32) * jnp.exp(dm - m_t)
    num = (jnp.dot(s.astype(BF16), v, preferred_element_type=F32)
           + w_inter * jnp.dot(q, c_ref[h].astype(BF16), preferred_element_type=F32))
    qn = jnp.sum(q.astype(F32) * n_ref[h][0:1, :], axis=1, keepdims=True)
    den = jnp.sum(s, axis=1, keepdims=True) + w_inter * qn
    return num / jnp.maximum(jnp.abs(den), jnp.exp(-m_t))


def _mlstm_body(*refs, direction, n_ctx_chunks, qk, vh):
    if direction:
        (q_ref, k_ref, v_ref, g_ref, gt_ref, kc_ref, vc_ref, gc_ref, gtc_ref,
         out_ref, c_ref, n_ref, m_ref) = refs
    else:
        (q_ref, k_ref, v_ref, g_ref, gt_ref, kc_ref, vc_ref, gc_ref, gtc_ref,
         hb_ref, og_ref, hg_ref, out_ref, c_ref, n_ref, m_ref) = refs
    step = pl.program_id(1)

    @pl.when(step == 0)
    def _():
        c_ref[...] = jnp.zeros_like(c_ref)
        n_ref[...] = jnp.zeros_like(n_ref)
        m_ref[...] = jnp.full_like(m_ref, _NEG_INF)

    @pl.when(step < n_ctx_chunks)
    def _():
        for h in range(N_HEADS):
            _mlstm_state_update(h, direction, kc_ref, vc_ref, gc_ref, gtc_ref, c_ref, n_ref, m_ref, qk, vh)

    @pl.when(step >= n_ctx_chunks)
    def _():
        for h in range(N_HEADS):
            hh = _mlstm_head_output(h, direction, q_ref, k_ref, v_ref, g_ref, gt_ref,
                                    c_ref, n_ref, m_ref, qk, vh)
            cols = slice(h * vh, (h + 1) * vh)
            if direction:
                out_ref[:, cols] = hh
            else:
                hs = hh + hb_ref[:, cols]
                hs = hs * lax.rsqrt(jnp.mean(hs * hs, axis=-1, keepdims=True) + EPS)
                out_ref[:, cols] = (hs * hg_ref[:, cols] * og_ref[:, cols].astype(F32)).astype(BF16)
            _mlstm_state_update(h, direction, k_ref, v_ref, g_ref, gt_ref, c_ref, n_ref, m_ref, qk, vh)


def _mlstm(direction, q, k, v, g, gt, kc, vc, gc, gtc, extra, bsz, head_g=None):
    n, qk_all = q.shape
    v_all = v.shape[1]
    qk, vh = qk_all // N_HEADS, v_all // N_HEADS
    nc = n // bsz // CHUNK
    ncc = kc.shape[0] // bsz // CHUNK

    def lat(b, s):
        j = jnp.clip(s - ncc, 0, nc - 1)
        return b * nc + (nc - 1 - j if direction else j)

    def ctx(b, s):
        j = jnp.clip(s, 0, ncc - 1)
        return b * ncc + (ncc - 1 - j if direction else j)

    lat_row = lambda b, s: (lat(b, s), 0)
    lat_col = lambda b, s: (0, lat(b, s))
    ctx_row = lambda b, s: (ctx(b, s), 0)
    ctx_col = lambda b, s: (0, ctx(b, s))
    in_specs = [pl.BlockSpec((CHUNK, qk_all), lat_row), pl.BlockSpec((CHUNK, qk_all), lat_row),
                pl.BlockSpec((CHUNK, v_all), lat_row), pl.BlockSpec((CHUNK, N_GATES), lat_row),
                pl.BlockSpec((N_GATES, CHUNK), lat_col),
                pl.BlockSpec((CHUNK, qk_all), ctx_row), pl.BlockSpec((CHUNK, v_all), ctx_row),
                pl.BlockSpec((CHUNK, N_GATES), ctx_row), pl.BlockSpec((N_GATES, CHUNK), ctx_col)]
    args = [q, k, v, g, gt, kc, vc, gc, gtc]
    if direction:
        out_dtype = F32
    else:
        hb, og = extra
        in_specs += [pl.BlockSpec((CHUNK, v_all), lat_row), pl.BlockSpec((CHUNK, v_all), lat_row),
                     pl.BlockSpec((1, v_all), lambda b, s: (0, 0))]
        args += [hb, og, head_g]
        out_dtype = BF16
    return pl.pallas_call(
        functools.partial(_mlstm_body, direction=direction, n_ctx_chunks=ncc, qk=qk, vh=vh),
        grid=(bsz, ncc + nc),
        in_specs=in_specs,
        out_specs=pl.BlockSpec((CHUNK, v_all), lat_row),
        out_shape=jax.ShapeDtypeStruct((n, v_all), out_dtype),
        scratch_shapes=[pltpu.VMEM((N_HEADS, qk, vh), F32),
                        pltpu.VMEM((N_HEADS, SUBLANES, qk), F32),
                        pltpu.VMEM((N_HEADS, SUBLANES, LANES), F32)],
        compiler_params=_params(2),
        name="mlstm_bwd" if direction else "mlstm_fwd",
    )(*args)


def _outproj_body(conv_ref, ml_ref, x_ref, gt1_ref, sh2_ref, sc2_ref, g2_ref, wo_ref, wr_ref, br_ref,
                  x1_ref, h_ref, idx_ref, gate_ref, rank_ref, cnt_ref, carry_ref):
    tm = x_ref.shape[0]
    half = conv_ref.shape[1]

    @pl.when(pl.program_id(0) == 0)
    def _():
        carry_ref[...] = jnp.zeros_like(carry_ref)

    y = (jnp.dot(conv_ref[...], wo_ref[0:half, :], preferred_element_type=F32)
         + jnp.dot(ml_ref[...], wo_ref[half:2 * half, :], preferred_element_type=F32))
    x1 = x_ref[...] + gt1_ref[0] * y
    x1_ref[...] = x1
    hn = _norm_mod(x1, g2_ref[...], sh2_ref[0], sc2_ref[0])
    h_ref[...] = hn

    scores = jax.nn.sigmoid(jnp.dot(hn, wr_ref[...], precision=_HIGHEST, preferred_element_type=F32))
    lane = lax.broadcasted_iota(I32, (tm, LANES), 1).astype(F32)
    biased = jnp.where(lane < N_EXPERTS, scores + br_ref[...], _NEG_INF)
    onehot = jnp.zeros((tm, LANES), F32)
    picks, sels = [], []
    for _ in range(TOP_K):
        mx = jnp.max(biased, axis=1, keepdims=True)
        pick = jnp.min(jnp.where(biased == mx, lane, float(LANES)), axis=1, keepdims=True)
        hit = lane == pick
        sels.append(jnp.sum(jnp.where(hit, scores, 0.0), axis=1, keepdims=True))
        picks.append(pick)
        biased = jnp.where(hit, _NEG_INF, biased)
        onehot = onehot + hit.astype(F32)
    total = sels[0]
    for s in sels[1:]:
        total = total + s

    r = lax.broadcasted_iota(I32, (tm, tm), 0)
    c = lax.broadcasted_iota(I32, (tm, tm), 1)
    strict = jnp.where(c < r, 1.0, 0.0).astype(BF16)
    before = jnp.dot(strict, onehot.astype(BF16), preferred_element_type=F32) + carry_ref[...]
    slot = lax.broadcasted_iota(I32, (tm, SUBLANES), 1)
    idx_out = jnp.zeros((tm, SUBLANES), F32)
    gate_out = jnp.zeros((tm, SUBLANES), F32)
    rank_out = jnp.zeros((tm, SUBLANES), F32)
    for j in range(TOP_K):
        rank = jnp.sum(jnp.where(lane == picks[j], before, 0.0), axis=1, keepdims=True)
        idx_out = jnp.where(slot == j, picks[j], idx_out)
        gate_out = jnp.where(slot == j, sels[j] / total * ROUTED_SCALE, gate_out)
        rank_out = jnp.where(slot == j, rank, rank_out)
    idx_ref[...] = idx_out.astype(I32)
    gate_ref[...] = gate_out
    rank_ref[...] = rank_out.astype(I32)
    carry_ref[...] = carry_ref[...] + jnp.sum(onehot, axis=0, keepdims=True)
    cnt_ref[...] = jnp.broadcast_to(carry_ref[...], cnt_ref.shape).astype(I32)


def _outproj(conv, ml, x2d, mod, g2, w_out, w_router, b_router, rows_per_batch):
    n, d = x2d.shape
    tm = ROW_TILE
    tiles_per_batch = rows_per_batch // tm
    row = lambda i: (i, 0)
    mod_block = (1, 1, d)
    half = conv.shape[1]
    return pl.pallas_call(
        _outproj_body,
        grid=(n // tm,),
        in_specs=[pl.BlockSpec((tm, half), row), pl.BlockSpec((tm, half), row), pl.BlockSpec((tm, d), row),
                  pl.BlockSpec(mod_block, _mod_spec(2, tiles_per_batch)),
                  pl.BlockSpec(mod_block, _mod_spec(3, tiles_per_batch)),
                  pl.BlockSpec(mod_block, _mod_spec(4, tiles_per_batch)),
                  _resident(g2.shape), _resident(w_out.shape), _resident(w_router.shape),
                  _resident(b_router.shape)],
        out_specs=(pl.BlockSpec((tm, d), row), pl.BlockSpec((tm, d), row),
                   pl.BlockSpec((tm, SUBLANES), row), pl.BlockSpec((tm, SUBLANES), row),
                   pl.BlockSpec((tm, SUBLANES), row),
                   pl.BlockSpec((SUBLANES, LANES), lambda i: (0, 0))),
        out_shape=(jax.ShapeDtypeStruct((n, d), F32), jax.ShapeDtypeStruct((n, d), F32),
                   jax.ShapeDtypeStruct((n, SUBLANES), I32), jax.ShapeDtypeStruct((n, SUBLANES), F32),
                   jax.ShapeDtypeStruct((n, SUBLANES), I32),
                   jax.ShapeDtypeStruct((SUBLANES, LANES), I32)),
        scratch_shapes=[pltpu.VMEM((1, LANES), F32)],
        compiler_params=_params(1),
        name="outproj_router",
    )(conv, ml, x2d, mod, mod, mod, g2, w_out, w_router, b_router)


def _moe_body(be_ref, valid_ref, nb_ref,
              tok_ref, tokn_ref, dst_ref, h_hbm, wg_hbm, wu_hbm, wd_hbm,
              y_hbm,
              xbuf, ybuf, wgu, wd, stage_a, stage_d, gsem, ssem, wsem, *, d_expert):
    b = pl.program_id(0)
    nb = nb_ref[0]
    slot = b % 2
    d_model = xbuf.shape[2]

    def row_in(tok, dst_slot, r):
        return pltpu.make_async_copy(h_hbm.at[pl.ds(tok, 1)], xbuf.at[dst_slot, pl.ds(r, 1)],
                                     gsem.at[dst_slot])

    def row_out(src_slot, r, dst):
        return pltpu.make_async_copy(ybuf.at[src_slot, pl.ds(r, 1)], y_hbm.at[pl.ds(dst, 1)],
                                     ssem.at[src_slot])

    def start_gather(toks, blk, dst_slot):
        def body(r, carry):
            row_in(toks[0, 0, r], dst_slot, r).start()
            return carry
        lax.fori_loop(0, valid_ref[blk], body, 0)

    def wait_rows(make, count):
        def body(r, carry):
            make().wait()
            return carry
        lax.fori_loop(0, count, body, 0)

    @pl.when(b == 0)
    def _():
        xbuf[...] = jnp.zeros_like(xbuf)
        start_gather(tok_ref, 0, 0)

    @pl.when(b + 1 < nb)
    def _():
        start_gather(tokn_ref, b + 1, 1 - slot)

    def load_matrix(src_hbm, e, stage, rows, store):
        def chunk(c, s):
            return pltpu.make_async_copy(src_hbm.at[e, pl.ds(c * rows, rows)], stage.at[s], wsem.at[s])

        chunk(0, 0).start()

        def body(c, carry):
            s = c % 2

            @pl.when(c + 1 < WEIGHT_CHUNKS)
            def _():
                chunk(c + 1, 1 - s).start()

            chunk(c, s).wait()
            store(pl.multiple_of(c * rows, rows), stage[s].astype(BF16))
            return carry

        lax.fori_loop(0, WEIGHT_CHUNKS, body, 0)

    @pl.when(b < nb)
    def _():
        e = be_ref[b]
        prev_e = be_ref[jnp.maximum(b - 1, 0)]

        @pl.when((b == 0) | (prev_e != e))
        def _():
            rows_a = d_model // WEIGHT_CHUNKS
            rows_d = d_expert // WEIGHT_CHUNKS

            def store_g(r0, val):
                wgu[pl.ds(r0, rows_a), 0:d_expert] = val

            def store_u(r0, val):
                wgu[pl.ds(r0, rows_a), d_expert:2 * d_expert] = val

            def store_d(r0, val):
                wd[pl.ds(r0, rows_d), :] = val

            load_matrix(wg_hbm, e, stage_a, rows_a, store_g)
            load_matrix(wu_hbm, e, stage_a, rows_a, store_u)
            load_matrix(wd_hbm, e, stage_d, rows_d, store_d)

        wait_rows(lambda: row_in(0, slot, 0), valid_ref[b])

        @pl.when(b >= 2)
        def _():
            wait_rows(lambda: row_out(slot, 0, 0), valid_ref[jnp.maximum(b - 2, 0)])

        x = xbuf[slot].astype(BF16)
        gu = jnp.dot(x, wgu[...], preferred_element_type=F32)
        hb = (_silu(gu[:, 0:d_expert]) * gu[:, d_expert:2 * d_expert]).astype(BF16)
        ybuf[slot] = jnp.dot(hb, wd[...], preferred_element_type=F32)

        def scatter(r, carry):
            row_out(slot, r, dst_ref[0, 0, r]).start()
            return carry
        lax.fori_loop(0, valid_ref[b], scatter, 0)

        @pl.when(b == nb - 1)
        def _():
            @pl.when(b >= 1)
            def _():
                wait_rows(lambda: row_out(1 - slot, 0, 0), valid_ref[jnp.maximum(b - 1, 0)])
            wait_rows(lambda: row_out(slot, 0, 0), valid_ref[b])


def _moe(h, we_gate, we_up, we_down, block_e, block_valid, n_blocks_used, slot_tok, slot_dst, n_assign):
    n, d = h.shape
    d_expert = we_gate.shape[2]
    nb_max = block_e.shape[0]
    smem_blk = (1, 1, MOE_BLOCK)
    any_spec = pl.BlockSpec(memory_space=pl.ANY)
    grid_spec = pltpu.PrefetchScalarGridSpec(
        num_scalar_prefetch=3,
        grid=(nb_max,),
        in_specs=[pl.BlockSpec(smem_blk, lambda b, *_: (b, 0, 0), memory_space=pltpu.SMEM),
                  pl.BlockSpec(smem_blk, lambda b, *_: (jnp.minimum(b + 1, nb_max - 1), 0, 0),
                               memory_space=pltpu.SMEM),
                  pl.BlockSpec(smem_blk, lambda b, *_: (b, 0, 0), memory_space=pltpu.SMEM),
                  any_spec, any_spec, any_spec, any_spec],
        out_specs=any_spec,
        scratch_shapes=[pltpu.VMEM((2, MOE_BLOCK, d), F32),
                        pltpu.VMEM((2, MOE_BLOCK, d), F32),
                        pltpu.VMEM((d, 2 * d_expert), BF16),
                        pltpu.VMEM((d_expert, d), BF16),
                        pltpu.VMEM((2, d // WEIGHT_CHUNKS, d_expert), F32),
                        pltpu.VMEM((2, d_expert // WEIGHT_CHUNKS, d), F32),
                        pltpu.SemaphoreType.DMA((2,)),
                        pltpu.SemaphoreType.DMA((2,)),
                        pltpu.SemaphoreType.DMA((2,))],
    )
    return pl.pallas_call(
        functools.partial(_moe_body, d_expert=d_expert),
        grid_spec=grid_spec,
        out_shape=jax.ShapeDtypeStruct((n_assign, d), F32),
        compiler_params=_params(1),
        name="moe_routed",
    )(block_e, block_valid, n_blocks_used, slot_tok, slot_tok, slot_dst, h, we_gate, we_up, we_down)


def _final_body(h_ref, x1_ref, y_ref, gate_ref, gt2_ref, wsgu_ref, wsd_ref, fg_ref, out_ref, *, d_shared):
    d = x1_ref.shape[1]
    gu = jnp.dot(h_ref[...].astype(BF16), wsgu_ref[...], preferred_element_type=F32)
    hb = (_silu(gu[:, 0:d_shared]) * gu[:, d_shared:2 * d_shared]).astype(BF16)
    acc = jnp.dot(hb, wsd_ref[...], preferred_element_type=F32)
    for j in range(TOP_K):
        acc = acc + gate_ref[:, j:j + 1] * y_ref[:, j * d:(j + 1) * d]
    x2 = x1_ref[...] + gt2_ref[0] * acc
    out_ref[...] = x2 * lax.rsqrt(jnp.mean(x2 * x2, axis=-1, keepdims=True) + EPS) * fg_ref[...]


def _final(h, x1, y_tok, gates, mod, ws_gu, ws_d, final_g, rows_per_batch):
    n, d = x1.shape
    tm = FINAL_TILE
    tiles_per_batch = rows_per_batch // tm
    row = lambda i: (i, 0)
    return pl.pallas_call(
        functools.partial(_final_body, d_shared=ws_d.shape[0]),
        grid=(n // tm,),
        in_specs=[pl.BlockSpec((tm, d), row), pl.BlockSpec((tm, d), row),
                  pl.BlockSpec((tm, TOP_K * d), row), pl.BlockSpec((tm, SUBLANES), row),
                  pl.BlockSpec((1, 1, d), _mod_spec(5, tiles_per_batch)),
                  _resident(ws_gu.shape), _resident(ws_d.shape), _resident(final_g.shape)],
        out_specs=pl.BlockSpec((tm, d), row),
        out_shape=jax.ShapeDtypeStruct((n, d), F32),
        compiler_params=_params(1),
        name="shared_combine_final",
    )(h, x1, y_tok, gates, mod, ws_gu, ws_d, final_g)


def _routing_tables(idx, rank, counts, n_tok):
    nb_max = -(-(n_tok * TOP_K) // MOE_BLOCK) + N_EXPERTS
    nblk = (counts + MOE_BLOCK - 1) // MOE_BLOCK
    blk_end = jnp.cumsum(nblk)
    blk_start = blk_end - nblk
    dest = (blk_start * MOE_BLOCK)[idx] + rank
    assign = jnp.arange(n_tok * TOP_K, dtype=I32)
    slot_dst = jnp.zeros((nb_max * MOE_BLOCK,), I32).at[dest.reshape(-1)].set(assign)
    blocks = jnp.arange(nb_max, dtype=I32)
    block_e = jnp.minimum(jnp.searchsorted(blk_end, blocks, side="right"), N_EXPERTS - 1).astype(I32)
    valid = jnp.clip(counts[block_e] - (blocks - blk_start[block_e]) * MOE_BLOCK, 0, MOE_BLOCK)
    valid = jnp.where(blocks < blk_end[-1], valid, 0).astype(I32)
    shape3 = (nb_max, 1, MOE_BLOCK)
    return (block_e, valid, blk_end[-1:].astype(I32),
            (slot_dst // TOP_K).reshape(shape3), slot_dst.reshape(shape3))


def kernel(x, c, ctx, c_ctx, norm1_g, norm2_g, w_ada, b_ada, w_in, conv_w, gate_b, head_g, w_out,
           w_router, b_router, we_gate, we_up, we_down, ws_gate, ws_up, ws_down, final_g):
    assert w_ada.shape[0] == 1, "single-layer block"
    bsz, seq, d = x.shape
    ctx_len = ctx.shape[1]
    n_tok = bsz * seq
    conv_dim = conv_w.shape[2]
    v_all = head_g.shape[1]
    qk_all = (w_in.shape[2] - 3 * conv_dim - 2 * v_all - N_GATES) // 2
    assert seq % ROW_TILE == 0 and ctx_len % ROW_TILE == 0 and ROW_TILE % GRID_W == 0
    assert bsz + 1 <= SUBLANES

    cc = jnp.zeros((SUBLANES, d), F32).at[:bsz].set(c).at[bsz].set(c_ctx)
    mod = _adaln(cc, w_ada[0], b_ada).reshape(SUBLANES * 6, 1, d)

    n_main = 3 * conv_dim + 2 * qk_all + 2 * v_all
    w_main = w_in[0, :, :n_main].astype(BF16)
    w_kv = w_in[0, :, 3 * conv_dim + qk_all:3 * conv_dim + 2 * qk_all + v_all].astype(BF16)
    w_gate = jnp.zeros((d, LANES), BF16).at[:, :N_GATES].set(w_in[0, :, n_main:].astype(BF16))
    gate_bias = jnp.zeros((1, LANES), F32).at[0, :N_GATES].set(gate_b[0].reshape(-1))

    x2d = x.reshape(n_tok, d)
    conv, q, k, v, og, g, gt = _inproj(x2d, mod, norm1_g, w_main, w_gate, gate_bias, conv_w[0],
                                       seq, conv_dim, qk_all, v_all)
    kc, vc, gc, gtc = _inproj_ctx(ctx.reshape(bsz * ctx_len, d), mod, norm1_g, w_kv, w_gate, gate_bias,
                                  bsz, qk_all, v_all)

    h_bwd = _mlstm(1, q, k, v, g, gt, kc, vc, gc, gtc, None, bsz)
    ml = _mlstm(0, q, k, v, g, gt, kc, vc, gc, gtc, (h_bwd, og), bsz, head_g)

    w_r = jnp.zeros((d, LANES), F32).at[:, :N_EXPERTS].set(w_router[0])
    b_r = jnp.zeros((1, LANES), F32).at[0, :N_EXPERTS].set(b_router[0])
    x1, hn, idx, gates, rank, cnt = _outproj(conv, ml, x2d, mod, norm2_g, w_out[0].astype(BF16),
                                             w_r, b_r, seq)

    block_e, valid, nb_used, slot_tok, slot_dst = _routing_tables(
        idx[:, :TOP_K], rank[:, :TOP_K], cnt[0, :N_EXPERTS], n_tok)
    y_rows = _moe(hn, we_gate[0], we_up[0], we_down[0], block_e, valid, nb_used, slot_tok, slot_dst,
                  n_tok * TOP_K)

    ws_gu = jnp.concatenate([ws_gate[0], ws_up[0]], axis=1).astype(BF16)
    out = _final(hn, x1, y_rows.reshape(n_tok, TOP_K * d), gates, mod, ws_gu, ws_down[0].astype(BF16),
                 final_g.reshape(1, d), seq)
    return out.reshape(bsz, seq, d)
```

```python
import functools

import jax
import jax.numpy as jnp
from jax import lax
from jax.experimental import pallas as pl
from jax.experimental.pallas import tpu as pltpu

F32 = jnp.float32
BF16 = jnp.bfloat16
I32 = jnp.int32

N_HEADS = 4
GRID_W = 64
CHUNK = 128
TOP_K = 6
N_EXPERTS = 64
ROUTED_SCALE = 2.446
EPS = 1e-6
N_GATES = 4 * N_HEADS

LANES = 128
SUBLANES = 8
MOE_BLOCK = 256
ROW_TILE = 256
FINAL_TILE = 128
ADALN_TILE = 1024
WEIGHT_PARTS = 8
ROW_DMA_UNROLL = 8
CODE_SHIFT = 3
SPARE_TOKENS = 88
VMEM_LIMIT = 56 * 1024 * 1024
MOE_VMEM_LIMIT = 62 * 1024 * 1024

_HIGHEST = lax.Precision.HIGHEST
_NEG_INF = float("-inf")


def _resident(shape):
    nd = len(shape)
    return pl.BlockSpec(shape, lambda *_: (0,) * nd, pipeline_mode=pl.Buffered(1))


def _params(n_axes):
    return pltpu.CompilerParams(
        dimension_semantics=("arbitrary",) * n_axes, vmem_limit_bytes=VMEM_LIMIT)


def _log_sigmoid(x):
    return jnp.minimum(x, 0.0) - jnp.log1p(jnp.exp(-jnp.abs(x)))


def _silu(x):
    return x * jax.nn.sigmoid(x)


def _adaln_body(c_ref, w_ref, b_ref, o_ref):
    s = _silu(c_ref[...])
    o_ref[...] = jnp.dot(s.astype(BF16), w_ref[...].astype(BF16),
                         preferred_element_type=F32) + b_ref[...]


def _adaln(cc, w, b):
    d, n6 = w.shape
    return pl.pallas_call(
        _adaln_body,
        grid=(n6 // ADALN_TILE,),
        in_specs=[pl.BlockSpec((SUBLANES, d), lambda j: (0, 0)),
                  pl.BlockSpec((d, ADALN_TILE), lambda j: (0, j)),
                  pl.BlockSpec((1, ADALN_TILE), lambda j: (0, j))],
        out_specs=pl.BlockSpec((SUBLANES, ADALN_TILE), lambda j: (0, j)),
        out_shape=jax.ShapeDtypeStruct((SUBLANES, n6), F32),
        compiler_params=_params(1),
        name="adaln",
    )(cc, w, b)


def _norm_mod(x, g, shift, scale):
    y = x * lax.rsqrt(jnp.mean(x * x, axis=-1, keepdims=True) + EPS) * g
    return y * (1.0 + scale) + shift


def _gate_prep(xb, wg_ref, gb_ref, g_ref, gt_ref):
    tm = xb.shape[0]
    gg = jnp.dot(xb, wg_ref[...], preferred_element_type=F32) + gb_ref[...]
    lane = lax.broadcasted_iota(I32, (tm, LANES), 1)
    is_f = (lane & N_HEADS) != 0
    is_bwd = (lane & (2 * N_HEADS)) != 0
    lf = jnp.where(is_f, _log_sigmoid(gg), 0.0)
    r = lax.broadcasted_iota(I32, (tm, tm), 0)
    c = lax.broadcasted_iota(I32, (tm, tm), 1)
    same = (r // CHUNK) == (c // CHUNK)
    tri_l = jnp.where(same & (c <= r), 1.0, 0.0).astype(F32)
    tri_u = jnp.where(same & (c >= r), 1.0, 0.0).astype(F32)
    pre = jnp.dot(tri_l, lf, precision=_HIGHEST, preferred_element_type=F32)
    suf = jnp.dot(tri_u, lf, precision=_HIGHEST, preferred_element_type=F32)
    out = jnp.where(is_f, jnp.where(is_bwd, suf, pre), gg)
    g_ref[...] = out[:, :N_GATES]
    gt_ref[...] = out.T[:N_GATES, :]


def _inproj_body(x_ref, sh_ref, sc_ref, g1_ref, w_ref, wg_ref, gb_ref, cw_ref,
                 conv_ref, q_ref, k_ref, v_ref, o_ref, g_ref, gt_ref, *, conv_dim, qk_all, v_all):
    tm = x_ref.shape[0]
    xb = _norm_mod(x_ref[...], g1_ref[...], sh_ref[0], sc_ref[0]).astype(BF16)

    def proj(lo, width):
        return jnp.dot(xb, w_ref[:, lo:lo + width], preferred_element_type=F32)

    u = proj(conv_dim, conv_dim) * proj(2 * conv_dim, conv_dim)
    pos = lax.broadcasted_iota(I32, (tm, 1), 0) % GRID_W
    um = jnp.where(pos == 0, 0.0, pltpu.roll(u, 1, axis=0))
    up = jnp.where(pos == GRID_W - 1, 0.0, pltpu.roll(u, tm - 1, axis=0))
    y = um * cw_ref[0:1, :] + u * cw_ref[1:2, :] + up * cw_ref[2:3, :]
    conv_ref[...] = (proj(0, conv_dim) * y).astype(BF16)

    off = 3 * conv_dim
    qscale = (qk_all // N_HEADS) ** -0.5
    q_ref[...] = (proj(off, qk_all) * qscale).astype(BF16)
    k_ref[...] = proj(off + qk_all, qk_all).astype(BF16)
    v_ref[...] = proj(off + 2 * qk_all, v_all).astype(BF16)
    o_ref[...] = jax.nn.sigmoid(proj(off + 2 * qk_all + v_all, v_all)).astype(BF16)
    _gate_prep(xb, wg_ref, gb_ref, g_ref, gt_ref)


def _inproj_ctx_body(x_ref, sh_ref, sc_ref, g1_ref, w_ref, wg_ref, gb_ref,
                     k_ref, v_ref, g_ref, gt_ref, *, qk_all, v_all):
    xb = _norm_mod(x_ref[...], g1_ref[...], sh_ref[0], sc_ref[0]).astype(BF16)
    k_ref[...] = jnp.dot(xb, w_ref[:, 0:qk_all], preferred_element_type=F32).astype(BF16)
    v_ref[...] = jnp.dot(xb, w_ref[:, qk_all:qk_all + v_all], preferred_element_type=F32).astype(BF16)
    _gate_prep(xb, wg_ref, gb_ref, g_ref, gt_ref)


def _mod_spec(part, tiles_per_row, fixed_row=None):
    def index(i):
        row = fixed_row if fixed_row is not None else i // tiles_per_row
        return (row * 6 + part, 0, 0)

    return index


def _inproj(x2d, mod, g1, w_main, w_gate, gate_b, conv_w, rows_per_batch, conv_dim, qk_all, v_all):
    n, d = x2d.shape
    tm = ROW_TILE
    tiles_per_batch = rows_per_batch // tm
    row = lambda i: (i, 0)
    mod_block = (1, 1, d)
    out_shapes = (
        jax.ShapeDtypeStruct((n, conv_dim), BF16),
        jax.ShapeDtypeStruct((n, qk_all), BF16),
        jax.ShapeDtypeStruct((n, qk_all), BF16),
        jax.ShapeDtypeStruct((n, v_all), BF16),
        jax.ShapeDtypeStruct((n, v_all), BF16),
        jax.ShapeDtypeStruct((n, N_GATES), F32),
        jax.ShapeDtypeStruct((N_GATES, n), F32),
    )
    out_specs = (
        pl.BlockSpec((tm, conv_dim), row),
        pl.BlockSpec((tm, qk_all), row),
        pl.BlockSpec((tm, qk_all), row),
        pl.BlockSpec((tm, v_all), row),
        pl.BlockSpec((tm, v_all), row),
        pl.BlockSpec((tm, N_GATES), row),
        pl.BlockSpec((N_GATES, tm), lambda i: (0, i)),
    )
    return pl.pallas_call(
        functools.partial(_inproj_body, conv_dim=conv_dim, qk_all=qk_all, v_all=v_all),
        grid=(n // tm,),
        in_specs=[pl.BlockSpec((tm, d), row),
                  pl.BlockSpec(mod_block, _mod_spec(0, tiles_per_batch)),
                  pl.BlockSpec(mod_block, _mod_spec(1, tiles_per_batch)),
                  _resident(g1.shape), _resident(w_main.shape), _resident(w_gate.shape),
                  _resident(gate_b.shape), _resident(conv_w.shape)],
        out_specs=out_specs,
        out_shape=out_shapes,
        compiler_params=_params(1),
        name="inproj",
    )(x2d, mod, mod, g1, w_main, w_gate, gate_b, conv_w)


def _inproj_ctx(c2d, mod, g1, w_kv, w_gate, gate_b, ctx_mod_row, qk_all, v_all):
    n, d = c2d.shape
    tm = ROW_TILE
    row = lambda i: (i, 0)
    mod_block = (1, 1, d)
    return pl.pallas_call(
        functools.partial(_inproj_ctx_body, qk_all=qk_all, v_all=v_all),
        grid=(n // tm,),
        in_specs=[pl.BlockSpec((tm, d), row),
                  pl.BlockSpec(mod_block, _mod_spec(0, 1, ctx_mod_row)),
                  pl.BlockSpec(mod_block, _mod_spec(1, 1, ctx_mod_row)),
                  _resident(g1.shape), _resident(w_kv.shape), _resident(w_gate.shape),
                  _resident(gate_b.shape)],
        out_specs=(pl.BlockSpec((tm, qk_all), row), pl.BlockSpec((tm, v_all), row),
                   pl.BlockSpec((tm, N_GATES), row), pl.BlockSpec((N_GATES, tm), lambda i: (0, i))),
        out_shape=(jax.ShapeDtypeStruct((n, qk_all), BF16), jax.ShapeDtypeStruct((n, v_all), BF16),
                   jax.ShapeDtypeStruct((n, N_GATES), F32), jax.ShapeDtypeStruct((N_GATES, n), F32)),
        compiler_params=_params(1),
        name="inproj_ctx",
    )(c2d, mod, mod, g1, w_kv, w_gate, gate_b)


def _mlstm_state_update(h, direction, k_ref, v_ref, g_ref, gt_ref, c_ref, n_ref, m_ref, qk, vh):
    ci = direction * 2 * N_HEADS + h
    cb = ci + N_HEADS
    last = 0 if direction else CHUNK - 1
    k = k_ref[:, h * qk:(h + 1) * qk].astype(F32)
    v = v_ref[:, h * vh:(h + 1) * vh]
    ig_c = g_ref[:, ci:ci + 1]
    b_c = g_ref[:, cb:cb + 1]
    b_last = gt_ref[cb:cb + 1, last:last + 1]
    m_prev = m_ref[h][0:1, 0:1]
    g_c = b_last - b_c + ig_c
    m_new = jnp.maximum(b_last + m_prev, jnp.max(g_c, axis=0, keepdims=True))
    a = jnp.exp(b_last + m_prev - m_new)
    kw = k * jnp.exp(g_c - m_new)
    upd = jnp.dot(kw.T.astype(BF16), v, preferred_element_type=F32)
    c_ref[h] = a * c_ref[h] + upd
    n_new = a * n_ref[h][0:1, :] + jnp.sum(kw, axis=0, keepdims=True)
    n_ref[h] = jnp.broadcast_to(n_new, n_ref.shape[1:])
    m_ref[h] = jnp.broadcast_to(m_new, m_ref.shape[1:])


def _mlstm_head_output(h, direction, q_ref, k_ref, v_ref, g_ref, gt_ref, c_ref, n_ref, m_ref, qk, vh):
    ci = direction * 2 * N_HEADS + h
    cb = ci + N_HEADS
    q = q_ref[:, h * qk:(h + 1) * qk]
    k = k_ref[:, h * qk:(h + 1) * qk]
    v = v_ref[:, h * vh:(h + 1) * vh]
    ig_r = gt_ref[ci:ci + 1, :]
    b_r = gt_ref[cb:cb + 1, :]
    b_c = g_ref[:, cb:cb + 1]
    m_prev = m_ref[h][0:1, 0:1]
    row = lax.broadcasted_iota(I32, (CHUNK, CHUNK), 0)
    col = lax.broadcasted_iota(I32, (CHUNK, CHUNK), 1)
    mask = (col >= row) if direction else (col <= row)
    dm = jnp.where(mask, b_c + (ig_r - b_r), _NEG_INF)
    inter = b_c + m_prev
    m_t = jnp.maximum(inter, jnp.max(dm, axis=1, keepdims=True))
    w_inter = jnp.exp(inter - m_t)
    s = lax.dot_general(q, k, (((1,), (1,)), ((), ())), preferred_element_type=F32) * jnp.exp(dm - m_t)
    num = (jnp.dot(s.astype(BF16), v, preferred_element_type=F32)
           + w_inter * jnp.dot(q, c_ref[h].astype(BF16), preferred_element_type=F32))
    qn = jnp.sum(q.astype(F32) * n_ref[h][0:1, :], axis=1, keepdims=True)
    den = jnp.sum(s, axis=1, keepdims=True) + w_inter * qn
    return num / jnp.maximum(jnp.abs(den), jnp.exp(-m_t))


def _mlstm_body(*refs, direction, n_ctx_chunks, qk, vh):
    if direction:
        (q_ref, k_ref, v_ref, g_ref, gt_ref, kc_ref, vc_ref, gc_ref, gtc_ref,
         out_ref, c_ref, n_ref, m_ref) = refs
    else:
        (q_ref, k_ref, v_ref, g_ref, gt_ref, kc_ref, vc_ref, gc_ref, gtc_ref,
         hb_ref, og_ref, hg_ref, out_ref, c_ref, n_ref, m_ref) = refs
    step = pl.program_id(1)

    @pl.when(step == 0)
    def _():
        c_ref[...] = jnp.zeros_like(c_ref)
        n_ref[...] = jnp.zeros_like(n_ref)
        m_ref[...] = jnp.full_like(m_ref, _NEG_INF)

    @pl.when(step < n_ctx_chunks)
    def _():
        for h in range(N_HEADS):
            _mlstm_state_update(h, direction, kc_ref, vc_ref, gc_ref, gtc_ref, c_ref, n_ref, m_ref, qk, vh)

    @pl.when(step >= n_ctx_chunks)
    def _():
        for h in range(N_HEADS):
            hh = _mlstm_head_output(h, direction, q_ref, k_ref, v_ref, g_ref, gt_ref,
                                    c_ref, n_ref, m_ref, qk, vh)
            cols = slice(h * vh, (h + 1) * vh)
            if direction:
                out_ref[:, cols] = hh
            else:
                hs = hh + hb_ref[:, cols]
                hs = hs * lax.rsqrt(jnp.mean(hs * hs, axis=-1, keepdims=True) + EPS)
                out_ref[:, cols] = (hs * hg_ref[:, cols] * og_ref[:, cols].astype(F32)).astype(BF16)
            _mlstm_state_update(h, direction, k_ref, v_ref, g_ref, gt_ref, c_ref, n_ref, m_ref, qk, vh)


def _mlstm(direction, q, k, v, g, gt, kc, vc, gc, gtc, extra, bsz, head_g=None):
    n, qk_all = q.shape
    v_all = v.shape[1]
    qk, vh = qk_all // N_HEADS, v_all // N_HEADS
    nc = n // bsz // CHUNK
    ncc = kc.shape[0] // bsz // CHUNK

    def lat(b, s):
        j = jnp.clip(s - ncc, 0, nc - 1)
        return b * nc + (nc - 1 - j if direction else j)

    def ctx(b, s):
        j = jnp.clip(s, 0, ncc - 1)
        return b * ncc + (ncc - 1 - j if direction else j)

    lat_row = lambda b, s: (lat(b, s), 0)
    lat_col = lambda b, s: (0, lat(b, s))
    ctx_row = lambda b, s: (ctx(b, s), 0)
    ctx_col = lambda b, s: (0, ctx(b, s))
    in_specs = [pl.BlockSpec((CHUNK, qk_all), lat_row), pl.BlockSpec((CHUNK, qk_all), lat_row),
                pl.BlockSpec((CHUNK, v_all), lat_row), pl.BlockSpec((CHUNK, N_GATES), lat_row),
                pl.BlockSpec((N_GATES, CHUNK), lat_col),
                pl.BlockSpec((CHUNK, qk_all), ctx_row), pl.BlockSpec((CHUNK, v_all), ctx_row),
                pl.BlockSpec((CHUNK, N_GATES), ctx_row), pl.BlockSpec((N_GATES, CHUNK), ctx_col)]
    args = [q, k, v, g, gt, kc, vc, gc, gtc]
    if direction:
        out_dtype = F32
    else:
        hb, og = extra
        in_specs += [pl.BlockSpec((CHUNK, v_all), lat_row), pl.BlockSpec((CHUNK, v_all), lat_row),
                     pl.BlockSpec((1, v_all), lambda b, s: (0, 0))]
        args += [hb, og, head_g]
        out_dtype = BF16
    return pl.pallas_call(
        functools.partial(_mlstm_body, direction=direction, n_ctx_chunks=ncc, qk=qk, vh=vh),
        grid=(bsz, ncc + nc),
        in_specs=in_specs,
        out_specs=pl.BlockSpec((CHUNK, v_all), lat_row),
        out_shape=jax.ShapeDtypeStruct((n, v_all), out_dtype),
        scratch_shapes=[pltpu.VMEM((N_HEADS, qk, vh), F32),
                        pltpu.VMEM((N_HEADS, SUBLANES, qk), F32),
                        pltpu.VMEM((N_HEADS, SUBLANES, LANES), F32)],
        compiler_params=_params(2),
        name="mlstm_bwd" if direction else "mlstm_fwd",
    )(*args)


def _outproj_body(conv_ref, ml_ref, x_ref, gt1_ref, sh2_ref, sc2_ref, g2_ref, wo_ref, wr_ref, br_ref,
                  x1_ref, h_ref, idx_ref, gate_ref, rank_ref, cnt_ref, carry_ref):
    tm = x_ref.shape[0]
    half = conv_ref.shape[1]

    @pl.when(pl.program_id(0) == 0)
    def _():
        carry_ref[...] = jnp.zeros_like(carry_ref)

    y = (jnp.dot(conv_ref[...], wo_ref[0:half, :], preferred_element_type=F32)
         + jnp.dot(ml_ref[...], wo_ref[half:2 * half, :], preferred_element_type=F32))
    x1 = x_ref[...] + gt1_ref[0] * y
    x1_ref[...] = x1
    hn = _norm_mod(x1, g2_ref[...], sh2_ref[0], sc2_ref[0])
    h_ref[...] = hn

    scores = jax.nn.sigmoid(jnp.dot(hn, wr_ref[...], precision=_HIGHEST, preferred_element_type=F32))
    lane = lax.broadcasted_iota(I32, (tm, LANES), 1).astype(F32)
    biased = jnp.where(lane < N_EXPERTS, scores + br_ref[...], _NEG_INF)
    onehot = jnp.zeros((tm, LANES), F32)
    picks, sels = [], []
    for _ in range(TOP_K):
        mx = jnp.max(biased, axis=1, keepdims=True)
        pick = jnp.min(jnp.where(biased == mx, lane, float(LANES)), axis=1, keepdims=True)
        hit = lane == pick
        sels.append(jnp.sum(jnp.where(hit, scores, 0.0), axis=1, keepdims=True))
        picks.append(pick)
        biased = jnp.where(hit, _NEG_INF, biased)
        onehot = onehot + hit.astype(F32)
    total = sels[0]
    for s in sels[1:]:
        total = total + s

    r = lax.broadcasted_iota(I32, (tm, tm), 0)
    c = lax.broadcasted_iota(I32, (tm, tm), 1)
    strict = jnp.where(c < r, 1.0, 0.0).astype(BF16)
    before = jnp.dot(strict, onehot.astype(BF16), preferred_element_type=F32) + carry_ref[...]
    slot = lax.broadcasted_iota(I32, (tm, SUBLANES), 1)
    idx_out = jnp.zeros((tm, SUBLANES), F32)
    gate_out = jnp.zeros((tm, SUBLANES), F32)
    rank_out = jnp.zeros((tm, SUBLANES), F32)
    for j in range(TOP_K):
        rank = jnp.sum(jnp.where(lane == picks[j], before, 0.0), axis=1, keepdims=True)
        idx_out = jnp.where(slot == j, picks[j], idx_out)
        gate_out = jnp.where(slot == j, sels[j] / total * ROUTED_SCALE, gate_out)
        rank_out = jnp.where(slot == j, rank, rank_out)
    idx_ref[...] = idx_out.astype(I32)
    gate_ref[...] = gate_out
    rank_ref[...] = rank_out.astype(I32)
    carry_ref[...] = carry_ref[...] + jnp.sum(onehot, axis=0, keepdims=True)
    cnt_ref[...] = jnp.broadcast_to(carry_ref[...], cnt_ref.shape).astype(I32)


def _outproj(conv, ml, x2d, mod, g2, w_out, w_router, b_router, rows_per_batch):
    n, d = x2d.shape
    tm = ROW_TILE
    tiles_per_batch = rows_per_batch // tm
    row = lambda i: (i, 0)
    mod_block = (1, 1, d)
    half = conv.shape[1]
    return pl.pallas_call(
        _outproj_body,
        grid=(n // tm,),
        in_specs=[pl.BlockSpec((tm, half), row), pl.BlockSpec((tm, half), row), pl.BlockSpec((tm, d), row),
                  pl.BlockSpec(mod_block, _mod_spec(2, tiles_per_batch)),
                  pl.BlockSpec(mod_block, _mod_spec(3, tiles_per_batch)),
                  pl.BlockSpec(mod_block, _mod_spec(4, tiles_per_batch)),
                  _resident(g2.shape), _resident(w_out.shape), _resident(w_router.shape),
                  _resident(b_router.shape)],
        out_specs=(pl.BlockSpec((tm, d), row), pl.BlockSpec((tm, d), row),
                   pl.BlockSpec((tm, SUBLANES), row), pl.BlockSpec((tm, SUBLANES), row),
                   pl.BlockSpec((tm, SUBLANES), row),
                   pl.BlockSpec((SUBLANES, LANES), lambda i: (0, 0))),
        out_shape=(jax.ShapeDtypeStruct((n, d), F32), jax.ShapeDtypeStruct((n, d), F32),
                   jax.ShapeDtypeStruct((n, SUBLANES), I32), jax.ShapeDtypeStruct((n, SUBLANES), F32),
                   jax.ShapeDtypeStruct((n, SUBLANES), I32),
                   jax.ShapeDtypeStruct((SUBLANES, LANES), I32)),
        scratch_shapes=[pltpu.VMEM((1, LANES), F32)],
        compiler_params=_params(1),
        name="outproj_router",
    )(conv, ml, x2d, mod, mod, mod, g2, w_out, w_router, b_router)


def _moe_body(be_ref, par_ref, pfe_ref, pflo_ref, pfhi_ref, nb_ref,
              code_ref, coden_ref, h_hbm, wg_hbm, wu_hbm, wd_hbm,
              y_hbm,
              xbuf, ybuf, wgu, wd, stage_a, stage_d, gsem, ssem, wsem, *, d_expert, n_tok):
    b = pl.program_id(0)
    nb = nb_ref[0]
    slot = b % 2
    d_model = xbuf.shape[2]
    rows_a = d_model // WEIGHT_PARTS
    rows_d = d_expert // WEIGHT_PARTS

    def start_gather(codes, dst_slot):
        def body(r, carry):
            tok = jnp.minimum(codes[0, 0, r] >> CODE_SHIFT, n_tok - 1)
            pltpu.make_async_copy(h_hbm.at[pl.ds(tok, 1)], xbuf.at[dst_slot, pl.ds(r, 1)],
                                  gsem.at[dst_slot]).start()
            return carry
        lax.fori_loop(0, MOE_BLOCK, body, 0, unroll=ROW_DMA_UNROLL)

    def wait_gather(s):
        pltpu.make_async_copy(h_hbm.at[pl.ds(0, MOE_BLOCK)], xbuf.at[s], gsem.at[s]).wait()

    def wait_scatter(s):
        pltpu.make_async_copy(ybuf.at[s], y_hbm.at[pl.ds(0, MOE_BLOCK), pl.ds(0, d_model)],
                              ssem.at[s]).wait()

    def part_copies(e, i, s):
        return (pltpu.make_async_copy(wg_hbm.at[e, pl.ds(i * rows_a, rows_a)], stage_a.at[s, 0],
                                      wsem.at[s, 0]),
                pltpu.make_async_copy(wu_hbm.at[e, pl.ds(i * rows_a, rows_a)], stage_a.at[s, 1],
                                      wsem.at[s, 1]),
                pltpu.make_async_copy(wd_hbm.at[e, pl.ds(i * rows_d, rows_d)], stage_d.at[s],
                                      wsem.at[s, 2]))

    def start_part(e, i):
        for cp in part_copies(e, i, i % 2):
            cp.start()

    def finish_part(e, i, par):
        s = i % 2
        for cp in part_copies(e, i, s):
            cp.wait()
        ra = pl.multiple_of(i * rows_a, rows_a)
        rd = pl.multiple_of(i * rows_d, rows_d)
        wgu[par, pl.ds(ra, rows_a), 0:d_expert] = stage_a[s, 0].astype(BF16)
        wgu[par, pl.ds(ra, rows_a), d_expert:2 * d_expert] = stage_a[s, 1].astype(BF16)
        wd[par, pl.ds(rd, rows_d), :] = stage_d[s].astype(BF16)

    def load_parts(e, par, lo, hi):
        def body(i, carry):
            finish_part(e, i, par)

            @pl.when(i + 2 < hi)
            def _():
                start_part(e, i + 2)
            return carry
        lax.fori_loop(lo, hi, body, 0)

    def start_first_two(e, lo, hi):
        @pl.when(lo < hi)
        def _():
            start_part(e, lo)

        @pl.when(lo + 1 < hi)
        def _():
            start_part(e, lo + 1)

    @pl.when(b == 0)
    def _():
        ybuf[0] = jnp.zeros(ybuf.shape[1:], F32)
        for j in range(TOP_K):
            fill = pltpu.make_async_copy(
                ybuf.at[0, pl.ds(0, SPARE_TOKENS)],
                y_hbm.at[pl.ds(n_tok, SPARE_TOKENS), pl.ds(j * d_model, d_model)], ssem.at[0])
            fill.start()
            fill.wait()
        start_gather(code_ref, 0)
        start_first_two(be_ref[0], 0, WEIGHT_PARTS)
        load_parts(be_ref[0], par_ref[0], 0, WEIGHT_PARTS)

    @pl.when(b + 1 < nb)
    def _():
        start_gather(coden_ref, 1 - slot)

    @pl.when(b < nb)
    def _():
        par = par_ref[b]
        next_e, lo, hi = pfe_ref[b], pflo_ref[b], pfhi_ref[b]
        start_first_two(next_e, lo, hi)
        wait_gather(slot)

        @pl.when(b >= 2)
        def _():
            wait_scatter(slot)

        x = xbuf[slot].astype(BF16)
        gu = jnp.dot(x, wgu[par], preferred_element_type=F32)
        hb = (_silu(gu[:, 0:d_expert]) * gu[:, d_expert:2 * d_expert]).astype(BF16)
        ybuf[slot] = jnp.dot(hb, wd[par], preferred_element_type=F32)

        def scatter(r, carry):
            code = code_ref[0, 0, r]
            col = pl.multiple_of((code & (2 ** CODE_SHIFT - 1)) * d_model, d_model)
            pltpu.make_async_copy(ybuf.at[slot, pl.ds(r, 1)],
                                  y_hbm.at[pl.ds(code >> CODE_SHIFT, 1), pl.ds(col, d_model)],
                                  ssem.at[slot]).start()
            return carry
        lax.fori_loop(0, MOE_BLOCK, scatter, 0, unroll=ROW_DMA_UNROLL)

        load_parts(next_e, 1 - par, lo, hi)

        @pl.when(b == nb - 1)
        def _():
            @pl.when(b >= 1)
            def _():
                wait_scatter(1 - slot)
            wait_scatter(slot)


def _moe(h, we_gate, we_up, we_down, tables, slot_code):
    n, d = h.shape
    d_expert = we_gate.shape[2]
    nb_max = slot_code.shape[0]
    smem_blk = (1, 1, MOE_BLOCK)
    any_spec = pl.BlockSpec(memory_space=pl.ANY)
    grid_spec = pltpu.PrefetchScalarGridSpec(
        num_scalar_prefetch=len(tables),
        grid=(nb_max,),
        in_specs=[pl.BlockSpec(smem_blk, lambda b, *_: (b, 0, 0), memory_space=pltpu.SMEM),
                  pl.BlockSpec(smem_blk, lambda b, *_: (jnp.minimum(b + 1, nb_max - 1), 0, 0),
                               memory_space=pltpu.SMEM),
                  any_spec, any_spec, any_spec, any_spec],
        out_specs=any_spec,
        scratch_shapes=[pltpu.VMEM((2, MOE_BLOCK, d), F32),
                        pltpu.VMEM((2, MOE_BLOCK, d), F32),
                        pltpu.VMEM((2, d, 2 * d_expert), BF16),
                        pltpu.VMEM((2, d_expert, d), BF16),
                        pltpu.VMEM((2, 2, d // WEIGHT_PARTS, d_expert), F32),
                        pltpu.VMEM((2, d_expert // WEIGHT_PARTS, d), F32),
                        pltpu.SemaphoreType.DMA((2,)),
                        pltpu.SemaphoreType.DMA((2,)),
                        pltpu.SemaphoreType.DMA((2, 3))],
    )
    return pl.pallas_call(
        functools.partial(_moe_body, d_expert=d_expert, n_tok=n),
        grid_spec=grid_spec,
        out_shape=jax.ShapeDtypeStruct((n + SPARE_TOKENS, TOP_K * d), F32),
        compiler_params=pltpu.CompilerParams(
            dimension_semantics=("arbitrary",), vmem_limit_bytes=MOE_VMEM_LIMIT,
            disable_bounds_checks=True),
        name="moe_routed",
    )(*tables, slot_code, slot_code, h, we_gate, we_up, we_down)


def _final_body(h_ref, x1_ref, y_ref, gate_ref, gt2_ref, wsgu_ref, wsd_ref, fg_ref, out_ref, *, d_shared):
    d = x1_ref.shape[1]
    gu = jnp.dot(h_ref[...].astype(BF16), wsgu_ref[...], preferred_element_type=F32)
    hb = (_silu(gu[:, 0:d_shared]) * gu[:, d_shared:2 * d_shared]).astype(BF16)
    acc = jnp.dot(hb, wsd_ref[...], preferred_element_type=F32)
    for j in range(TOP_K):
        acc = acc + gate_ref[:, j:j + 1] * y_ref[:, j * d:(j + 1) * d]
    x2 = x1_ref[...] + gt2_ref[0] * acc
    out_ref[...] = x2 * lax.rsqrt(jnp.mean(x2 * x2, axis=-1, keepdims=True) + EPS) * fg_ref[...]


def _final(h, x1, y_tok, gates, mod, ws_gu, ws_d, final_g, rows_per_batch):
    n, d = x1.shape
    tm = FINAL_TILE
    tiles_per_batch = rows_per_batch // tm
    row = lambda i: (i, 0)
    return pl.pallas_call(
        functools.partial(_final_body, d_shared=ws_d.shape[0]),
        grid=(n // tm,),
        in_specs=[pl.BlockSpec((tm, d), row), pl.BlockSpec((tm, d), row),
                  pl.BlockSpec((tm, TOP_K * d), row), pl.BlockSpec((tm, SUBLANES), row),
                  pl.BlockSpec((1, 1, d), _mod_spec(5, tiles_per_batch)),
                  _resident(ws_gu.shape), _resident(ws_d.shape), _resident(final_g.shape)],
        out_specs=pl.BlockSpec((tm, d), row),
        out_shape=jax.ShapeDtypeStruct((n, d), F32),
        compiler_params=_params(1),
        name="shared_combine_final",
    )(h, x1, y_tok, gates, mod, ws_gu, ws_d, final_g)


def _routing_tables(idx, rank, counts, n_tok):
    nb_max = -(-(n_tok * TOP_K) // MOE_BLOCK) + N_EXPERTS
    nblk = (counts + MOE_BLOCK - 1) // MOE_BLOCK
    blk_end = jnp.cumsum(nblk)
    blk_start = blk_end - nblk
    dest = (blk_start * MOE_BLOCK)[idx] + rank
    tok = jnp.arange(n_tok, dtype=I32)[:, None]
    real_code = (tok << CODE_SHIFT) | jnp.arange(TOP_K, dtype=I32)[None, :]
    slots = jnp.arange(nb_max * MOE_BLOCK, dtype=I32)
    spare = ((slots // MOE_BLOCK) % 2) * MOE_BLOCK + slots % MOE_BLOCK
    pad_code = ((n_tok + spare // TOP_K) << CODE_SHIFT) | (spare % TOP_K)
    slot_code = pad_code.at[dest.reshape(-1)].set(real_code.reshape(-1))

    blocks = jnp.arange(nb_max, dtype=I32)
    experts = jnp.arange(N_EXPERTS, dtype=I32)
    block_e = jnp.minimum(jnp.searchsorted(blk_end, blocks, side="right"), N_EXPERTS - 1).astype(I32)
    nonempty = nblk > 0
    parity = ((jnp.cumsum(nonempty.astype(I32)) - 1) % 2)[block_e]
    later = lax.cummin(jnp.where(nonempty, experts, N_EXPERTS), reverse=True)
    next_e = jnp.concatenate([later[1:], jnp.full((1,), N_EXPERTS, I32)])[block_e]
    k_in_e = blocks - blk_start[block_e]
    nb_e = jnp.maximum(nblk[block_e], 1)
    live = (next_e < N_EXPERTS) & (blocks < blk_end[-1])
    lo = jnp.where(live, WEIGHT_PARTS * k_in_e // nb_e, 0)
    hi = jnp.where(live, WEIGHT_PARTS * (k_in_e + 1) // nb_e, 0)
    tables = (block_e, parity.astype(I32), jnp.minimum(next_e, N_EXPERTS - 1).astype(I32),
              lo.astype(I32), hi.astype(I32), blk_end[-1:].astype(I32))
    return tables, slot_code.reshape(nb_max, 1, MOE_BLOCK)


def kernel(x, c, ctx, c_ctx, norm1_g, norm2_g, w_ada, b_ada, w_in, conv_w, gate_b, head_g, w_out,
           w_router, b_router, we_gate, we_up, we_down, ws_gate, ws_up, ws_down, final_g):
    assert w_ada.shape[0] == 1, "single-layer block"
    bsz, seq, d = x.shape
    ctx_len = ctx.shape[1]
    n_tok = bsz * seq
    conv_dim = conv_w.shape[2]
    v_all = head_g.shape[1]
    qk_all = (w_in.shape[2] - 3 * conv_dim - 2 * v_all - N_GATES) // 2
    assert seq % ROW_TILE == 0 and ctx_len % ROW_TILE == 0 and ROW_TILE % GRID_W == 0
    assert bsz + 1 <= SUBLANES

    cc = jnp.zeros((SUBLANES, d), F32).at[:bsz].set(c).at[bsz].set(c_ctx)
    mod = _adaln(cc, w_ada[0], b_ada).reshape(SUBLANES * 6, 1, d)

    n_main = 3 * conv_dim + 2 * qk_all + 2 * v_all
    w_main = w_in[0, :, :n_main].astype(BF16)
    w_kv = w_in[0, :, 3 * conv_dim + qk_all:3 * conv_dim + 2 * qk_all + v_all].astype(BF16)
    w_gate = jnp.zeros((d, LANES), BF16).at[:, :N_GATES].set(w_in[0, :, n_main:].astype(BF16))
    gate_bias = jnp.zeros((1, LANES), F32).at[0, :N_GATES].set(gate_b[0].reshape(-1))

    x2d = x.reshape(n_tok, d)
    conv, q, k, v, og, g, gt = _inproj(x2d, mod, norm1_g, w_main, w_gate, gate_bias, conv_w[0],
                                       seq, conv_dim, qk_all, v_all)
    kc, vc, gc, gtc = _inproj_ctx(ctx.reshape(bsz * ctx_len, d), mod, norm1_g, w_kv, w_gate, gate_bias,
                                  bsz, qk_all, v_all)

    h_bwd = _mlstm(1, q, k, v, g, gt, kc, vc, gc, gtc, None, bsz)
    ml = _mlstm(0, q, k, v, g, gt, kc, vc, gc, gtc, (h_bwd, og), bsz, head_g)

    w_r = jnp.zeros((d, LANES), F32).at[:, :N_EXPERTS].set(w_router[0])
    b_r = jnp.zeros((1, LANES), F32).at[0, :N_EXPERTS].set(b_router[0])
    x1, hn, idx, gates, rank, cnt = _outproj(conv, ml, x2d, mod, norm2_g, w_out[0].astype(BF16),
                                             w_r, b_r, seq)

    tables, slot_code = _routing_tables(idx[:, :TOP_K], rank[:, :TOP_K], cnt[0, :N_EXPERTS], n_tok)
    y_tok = _moe(hn, we_gate[0], we_up[0], we_down[0], tables, slot_code)

    ws_gu = jnp.concatenate([ws_gate[0], ws_up[0]], axis=1).astype(BF16)
    out = _final(hn, x1, y_tok, gates, mod, ws_gu, ws_down[0].astype(BF16),
                 final_g.reshape(1, d), seq)
    return out.reshape(bsz, seq, d)
```

```python
import functools

import jax
import jax.numpy as jnp
from jax import lax
from jax.experimental import pallas as pl
from jax.experimental.pallas import tpu as pltpu

F32 = jnp.float32
BF16 = jnp.bfloat16
I32 = jnp.int32

N_HEADS = 4
GRID_W = 64
CHUNK = 128
TOP_K = 6
N_EXPERTS = 64
ROUTED_SCALE = 2.446
EPS = 1e-6
N_GATES = 4 * N_HEADS

LANES = 128
SUBLANES = 8
MOE_BLOCK = 256
ROW_TILE = 256
FINAL_TILE = 128
ADALN_TILE = 1024
WEIGHT_PARTS = 8
CODE_SHIFT = 3
SPARE_TOKENS = 88
VMEM_LIMIT = 56 * 1024 * 1024
MOE_VMEM_LIMIT = 62 * 1024 * 1024

_HIGHEST = lax.Precision.HIGHEST
_NEG_INF = float("-inf")


def _resident(shape):
    nd = len(shape)
    return pl.BlockSpec(shape, lambda *_: (0,) * nd, pipeline_mode=pl.Buffered(1))


def _params(n_axes):
    return pltpu.CompilerParams(
        dimension_semantics=("arbitrary",) * n_axes, vmem_limit_bytes=VMEM_LIMIT)


def _log_sigmoid(x):
    return jnp.minimum(x, 0.0) - jnp.log1p(jnp.exp(-jnp.abs(x)))


def _silu(x):
    return x * jax.nn.sigmoid(x)


def _adaln_body(c_ref, w_ref, b_ref, o_ref):
    s = _silu(c_ref[...])
    o_ref[...] = jnp.dot(s.astype(BF16), w_ref[...].astype(BF16),
                         preferred_element_type=F32) + b_ref[...]


def _adaln(cc, w, b):
    d, n6 = w.shape
    return pl.pallas_call(
        _adaln_body,
        grid=(n6 // ADALN_TILE,),
        in_specs=[pl.BlockSpec((SUBLANES, d), lambda j: (0, 0)),
                  pl.BlockSpec((d, ADALN_TILE), lambda j: (0, j)),
                  pl.BlockSpec((1, ADALN_TILE), lambda j: (0, j))],
        out_specs=pl.BlockSpec((SUBLANES, ADALN_TILE), lambda j: (0, j)),
        out_shape=jax.ShapeDtypeStruct((SUBLANES, n6), F32),
        compiler_params=_params(1),
        name="adaln",
    )(cc, w, b)


def _norm_mod(x, g, shift, scale):
    y = x * lax.rsqrt(jnp.mean(x * x, axis=-1, keepdims=True) + EPS) * g
    return y * (1.0 + scale) + shift


def _gate_prep(xb, wg_ref, gb_ref, g_ref, gt_ref):
    tm = xb.shape[0]
    gg = jnp.dot(xb, wg_ref[...], preferred_element_type=F32) + gb_ref[...]
    lane = lax.broadcasted_iota(I32, (tm, LANES), 1)
    is_f = (lane & N_HEADS) != 0
    is_bwd = (lane & (2 * N_HEADS)) != 0
    lf = jnp.where(is_f, _log_sigmoid(gg), 0.0)
    r = lax.broadcasted_iota(I32, (tm, tm), 0)
    c = lax.broadcasted_iota(I32, (tm, tm), 1)
    same = (r // CHUNK) == (c // CHUNK)
    tri_l = jnp.where(same & (c <= r), 1.0, 0.0).astype(F32)
    tri_u = jnp.where(same & (c >= r), 1.0, 0.0).astype(F32)
    pre = jnp.dot(tri_l, lf, precision=_HIGHEST, preferred_element_type=F32)
    suf = jnp.dot(tri_u, lf, precision=_HIGHEST, preferred_element_type=F32)
    out = jnp.where(is_f, jnp.where(is_bwd, suf, pre), gg)
    g_ref[...] = out[:, :N_GATES]
    gt_ref[...] = out.T[:N_GATES, :]


def _project_transposed(wt_ref, xb):
    return lax.dot_general(wt_ref[...], xb, (((1,), (1,)), ((), ())),
                           preferred_element_type=F32).astype(BF16)


def _inproj_body(x_ref, sh_ref, sc_ref, g1_ref, w_ref, wkt_ref, wg_ref, gb_ref, cw_ref,
                 conv_ref, q_ref, k_ref, v_ref, o_ref, g_ref, gt_ref, *, conv_dim, qk_all, v_all):
    tm = x_ref.shape[0]
    xb = _norm_mod(x_ref[...], g1_ref[...], sh_ref[0], sc_ref[0]).astype(BF16)

    def proj(lo, width):
        return jnp.dot(xb, w_ref[:, lo:lo + width], preferred_element_type=F32)

    u = proj(conv_dim, conv_dim) * proj(2 * conv_dim, conv_dim)
    pos = lax.broadcasted_iota(I32, (tm, 1), 0) % GRID_W
    um = jnp.where(pos == 0, 0.0, pltpu.roll(u, 1, axis=0))
    up = jnp.where(pos == GRID_W - 1, 0.0, pltpu.roll(u, tm - 1, axis=0))
    y = um * cw_ref[0:1, :] + u * cw_ref[1:2, :] + up * cw_ref[2:3, :]
    conv_ref[...] = (proj(0, conv_dim) * y).astype(BF16)

    off = 3 * conv_dim
    qscale = (qk_all // N_HEADS) ** -0.5
    q_ref[...] = (proj(off, qk_all) * qscale).astype(BF16)
    k_ref[...] = _project_transposed(wkt_ref, xb)
    v_ref[...] = proj(off + 2 * qk_all, v_all).astype(BF16)
    o_ref[...] = jax.nn.sigmoid(proj(off + 2 * qk_all + v_all, v_all)).astype(BF16)
    _gate_prep(xb, wg_ref, gb_ref, g_ref, gt_ref)


def _inproj_ctx_body(x_ref, sh_ref, sc_ref, g1_ref, w_ref, wkt_ref, wg_ref, gb_ref,
                     k_ref, v_ref, g_ref, gt_ref):
    xb = _norm_mod(x_ref[...], g1_ref[...], sh_ref[0], sc_ref[0]).astype(BF16)
    k_ref[...] = _project_transposed(wkt_ref, xb)
    v_ref[...] = jnp.dot(xb, w_ref[...], preferred_element_type=F32).astype(BF16)
    _gate_prep(xb, wg_ref, gb_ref, g_ref, gt_ref)


def _mod_spec(part, tiles_per_row, fixed_row=None):
    def index(i):
        row = fixed_row if fixed_row is not None else i // tiles_per_row
        return (row * 6 + part, 0, 0)

    return index


def _inproj(x2d, mod, g1, w_main, w_kt, w_gate, gate_b, conv_w, rows_per_batch, conv_dim, qk_all, v_all):
    n, d = x2d.shape
    tm = ROW_TILE
    tiles_per_batch = rows_per_batch // tm
    row = lambda i: (i, 0)
    mod_block = (1, 1, d)
    out_shapes = (
        jax.ShapeDtypeStruct((n, conv_dim), BF16),
        jax.ShapeDtypeStruct((n, qk_all), BF16),
        jax.ShapeDtypeStruct((qk_all, n), BF16),
        jax.ShapeDtypeStruct((n, v_all), BF16),
        jax.ShapeDtypeStruct((n, v_all), BF16),
        jax.ShapeDtypeStruct((n, N_GATES), F32),
        jax.ShapeDtypeStruct((N_GATES, n), F32),
    )
    out_specs = (
        pl.BlockSpec((tm, conv_dim), row),
        pl.BlockSpec((tm, qk_all), row),
        pl.BlockSpec((qk_all, tm), lambda i: (0, i)),
        pl.BlockSpec((tm, v_all), row),
        pl.BlockSpec((tm, v_all), row),
        pl.BlockSpec((tm, N_GATES), row),
        pl.BlockSpec((N_GATES, tm), lambda i: (0, i)),
    )
    return pl.pallas_call(
        functools.partial(_inproj_body, conv_dim=conv_dim, qk_all=qk_all, v_all=v_all),
        grid=(n // tm,),
        in_specs=[pl.BlockSpec((tm, d), row),
                  pl.BlockSpec(mod_block, _mod_spec(0, tiles_per_batch)),
                  pl.BlockSpec(mod_block, _mod_spec(1, tiles_per_batch)),
                  _resident(g1.shape), _resident(w_main.shape), _resident(w_kt.shape),
                  _resident(w_gate.shape), _resident(gate_b.shape), _resident(conv_w.shape)],
        out_specs=out_specs,
        out_shape=out_shapes,
        compiler_params=_params(1),
        name="inproj",
    )(x2d, mod, mod, g1, w_main, w_kt, w_gate, gate_b, conv_w)


def _inproj_ctx(c2d, mod, g1, w_v, w_kt, w_gate, gate_b, ctx_mod_row):
    n, d = c2d.shape
    tm = ROW_TILE
    row = lambda i: (i, 0)
    mod_block = (1, 1, d)
    qk_all, v_all = w_kt.shape[0], w_v.shape[1]
    return pl.pallas_call(
        _inproj_ctx_body,
        grid=(n // tm,),
        in_specs=[pl.BlockSpec((tm, d), row),
                  pl.BlockSpec(mod_block, _mod_spec(0, 1, ctx_mod_row)),
                  pl.BlockSpec(mod_block, _mod_spec(1, 1, ctx_mod_row)),
                  _resident(g1.shape), _resident(w_v.shape), _resident(w_kt.shape),
                  _resident(w_gate.shape), _resident(gate_b.shape)],
        out_specs=(pl.BlockSpec((qk_all, tm), lambda i: (0, i)), pl.BlockSpec((tm, v_all), row),
                   pl.BlockSpec((tm, N_GATES), row), pl.BlockSpec((N_GATES, tm), lambda i: (0, i))),
        out_shape=(jax.ShapeDtypeStruct((qk_all, n), BF16), jax.ShapeDtypeStruct((n, v_all), BF16),
                   jax.ShapeDtypeStruct((n, N_GATES), F32), jax.ShapeDtypeStruct((N_GATES, n), F32)),
        compiler_params=_params(1),
        name="inproj_ctx",
    )(c2d, mod, mod, g1, w_v, w_kt, w_gate, gate_b)


def _with_ones(v):
    return jnp.concatenate([v, jnp.ones((v.shape[0], LANES), v.dtype)], axis=1)


def _mlstm_state_update(h, direction, kt_ref, v_ref, gt_ref, s_ref, m_ref, qk, vh):
    ci = direction * 2 * N_HEADS + h
    cb = ci + N_HEADS
    last = 0 if direction else CHUNK - 1
    kt = kt_ref[h * qk:(h + 1) * qk, :].astype(F32)
    va = _with_ones(v_ref[:, h * vh:(h + 1) * vh])
    b_last = gt_ref[cb:cb + 1, last:last + 1]
    m_prev = m_ref[h][0:1, 0:1]
    g_r = b_last - gt_ref[cb:cb + 1, :] + gt_ref[ci:ci + 1, :]
    m_new = jnp.maximum(b_last + m_prev, jnp.max(g_r, axis=1, keepdims=True))
    a = jnp.exp(b_last + m_prev - m_new)
    kw = (kt * jnp.exp(g_r - m_new)).astype(BF16)
    s_ref[h] = a * s_ref[h] + jnp.dot(kw, va, preferred_element_type=F32)
    m_ref[h] = jnp.broadcast_to(m_new, m_ref.shape[1:])


def _mlstm_head_output(h, direction, q_ref, kt_ref, v_ref, g_ref, gt_ref, s_ref, m_ref, qk, vh):
    ci = direction * 2 * N_HEADS + h
    cb = ci + N_HEADS
    q = q_ref[:, h * qk:(h + 1) * qk]
    kt = kt_ref[h * qk:(h + 1) * qk, :]
    va = _with_ones(v_ref[:, h * vh:(h + 1) * vh])
    ig_r = gt_ref[ci:ci + 1, :]
    b_r = gt_ref[cb:cb + 1, :]
    b_c = g_ref[:, cb:cb + 1]
    m_prev = m_ref[h][0:1, 0:1]
    row = lax.broadcasted_iota(I32, (CHUNK, CHUNK), 0)
    col = lax.broadcasted_iota(I32, (CHUNK, CHUNK), 1)
    mask = (col >= row) if direction else (col <= row)
    dm = jnp.where(mask, b_c + (ig_r - b_r), _NEG_INF)
    inter = b_c + m_prev
    m_t = jnp.maximum(inter, jnp.max(dm, axis=1, keepdims=True))
    w_inter = jnp.exp(inter - m_t)
    s = jnp.dot(q, kt, preferred_element_type=F32) * jnp.exp(dm - m_t)
    intra = jnp.dot(s.astype(BF16), va, preferred_element_type=F32)
    carried = jnp.dot(q, s_ref[h].astype(BF16), preferred_element_type=F32)
    num = intra[:, 0:vh] + w_inter * carried[:, 0:vh]
    den = intra[:, vh:vh + 1] + w_inter * carried[:, vh:vh + 1]
    return num / jnp.maximum(jnp.abs(den), jnp.exp(-m_t))


def _mlstm_body(*refs, direction, n_ctx_chunks, qk, vh):
    if direction:
        (q_ref, kt_ref, v_ref, g_ref, gt_ref, ktc_ref, vc_ref, gtc_ref,
         out_ref, s_ref, m_ref) = refs
    else:
        (q_ref, kt_ref, v_ref, g_ref, gt_ref, ktc_ref, vc_ref, gtc_ref,
         hb_ref, og_ref, hg_ref, out_ref, s_ref, m_ref) = refs
    step = pl.program_id(1)

    @pl.when(step == 0)
    def _():
        s_ref[...] = jnp.zeros_like(s_ref)
        m_ref[...] = jnp.full_like(m_ref, _NEG_INF)

    @pl.when(step < n_ctx_chunks)
    def _():
        for h in range(N_HEADS):
            _mlstm_state_update(h, direction, ktc_ref, vc_ref, gtc_ref, s_ref, m_ref, qk, vh)

    @pl.when(step >= n_ctx_chunks)
    def _():
        for h in range(N_HEADS):
            hh = _mlstm_head_output(h, direction, q_ref, kt_ref, v_ref, g_ref, gt_ref, s_ref, m_ref, qk, vh)
            cols = slice(h * vh, (h + 1) * vh)
            if direction:
                out_ref[:, cols] = hh
            else:
                hs = hh + hb_ref[:, cols]
                hs = hs * lax.rsqrt(jnp.mean(hs * hs, axis=-1, keepdims=True) + EPS)
                out_ref[:, cols] = (hs * hg_ref[:, cols] * og_ref[:, cols].astype(F32)).astype(BF16)
            _mlstm_state_update(h, direction, kt_ref, v_ref, gt_ref, s_ref, m_ref, qk, vh)


def _mlstm(direction, q, kt, v, g, gt, ktc, vc, gtc, extra, bsz, head_g=None):
    n, qk_all = q.shape
    v_all = v.shape[1]
    qk, vh = qk_all // N_HEADS, v_all // N_HEADS
    nc = n // bsz // CHUNK
    ncc = vc.shape[0] // bsz // CHUNK

    def lat(b, s):
        j = jnp.clip(s - ncc, 0, nc - 1)
        return b * nc + (nc - 1 - j if direction else j)

    def ctx(b, s):
        j = jnp.clip(s, 0, ncc - 1)
        return b * ncc + (ncc - 1 - j if direction else j)

    lat_row = lambda b, s: (lat(b, s), 0)
    lat_col = lambda b, s: (0, lat(b, s))
    ctx_row = lambda b, s: (ctx(b, s), 0)
    ctx_col = lambda b, s: (0, ctx(b, s))
    in_specs = [pl.BlockSpec((CHUNK, qk_all), lat_row), pl.BlockSpec((qk_all, CHUNK), lat_col),
                pl.BlockSpec((CHUNK, v_all), lat_row), pl.BlockSpec((CHUNK, N_GATES), lat_row),
                pl.BlockSpec((N_GATES, CHUNK), lat_col),
                pl.BlockSpec((qk_all, CHUNK), ctx_col), pl.BlockSpec((CHUNK, v_all), ctx_row),
                pl.BlockSpec((N_GATES, CHUNK), ctx_col)]
    args = [q, kt, v, g, gt, ktc, vc, gtc]
    if direction:
        out_dtype = F32
    else:
        hb, og = extra
        in_specs += [pl.BlockSpec((CHUNK, v_all), lat_row), pl.BlockSpec((CHUNK, v_all), lat_row),
                     pl.BlockSpec((1, v_all), lambda b, s: (0, 0))]
        args += [hb, og, head_g]
        out_dtype = BF16
    return pl.pallas_call(
        functools.partial(_mlstm_body, direction=direction, n_ctx_chunks=ncc, qk=qk, vh=vh),
        grid=(bsz, ncc + nc),
        in_specs=in_specs,
        out_specs=pl.BlockSpec((CHUNK, v_all), lat_row),
        out_shape=jax.ShapeDtypeStruct((n, v_all), out_dtype),
        scratch_shapes=[pltpu.VMEM((N_HEADS, qk, vh + LANES), F32),
                        pltpu.VMEM((N_HEADS, SUBLANES, LANES), F32)],
        compiler_params=_params(2),
        name="mlstm_bwd" if direction else "mlstm_fwd",
    )(*args)


def _outproj_body(conv_ref, ml_ref, x_ref, gt1_ref, sh2_ref, sc2_ref, g2_ref, wo_ref, wr_ref, br_ref,
                  x1_ref, h_ref, idx_ref, gate_ref, rank_ref, cnt_ref, carry_ref):
    tm = x_ref.shape[0]
    half = conv_ref.shape[1]

    @pl.when(pl.program_id(0) == 0)
    def _():
        carry_ref[...] = jnp.zeros_like(carry_ref)

    y = (jnp.dot(conv_ref[...], wo_ref[0:half, :], preferred_element_type=F32)
         + jnp.dot(ml_ref[...], wo_ref[half:2 * half, :], preferred_element_type=F32))
    x1 = x_ref[...] + gt1_ref[0] * y
    x1_ref[...] = x1
    hn = _norm_mod(x1, g2_ref[...], sh2_ref[0], sc2_ref[0])
    h_ref[...] = hn

    h_hi = hn.astype(BF16)
    h_lo = (hn - h_hi.astype(F32)).astype(BF16)
    parts = (jnp.dot(h_hi, wr_ref[...], preferred_element_type=F32)
             + jnp.dot(h_lo, wr_ref[...], preferred_element_type=F32))
    scores = jax.nn.sigmoid(parts + pltpu.roll(parts, N_EXPERTS, axis=1))
    lane = lax.broadcasted_iota(I32, (tm, LANES), 1).astype(F32)
    biased = jnp.where(lane < N_EXPERTS, scores + br_ref[...], _NEG_INF)
    onehot = jnp.zeros((tm, LANES), F32)
    picks, sels = [], []
    for _ in range(TOP_K):
        mx = jnp.max(biased, axis=1, keepdims=True)
        pick = jnp.min(jnp.where(biased == mx, lane, float(LANES)), axis=1, keepdims=True)
        hit = lane == pick
        sels.append(jnp.sum(jnp.where(hit, scores, 0.0), axis=1, keepdims=True))
        picks.append(pick)
        biased = jnp.where(hit, _NEG_INF, biased)
        onehot = onehot + hit.astype(F32)
    total = sels[0]
    for s in sels[1:]:
        total = total + s

    r = lax.broadcasted_iota(I32, (tm, tm), 0)
    c = lax.broadcasted_iota(I32, (tm, tm), 1)
    strict = jnp.where(c < r, 1.0, 0.0).astype(BF16)
    before = jnp.dot(strict, onehot.astype(BF16), preferred_element_type=F32) + carry_ref[...]
    slot = lax.broadcasted_iota(I32, (tm, SUBLANES), 1)
    idx_out = jnp.zeros((tm, SUBLANES), F32)
    gate_out = jnp.zeros((tm, SUBLANES), F32)
    rank_out = jnp.zeros((tm, SUBLANES), F32)
    for j in range(TOP_K):
        rank = jnp.sum(jnp.where(lane == picks[j], before, 0.0), axis=1, keepdims=True)
        idx_out = jnp.where(slot == j, picks[j], idx_out)
        gate_out = jnp.where(slot == j, sels[j] / total * ROUTED_SCALE, gate_out)
        rank_out = jnp.where(slot == j, rank, rank_out)
    idx_ref[...] = idx_out.astype(I32)
    gate_ref[...] = gate_out
    rank_ref[...] = rank_out.astype(I32)
    carry_ref[...] = carry_ref[...] + jnp.sum(onehot, axis=0, keepdims=True)
    cnt_ref[...] = jnp.broadcast_to(carry_ref[...], cnt_ref.shape).astype(I32)


def _outproj(conv, ml, x2d, mod, g2, w_out, w_router, b_router, rows_per_batch):
    n, d = x2d.shape
    tm = ROW_TILE
    tiles_per_batch = rows_per_batch // tm
    row = lambda i: (i, 0)
    mod_block = (1, 1, d)
    half = conv.shape[1]
    return pl.pallas_call(
        _outproj_body,
        grid=(n // tm,),
        in_specs=[pl.BlockSpec((tm, half), row), pl.BlockSpec((tm, half), row), pl.BlockSpec((tm, d), row),
                  pl.BlockSpec(mod_block, _mod_spec(2, tiles_per_batch)),
                  pl.BlockSpec(mod_block, _mod_spec(3, tiles_per_batch)),
                  pl.BlockSpec(mod_block, _mod_spec(4, tiles_per_batch)),
                  _resident(g2.shape), _resident(w_out.shape), _resident(w_router.shape),
                  _resident(b_router.shape)],
        out_specs=(pl.BlockSpec((tm, d), row), pl.BlockSpec((tm, d), row),
                   pl.BlockSpec((tm, SUBLANES), row), pl.BlockSpec((tm, SUBLANES), row),
                   pl.BlockSpec((tm, SUBLANES), row),
                   pl.BlockSpec((SUBLANES, LANES), lambda i: (0, 0))),
        out_shape=(jax.ShapeDtypeStruct((n, d), F32), jax.ShapeDtypeStruct((n, d), F32),
                   jax.ShapeDtypeStruct((n, SUBLANES), I32), jax.ShapeDtypeStruct((n, SUBLANES), F32),
                   jax.ShapeDtypeStruct((n, SUBLANES), I32),
                   jax.ShapeDtypeStruct((SUBLANES, LANES), I32)),
        scratch_shapes=[pltpu.VMEM((1, LANES), F32)],
        compiler_params=_params(1),
        name="outproj_router",
    )(conv, ml, x2d, mod, mod, mod, g2, w_out, w_router, b_router)


def _moe_body(be_ref, par_ref, pfe_ref, pflo_ref, pfhi_ref, nb_ref,
              code_ref, coden_ref, h_hbm, wg_hbm, wu_hbm, wd_hbm,
              y_hbm,
              xbuf, ybuf, wgu, wd, stage_a, stage_d, gsem, ssem, wsem, *, d_expert, n_tok):
    b = pl.program_id(0)
    nb = nb_ref[0]
    slot = b % 2
    d_model = xbuf.shape[2]
    rows_a = d_model // WEIGHT_PARTS
    rows_d = d_expert // WEIGHT_PARTS

    def start_gather(codes, dst_slot):
        def body(g, carry):
            base = pl.multiple_of(g * SUBLANES, SUBLANES)
            for u in range(SUBLANES):
                tok = jnp.minimum(codes[0, 0, base + u] >> CODE_SHIFT, n_tok - 1)
                pltpu.make_async_copy(h_hbm.at[pl.ds(tok, 1)], xbuf.at[dst_slot, pl.ds(base + u, 1)],
                                      gsem.at[dst_slot]).start()
            return carry
        lax.fori_loop(0, MOE_BLOCK // SUBLANES, body, 0)

    def wait_gather(s):
        pltpu.make_async_copy(h_hbm.at[pl.ds(0, MOE_BLOCK)], xbuf.at[s], gsem.at[s]).wait()

    def wait_scatter(s):
        pltpu.make_async_copy(ybuf.at[s], y_hbm.at[pl.ds(0, MOE_BLOCK), pl.ds(0, d_model)],
                              ssem.at[s]).wait()

    def part_copies(e, i, s):
        return (pltpu.make_async_copy(wg_hbm.at[e, pl.ds(i * rows_a, rows_a)], stage_a.at[s, 0],
                                      wsem.at[s, 0]),
                pltpu.make_async_copy(wu_hbm.at[e, pl.ds(i * rows_a, rows_a)], stage_a.at[s, 1],
                                      wsem.at[s, 1]),
                pltpu.make_async_copy(wd_hbm.at[e, pl.ds(i * rows_d, rows_d)], stage_d.at[s],
                                      wsem.at[s, 2]))

    def start_part(e, i):
        for cp in part_copies(e, i, i % 2):
            cp.start()

    def finish_part(e, i, par):
        s = i % 2
        for cp in part_copies(e, i, s):
            cp.wait()
        ra = pl.multiple_of(i * rows_a, rows_a)
        rd = pl.multiple_of(i * rows_d, rows_d)
        wgu[par, pl.ds(ra, rows_a), 0:d_expert] = stage_a[s, 0].astype(BF16)
        wgu[par, pl.ds(ra, rows_a), d_expert:2 * d_expert] = stage_a[s, 1].astype(BF16)
        wd[par, pl.ds(rd, rows_d), :] = stage_d[s].astype(BF16)

    def load_parts(e, par, lo, hi):
        def body(i, carry):
            finish_part(e, i, par)

            @pl.when(i + 2 < hi)
            def _():
                start_part(e, i + 2)
            return carry
        lax.fori_loop(lo, hi, body, 0)

    def start_first_two(e, lo, hi):
        @pl.when(lo < hi)
        def _():
            start_part(e, lo)

        @pl.when(lo + 1 < hi)
        def _():
            start_part(e, lo + 1)

    @pl.when(b == 0)
    def _():
        ybuf[0] = jnp.zeros(ybuf.shape[1:], F32)
        for j in range(TOP_K):
            fill = pltpu.make_async_copy(
                ybuf.at[0, pl.ds(0, SPARE_TOKENS)],
                y_hbm.at[pl.ds(n_tok, SPARE_TOKENS), pl.ds(j * d_model, d_model)], ssem.at[0])
            fill.start()
            fill.wait()
        start_gather(code_ref, 0)
        start_first_two(be_ref[0], 0, WEIGHT_PARTS)
        load_parts(be_ref[0], par_ref[0], 0, WEIGHT_PARTS)

    @pl.when(b + 1 < nb)
    def _():
        start_gather(coden_ref, 1 - slot)

    @pl.when(b < nb)
    def _():
        par = par_ref[b]
        next_e, lo, hi = pfe_ref[b], pflo_ref[b], pfhi_ref[b]
        start_first_two(next_e, lo, hi)
        wait_gather(slot)

        @pl.when(b >= 2)
        def _():
            wait_scatter(slot)

        x = xbuf[slot].astype(BF16)
        gu = jnp.dot(x, wgu[par], preferred_element_type=F32)
        hb = (_silu(gu[:, 0:d_expert]) * gu[:, d_expert:2 * d_expert]).astype(BF16)
        ybuf[slot] = jnp.dot(hb, wd[par], preferred_element_type=F32)

        def scatter(g, carry):
            base = pl.multiple_of(g * SUBLANES, SUBLANES)
            for u in range(SUBLANES):
                code = code_ref[0, 0, base + u]
                col = pl.multiple_of((code & (2 ** CODE_SHIFT - 1)) * d_model, d_model)
                pltpu.make_async_copy(ybuf.at[slot, pl.ds(base + u, 1)],
                                      y_hbm.at[pl.ds(code >> CODE_SHIFT, 1), pl.ds(col, d_model)],
                                      ssem.at[slot]).start()
            return carry
        lax.fori_loop(0, MOE_BLOCK // SUBLANES, scatter, 0)

        load_parts(next_e, 1 - par, lo, hi)

        @pl.when(b == nb - 1)
        def _():
            @pl.when(b >= 1)
            def _():
                wait_scatter(1 - slot)
            wait_scatter(slot)


def _moe(h, we_gate, we_up, we_down, tables, slot_code):
    n, d = h.shape
    d_expert = we_gate.shape[2]
    nb_max = slot_code.shape[0]
    smem_blk = (1, 1, MOE_BLOCK)
    any_spec = pl.BlockSpec(memory_space=pl.ANY)
    grid_spec = pltpu.PrefetchScalarGridSpec(
        num_scalar_prefetch=len(tables),
        grid=(nb_max,),
        in_specs=[pl.BlockSpec(smem_blk, lambda b, *_: (b, 0, 0), memory_space=pltpu.SMEM),
                  pl.BlockSpec(smem_blk, lambda b, *_: (jnp.minimum(b + 1, nb_max - 1), 0, 0),
                               memory_space=pltpu.SMEM),
                  any_spec, any_spec, any_spec, any_spec],
        out_specs=any_spec,
        scratch_shapes=[pltpu.VMEM((2, MOE_BLOCK, d), F32),
                        pltpu.VMEM((2, MOE_BLOCK, d), F32),
                        pltpu.VMEM((2, d, 2 * d_expert), BF16),
                        pltpu.VMEM((2, d_expert, d), BF16),
                        pltpu.VMEM((2, 2, d // WEIGHT_PARTS, d_expert), F32),
                        pltpu.VMEM((2, d_expert // WEIGHT_PARTS, d), F32),
                        pltpu.SemaphoreType.DMA((2,)),
                        pltpu.SemaphoreType.DMA((2,)),
                        pltpu.SemaphoreType.DMA((2, 3))],
    )
    return pl.pallas_call(
        functools.partial(_moe_body, d_expert=d_expert, n_tok=n),
        grid_spec=grid_spec,
        out_shape=jax.ShapeDtypeStruct((n + SPARE_TOKENS, TOP_K * d), F32),
        compiler_params=pltpu.CompilerParams(
            dimension_semantics=("arbitrary",), vmem_limit_bytes=MOE_VMEM_LIMIT,
            disable_bounds_checks=True),
        name="moe_routed",
    )(*tables, slot_code, slot_code, h, we_gate, we_up, we_down)


def _final_body(h_ref, x1_ref, y_ref, gate_ref, gt2_ref, wsgu_ref, wsd_ref, fg_ref, out_ref, *, d_shared):
    d = x1_ref.shape[1]
    gu = jnp.dot(h_ref[...].astype(BF16), wsgu_ref[...], preferred_element_type=F32)
    hb = (_silu(gu[:, 0:d_shared]) * gu[:, d_shared:2 * d_shared]).astype(BF16)
    acc = jnp.dot(hb, wsd_ref[...], preferred_element_type=F32)
    for j in range(TOP_K):
        acc = acc + gate_ref[:, j:j + 1] * y_ref[:, j * d:(j + 1) * d]
    x2 = x1_ref[...] + gt2_ref[0] * acc
    out_ref[...] = x2 * lax.rsqrt(jnp.mean(x2 * x2, axis=-1, keepdims=True) + EPS) * fg_ref[...]


def _final(h, x1, y_tok, gates, mod, ws_gu, ws_d, final_g, rows_per_batch):
    n, d = x1.shape
    tm = FINAL_TILE
    tiles_per_batch = rows_per_batch // tm
    row = lambda i: (i, 0)
    return pl.pallas_call(
        functools.partial(_final_body, d_shared=ws_d.shape[0]),
        grid=(n // tm,),
        in_specs=[pl.BlockSpec((tm, d), row), pl.BlockSpec((tm, d), row),
                  pl.BlockSpec((tm, TOP_K * d), row), pl.BlockSpec((tm, SUBLANES), row),
                  pl.BlockSpec((1, 1, d), _mod_spec(5, tiles_per_batch)),
                  _resident(ws_gu.shape), _resident(ws_d.shape), _resident(final_g.shape)],
        out_specs=pl.BlockSpec((tm, d), row),
        out_shape=jax.ShapeDtypeStruct((n, d), F32),
        compiler_params=_params(1),
        name="shared_combine_final",
    )(h, x1, y_tok, gates, mod, ws_gu, ws_d, final_g)


def _routing_tables(idx, rank, counts, n_tok):
    nb_max = -(-(n_tok * TOP_K) // MOE_BLOCK) + N_EXPERTS
    nblk = (counts + MOE_BLOCK - 1) // MOE_BLOCK
    blk_end = jnp.cumsum(nblk)
    blk_start = blk_end - nblk
    dest = (blk_start * MOE_BLOCK)[idx] + rank
    tok = jnp.arange(n_tok, dtype=I32)[:, None]
    real_code = (tok << CODE_SHIFT) | jnp.arange(TOP_K, dtype=I32)[None, :]
    slots = jnp.arange(nb_max * MOE_BLOCK, dtype=I32)
    spare = ((slots // MOE_BLOCK) % 2) * MOE_BLOCK + slots % MOE_BLOCK
    pad_code = ((n_tok + spare // TOP_K) << CODE_SHIFT) | (spare % TOP_K)
    slot_code = pad_code.at[dest.reshape(-1)].set(real_code.reshape(-1))

    blocks = jnp.arange(nb_max, dtype=I32)
    experts = jnp.arange(N_EXPERTS, dtype=I32)
    block_e = jnp.minimum(jnp.searchsorted(blk_end, blocks, side="right"), N_EXPERTS - 1).astype(I32)
    nonempty = nblk > 0
    parity = ((jnp.cumsum(nonempty.astype(I32)) - 1) % 2)[block_e]
    later = lax.cummin(jnp.where(nonempty, experts, N_EXPERTS), reverse=True)
    next_e = jnp.concatenate([later[1:], jnp.full((1,), N_EXPERTS, I32)])[block_e]
    k_in_e = blocks - blk_start[block_e]
    nb_e = jnp.maximum(nblk[block_e], 1)
    live = (next_e < N_EXPERTS) & (blocks < blk_end[-1])
    lo = jnp.where(live, WEIGHT_PARTS * k_in_e // nb_e, 0)
    hi = jnp.where(live, WEIGHT_PARTS * (k_in_e + 1) // nb_e, 0)
    tables = (block_e, parity.astype(I32), jnp.minimum(next_e, N_EXPERTS - 1).astype(I32),
              lo.astype(I32), hi.astype(I32), blk_end[-1:].astype(I32))
    return tables, slot_code.reshape(nb_max, 1, MOE_BLOCK)


def kernel(x, c, ctx, c_ctx, norm1_g, norm2_g, w_ada, b_ada, w_in, conv_w, gate_b, head_g, w_out,
           w_router, b_router, we_gate, we_up, we_down, ws_gate, ws_up, ws_down, final_g):
    assert w_ada.shape[0] == 1, "single-layer block"
    bsz, seq, d = x.shape
    ctx_len = ctx.shape[1]
    n_tok = bsz * seq
    conv_dim = conv_w.shape[2]
    v_all = head_g.shape[1]
    qk_all = (w_in.shape[2] - 3 * conv_dim - 2 * v_all - N_GATES) // 2
    assert seq % ROW_TILE == 0 and ctx_len % ROW_TILE == 0 and ROW_TILE % GRID_W == 0
    assert bsz + 1 <= SUBLANES

    cc = jnp.zeros((SUBLANES, d), F32).at[:bsz].set(c).at[bsz].set(c_ctx)
    mod = _adaln(cc, w_ada[0], b_ada).reshape(SUBLANES * 6, 1, d)

    n_main = 3 * conv_dim + 2 * qk_all + 2 * v_all
    w_main = w_in[0, :, :n_main].astype(BF16)
    k_lo = 3 * conv_dim + qk_all
    w_kt = w_in[0, :, k_lo:k_lo + qk_all].T.astype(BF16)
    w_v = w_in[0, :, k_lo + qk_all:k_lo + qk_all + v_all].astype(BF16)
    w_gate = jnp.zeros((d, LANES), BF16).at[:, :N_GATES].set(w_in[0, :, n_main:].astype(BF16))
    gate_bias = jnp.zeros((1, LANES), F32).at[0, :N_GATES].set(gate_b[0].reshape(-1))

    x2d = x.reshape(n_tok, d)
    conv, q, kt, v, og, g, gt = _inproj(x2d, mod, norm1_g, w_main, w_kt, w_gate, gate_bias, conv_w[0],
                                       seq, conv_dim, qk_all, v_all)
    ktc, vc, _, gtc = _inproj_ctx(ctx.reshape(bsz * ctx_len, d), mod, norm1_g, w_v, w_kt, w_gate,
                                  gate_bias, bsz)

    h_bwd = _mlstm(1, q, kt, v, g, gt, ktc, vc, gtc, None, bsz)
    ml = _mlstm(0, q, kt, v, g, gt, ktc, vc, gtc, (h_bwd, og), bsz, head_g)

    assert 2 * N_EXPERTS == LANES
    w_r_hi = w_router[0].astype(BF16)
    w_r = jnp.concatenate([w_r_hi, (w_router[0] - w_r_hi.astype(F32)).astype(BF16)], axis=1)
    b_r = jnp.zeros((1, LANES), F32).at[0, :N_EXPERTS].set(b_router[0])
    x1, hn, idx, gates, rank, cnt = _outproj(conv, ml, x2d, mod, norm2_g, w_out[0].astype(BF16),
                                             w_r, b_r, seq)

    tables, slot_code = _routing_tables(idx[:, :TOP_K], rank[:, :TOP_K], cnt[0, :N_EXPERTS], n_tok)
    y_tok = _moe(hn, we_gate[0], we_up[0], we_down[0], tables, slot_code)

    ws_gu = jnp.concatenate([ws_gate[0], ws_up[0]], axis=1).astype(BF16)
    out = _final(hn, x1, y_tok, gates, mod, ws_gu, ws_down[0].astype(BF16),
                 final_g.reshape(1, d), seq)
    return out.reshape(bsz, seq, d)
```

```python
import functools

import jax
import jax.numpy as jnp
from jax import lax
from jax.experimental import pallas as pl
from jax.experimental.pallas import tpu as pltpu
from jax.experimental.pallas import tpu_sc as plsc

F32 = jnp.float32
BF16 = jnp.bfloat16
I32 = jnp.int32

N_HEADS = 4
GRID_W = 64
CHUNK = 128
TOP_K = 6
N_EXPERTS = 64
ROUTED_SCALE = 2.446
EPS = 1e-6
N_GATES = 4 * N_HEADS

LANES = 128
SUBLANES = 8
MOE_BLOCK = 256
ROW_TILE = 256
ADALN_TILE = 1024
WEIGHT_PARTS = 8
SC_CHUNK = 64
HIGH_HALF = -65536
VMEM_LIMIT = 56 * 1024 * 1024
MOE_VMEM_LIMIT = 62 * 1024 * 1024

_HIGHEST = lax.Precision.HIGHEST
_NEG_INF = float("-inf")


def _resident(shape):
    nd = len(shape)
    return pl.BlockSpec(shape, lambda *_: (0,) * nd, pipeline_mode=pl.Buffered(1))


def _params(n_axes):
    return pltpu.CompilerParams(
        dimension_semantics=("arbitrary",) * n_axes, vmem_limit_bytes=VMEM_LIMIT)


def _log_sigmoid(x):
    return jnp.minimum(x, 0.0) - jnp.log1p(jnp.exp(-jnp.abs(x)))


def _silu(x):
    return x * jax.nn.sigmoid(x)


def _pack_rows(val, ref):
    half = val.shape[1] // 2
    lo = lax.bitcast_convert_type(val[:, :half].astype(BF16).astype(F32), I32)
    hi = lax.bitcast_convert_type(val[:, half:].astype(BF16).astype(F32), I32)
    word = (hi & HIGH_HALF) | lax.shift_right_logical(lo, 16)
    for c in range(half // LANES):
        ref[:, c, :] = word[:, c * LANES:(c + 1) * LANES]


def _unpack_rows(ref):
    word = jnp.concatenate([ref[:, c, :] for c in range(ref.shape[1])], axis=1)
    lo = lax.bitcast_convert_type(lax.shift_left(word, 16), F32)
    hi = lax.bitcast_convert_type(word & HIGH_HALF, F32)
    return lo, hi


def _adaln_body(c_ref, w_ref, b_ref, o_ref):
    s = _silu(c_ref[...])
    o_ref[...] = jnp.dot(s.astype(BF16), w_ref[...].astype(BF16),
                         preferred_element_type=F32) + b_ref[...]


def _adaln(cc, w, b):
    d, n6 = w.shape
    return pl.pallas_call(
        _adaln_body,
        grid=(n6 // ADALN_TILE,),
        in_specs=[pl.BlockSpec((SUBLANES, d), lambda j: (0, 0)),
                  pl.BlockSpec((d, ADALN_TILE), lambda j: (0, j)),
                  pl.BlockSpec((1, ADALN_TILE), lambda j: (0, j))],
        out_specs=pl.BlockSpec((SUBLANES, ADALN_TILE), lambda j: (0, j)),
        out_shape=jax.ShapeDtypeStruct((SUBLANES, n6), F32),
        compiler_params=_params(1),
        name="adaln",
    )(cc, w, b)


def _norm_mod(x, g, shift, scale):
    y = x * lax.rsqrt(jnp.mean(x * x, axis=-1, keepdims=True) + EPS) * g
    return y * (1.0 + scale) + shift


def _gate_prep(xb, wg_ref, gb_ref, g_ref, gt_ref):
    tm = xb.shape[0]
    gg = jnp.dot(xb, wg_ref[...], preferred_element_type=F32) + gb_ref[...]
    lane = lax.broadcasted_iota(I32, (tm, LANES), 1)
    is_f = (lane & N_HEADS) != 0
    is_bwd = (lane & (2 * N_HEADS)) != 0
    lf = jnp.where(is_f, _log_sigmoid(gg), 0.0)
    r = lax.broadcasted_iota(I32, (tm, tm), 0)
    c = lax.broadcasted_iota(I32, (tm, tm), 1)
    same = (r // CHUNK) == (c // CHUNK)
    tri_l = jnp.where(same & (c <= r), 1.0, 0.0).astype(F32)
    tri_u = jnp.where(same & (c >= r), 1.0, 0.0).astype(F32)
    pre = jnp.dot(tri_l, lf, precision=_HIGHEST, preferred_element_type=F32)
    suf = jnp.dot(tri_u, lf, precision=_HIGHEST, preferred_element_type=F32)
    out = jnp.where(is_f, jnp.where(is_bwd, suf, pre), gg)
    g_ref[...] = out[:, :N_GATES]
    gt_ref[...] = out.T[:N_GATES, :]


def _project_transposed(wt_ref, xb):
    return lax.dot_general(wt_ref[...], xb, (((1,), (1,)), ((), ())),
                           preferred_element_type=F32).astype(BF16)


def _inproj_body(x_ref, sh_ref, sc_ref, g1_ref, w_ref, wkt_ref, wg_ref, gb_ref, cw_ref,
                 conv_ref, q_ref, k_ref, v_ref, o_ref, g_ref, gt_ref, *, conv_dim, qk_all, v_all):
    tm = x_ref.shape[0]
    xb = _norm_mod(x_ref[...], g1_ref[...], sh_ref[0], sc_ref[0]).astype(BF16)

    def proj(lo, width):
        return jnp.dot(xb, w_ref[:, lo:lo + width], preferred_element_type=F32)

    u = proj(conv_dim, conv_dim) * proj(2 * conv_dim, conv_dim)
    pos = lax.broadcasted_iota(I32, (tm, 1), 0) % GRID_W
    um = jnp.where(pos == 0, 0.0, pltpu.roll(u, 1, axis=0))
    up = jnp.where(pos == GRID_W - 1, 0.0, pltpu.roll(u, tm - 1, axis=0))
    y = um * cw_ref[0:1, :] + u * cw_ref[1:2, :] + up * cw_ref[2:3, :]
    conv_ref[...] = (proj(0, conv_dim) * y).astype(BF16)

    off = 3 * conv_dim
    qscale = (qk_all // N_HEADS) ** -0.5
    q_ref[...] = (proj(off, qk_all) * qscale).astype(BF16)
    k_ref[...] = _project_transposed(wkt_ref, xb)
    v_ref[...] = proj(off + 2 * qk_all, v_all).astype(BF16)
    o_ref[...] = jax.nn.sigmoid(proj(off + 2 * qk_all + v_all, v_all)).astype(BF16)
    _gate_prep(xb, wg_ref, gb_ref, g_ref, gt_ref)


def _inproj_ctx_body(x_ref, sh_ref, sc_ref, g1_ref, w_ref, wkt_ref, wg_ref, gb_ref,
                     k_ref, v_ref, g_ref, gt_ref):
    xb = _norm_mod(x_ref[...], g1_ref[...], sh_ref[0], sc_ref[0]).astype(BF16)
    k_ref[...] = _project_transposed(wkt_ref, xb)
    v_ref[...] = jnp.dot(xb, w_ref[...], preferred_element_type=F32).astype(BF16)
    _gate_prep(xb, wg_ref, gb_ref, g_ref, gt_ref)


def _mod_spec(part, tiles_per_row, fixed_row=None):
    def index(i):
        row = fixed_row if fixed_row is not None else i // tiles_per_row
        return (row * 6 + part, 0, 0)

    return index


def _inproj(x2d, mod, g1, w_main, w_kt, w_gate, gate_b, conv_w, rows_per_batch, conv_dim, qk_all, v_all):
    n, d = x2d.shape
    tm = ROW_TILE
    tiles_per_batch = rows_per_batch // tm
    row = lambda i: (i, 0)
    mod_block = (1, 1, d)
    out_shapes = (
        jax.ShapeDtypeStruct((n, conv_dim), BF16),
        jax.ShapeDtypeStruct((n, qk_all), BF16),
        jax.ShapeDtypeStruct((qk_all, n), BF16),
        jax.ShapeDtypeStruct((n, v_all), BF16),
        jax.ShapeDtypeStruct((n, v_all), BF16),
        jax.ShapeDtypeStruct((n, N_GATES), F32),
        jax.ShapeDtypeStruct((N_GATES, n), F32),
    )
    out_specs = (
        pl.BlockSpec((tm, conv_dim), row),
        pl.BlockSpec((tm, qk_all), row),
        pl.BlockSpec((qk_all, tm), lambda i: (0, i)),
        pl.BlockSpec((tm, v_all), row),
        pl.BlockSpec((tm, v_all), row),
        pl.BlockSpec((tm, N_GATES), row),
        pl.BlockSpec((N_GATES, tm), lambda i: (0, i)),
    )
    return pl.pallas_call(
        functools.partial(_inproj_body, conv_dim=conv_dim, qk_all=qk_all, v_all=v_all),
        grid=(n // tm,),
        in_specs=[pl.BlockSpec((tm, d), row),
                  pl.BlockSpec(mod_block, _mod_spec(0, tiles_per_batch)),
                  pl.BlockSpec(mod_block, _mod_spec(1, tiles_per_batch)),
                  _resident(g1.shape), _resident(w_main.shape), _resident(w_kt.shape),
                  _resident(w_gate.shape), _resident(gate_b.shape), _resident(conv_w.shape)],
        out_specs=out_specs,
        out_shape=out_shapes,
        compiler_params=_params(1),
        name="inproj",
    )(x2d, mod, mod, g1, w_main, w_kt, w_gate, gate_b, conv_w)


def _inproj_ctx(c2d, mod, g1, w_v, w_kt, w_gate, gate_b, ctx_mod_row):
    n, d = c2d.shape
    tm = ROW_TILE
    row = lambda i: (i, 0)
    mod_block = (1, 1, d)
    qk_all, v_all = w_kt.shape[0], w_v.shape[1]
    return pl.pallas_call(
        _inproj_ctx_body,
        grid=(n // tm,),
        in_specs=[pl.BlockSpec((tm, d), row),
                  pl.BlockSpec(mod_block, _mod_spec(0, 1, ctx_mod_row)),
                  pl.BlockSpec(mod_block, _mod_spec(1, 1, ctx_mod_row)),
                  _resident(g1.shape), _resident(w_v.shape), _resident(w_kt.shape),
                  _resident(w_gate.shape), _resident(gate_b.shape)],
        out_specs=(pl.BlockSpec((qk_all, tm), lambda i: (0, i)), pl.BlockSpec((tm, v_all), row),
                   pl.BlockSpec((tm, N_GATES), row), pl.BlockSpec((N_GATES, tm), lambda i: (0, i))),
        out_shape=(jax.ShapeDtypeStruct((qk_all, n), BF16), jax.ShapeDtypeStruct((n, v_all), BF16),
                   jax.ShapeDtypeStruct((n, N_GATES), F32), jax.ShapeDtypeStruct((N_GATES, n), F32)),
        compiler_params=_params(1),
        name="inproj_ctx",
    )(c2d, mod, mod, g1, w_v, w_kt, w_gate, gate_b)


def _with_ones(v):
    return jnp.concatenate([v, jnp.ones((v.shape[0], LANES), v.dtype)], axis=1)


def _mlstm_state_update(h, direction, kt_ref, v_ref, gt_ref, s_ref, m_ref, qk, vh):
    ci = direction * 2 * N_HEADS + h
    cb = ci + N_HEADS
    last = 0 if direction else CHUNK - 1
    kt = kt_ref[h * qk:(h + 1) * qk, :].astype(F32)
    va = _with_ones(v_ref[:, h * vh:(h + 1) * vh])
    b_last = gt_ref[cb:cb + 1, last:last + 1]
    m_prev = m_ref[h][0:1, 0:1]
    g_r = b_last - gt_ref[cb:cb + 1, :] + gt_ref[ci:ci + 1, :]
    m_new = jnp.maximum(b_last + m_prev, jnp.max(g_r, axis=1, keepdims=True))
    a = jnp.exp(b_last + m_prev - m_new)
    kw = (kt * jnp.exp(g_r - m_new)).astype(BF16)
    s_ref[h] = a * s_ref[h] + jnp.dot(kw, va, preferred_element_type=F32)
    m_ref[h] = jnp.broadcast_to(m_new, m_ref.shape[1:])


def _mlstm_head_output(h, direction, q_ref, kt_ref, v_ref, g_ref, gt_ref, s_ref, m_ref, qk, vh):
    ci = direction * 2 * N_HEADS + h
    cb = ci + N_HEADS
    q = q_ref[:, h * qk:(h + 1) * qk]
    kt = kt_ref[h * qk:(h + 1) * qk, :]
    va = _with_ones(v_ref[:, h * vh:(h + 1) * vh])
    ig_r = gt_ref[ci:ci + 1, :]
    b_r = gt_ref[cb:cb + 1, :]
    b_c = g_ref[:, cb:cb + 1]
    m_prev = m_ref[h][0:1, 0:1]
    row = lax.broadcasted_iota(I32, (CHUNK, CHUNK), 0)
    col = lax.broadcasted_iota(I32, (CHUNK, CHUNK), 1)
    mask = (col >= row) if direction else (col <= row)
    dm = jnp.where(mask, b_c + (ig_r - b_r), _NEG_INF)
    inter = b_c + m_prev
    m_t = jnp.maximum(inter, jnp.max(dm, axis=1, keepdims=True))
    w_inter = jnp.exp(inter - m_t)
    s = jnp.dot(q, kt, preferred_element_type=F32) * jnp.exp(dm - m_t)
    intra = jnp.dot(s.astype(BF16), va, preferred_element_type=F32)
    carried = jnp.dot(q, s_ref[h].astype(BF16), preferred_element_type=F32)
    num = intra[:, 0:vh] + w_inter * carried[:, 0:vh]
    den = intra[:, vh:vh + 1] + w_inter * carried[:, vh:vh + 1]
    return num / jnp.maximum(jnp.abs(den), jnp.exp(-m_t))


def _mlstm_body(*refs, direction, n_ctx_chunks, qk, vh):
    if direction:
        (q_ref, kt_ref, v_ref, g_ref, gt_ref, ktc_ref, vc_ref, gtc_ref,
         out_ref, s_ref, m_ref) = refs
    else:
        (q_ref, kt_ref, v_ref, g_ref, gt_ref, ktc_ref, vc_ref, gtc_ref,
         hb_ref, og_ref, hg_ref, out_ref, s_ref, m_ref) = refs
    step = pl.program_id(1)

    @pl.when(step == 0)
    def _():
        s_ref[...] = jnp.zeros_like(s_ref)
        m_ref[...] = jnp.full_like(m_ref, _NEG_INF)

    @pl.when(step < n_ctx_chunks)
    def _():
        for h in range(N_HEADS):
            _mlstm_state_update(h, direction, ktc_ref, vc_ref, gtc_ref, s_ref, m_ref, qk, vh)

    @pl.when(step >= n_ctx_chunks)
    def _():
        for h in range(N_HEADS):
            hh = _mlstm_head_output(h, direction, q_ref, kt_ref, v_ref, g_ref, gt_ref, s_ref, m_ref, qk, vh)
            cols = slice(h * vh, (h + 1) * vh)
            if direction:
                out_ref[:, cols] = hh
            else:
                hs = hh + hb_ref[:, cols]
                hs = hs * lax.rsqrt(jnp.mean(hs * hs, axis=-1, keepdims=True) + EPS)
                out_ref[:, cols] = (hs * hg_ref[:, cols] * og_ref[:, cols].astype(F32)).astype(BF16)
            _mlstm_state_update(h, direction, kt_ref, v_ref, gt_ref, s_ref, m_ref, qk, vh)


def _mlstm(direction, q, kt, v, g, gt, ktc, vc, gtc, extra, bsz, head_g=None):
    n, qk_all = q.shape
    v_all = v.shape[1]
    qk, vh = qk_all // N_HEADS, v_all // N_HEADS
    nc = n // bsz // CHUNK
    ncc = vc.shape[0] // bsz // CHUNK

    def lat(b, s):
        j = jnp.clip(s - ncc, 0, nc - 1)
        return b * nc + (nc - 1 - j if direction else j)

    def ctx(b, s):
        j = jnp.clip(s, 0, ncc - 1)
        return b * ncc + (ncc - 1 - j if direction else j)

    lat_row = lambda b, s: (lat(b, s), 0)
    lat_col = lambda b, s: (0, lat(b, s))
    ctx_row = lambda b, s: (ctx(b, s), 0)
    ctx_col = lambda b, s: (0, ctx(b, s))
    in_specs = [pl.BlockSpec((CHUNK, qk_all), lat_row), pl.BlockSpec((qk_all, CHUNK), lat_col),
                pl.BlockSpec((CHUNK, v_all), lat_row), pl.BlockSpec((CHUNK, N_GATES), lat_row),
                pl.BlockSpec((N_GATES, CHUNK), lat_col),
                pl.BlockSpec((qk_all, CHUNK), ctx_col), pl.BlockSpec((CHUNK, v_all), ctx_row),
                pl.BlockSpec((N_GATES, CHUNK), ctx_col)]
    args = [q, kt, v, g, gt, ktc, vc, gtc]
    if direction:
        out_dtype = F32
    else:
        hb, og = extra
        in_specs += [pl.BlockSpec((CHUNK, v_all), lat_row), pl.BlockSpec((CHUNK, v_all), lat_row),
                     pl.BlockSpec((1, v_all), lambda b, s: (0, 0))]
        args += [hb, og, head_g]
        out_dtype = BF16
    return pl.pallas_call(
        functools.partial(_mlstm_body, direction=direction, n_ctx_chunks=ncc, qk=qk, vh=vh),
        grid=(bsz, ncc + nc),
        in_specs=in_specs,
        out_specs=pl.BlockSpec((CHUNK, v_all), lat_row),
        out_shape=jax.ShapeDtypeStruct((n, v_all), out_dtype),
        scratch_shapes=[pltpu.VMEM((N_HEADS, qk, vh + LANES), F32),
                        pltpu.VMEM((N_HEADS, SUBLANES, LANES), F32)],
        compiler_params=_params(2),
        name="mlstm_bwd" if direction else "mlstm_fwd",
    )(*args)


def _outproj_body(conv_ref, ml_ref, x_ref, gt1_ref, sh2_ref, sc2_ref, g2_ref, wo_ref, wr_ref, br_ref,
                  x1_ref, h_ref, idx_ref, gate_ref, rank_ref, cnt_ref, carry_ref):
    tm = x_ref.shape[0]
    half = conv_ref.shape[1]

    @pl.when(pl.program_id(0) == 0)
    def _():
        carry_ref[...] = jnp.zeros_like(carry_ref)

    y = (jnp.dot(conv_ref[...], wo_ref[0:half, :], preferred_element_type=F32)
         + jnp.dot(ml_ref[...], wo_ref[half:2 * half, :], preferred_element_type=F32))
    x1 = x_ref[...] + gt1_ref[0] * y
    x1_ref[...] = x1
    hn = _norm_mod(x1, g2_ref[...], sh2_ref[0], sc2_ref[0])
    _pack_rows(hn, h_ref)

    h_hi = hn.astype(BF16)
    h_lo = (hn - h_hi.astype(F32)).astype(BF16)
    parts = (jnp.dot(h_hi, wr_ref[...], preferred_element_type=F32)
             + jnp.dot(h_lo, wr_ref[...], preferred_element_type=F32))
    scores = jax.nn.sigmoid(parts + pltpu.roll(parts, N_EXPERTS, axis=1))
    lane = lax.broadcasted_iota(I32, (tm, LANES), 1).astype(F32)
    biased = jnp.where(lane < N_EXPERTS, scores + br_ref[...], _NEG_INF)
    onehot = jnp.zeros((tm, LANES), F32)
    picks, sels = [], []
    for _ in range(TOP_K):
        mx = jnp.max(biased, axis=1, keepdims=True)
        pick = jnp.min(jnp.where(biased == mx, lane, float(LANES)), axis=1, keepdims=True)
        hit = lane == pick
        sels.append(jnp.sum(jnp.where(hit, scores, 0.0), axis=1, keepdims=True))
        picks.append(pick)
        biased = jnp.where(hit, _NEG_INF, biased)
        onehot = onehot + hit.astype(F32)
    total = sels[0]
    for s in sels[1:]:
        total = total + s

    r = lax.broadcasted_iota(I32, (tm, tm), 0)
    c = lax.broadcasted_iota(I32, (tm, tm), 1)
    strict = jnp.where(c < r, 1.0, 0.0).astype(BF16)
    before = jnp.dot(strict, onehot.astype(BF16), preferred_element_type=F32) + carry_ref[...]
    slot = lax.broadcasted_iota(I32, (tm, SUBLANES), 1)
    idx_out = jnp.zeros((tm, SUBLANES), F32)
    gate_out = jnp.zeros((tm, SUBLANES), F32)
    rank_out = jnp.zeros((tm, SUBLANES), F32)
    for j in range(TOP_K):
        rank = jnp.sum(jnp.where(lane == picks[j], before, 0.0), axis=1, keepdims=True)
        idx_out = jnp.where(slot == j, picks[j], idx_out)
        gate_out = jnp.where(slot == j, sels[j] / total * ROUTED_SCALE, gate_out)
        rank_out = jnp.where(slot == j, rank, rank_out)
    idx_ref[...] = idx_out.astype(I32)
    gate_ref[...] = gate_out
    rank_ref[...] = rank_out.astype(I32)
    carry_ref[...] = carry_ref[...] + jnp.sum(onehot, axis=0, keepdims=True)
    cnt_ref[...] = jnp.broadcast_to(carry_ref[...], cnt_ref.shape).astype(I32)


def _outproj(conv, ml, x2d, mod, g2, w_out, w_router, b_router, rows_per_batch):
    n, d = x2d.shape
    tm = ROW_TILE
    tiles_per_batch = rows_per_batch // tm
    row = lambda i: (i, 0)
    mod_block = (1, 1, d)
    half = conv.shape[1]
    return pl.pallas_call(
        _outproj_body,
        grid=(n // tm,),
        in_specs=[pl.BlockSpec((tm, half), row), pl.BlockSpec((tm, half), row), pl.BlockSpec((tm, d), row),
                  pl.BlockSpec(mod_block, _mod_spec(2, tiles_per_batch)),
                  pl.BlockSpec(mod_block, _mod_spec(3, tiles_per_batch)),
                  pl.BlockSpec(mod_block, _mod_spec(4, tiles_per_batch)),
                  _resident(g2.shape), _resident(w_out.shape), _resident(w_router.shape),
                  _resident(b_router.shape)],
        out_specs=(pl.BlockSpec((tm, d), row), pl.BlockSpec((tm, SUBLANES, LANES), lambda i: (i, 0, 0)),
                   pl.BlockSpec((tm, SUBLANES), row), pl.BlockSpec((tm, SUBLANES), row),
                   pl.BlockSpec((tm, SUBLANES), row),
                   pl.BlockSpec((SUBLANES, LANES), lambda i: (0, 0))),
        out_shape=(jax.ShapeDtypeStruct((n, d), F32), jax.ShapeDtypeStruct((n, SUBLANES, LANES), I32),
                   jax.ShapeDtypeStruct((n, SUBLANES), I32), jax.ShapeDtypeStruct((n, SUBLANES), F32),
                   jax.ShapeDtypeStruct((n, SUBLANES), I32),
                   jax.ShapeDtypeStruct((SUBLANES, LANES), I32)),
        scratch_shapes=[pltpu.VMEM((1, LANES), F32)],
        compiler_params=_params(1),
        name="outproj_router",
    )(conv, ml, x2d, mod, mod, mod, g2, w_out, w_router, b_router)


def _sc_workers():
    info = plsc.get_sparse_core_info()
    return info.num_cores, info.num_cores * info.num_subcores


def _sc_dispatch(h_rows, dest_chunks, n_slots):
    n_tok = h_rows.shape[0]
    n_cores, n_workers = _sc_workers()
    per_worker = n_tok // (n_workers * SC_CHUNK)
    assert per_worker * n_workers * SC_CHUNK == n_tok
    mesh = plsc.VectorSubcoreMesh(core_axis_name="c", subcore_axis_name="s")

    @functools.partial(
        pl.kernel, mesh=mesh,
        out_type=jax.ShapeDtypeStruct((n_slots,) + h_rows.shape[1:], h_rows.dtype),
        scratch_types=[pltpu.VMEM((TOP_K, SC_CHUNK), I32),
                       pltpu.VMEM((SC_CHUNK,) + h_rows.shape[1:], h_rows.dtype)],
    )
    def dispatch(h_hbm, dest_hbm, out_hbm, idx_v, rows_v):
        wid = lax.axis_index("s") * n_cores + lax.axis_index("c")

        @pl.loop(0, per_worker)
        def _(i):
            chunk = wid * per_worker + i
            pltpu.sync_copy(dest_hbm.at[chunk], idx_v)
            pltpu.sync_copy(h_hbm.at[pl.ds(chunk * SC_CHUNK, SC_CHUNK)], rows_v)
            for k in range(TOP_K):
                pltpu.sync_copy(rows_v, out_hbm.at[idx_v.at[k]])

    return dispatch(h_rows, dest_chunks)


def _sc_combine(y_sorted, dest_chunks, n_tok):
    n_cores, n_workers = _sc_workers()
    per_worker = n_tok // (n_workers * SC_CHUNK)
    mesh = plsc.VectorSubcoreMesh(core_axis_name="c", subcore_axis_name="s")

    @functools.partial(
        pl.kernel, mesh=mesh,
        out_type=jax.ShapeDtypeStruct((TOP_K, n_tok) + y_sorted.shape[1:], y_sorted.dtype),
        scratch_types=[pltpu.VMEM((TOP_K, SC_CHUNK), I32),
                       pltpu.VMEM((SC_CHUNK,) + y_sorted.shape[1:], y_sorted.dtype)],
    )
    def combine(y_hbm, dest_hbm, out_hbm, idx_v, rows_v):
        wid = lax.axis_index("s") * n_cores + lax.axis_index("c")

        @pl.loop(0, per_worker)
        def _(i):
            chunk = wid * per_worker + i
            pltpu.sync_copy(dest_hbm.at[chunk], idx_v)
            for k in range(TOP_K):
                pltpu.sync_copy(y_hbm.at[idx_v.at[k]], rows_v)
                pltpu.sync_copy(rows_v, out_hbm.at[k, pl.ds(chunk * SC_CHUNK, SC_CHUNK)])

    return combine(y_sorted, dest_chunks)


def _moe_body(be_ref, par_ref, pfe_ref, pflo_ref, pfhi_ref, nb_ref,
              x_ref, wg_hbm, wu_hbm, wd_hbm, y_ref,
              wgu, wd, stage_a, stage_d, wsem, *, d_expert):
    b = pl.program_id(0)
    nb = nb_ref[0]
    d_model = wgu.shape[1]
    rows_a = d_model // WEIGHT_PARTS
    rows_d = d_expert // WEIGHT_PARTS

    def part_copies(e, i, s):
        return (pltpu.make_async_copy(wg_hbm.at[e, pl.ds(i * rows_a, rows_a)], stage_a.at[s, 0],
                                      wsem.at[s, 0]),
                pltpu.make_async_copy(wu_hbm.at[e, pl.ds(i * rows_a, rows_a)], stage_a.at[s, 1],
                                      wsem.at[s, 1]),
                pltpu.make_async_copy(wd_hbm.at[e, pl.ds(i * rows_d, rows_d)], stage_d.at[s],
                                      wsem.at[s, 2]))

    def start_part(e, i):
        for cp in part_copies(e, i, i % 2):
            cp.start()

    def finish_part(e, i, par):
        s = i % 2
        for cp in part_copies(e, i, s):
            cp.wait()
        ra = pl.multiple_of(i * rows_a, rows_a)
        rd = pl.multiple_of(i * rows_d, rows_d)
        wgu[par, pl.ds(ra, rows_a), 0:d_expert] = stage_a[s, 0].astype(BF16)
        wgu[par, pl.ds(ra, rows_a), d_expert:2 * d_expert] = stage_a[s, 1].astype(BF16)
        wd[par, pl.ds(rd, rows_d), :] = stage_d[s].astype(BF16)

    def load_parts(e, par, lo, hi):
        def body(i, carry):
            finish_part(e, i, par)

            @pl.when(i + 2 < hi)
            def _():
                start_part(e, i + 2)
            return carry
        lax.fori_loop(lo, hi, body, 0)

    def start_first_two(e, lo, hi):
        @pl.when(lo < hi)
        def _():
            start_part(e, lo)

        @pl.when(lo + 1 < hi)
        def _():
            start_part(e, lo + 1)

    @pl.when(b == 0)
    def _():
        start_first_two(be_ref[0], 0, WEIGHT_PARTS)
        load_parts(be_ref[0], par_ref[0], 0, WEIGHT_PARTS)

    @pl.when(b < nb)
    def _():
        par = par_ref[b]
        next_e, lo, hi = pfe_ref[b], pflo_ref[b], pfhi_ref[b]
        start_first_two(next_e, lo, hi)
        x = jnp.concatenate(_unpack_rows(x_ref), axis=1).astype(BF16)
        gu = jnp.dot(x, wgu[par], preferred_element_type=F32)
        hb = (_silu(gu[:, 0:d_expert]) * gu[:, d_expert:2 * d_expert]).astype(BF16)
        _pack_rows(jnp.dot(hb, wd[par], preferred_element_type=F32), y_ref)
        load_parts(next_e, 1 - par, lo, hi)


def _moe(x_sorted, we_gate, we_up, we_down, tables):
    d, d_expert = we_gate.shape[1], we_gate.shape[2]
    nb_max = x_sorted.shape[0] // MOE_BLOCK
    any_spec = pl.BlockSpec(memory_space=pl.ANY)
    row_blk = pl.BlockSpec((MOE_BLOCK,) + x_sorted.shape[1:],
                           lambda b, be, par, pfe, pflo, pfhi, nb: (jnp.minimum(b, nb[0] - 1), 0, 0))
    grid_spec = pltpu.PrefetchScalarGridSpec(
        num_scalar_prefetch=len(tables),
        grid=(nb_max,),
        in_specs=[row_blk, any_spec, any_spec, any_spec],
        out_specs=row_blk,
        scratch_shapes=[pltpu.VMEM((2, d, 2 * d_expert), BF16),
                        pltpu.VMEM((2, d_expert, d), BF16),
                        pltpu.VMEM((2, 2, d // WEIGHT_PARTS, d_expert), F32),
                        pltpu.VMEM((2, d_expert // WEIGHT_PARTS, d), F32),
                        pltpu.SemaphoreType.DMA((2, 3))],
    )
    return pl.pallas_call(
        functools.partial(_moe_body, d_expert=d_expert),
        grid_spec=grid_spec,
        out_shape=jax.ShapeDtypeStruct(x_sorted.shape, x_sorted.dtype),
        compiler_params=pltpu.CompilerParams(
            dimension_semantics=("arbitrary",), vmem_limit_bytes=MOE_VMEM_LIMIT),
        name="moe_routed",
    )(*tables, x_sorted, we_gate, we_up, we_down)


def _final_body(h_ref, x1_ref, y_ref, gate_ref, gt2_ref, wsgu_ref, wsd_ref, fg_ref, out_ref, *, d_shared):
    h = jnp.concatenate(_unpack_rows(h_ref), axis=1).astype(BF16)
    gu = jnp.dot(h, wsgu_ref[...], preferred_element_type=F32)
    hb = (_silu(gu[:, 0:d_shared]) * gu[:, d_shared:2 * d_shared]).astype(BF16)
    acc = jnp.dot(hb, wsd_ref[...], preferred_element_type=F32)
    for k in range(TOP_K):
        lo, hi = _unpack_rows(y_ref.at[k])
        acc = acc + gate_ref[:, k:k + 1] * jnp.concatenate([lo, hi], axis=1)
    x2 = x1_ref[...] + gt2_ref[0] * acc
    out_ref[...] = x2 * lax.rsqrt(jnp.mean(x2 * x2, axis=-1, keepdims=True) + EPS) * fg_ref[...]


def _final(h_rows, x1, y_tok, gates, mod, ws_gu, ws_d, final_g, rows_per_batch):
    n, d = x1.shape
    tm = ROW_TILE
    tiles_per_batch = rows_per_batch // tm
    row = lambda i: (i, 0)
    packed = h_rows.shape[1:]
    return pl.pallas_call(
        functools.partial(_final_body, d_shared=ws_d.shape[0]),
        grid=(n // tm,),
        in_specs=[pl.BlockSpec((tm,) + packed, lambda i: (i, 0, 0)), pl.BlockSpec((tm, d), row),
                  pl.BlockSpec((TOP_K, tm) + packed, lambda i: (0, i, 0, 0)),
                  pl.BlockSpec((tm, SUBLANES), row),
                  pl.BlockSpec((1, 1, d), _mod_spec(5, tiles_per_batch)),
                  _resident(ws_gu.shape), _resident(ws_d.shape), _resident(final_g.shape)],
        out_specs=pl.BlockSpec((tm, d), row),
        out_shape=jax.ShapeDtypeStruct((n, d), F32),
        compiler_params=_params(1),
        name="shared_combine_final",
    )(h_rows, x1, y_tok, gates, mod, ws_gu, ws_d, final_g)


def _routing_tables(idx, rank, counts, n_tok):
    nb_max = -(-(n_tok * TOP_K) // MOE_BLOCK) + N_EXPERTS
    nblk = (counts + MOE_BLOCK - 1) // MOE_BLOCK
    blk_end = jnp.cumsum(nblk)
    blk_start = blk_end - nblk
    dest = (blk_start * MOE_BLOCK)[idx] + rank
    dest_chunks = dest.reshape(n_tok // SC_CHUNK, SC_CHUNK, TOP_K).transpose(0, 2, 1)

    blocks = jnp.arange(nb_max, dtype=I32)
    experts = jnp.arange(N_EXPERTS, dtype=I32)
    block_e = jnp.minimum(jnp.searchsorted(blk_end, blocks, side="right"), N_EXPERTS - 1).astype(I32)
    nonempty = nblk > 0
    parity = ((jnp.cumsum(nonempty.astype(I32)) - 1) % 2)[block_e]
    later = lax.cummin(jnp.where(nonempty, experts, N_EXPERTS), reverse=True)
    next_e = jnp.concatenate([later[1:], jnp.full((1,), N_EXPERTS, I32)])[block_e]
    k_in_e = blocks - blk_start[block_e]
    nb_e = jnp.maximum(nblk[block_e], 1)
    live = (next_e < N_EXPERTS) & (blocks < blk_end[-1])
    lo = jnp.where(live, WEIGHT_PARTS * k_in_e // nb_e, 0)
    hi = jnp.where(live, WEIGHT_PARTS * (k_in_e + 1) // nb_e, 0)
    tables = (block_e, parity.astype(I32), jnp.minimum(next_e, N_EXPERTS - 1).astype(I32),
              lo.astype(I32), hi.astype(I32), blk_end[-1:].astype(I32))
    return tables, dest_chunks, nb_max * MOE_BLOCK


def kernel(x, c, ctx, c_ctx, norm1_g, norm2_g, w_ada, b_ada, w_in, conv_w, gate_b, head_g, w_out,
           w_router, b_router, we_gate, we_up, we_down, ws_gate, ws_up, ws_down, final_g):
    assert w_ada.shape[0] == 1, "single-layer block"
    bsz, seq, d = x.shape
    ctx_len = ctx.shape[1]
    n_tok = bsz * seq
    conv_dim = conv_w.shape[2]
    v_all = head_g.shape[1]
    qk_all = (w_in.shape[2] - 3 * conv_dim - 2 * v_all - N_GATES) // 2
    assert seq % ROW_TILE == 0 and ctx_len % ROW_TILE == 0 and ROW_TILE % GRID_W == 0
    assert bsz + 1 <= SUBLANES

    cc = jnp.zeros((SUBLANES, d), F32).at[:bsz].set(c).at[bsz].set(c_ctx)
    mod = _adaln(cc, w_ada[0], b_ada).reshape(SUBLANES * 6, 1, d)

    n_main = 3 * conv_dim + 2 * qk_all + 2 * v_all
    w_main = w_in[0, :, :n_main].astype(BF16)
    k_lo = 3 * conv_dim + qk_all
    w_kt = w_in[0, :, k_lo:k_lo + qk_all].T.astype(BF16)
    w_v = w_in[0, :, k_lo + qk_all:k_lo + qk_all + v_all].astype(BF16)
    w_gate = jnp.zeros((d, LANES), BF16).at[:, :N_GATES].set(w_in[0, :, n_main:].astype(BF16))
    gate_bias = jnp.zeros((1, LANES), F32).at[0, :N_GATES].set(gate_b[0].reshape(-1))

    x2d = x.reshape(n_tok, d)
    conv, q, kt, v, og, g, gt = _inproj(x2d, mod, norm1_g, w_main, w_kt, w_gate, gate_bias, conv_w[0],
                                       seq, conv_dim, qk_all, v_all)
    ktc, vc, _, gtc = _inproj_ctx(ctx.reshape(bsz * ctx_len, d), mod, norm1_g, w_v, w_kt, w_gate,
                                  gate_bias, bsz)

    h_bwd = _mlstm(1, q, kt, v, g, gt, ktc, vc, gtc, None, bsz)
    ml = _mlstm(0, q, kt, v, g, gt, ktc, vc, gtc, (h_bwd, og), bsz, head_g)

    assert 2 * N_EXPERTS == LANES
    w_r_hi = w_router[0].astype(BF16)
    w_r = jnp.concatenate([w_r_hi, (w_router[0] - w_r_hi.astype(F32)).astype(BF16)], axis=1)
    b_r = jnp.zeros((1, LANES), F32).at[0, :N_EXPERTS].set(b_router[0])
    x1, h_rows, idx, gates, rank, cnt = _outproj(conv, ml, x2d, mod, norm2_g, w_out[0].astype(BF16),
                                             w_r, b_r, seq)

    tables, dest_chunks, n_slots = _routing_tables(idx[:, :TOP_K], rank[:, :TOP_K], cnt[0, :N_EXPERTS],
                                                   n_tok)
    x_sorted = _sc_dispatch(h_rows, dest_chunks, n_slots)
    y_sorted = _moe(x_sorted, we_gate[0], we_up[0], we_down[0], tables)
    y_tok = _sc_combine(y_sorted, dest_chunks, n_tok)

    ws_gu = jnp.concatenate([ws_gate[0], ws_up[0]], axis=1).astype(BF16)
    out = _final(h_rows, x1, y_tok, gates, mod, ws_gu, ws_down[0].astype(BF16),
                 final_g.reshape(1, d), seq)
    return out.reshape(bsz, seq, d)
```

```python
import functools

import jax
import jax.numpy as jnp
from jax import lax
from jax.experimental import pallas as pl
from jax.experimental.pallas import tpu as pltpu
from jax.experimental.pallas import tpu_sc as plsc

F32 = jnp.float32
BF16 = jnp.bfloat16
I32 = jnp.int32

N_HEADS = 4
GRID_W = 64
CHUNK = 128
TOP_K = 6
N_EXPERTS = 64
ROUTED_SCALE = 2.446
EPS = 1e-6
N_GATES = 4 * N_HEADS

LANES = 128
SUBLANES = 8
MOE_BLOCK = 256
ROW_TILE = 256
ADALN_TILE = 1024
WEIGHT_PARTS = 8
PART_SHIFT = 3
WEIGHT_RING = 3
SC_CHUNK = 64
HIGH_HALF = -65536
VMEM_LIMIT = 56 * 1024 * 1024
MOE_VMEM_LIMIT = 62 * 1024 * 1024

_HIGHEST = lax.Precision.HIGHEST
_NEG_INF = float("-inf")


def _resident(shape):
    nd = len(shape)
    return pl.BlockSpec(shape, lambda *_: (0,) * nd, pipeline_mode=pl.Buffered(1))


def _params(n_axes):
    return pltpu.CompilerParams(
        dimension_semantics=("arbitrary",) * n_axes, vmem_limit_bytes=VMEM_LIMIT)


def _log_sigmoid(x):
    return jnp.minimum(x, 0.0) - jnp.log1p(jnp.exp(-jnp.abs(x)))


def _silu(x):
    return x * jax.nn.sigmoid(x)


def _pack_rows(val, ref):
    half = val.shape[1] // 2
    lo = lax.bitcast_convert_type(val[:, :half].astype(BF16).astype(F32), I32)
    hi = lax.bitcast_convert_type(val[:, half:].astype(BF16).astype(F32), I32)
    word = (hi & HIGH_HALF) | lax.shift_right_logical(lo, 16)
    for c in range(half // LANES):
        ref[:, c, :] = word[:, c * LANES:(c + 1) * LANES]


def _unpack_rows(ref):
    word = jnp.concatenate([ref[:, c, :] for c in range(ref.shape[1])], axis=1)
    lo = lax.bitcast_convert_type(lax.shift_left(word, 16), F32)
    hi = lax.bitcast_convert_type(word & HIGH_HALF, F32)
    return lo, hi


def _adaln_body(c_ref, w_ref, b_ref, o_ref):
    s = _silu(c_ref[...])
    o_ref[...] = jnp.dot(s.astype(BF16), w_ref[...].astype(BF16),
                         preferred_element_type=F32) + b_ref[...]


def _adaln(cc, w, b):
    d, n6 = w.shape
    return pl.pallas_call(
        _adaln_body,
        grid=(n6 // ADALN_TILE,),
        in_specs=[pl.BlockSpec((SUBLANES, d), lambda j: (0, 0)),
                  pl.BlockSpec((d, ADALN_TILE), lambda j: (0, j)),
                  pl.BlockSpec((1, ADALN_TILE), lambda j: (0, j))],
        out_specs=pl.BlockSpec((SUBLANES, ADALN_TILE), lambda j: (0, j)),
        out_shape=jax.ShapeDtypeStruct((SUBLANES, n6), F32),
        compiler_params=_params(1),
        name="adaln",
    )(cc, w, b)


def _norm_mod(x, g, shift, scale):
    y = x * lax.rsqrt(jnp.mean(x * x, axis=-1, keepdims=True) + EPS) * g
    return y * (1.0 + scale) + shift


def _gate_prep(xb, wg_ref, gb_ref, g_ref, gt_ref):
    tm = xb.shape[0]
    gg = jnp.dot(xb, wg_ref[...], preferred_element_type=F32) + gb_ref[...]
    lane = lax.broadcasted_iota(I32, (tm, LANES), 1)
    is_f = (lane & N_HEADS) != 0
    is_bwd = (lane & (2 * N_HEADS)) != 0
    lf = jnp.where(is_f, _log_sigmoid(gg), 0.0)
    r = lax.broadcasted_iota(I32, (tm, tm), 0)
    c = lax.broadcasted_iota(I32, (tm, tm), 1)
    same = (r // CHUNK) == (c // CHUNK)
    tri_l = jnp.where(same & (c <= r), 1.0, 0.0).astype(F32)
    tri_u = jnp.where(same & (c >= r), 1.0, 0.0).astype(F32)
    pre = jnp.dot(tri_l, lf, precision=_HIGHEST, preferred_element_type=F32)
    suf = jnp.dot(tri_u, lf, precision=_HIGHEST, preferred_element_type=F32)
    out = jnp.where(is_f, jnp.where(is_bwd, suf, pre), gg)
    g_ref[...] = out[:, :N_GATES]
    gt_ref[...] = out.T[:N_GATES, :]


def _project_transposed(wt_ref, xb):
    return lax.dot_general(wt_ref[...], xb, (((1,), (1,)), ((), ())),
                           preferred_element_type=F32).astype(BF16)


def _inproj_body(x_ref, sh_ref, sc_ref, g1_ref, w_ref, wkt_ref, wg_ref, gb_ref, cw_ref,
                 conv_ref, q_ref, k_ref, v_ref, o_ref, g_ref, gt_ref, *, conv_dim, qk_all, v_all):
    tm = x_ref.shape[0]
    xb = _norm_mod(x_ref[...], g1_ref[...], sh_ref[0], sc_ref[0]).astype(BF16)

    def proj(lo, width):
        return jnp.dot(xb, w_ref[:, lo:lo + width], preferred_element_type=F32)

    u = proj(conv_dim, conv_dim) * proj(2 * conv_dim, conv_dim)
    pos = lax.broadcasted_iota(I32, (tm, 1), 0) % GRID_W
    um = jnp.where(pos == 0, 0.0, pltpu.roll(u, 1, axis=0))
    up = jnp.where(pos == GRID_W - 1, 0.0, pltpu.roll(u, tm - 1, axis=0))
    y = um * cw_ref[0:1, :] + u * cw_ref[1:2, :] + up * cw_ref[2:3, :]
    conv_ref[...] = (proj(0, conv_dim) * y).astype(BF16)

    off = 3 * conv_dim
    qscale = (qk_all // N_HEADS) ** -0.5
    q_ref[...] = (proj(off, qk_all) * qscale).astype(BF16)
    k_ref[...] = _project_transposed(wkt_ref, xb)
    v_ref[...] = proj(off + 2 * qk_all, v_all).astype(BF16)
    o_ref[...] = jax.nn.sigmoid(proj(off + 2 * qk_all + v_all, v_all)).astype(BF16)
    _gate_prep(xb, wg_ref, gb_ref, g_ref, gt_ref)


def _inproj_ctx_body(x_ref, sh_ref, sc_ref, g1_ref, w_ref, wkt_ref, wg_ref, gb_ref,
                     k_ref, v_ref, g_ref, gt_ref):
    xb = _norm_mod(x_ref[...], g1_ref[...], sh_ref[0], sc_ref[0]).astype(BF16)
    k_ref[...] = _project_transposed(wkt_ref, xb)
    v_ref[...] = jnp.dot(xb, w_ref[...], preferred_element_type=F32).astype(BF16)
    _gate_prep(xb, wg_ref, gb_ref, g_ref, gt_ref)


def _mod_spec(part, tiles_per_row, fixed_row=None):
    def index(i):
        row = fixed_row if fixed_row is not None else i // tiles_per_row
        return (row * 6 + part, 0, 0)

    return index


def _inproj(x2d, mod, g1, w_main, w_kt, w_gate, gate_b, conv_w, rows_per_batch, conv_dim, qk_all, v_all):
    n, d = x2d.shape
    tm = ROW_TILE
    tiles_per_batch = rows_per_batch // tm
    row = lambda i: (i, 0)
    mod_block = (1, 1, d)
    out_shapes = (
        jax.ShapeDtypeStruct((n, conv_dim), BF16),
        jax.ShapeDtypeStruct((n, qk_all), BF16),
        jax.ShapeDtypeStruct((qk_all, n), BF16),
        jax.ShapeDtypeStruct((n, v_all), BF16),
        jax.ShapeDtypeStruct((n, v_all), BF16),
        jax.ShapeDtypeStruct((n, N_GATES), F32),
        jax.ShapeDtypeStruct((N_GATES, n), F32),
    )
    out_specs = (
        pl.BlockSpec((tm, conv_dim), row),
        pl.BlockSpec((tm, qk_all), row),
        pl.BlockSpec((qk_all, tm), lambda i: (0, i)),
        pl.BlockSpec((tm, v_all), row),
        pl.BlockSpec((tm, v_all), row),
        pl.BlockSpec((tm, N_GATES), row),
        pl.BlockSpec((N_GATES, tm), lambda i: (0, i)),
    )
    return pl.pallas_call(
        functools.partial(_inproj_body, conv_dim=conv_dim, qk_all=qk_all, v_all=v_all),
        grid=(n // tm,),
        in_specs=[pl.BlockSpec((tm, d), row),
                  pl.BlockSpec(mod_block, _mod_spec(0, tiles_per_batch)),
                  pl.BlockSpec(mod_block, _mod_spec(1, tiles_per_batch)),
                  _resident(g1.shape), _resident(w_main.shape), _resident(w_kt.shape),
                  _resident(w_gate.shape), _resident(gate_b.shape), _resident(conv_w.shape)],
        out_specs=out_specs,
        out_shape=out_shapes,
        compiler_params=_params(1),
        name="inproj",
    )(x2d, mod, mod, g1, w_main, w_kt, w_gate, gate_b, conv_w)


def _inproj_ctx(c2d, mod, g1, w_v, w_kt, w_gate, gate_b, ctx_mod_row):
    n, d = c2d.shape
    tm = ROW_TILE
    row = lambda i: (i, 0)
    mod_block = (1, 1, d)
    qk_all, v_all = w_kt.shape[0], w_v.shape[1]
    return pl.pallas_call(
        _inproj_ctx_body,
        grid=(n // tm,),
        in_specs=[pl.BlockSpec((tm, d), row),
                  pl.BlockSpec(mod_block, _mod_spec(0, 1, ctx_mod_row)),
                  pl.BlockSpec(mod_block, _mod_spec(1, 1, ctx_mod_row)),
                  _resident(g1.shape), _resident(w_v.shape), _resident(w_kt.shape),
                  _resident(w_gate.shape), _resident(gate_b.shape)],
        out_specs=(pl.BlockSpec((qk_all, tm), lambda i: (0, i)), pl.BlockSpec((tm, v_all), row),
                   pl.BlockSpec((tm, N_GATES), row), pl.BlockSpec((N_GATES, tm), lambda i: (0, i))),
        out_shape=(jax.ShapeDtypeStruct((qk_all, n), BF16), jax.ShapeDtypeStruct((n, v_all), BF16),
                   jax.ShapeDtypeStruct((n, N_GATES), F32), jax.ShapeDtypeStruct((N_GATES, n), F32)),
        compiler_params=_params(1),
        name="inproj_ctx",
    )(c2d, mod, mod, g1, w_v, w_kt, w_gate, gate_b)


def _with_ones(v):
    return jnp.concatenate([v, jnp.ones((v.shape[0], LANES), v.dtype)], axis=1)


def _mlstm_state_update(h, direction, kt_ref, v_ref, gt_ref, s_ref, m_ref, qk, vh):
    ci = direction * 2 * N_HEADS + h
    cb = ci + N_HEADS
    last = 0 if direction else CHUNK - 1
    kt = kt_ref[h * qk:(h + 1) * qk, :].astype(F32)
    va = _with_ones(v_ref[:, h * vh:(h + 1) * vh])
    b_last = gt_ref[cb:cb + 1, last:last + 1]
    m_prev = m_ref[h][0:1, 0:1]
    g_r = b_last - gt_ref[cb:cb + 1, :] + gt_ref[ci:ci + 1, :]
    m_new = jnp.maximum(b_last + m_prev, jnp.max(g_r, axis=1, keepdims=True))
    a = jnp.exp(b_last + m_prev - m_new)
    kw = (kt * jnp.exp(g_r - m_new)).astype(BF16)
    s_ref[h] = a * s_ref[h] + jnp.dot(kw, va, preferred_element_type=F32)
    m_ref[h] = jnp.broadcast_to(m_new, m_ref.shape[1:])


def _mlstm_head_output(h, direction, q_ref, kt_ref, v_ref, g_ref, gt_ref, s_ref, m_ref, qk, vh):
    ci = direction * 2 * N_HEADS + h
    cb = ci + N_HEADS
    q = q_ref[:, h * qk:(h + 1) * qk]
    kt = kt_ref[h * qk:(h + 1) * qk, :]
    va = _with_ones(v_ref[:, h * vh:(h + 1) * vh])
    ig_r = gt_ref[ci:ci + 1, :]
    b_r = gt_ref[cb:cb + 1, :]
    b_c = g_ref[:, cb:cb + 1]
    m_prev = m_ref[h][0:1, 0:1]
    row = lax.broadcasted_iota(I32, (CHUNK, CHUNK), 0)
    col = lax.broadcasted_iota(I32, (CHUNK, CHUNK), 1)
    mask = (col >= row) if direction else (col <= row)
    dm = jnp.where(mask, b_c + (ig_r - b_r), _NEG_INF)
    inter = b_c + m_prev
    m_t = jnp.maximum(inter, jnp.max(dm, axis=1, keepdims=True))
    w_inter = jnp.exp(inter - m_t)
    s = jnp.dot(q, kt, preferred_element_type=F32) * jnp.exp(dm - m_t)
    intra = jnp.dot(s.astype(BF16), va, preferred_element_type=F32)
    carried = jnp.dot(q, s_ref[h].astype(BF16), preferred_element_type=F32)
    num = intra[:, 0:vh] + w_inter * carried[:, 0:vh]
    den = intra[:, vh:vh + 1] + w_inter * carried[:, vh:vh + 1]
    return num / jnp.maximum(jnp.abs(den), jnp.exp(-m_t))


def _mlstm_body(*refs, direction, n_ctx_chunks, qk, vh):
    if direction:
        (q_ref, kt_ref, v_ref, g_ref, gt_ref, ktc_ref, vc_ref, gtc_ref,
         out_ref, s_ref, m_ref) = refs
    else:
        (q_ref, kt_ref, v_ref, g_ref, gt_ref, ktc_ref, vc_ref, gtc_ref,
         hb_ref, og_ref, hg_ref, out_ref, s_ref, m_ref) = refs
    step = pl.program_id(1)

    @pl.when(step == 0)
    def _():
        s_ref[...] = jnp.zeros_like(s_ref)
        m_ref[...] = jnp.full_like(m_ref, _NEG_INF)

    @pl.when(step < n_ctx_chunks)
    def _():
        for h in range(N_HEADS):
            _mlstm_state_update(h, direction, ktc_ref, vc_ref, gtc_ref, s_ref, m_ref, qk, vh)

    @pl.when(step >= n_ctx_chunks)
    def _():
        for h in range(N_HEADS):
            hh = _mlstm_head_output(h, direction, q_ref, kt_ref, v_ref, g_ref, gt_ref, s_ref, m_ref, qk, vh)
            cols = slice(h * vh, (h + 1) * vh)
            if direction:
                out_ref[:, cols] = hh
            else:
                hs = hh + hb_ref[:, cols]
                hs = hs * lax.rsqrt(jnp.mean(hs * hs, axis=-1, keepdims=True) + EPS)
                out_ref[:, cols] = (hs * hg_ref[:, cols] * og_ref[:, cols].astype(F32)).astype(BF16)
            _mlstm_state_update(h, direction, kt_ref, v_ref, gt_ref, s_ref, m_ref, qk, vh)


def _mlstm(direction, q, kt, v, g, gt, ktc, vc, gtc, extra, bsz, head_g=None):
    n, qk_all = q.shape
    v_all = v.shape[1]
    qk, vh = qk_all // N_HEADS, v_all // N_HEADS
    nc = n // bsz // CHUNK
    ncc = vc.shape[0] // bsz // CHUNK

    def lat(b, s):
        j = jnp.clip(s - ncc, 0, nc - 1)
        return b * nc + (nc - 1 - j if direction else j)

    def ctx(b, s):
        j = jnp.clip(s, 0, ncc - 1)
        return b * ncc + (ncc - 1 - j if direction else j)

    lat_row = lambda b, s: (lat(b, s), 0)
    lat_col = lambda b, s: (0, lat(b, s))
    ctx_row = lambda b, s: (ctx(b, s), 0)
    ctx_col = lambda b, s: (0, ctx(b, s))
    in_specs = [pl.BlockSpec((CHUNK, qk_all), lat_row), pl.BlockSpec((qk_all, CHUNK), lat_col),
                pl.BlockSpec((CHUNK, v_all), lat_row), pl.BlockSpec((CHUNK, N_GATES), lat_row),
                pl.BlockSpec((N_GATES, CHUNK), lat_col),
                pl.BlockSpec((qk_all, CHUNK), ctx_col), pl.BlockSpec((CHUNK, v_all), ctx_row),
                pl.BlockSpec((N_GATES, CHUNK), ctx_col)]
    args = [q, kt, v, g, gt, ktc, vc, gtc]
    if direction:
        out_dtype = F32
    else:
        hb, og = extra
        in_specs += [pl.BlockSpec((CHUNK, v_all), lat_row), pl.BlockSpec((CHUNK, v_all), lat_row),
                     pl.BlockSpec((1, v_all), lambda b, s: (0, 0))]
        args += [hb, og, head_g]
        out_dtype = BF16
    return pl.pallas_call(
        functools.partial(_mlstm_body, direction=direction, n_ctx_chunks=ncc, qk=qk, vh=vh),
        grid=(bsz, ncc + nc),
        in_specs=in_specs,
        out_specs=pl.BlockSpec((CHUNK, v_all), lat_row),
        out_shape=jax.ShapeDtypeStruct((n, v_all), out_dtype),
        scratch_shapes=[pltpu.VMEM((N_HEADS, qk, vh + LANES), F32),
                        pltpu.VMEM((N_HEADS, SUBLANES, LANES), F32)],
        compiler_params=_params(2),
        name="mlstm_bwd" if direction else "mlstm_fwd",
    )(*args)


def _outproj_body(conv_ref, ml_ref, x_ref, gt1_ref, sh2_ref, sc2_ref, g2_ref, wo_ref, wr_ref, br_ref,
                  x1_ref, h_ref, idx_ref, gate_ref, rank_ref, cnt_ref, carry_ref):
    tm = x_ref.shape[0]
    half = conv_ref.shape[1]

    @pl.when(pl.program_id(0) == 0)
    def _():
        carry_ref[...] = jnp.zeros_like(carry_ref)

    y = (jnp.dot(conv_ref[...], wo_ref[0:half, :], preferred_element_type=F32)
         + jnp.dot(ml_ref[...], wo_ref[half:2 * half, :], preferred_element_type=F32))
    x1 = x_ref[...] + gt1_ref[0] * y
    x1_ref[...] = x1
    hn = _norm_mod(x1, g2_ref[...], sh2_ref[0], sc2_ref[0])
    _pack_rows(hn, h_ref)

    h_hi = hn.astype(BF16)
    h_lo = (hn - h_hi.astype(F32)).astype(BF16)
    parts = (jnp.dot(h_hi, wr_ref[...], preferred_element_type=F32)
             + jnp.dot(h_lo, wr_ref[...], preferred_element_type=F32))
    scores = jax.nn.sigmoid(parts + pltpu.roll(parts, N_EXPERTS, axis=1))
    lane = lax.broadcasted_iota(I32, (tm, LANES), 1).astype(F32)
    biased = jnp.where(lane < N_EXPERTS, scores + br_ref[...], _NEG_INF)
    onehot = jnp.zeros((tm, LANES), F32)
    picks, sels = [], []
    for _ in range(TOP_K):
        mx = jnp.max(biased, axis=1, keepdims=True)
        pick = jnp.min(jnp.where(biased == mx, lane, float(LANES)), axis=1, keepdims=True)
        hit = lane == pick
        sels.append(jnp.sum(jnp.where(hit, scores, 0.0), axis=1, keepdims=True))
        picks.append(pick)
        biased = jnp.where(hit, _NEG_INF, biased)
        onehot = onehot + hit.astype(F32)
    total = sels[0]
    for s in sels[1:]:
        total = total + s

    r = lax.broadcasted_iota(I32, (tm, tm), 0)
    c = lax.broadcasted_iota(I32, (tm, tm), 1)
    strict = jnp.where(c < r, 1.0, 0.0).astype(BF16)
    before = jnp.dot(strict, onehot.astype(BF16), preferred_element_type=F32) + carry_ref[...]
    slot = lax.broadcasted_iota(I32, (tm, SUBLANES), 1)
    idx_out = jnp.zeros((tm, SUBLANES), F32)
    gate_out = jnp.zeros((tm, SUBLANES), F32)
    rank_out = jnp.zeros((tm, SUBLANES), F32)
    for j in range(TOP_K):
        rank = jnp.sum(jnp.where(lane == picks[j], before, 0.0), axis=1, keepdims=True)
        idx_out = jnp.where(slot == j, picks[j], idx_out)
        gate_out = jnp.where(slot == j, sels[j] / total * ROUTED_SCALE, gate_out)
        rank_out = jnp.where(slot == j, rank, rank_out)
    idx_ref[...] = idx_out.astype(I32)
    gate_ref[...] = gate_out
    rank_ref[...] = rank_out.astype(I32)
    carry_ref[...] = carry_ref[...] + jnp.sum(onehot, axis=0, keepdims=True)
    cnt_ref[...] = jnp.broadcast_to(carry_ref[...], cnt_ref.shape).astype(I32)


def _outproj(conv, ml, x2d, mod, g2, w_out, w_router, b_router, rows_per_batch):
    n, d = x2d.shape
    tm = ROW_TILE
    tiles_per_batch = rows_per_batch // tm
    row = lambda i: (i, 0)
    mod_block = (1, 1, d)
    half = conv.shape[1]
    return pl.pallas_call(
        _outproj_body,
        grid=(n // tm,),
        in_specs=[pl.BlockSpec((tm, half), row), pl.BlockSpec((tm, half), row), pl.BlockSpec((tm, d), row),
                  pl.BlockSpec(mod_block, _mod_spec(2, tiles_per_batch)),
                  pl.BlockSpec(mod_block, _mod_spec(3, tiles_per_batch)),
                  pl.BlockSpec(mod_block, _mod_spec(4, tiles_per_batch)),
                  _resident(g2.shape), _resident(w_out.shape), _resident(w_router.shape),
                  _resident(b_router.shape)],
        out_specs=(pl.BlockSpec((tm, d), row), pl.BlockSpec((tm, SUBLANES, LANES), lambda i: (i, 0, 0)),
                   pl.BlockSpec((tm, SUBLANES), row), pl.BlockSpec((tm, SUBLANES), row),
                   pl.BlockSpec((tm, SUBLANES), row),
                   pl.BlockSpec((SUBLANES, LANES), lambda i: (0, 0))),
        out_shape=(jax.ShapeDtypeStruct((n, d), F32), jax.ShapeDtypeStruct((n, SUBLANES, LANES), I32),
                   jax.ShapeDtypeStruct((n, SUBLANES), I32), jax.ShapeDtypeStruct((n, SUBLANES), F32),
                   jax.ShapeDtypeStruct((n, SUBLANES), I32),
                   jax.ShapeDtypeStruct((SUBLANES, LANES), I32)),
        scratch_shapes=[pltpu.VMEM((1, LANES), F32)],
        compiler_params=_params(1),
        name="outproj_router",
    )(conv, ml, x2d, mod, mod, mod, g2, w_out, w_router, b_router)


def _sc_workers():
    info = plsc.get_sparse_core_info()
    return info.num_cores, info.num_cores * info.num_subcores


def _sc_dispatch(h_rows, dest_chunks, n_slots):
    n_tok = h_rows.shape[0]
    n_cores, n_workers = _sc_workers()
    per_worker = n_tok // (n_workers * SC_CHUNK)
    assert per_worker * n_workers * SC_CHUNK == n_tok
    mesh = plsc.VectorSubcoreMesh(core_axis_name="c", subcore_axis_name="s")

    @functools.partial(
        pl.kernel, mesh=mesh,
        out_type=jax.ShapeDtypeStruct((n_slots,) + h_rows.shape[1:], h_rows.dtype),
        scratch_types=[pltpu.VMEM((TOP_K, SC_CHUNK), I32),
                       pltpu.VMEM((SC_CHUNK,) + h_rows.shape[1:], h_rows.dtype)],
    )
    def dispatch(h_hbm, dest_hbm, out_hbm, idx_v, rows_v):
        wid = lax.axis_index("s") * n_cores + lax.axis_index("c")

        @pl.loop(0, per_worker)
        def _(i):
            chunk = wid * per_worker + i
            pltpu.sync_copy(dest_hbm.at[chunk], idx_v)
            pltpu.sync_copy(h_hbm.at[pl.ds(chunk * SC_CHUNK, SC_CHUNK)], rows_v)
            for k in range(TOP_K):
                pltpu.sync_copy(rows_v, out_hbm.at[idx_v.at[k]])

    return dispatch(h_rows, dest_chunks)


def _sc_combine(y_sorted, dest_chunks, n_tok):
    n_cores, n_workers = _sc_workers()
    per_worker = n_tok // (n_workers * SC_CHUNK)
    mesh = plsc.VectorSubcoreMesh(core_axis_name="c", subcore_axis_name="s")

    @functools.partial(
        pl.kernel, mesh=mesh,
        out_type=jax.ShapeDtypeStruct((TOP_K, n_tok) + y_sorted.shape[1:], y_sorted.dtype),
        scratch_types=[pltpu.VMEM((TOP_K, SC_CHUNK), I32),
                       pltpu.VMEM((SC_CHUNK,) + y_sorted.shape[1:], y_sorted.dtype)],
    )
    def combine(y_hbm, dest_hbm, out_hbm, idx_v, rows_v):
        wid = lax.axis_index("s") * n_cores + lax.axis_index("c")

        @pl.loop(0, per_worker)
        def _(i):
            chunk = wid * per_worker + i
            pltpu.sync_copy(dest_hbm.at[chunk], idx_v)
            for k in range(TOP_K):
                pltpu.sync_copy(y_hbm.at[idx_v.at[k]], rows_v)
                pltpu.sync_copy(rows_v, out_hbm.at[k, pl.ds(chunk * SC_CHUNK, SC_CHUNK)])

    return combine(y_sorted, dest_chunks)


def _moe_body(ord_ref, order_ref, glo_ref, ghi_ref, tot_ref, nb_ref,
              x_ref, wg_hbm, wu_hbm, wd_hbm, y_ref,
              wgu, wd, stage_a, stage_d, wsem, *, d_expert):
    b = pl.program_id(0)
    nb = nb_ref[0]
    total = tot_ref[0]
    d_model = wgu.shape[1]
    rows_a = d_model // WEIGHT_PARTS
    rows_d = d_expert // WEIGHT_PARTS

    def part_copies(g):
        e = order_ref[lax.shift_right_logical(g, PART_SHIFT)]
        i = g & (WEIGHT_PARTS - 1)
        s = lax.rem(g, WEIGHT_RING)
        return (pltpu.make_async_copy(wg_hbm.at[e, pl.ds(i * rows_a, rows_a)], stage_a.at[s, 0],
                                      wsem.at[s, 0]),
                pltpu.make_async_copy(wu_hbm.at[e, pl.ds(i * rows_a, rows_a)], stage_a.at[s, 1],
                                      wsem.at[s, 1]),
                pltpu.make_async_copy(wd_hbm.at[e, pl.ds(i * rows_d, rows_d)], stage_d.at[s],
                                      wsem.at[s, 2]))

    def start_part(g):
        for cp in part_copies(g):
            cp.start()

    def finish_part(g):
        for cp in part_copies(g):
            cp.wait()
        i = g & (WEIGHT_PARTS - 1)
        s = lax.rem(g, WEIGHT_RING)
        par = lax.shift_right_logical(g, PART_SHIFT) & 1
        ra = pl.multiple_of(i * rows_a, rows_a)
        rd = pl.multiple_of(i * rows_d, rows_d)
        wgu[par, pl.ds(ra, rows_a), 0:d_expert] = stage_a[s, 0].astype(BF16)
        wgu[par, pl.ds(ra, rows_a), d_expert:2 * d_expert] = stage_a[s, 1].astype(BF16)
        wd[par, pl.ds(rd, rows_d), :] = stage_d[s].astype(BF16)

    def cast_parts(lo, hi):
        def body(g, carry):
            finish_part(g)

            @pl.when(g + WEIGHT_RING < total)
            def _():
                start_part(g + WEIGHT_RING)
            return carry
        lax.fori_loop(lo, hi, body, 0)

    @pl.when(b == 0)
    def _():
        for g in range(WEIGHT_RING):
            start_part(g)
        cast_parts(0, WEIGHT_PARTS)

    @pl.when(b < nb)
    def _():
        par = ord_ref[b] & 1
        x = jnp.concatenate(_unpack_rows(x_ref), axis=1).astype(BF16)
        gu = jnp.dot(x, wgu[par], preferred_element_type=F32)
        hb = (_silu(gu[:, 0:d_expert]) * gu[:, d_expert:2 * d_expert]).astype(BF16)
        _pack_rows(jnp.dot(hb, wd[par], preferred_element_type=F32), y_ref)
        cast_parts(glo_ref[b], ghi_ref[b])


def _moe(x_sorted, we_gate, we_up, we_down, tables):
    d, d_expert = we_gate.shape[1], we_gate.shape[2]
    nb_max = x_sorted.shape[0] // MOE_BLOCK
    any_spec = pl.BlockSpec(memory_space=pl.ANY)
    row_blk = pl.BlockSpec((MOE_BLOCK,) + x_sorted.shape[1:],
                           lambda b, o, order, glo, ghi, tot, nb: (jnp.minimum(b, nb[0] - 1), 0, 0))
    grid_spec = pltpu.PrefetchScalarGridSpec(
        num_scalar_prefetch=len(tables),
        grid=(nb_max,),
        in_specs=[row_blk, any_spec, any_spec, any_spec],
        out_specs=row_blk,
        scratch_shapes=[pltpu.VMEM((2, d, 2 * d_expert), BF16),
                        pltpu.VMEM((2, d_expert, d), BF16),
                        pltpu.VMEM((WEIGHT_RING, 2, d // WEIGHT_PARTS, d_expert), F32),
                        pltpu.VMEM((WEIGHT_RING, d_expert // WEIGHT_PARTS, d), F32),
                        pltpu.SemaphoreType.DMA((WEIGHT_RING, 3))],
    )
    return pl.pallas_call(
        functools.partial(_moe_body, d_expert=d_expert),
        grid_spec=grid_spec,
        out_shape=jax.ShapeDtypeStruct(x_sorted.shape, x_sorted.dtype),
        compiler_params=pltpu.CompilerParams(
            dimension_semantics=("arbitrary",), vmem_limit_bytes=MOE_VMEM_LIMIT),
        name="moe_routed",
    )(*tables, x_sorted, we_gate, we_up, we_down)


def _final_body(h_ref, x1_ref, y_ref, gate_ref, gt2_ref, wsgu_ref, wsd_ref, fg_ref, out_ref, *, d_shared):
    h = jnp.concatenate(_unpack_rows(h_ref), axis=1).astype(BF16)
    gu = jnp.dot(h, wsgu_ref[...], preferred_element_type=F32)
    hb = (_silu(gu[:, 0:d_shared]) * gu[:, d_shared:2 * d_shared]).astype(BF16)
    acc = jnp.dot(hb, wsd_ref[...], preferred_element_type=F32)
    for k in range(TOP_K):
        lo, hi = _unpack_rows(y_ref.at[k])
        acc = acc + gate_ref[:, k:k + 1] * jnp.concatenate([lo, hi], axis=1)
    x2 = x1_ref[...] + gt2_ref[0] * acc
    out_ref[...] = x2 * lax.rsqrt(jnp.mean(x2 * x2, axis=-1, keepdims=True) + EPS) * fg_ref[...]


def _final(h_rows, x1, y_tok, gates, mod, ws_gu, ws_d, final_g, rows_per_batch):
    n, d = x1.shape
    tm = ROW_TILE
    tiles_per_batch = rows_per_batch // tm
    row = lambda i: (i, 0)
    packed = h_rows.shape[1:]
    return pl.pallas_call(
        functools.partial(_final_body, d_shared=ws_d.shape[0]),
        grid=(n // tm,),
        in_specs=[pl.BlockSpec((tm,) + packed, lambda i: (i, 0, 0)), pl.BlockSpec((tm, d), row),
                  pl.BlockSpec((TOP_K, tm) + packed, lambda i: (0, i, 0, 0)),
                  pl.BlockSpec((tm, SUBLANES), row),
                  pl.BlockSpec((1, 1, d), _mod_spec(5, tiles_per_batch)),
                  _resident(ws_gu.shape), _resident(ws_d.shape), _resident(final_g.shape)],
        out_specs=pl.BlockSpec((tm, d), row),
        out_shape=jax.ShapeDtypeStruct((n, d), F32),
        compiler_params=_params(1),
        name="shared_combine_final",
    )(h_rows, x1, y_tok, gates, mod, ws_gu, ws_d, final_g)


def _routing_tables(idx, rank, counts, n_tok):
    nb_max = -(-(n_tok * TOP_K) // MOE_BLOCK) + N_EXPERTS
    nblk = (counts + MOE_BLOCK - 1) // MOE_BLOCK
    blk_end = jnp.cumsum(nblk)
    blk_start = blk_end - nblk
    dest = (blk_start * MOE_BLOCK)[idx] + rank
    dest_chunks = dest.reshape(n_tok // SC_CHUNK, SC_CHUNK, TOP_K).transpose(0, 2, 1)

    blocks = jnp.arange(nb_max, dtype=I32)
    block_e = jnp.minimum(jnp.searchsorted(blk_end, blocks, side="right"), N_EXPERTS - 1).astype(I32)
    nonempty = nblk > 0
    n_visited = jnp.sum(nonempty.astype(I32))
    ordinal_of = jnp.cumsum(nonempty.astype(I32)) - 1
    order = jnp.argsort(jnp.where(nonempty, 0, 1), stable=True).astype(I32)
    ordinal = ordinal_of[block_e]
    k_in_e = blocks - blk_start[block_e]
    nb_e = jnp.maximum(nblk[block_e], 1)
    live = (ordinal + 1 < n_visited) & (blocks < blk_end[-1])
    first = WEIGHT_PARTS * (ordinal + 1)
    lo = jnp.where(live, first + WEIGHT_PARTS * k_in_e // nb_e, 0)
    hi = jnp.where(live, first + WEIGHT_PARTS * (k_in_e + 1) // nb_e, 0)
    tables = (ordinal.astype(I32), order, lo.astype(I32), hi.astype(I32),
              (WEIGHT_PARTS * n_visited).reshape(1).astype(I32), blk_end[-1:].astype(I32))
    return tables, dest_chunks, nb_max * MOE_BLOCK


def kernel(x, c, ctx, c_ctx, norm1_g, norm2_g, w_ada, b_ada, w_in, conv_w, gate_b, head_g, w_out,
           w_router, b_router, we_gate, we_up, we_down, ws_gate, ws_up, ws_down, final_g):
    assert w_ada.shape[0] == 1, "single-layer block"
    bsz, seq, d = x.shape
    ctx_len = ctx.shape[1]
    n_tok = bsz * seq
    conv_dim = conv_w.shape[2]
    v_all = head_g.shape[1]
    qk_all = (w_in.shape[2] - 3 * conv_dim - 2 * v_all - N_GATES) // 2
    assert seq % ROW_TILE == 0 and ctx_len % ROW_TILE == 0 and ROW_TILE % GRID_W == 0
    assert bsz + 1 <= SUBLANES

    cc = jnp.zeros((SUBLANES, d), F32).at[:bsz].set(c).at[bsz].set(c_ctx)
    mod = _adaln(cc, w_ada[0], b_ada).reshape(SUBLANES * 6, 1, d)

    n_main = 3 * conv_dim + 2 * qk_all + 2 * v_all
    w_main = w_in[0, :, :n_main].astype(BF16)
    k_lo = 3 * conv_dim + qk_all
    w_kt = w_in[0, :, k_lo:k_lo + qk_all].T.astype(BF16)
    w_v = w_in[0, :, k_lo + qk_all:k_lo + qk_all + v_all].astype(BF16)
    w_gate = jnp.zeros((d, LANES), BF16).at[:, :N_GATES].set(w_in[0, :, n_main:].astype(BF16))
    gate_bias = jnp.zeros((1, LANES), F32).at[0, :N_GATES].set(gate_b[0].reshape(-1))

    x2d = x.reshape(n_tok, d)
    conv, q, kt, v, og, g, gt = _inproj(x2d, mod, norm1_g, w_main, w_kt, w_gate, gate_bias, conv_w[0],
                                       seq, conv_dim, qk_all, v_all)
    ktc, vc, _, gtc = _inproj_ctx(ctx.reshape(bsz * ctx_len, d), mod, norm1_g, w_v, w_kt, w_gate,
                                  gate_bias, bsz)

    h_bwd = _mlstm(1, q, kt, v, g, gt, ktc, vc, gtc, None, bsz)
    ml = _mlstm(0, q, kt, v, g, gt, ktc, vc, gtc, (h_bwd, og), bsz, head_g)

    assert 2 * N_EXPERTS == LANES
    w_r_hi = w_router[0].astype(BF16)
    w_r = jnp.concatenate([w_r_hi, (w_router[0] - w_r_hi.astype(F32)).astype(BF16)], axis=1)
    b_r = jnp.zeros((1, LANES), F32).at[0, :N_EXPERTS].set(b_router[0])
    x1, h_rows, idx, gates, rank, cnt = _outproj(conv, ml, x2d, mod, norm2_g, w_out[0].astype(BF16),
                                             w_r, b_r, seq)

    tables, dest_chunks, n_slots = _routing_tables(idx[:, :TOP_K], rank[:, :TOP_K], cnt[0, :N_EXPERTS],
                                                   n_tok)
    x_sorted = _sc_dispatch(h_rows, dest_chunks, n_slots)
    y_sorted = _moe(x_sorted, we_gate[0], we_up[0], we_down[0], tables)
    y_tok = _sc_combine(y_sorted, dest_chunks, n_tok)

    ws_gu = jnp.concatenate([ws_gate[0], ws_up[0]], axis=1).astype(BF16)
    out = _final(h_rows, x1, y_tok, gates, mod, ws_gu, ws_down[0].astype(BF16),
                 final_g.reshape(1, d), seq)
    return out.reshape(bsz, seq, d)
```

```python
import functools

import jax
import jax.numpy as jnp
from jax import lax
from jax.experimental import pallas as pl
from jax.experimental.pallas import tpu as pltpu
from jax.experimental.pallas import tpu_sc as plsc

F32 = jnp.float32
BF16 = jnp.bfloat16
I32 = jnp.int32

N_HEADS = 4
GRID_W = 64
CHUNK = 128
TOP_K = 6
N_EXPERTS = 64
ROUTED_SCALE = 2.446
EPS = 1e-6
N_GATES = 4 * N_HEADS

LANES = 128
SUBLANES = 8
MOE_BLOCK = 256
ROW_TILE = 256
ADALN_TILE = 1024
WEIGHT_PARTS = 8
PART_SHIFT = 3
WEIGHT_RING = 3
SC_CHUNK = 64
HIGH_HALF = -65536
VMEM_LIMIT = 56 * 1024 * 1024
MOE_VMEM_LIMIT = 62 * 1024 * 1024

_HIGHEST = lax.Precision.HIGHEST
_NEG_INF = float("-inf")


def _resident(shape):
    nd = len(shape)
    return pl.BlockSpec(shape, lambda *_: (0,) * nd, pipeline_mode=pl.Buffered(1))


def _params(n_axes):
    return pltpu.CompilerParams(
        dimension_semantics=("arbitrary",) * n_axes, vmem_limit_bytes=VMEM_LIMIT)


def _log_sigmoid(x):
    return jnp.minimum(x, 0.0) - jnp.log1p(jnp.exp(-jnp.abs(x)))


def _silu(x):
    return x * jax.nn.sigmoid(x)


def _pack_words(val):
    half = val.shape[1] // 2
    lo = lax.bitcast_convert_type(val[:, :half].astype(BF16).astype(F32), I32)
    hi = lax.bitcast_convert_type(val[:, half:].astype(BF16).astype(F32), I32)
    return (hi & HIGH_HALF) | lax.shift_right_logical(lo, 16)


def _unpack_words(word):
    lo = lax.bitcast_convert_type(lax.shift_left(word, 16), F32)
    hi = lax.bitcast_convert_type(word & HIGH_HALF, F32)
    return lo, hi


def _row_tile_copies(hbm_rows, row0, tile, sem, to_hbm):
    n = tile.shape[0]
    copies = []
    for c in range(SUBLANES):
        hbm = hbm_rows.at[pl.ds(row0, n), c, :]
        vmem = tile.at[:, pl.ds(c * LANES, LANES)]
        copies.append(pltpu.make_async_copy(vmem, hbm, sem) if to_hbm
                      else pltpu.make_async_copy(hbm, vmem, sem))
    return copies


def _start_all(copies):
    for cp in copies:
        cp.start()


def _wait_all(copies):
    for cp in copies:
        cp.wait()


def _adaln_body(c_ref, w_ref, b_ref, o_ref):
    s = _silu(c_ref[...])
    o_ref[...] = jnp.dot(s.astype(BF16), w_ref[...].astype(BF16),
                         preferred_element_type=F32) + b_ref[...]


def _adaln(cc, w, b):
    d, n6 = w.shape
    return pl.pallas_call(
        _adaln_body,
        grid=(n6 // ADALN_TILE,),
        in_specs=[pl.BlockSpec((SUBLANES, d), lambda j: (0, 0)),
                  pl.BlockSpec((d, ADALN_TILE), lambda j: (0, j)),
                  pl.BlockSpec((1, ADALN_TILE), lambda j: (0, j))],
        out_specs=pl.BlockSpec((SUBLANES, ADALN_TILE), lambda j: (0, j)),
        out_shape=jax.ShapeDtypeStruct((SUBLANES, n6), F32),
        compiler_params=_params(1),
        name="adaln",
    )(cc, w, b)


def _norm_mod(x, g, shift, scale):
    y = x * lax.rsqrt(jnp.mean(x * x, axis=-1, keepdims=True) + EPS) * g
    return y * (1.0 + scale) + shift


def _gate_prep(xb, wg_ref, gb_ref, g_ref, gt_ref):
    tm = xb.shape[0]
    gg = jnp.dot(xb, wg_ref[...], preferred_element_type=F32) + gb_ref[...]
    lane = lax.broadcasted_iota(I32, (tm, LANES), 1)
    is_f = (lane & N_HEADS) != 0
    is_bwd = (lane & (2 * N_HEADS)) != 0
    lf = jnp.where(is_f, _log_sigmoid(gg), 0.0)
    r = lax.broadcasted_iota(I32, (tm, tm), 0)
    c = lax.broadcasted_iota(I32, (tm, tm), 1)
    same = (r // CHUNK) == (c // CHUNK)
    tri_l = jnp.where(same & (c <= r), 1.0, 0.0).astype(F32)
    tri_u = jnp.where(same & (c >= r), 1.0, 0.0).astype(F32)
    pre = jnp.dot(tri_l, lf, precision=_HIGHEST, preferred_element_type=F32)
    suf = jnp.dot(tri_u, lf, precision=_HIGHEST, preferred_element_type=F32)
    out = jnp.where(is_f, jnp.where(is_bwd, suf, pre), gg)
    g_ref[...] = out[:, :N_GATES]
    gt_ref[...] = out.T[:N_GATES, :]


def _project_transposed(wt_ref, xb):
    return lax.dot_general(wt_ref[...], xb, (((1,), (1,)), ((), ())),
                           preferred_element_type=F32).astype(BF16)


def _inproj_body(x_ref, sh_ref, sc_ref, g1_ref, w_ref, wkt_ref, wg_ref, gb_ref, cw_ref,
                 conv_ref, q_ref, k_ref, v_ref, o_ref, g_ref, gt_ref, *, conv_dim, qk_all, v_all):
    tm = x_ref.shape[0]
    xb = _norm_mod(x_ref[...], g1_ref[...], sh_ref[0], sc_ref[0]).astype(BF16)

    def proj(lo, width):
        return jnp.dot(xb, w_ref[:, lo:lo + width], preferred_element_type=F32)

    u = proj(conv_dim, conv_dim) * proj(2 * conv_dim, conv_dim)
    pos = lax.broadcasted_iota(I32, (tm, 1), 0) % GRID_W
    um = jnp.where(pos == 0, 0.0, pltpu.roll(u, 1, axis=0))
    up = jnp.where(pos == GRID_W - 1, 0.0, pltpu.roll(u, tm - 1, axis=0))
    y = um * cw_ref[0:1, :] + u * cw_ref[1:2, :] + up * cw_ref[2:3, :]
    conv_ref[...] = (proj(0, conv_dim) * y).astype(BF16)

    off = 3 * conv_dim
    qscale = (qk_all // N_HEADS) ** -0.5
    q_ref[...] = (proj(off, qk_all) * qscale).astype(BF16)
    k_ref[...] = _project_transposed(wkt_ref, xb)
    v_ref[...] = proj(off + 2 * qk_all, v_all).astype(BF16)
    o_ref[...] = jax.nn.sigmoid(proj(off + 2 * qk_all + v_all, v_all)).astype(BF16)
    _gate_prep(xb, wg_ref, gb_ref, g_ref, gt_ref)


def _inproj_ctx_body(x_ref, sh_ref, sc_ref, g1_ref, w_ref, wkt_ref, wg_ref, gb_ref,
                     k_ref, v_ref, g_ref, gt_ref):
    xb = _norm_mod(x_ref[...], g1_ref[...], sh_ref[0], sc_ref[0]).astype(BF16)
    k_ref[...] = _project_transposed(wkt_ref, xb)
    v_ref[...] = jnp.dot(xb, w_ref[...], preferred_element_type=F32).astype(BF16)
    _gate_prep(xb, wg_ref, gb_ref, g_ref, gt_ref)


def _mod_spec(part, tiles_per_row, fixed_row=None):
    def index(i):
        row = fixed_row if fixed_row is not None else i // tiles_per_row
        return (row * 6 + part, 0, 0)

    return index


def _inproj(x2d, mod, g1, w_main, w_kt, w_gate, gate_b, conv_w, rows_per_batch, conv_dim, qk_all, v_all):
    n, d = x2d.shape
    tm = ROW_TILE
    tiles_per_batch = rows_per_batch // tm
    row = lambda i: (i, 0)
    mod_block = (1, 1, d)
    out_shapes = (
        jax.ShapeDtypeStruct((n, conv_dim), BF16),
        jax.ShapeDtypeStruct((n, qk_all), BF16),
        jax.ShapeDtypeStruct((qk_all, n), BF16),
        jax.ShapeDtypeStruct((n, v_all), BF16),
        jax.ShapeDtypeStruct((n, v_all), BF16),
        jax.ShapeDtypeStruct((n, N_GATES), F32),
        jax.ShapeDtypeStruct((N_GATES, n), F32),
    )
    out_specs = (
        pl.BlockSpec((tm, conv_dim), row),
        pl.BlockSpec((tm, qk_all), row),
        pl.BlockSpec((qk_all, tm), lambda i: (0, i)),
        pl.BlockSpec((tm, v_all), row),
        pl.BlockSpec((tm, v_all), row),
        pl.BlockSpec((tm, N_GATES), row),
        pl.BlockSpec((N_GATES, tm), lambda i: (0, i)),
    )
    return pl.pallas_call(
        functools.partial(_inproj_body, conv_dim=conv_dim, qk_all=qk_all, v_all=v_all),
        grid=(n // tm,),
        in_specs=[pl.BlockSpec((tm, d), row),
                  pl.BlockSpec(mod_block, _mod_spec(0, tiles_per_batch)),
                  pl.BlockSpec(mod_block, _mod_spec(1, tiles_per_batch)),
                  _resident(g1.shape), _resident(w_main.shape), _resident(w_kt.shape),
                  _resident(w_gate.shape), _resident(gate_b.shape), _resident(conv_w.shape)],
        out_specs=out_specs,
        out_shape=out_shapes,
        compiler_params=_params(1),
        name="inproj",
    )(x2d, mod, mod, g1, w_main, w_kt, w_gate, gate_b, conv_w)


def _inproj_ctx(c2d, mod, g1, w_v, w_kt, w_gate, gate_b, ctx_mod_row):
    n, d = c2d.shape
    tm = ROW_TILE
    row = lambda i: (i, 0)
    mod_block = (1, 1, d)
    qk_all, v_all = w_kt.shape[0], w_v.shape[1]
    return pl.pallas_call(
        _inproj_ctx_body,
        grid=(n // tm,),
        in_specs=[pl.BlockSpec((tm, d), row),
                  pl.BlockSpec(mod_block, _mod_spec(0, 1, ctx_mod_row)),
                  pl.BlockSpec(mod_block, _mod_spec(1, 1, ctx_mod_row)),
                  _resident(g1.shape), _resident(w_v.shape), _resident(w_kt.shape),
                  _resident(w_gate.shape), _resident(gate_b.shape)],
        out_specs=(pl.BlockSpec((qk_all, tm), lambda i: (0, i)), pl.BlockSpec((tm, v_all), row),
                   pl.BlockSpec((tm, N_GATES), row), pl.BlockSpec((N_GATES, tm), lambda i: (0, i))),
        out_shape=(jax.ShapeDtypeStruct((qk_all, n), BF16), jax.ShapeDtypeStruct((n, v_all), BF16),
                   jax.ShapeDtypeStruct((n, N_GATES), F32), jax.ShapeDtypeStruct((N_GATES, n), F32)),
        compiler_params=_params(1),
        name="inproj_ctx",
    )(c2d, mod, mod, g1, w_v, w_kt, w_gate, gate_b)


def _with_ones(v):
    return jnp.concatenate([v, jnp.ones((v.shape[0], LANES), v.dtype)], axis=1)


def _mlstm_state_update(h, direction, kt_ref, v_ref, gt_ref, s_ref, m_ref, qk, vh):
    ci = direction * 2 * N_HEADS + h
    cb = ci + N_HEADS
    last = 0 if direction else CHUNK - 1
    kt = kt_ref[h * qk:(h + 1) * qk, :].astype(F32)
    va = _with_ones(v_ref[:, h * vh:(h + 1) * vh])
    b_last = gt_ref[cb:cb + 1, last:last + 1]
    m_prev = m_ref[h][0:1, 0:1]
    g_r = b_last - gt_ref[cb:cb + 1, :] + gt_ref[ci:ci + 1, :]
    m_new = jnp.maximum(b_last + m_prev, jnp.max(g_r, axis=1, keepdims=True))
    a = jnp.exp(b_last + m_prev - m_new)
    kw = (kt * jnp.exp(g_r - m_new)).astype(BF16)
    s_ref[h] = a * s_ref[h] + jnp.dot(kw, va, preferred_element_type=F32)
    m_ref[h] = jnp.broadcast_to(m_new, m_ref.shape[1:])


def _mlstm_head_output(h, direction, q_ref, kt_ref, v_ref, g_ref, gt_ref, s_ref, m_ref, qk, vh):
    ci = direction * 2 * N_HEADS + h
    cb = ci + N_HEADS
    q = q_ref[:, h * qk:(h + 1) * qk]
    kt = kt_ref[h * qk:(h + 1) * qk, :]
    va = _with_ones(v_ref[:, h * vh:(h + 1) * vh])
    ig_r = gt_ref[ci:ci + 1, :]
    b_r = gt_ref[cb:cb + 1, :]
    b_c = g_ref[:, cb:cb + 1]
    m_prev = m_ref[h][0:1, 0:1]
    row = lax.broadcasted_iota(I32, (CHUNK, CHUNK), 0)
    col = lax.broadcasted_iota(I32, (CHUNK, CHUNK), 1)
    mask = (col >= row) if direction else (col <= row)
    dm = jnp.where(mask, b_c + (ig_r - b_r), _NEG_INF)
    inter = b_c + m_prev
    m_t = jnp.maximum(inter, jnp.max(dm, axis=1, keepdims=True))
    w_inter = jnp.exp(inter - m_t)
    s = jnp.dot(q, kt, preferred_element_type=F32) * jnp.exp(dm - m_t)
    intra = jnp.dot(s.astype(BF16), va, preferred_element_type=F32)
    carried = jnp.dot(q, s_ref[h].astype(BF16), preferred_element_type=F32)
    num = intra[:, 0:vh] + w_inter * carried[:, 0:vh]
    den = intra[:, vh:vh + 1] + w_inter * carried[:, vh:vh + 1]
    return num / jnp.maximum(jnp.abs(den), jnp.exp(-m_t))


def _mlstm_body(*refs, direction, n_ctx_chunks, qk, vh):
    if direction:
        (q_ref, kt_ref, v_ref, g_ref, gt_ref, ktc_ref, vc_ref, gtc_ref,
         out_ref, s_ref, m_ref) = refs
    else:
        (q_ref, kt_ref, v_ref, g_ref, gt_ref, ktc_ref, vc_ref, gtc_ref,
         hb_ref, og_ref, hg_ref, out_ref, s_ref, m_ref) = refs
    step = pl.program_id(1)

    @pl.when(step == 0)
    def _():
        s_ref[...] = jnp.zeros_like(s_ref)
        m_ref[...] = jnp.full_like(m_ref, _NEG_INF)

    @pl.when(step < n_ctx_chunks)
    def _():
        for h in range(N_HEADS):
            _mlstm_state_update(h, direction, ktc_ref, vc_ref, gtc_ref, s_ref, m_ref, qk, vh)

    @pl.when(step >= n_ctx_chunks)
    def _():
        for h in range(N_HEADS):
            hh = _mlstm_head_output(h, direction, q_ref, kt_ref, v_ref, g_ref, gt_ref, s_ref, m_ref, qk, vh)
            cols = slice(h * vh, (h + 1) * vh)
            if direction:
                out_ref[:, cols] = hh
            else:
                hs = hh + hb_ref[:, cols]
                hs = hs * lax.rsqrt(jnp.mean(hs * hs, axis=-1, keepdims=True) + EPS)
                out_ref[:, cols] = (hs * hg_ref[:, cols] * og_ref[:, cols].astype(F32)).astype(BF16)
            _mlstm_state_update(h, direction, kt_ref, v_ref, gt_ref, s_ref, m_ref, qk, vh)


def _mlstm(direction, q, kt, v, g, gt, ktc, vc, gtc, extra, bsz, head_g=None):
    n, qk_all = q.shape
    v_all = v.shape[1]
    qk, vh = qk_all // N_HEADS, v_all // N_HEADS
    nc = n // bsz // CHUNK
    ncc = vc.shape[0] // bsz // CHUNK

    def lat(b, s):
        j = jnp.clip(s - ncc, 0, nc - 1)
        return b * nc + (nc - 1 - j if direction else j)

    def ctx(b, s):
        j = jnp.clip(s, 0, ncc - 1)
        return b * ncc + (ncc - 1 - j if direction else j)

    lat_row = lambda b, s: (lat(b, s), 0)
    lat_col = lambda b, s: (0, lat(b, s))
    ctx_row = lambda b, s: (ctx(b, s), 0)
    ctx_col = lambda b, s: (0, ctx(b, s))
    in_specs = [pl.BlockSpec((CHUNK, qk_all), lat_row), pl.BlockSpec((qk_all, CHUNK), lat_col),
                pl.BlockSpec((CHUNK, v_all), lat_row), pl.BlockSpec((CHUNK, N_GATES), lat_row),
                pl.BlockSpec((N_GATES, CHUNK), lat_col),
                pl.BlockSpec((qk_all, CHUNK), ctx_col), pl.BlockSpec((CHUNK, v_all), ctx_row),
                pl.BlockSpec((N_GATES, CHUNK), ctx_col)]
    args = [q, kt, v, g, gt, ktc, vc, gtc]
    if direction:
        out_dtype = F32
    else:
        hb, og = extra
        in_specs += [pl.BlockSpec((CHUNK, v_all), lat_row), pl.BlockSpec((CHUNK, v_all), lat_row),
                     pl.BlockSpec((1, v_all), lambda b, s: (0, 0))]
        args += [hb, og, head_g]
        out_dtype = BF16
    return pl.pallas_call(
        functools.partial(_mlstm_body, direction=direction, n_ctx_chunks=ncc, qk=qk, vh=vh),
        grid=(bsz, ncc + nc),
        in_specs=in_specs,
        out_specs=pl.BlockSpec((CHUNK, v_all), lat_row),
        out_shape=jax.ShapeDtypeStruct((n, v_all), out_dtype),
        scratch_shapes=[pltpu.VMEM((N_HEADS, qk, vh + LANES), F32),
                        pltpu.VMEM((N_HEADS, SUBLANES, LANES), F32)],
        compiler_params=_params(2),
        name="mlstm_bwd" if direction else "mlstm_fwd",
    )(*args)


def _outproj_body(conv_ref, ml_ref, x_ref, gt1_ref, sh2_ref, sc2_ref, g2_ref, wo_ref, wr_ref, br_ref,
                  x1_ref, idx_ref, gate_ref, rank_ref, cnt_ref, h_hbm, carry_ref, hw, hsem):
    tm = x_ref.shape[0]
    half = conv_ref.shape[1]
    step = pl.program_id(0)
    buf = step % 2

    def h_out(i, s):
        return _row_tile_copies(h_hbm, i * tm, hw.at[s], hsem.at[s], to_hbm=True)

    @pl.when(step == 0)
    def _():
        carry_ref[...] = jnp.zeros_like(carry_ref)

    @pl.when(step >= 2)
    def _():
        _wait_all(h_out(step - 2, buf))

    y = (jnp.dot(conv_ref[...], wo_ref[0:half, :], preferred_element_type=F32)
         + jnp.dot(ml_ref[...], wo_ref[half:2 * half, :], preferred_element_type=F32))
    x1 = x_ref[...] + gt1_ref[0] * y
    x1_ref[...] = x1
    hn = _norm_mod(x1, g2_ref[...], sh2_ref[0], sc2_ref[0])
    hw[buf] = _pack_words(hn)
    _start_all(h_out(step, buf))

    h_hi = hn.astype(BF16)
    h_lo = (hn - h_hi.astype(F32)).astype(BF16)
    parts = (jnp.dot(h_hi, wr_ref[...], preferred_element_type=F32)
             + jnp.dot(h_lo, wr_ref[...], preferred_element_type=F32))
    scores = jax.nn.sigmoid(parts + pltpu.roll(parts, N_EXPERTS, axis=1))
    lane = lax.broadcasted_iota(I32, (tm, LANES), 1).astype(F32)
    biased = jnp.where(lane < N_EXPERTS, scores + br_ref[...], _NEG_INF)
    onehot = jnp.zeros((tm, LANES), F32)
    picks, sels = [], []
    for _ in range(TOP_K):
        mx = jnp.max(biased, axis=1, keepdims=True)
        pick = jnp.min(jnp.where(biased == mx, lane, float(LANES)), axis=1, keepdims=True)
        hit = lane == pick
        sels.append(jnp.sum(jnp.where(hit, scores, 0.0), axis=1, keepdims=True))
        picks.append(pick)
        biased = jnp.where(hit, _NEG_INF, biased)
        onehot = onehot + hit.astype(F32)
    total = sels[0]
    for s in sels[1:]:
        total = total + s

    r = lax.broadcasted_iota(I32, (tm, tm), 0)
    c = lax.broadcasted_iota(I32, (tm, tm), 1)
    strict = jnp.where(c < r, 1.0, 0.0).astype(BF16)
    before = jnp.dot(strict, onehot.astype(BF16), preferred_element_type=F32) + carry_ref[...]
    slot = lax.broadcasted_iota(I32, (tm, SUBLANES), 1)
    idx_out = jnp.zeros((tm, SUBLANES), F32)
    gate_out = jnp.zeros((tm, SUBLANES), F32)
    rank_out = jnp.zeros((tm, SUBLANES), F32)
    for j in range(TOP_K):
        rank = jnp.sum(jnp.where(lane == picks[j], before, 0.0), axis=1, keepdims=True)
        idx_out = jnp.where(slot == j, picks[j], idx_out)
        gate_out = jnp.where(slot == j, sels[j] / total * ROUTED_SCALE, gate_out)
        rank_out = jnp.where(slot == j, rank, rank_out)
    idx_ref[...] = idx_out.astype(I32)
    gate_ref[...] = gate_out
    rank_ref[...] = rank_out.astype(I32)
    carry_ref[...] = carry_ref[...] + jnp.sum(onehot, axis=0, keepdims=True)
    cnt_ref[...] = jnp.broadcast_to(carry_ref[...], cnt_ref.shape).astype(I32)

    @pl.when(step == pl.num_programs(0) - 1)
    def _():
        @pl.when(step >= 1)
        def _():
            _wait_all(h_out(step - 1, 1 - buf))
        _wait_all(h_out(step, buf))


def _outproj(conv, ml, x2d, mod, g2, w_out, w_router, b_router, rows_per_batch):
    n, d = x2d.shape
    tm = ROW_TILE
    tiles_per_batch = rows_per_batch // tm
    row = lambda i: (i, 0)
    mod_block = (1, 1, d)
    half = conv.shape[1]
    return pl.pallas_call(
        _outproj_body,
        grid=(n // tm,),
        in_specs=[pl.BlockSpec((tm, half), row), pl.BlockSpec((tm, half), row), pl.BlockSpec((tm, d), row),
                  pl.BlockSpec(mod_block, _mod_spec(2, tiles_per_batch)),
                  pl.BlockSpec(mod_block, _mod_spec(3, tiles_per_batch)),
                  pl.BlockSpec(mod_block, _mod_spec(4, tiles_per_batch)),
                  _resident(g2.shape), _resident(w_out.shape), _resident(w_router.shape),
                  _resident(b_router.shape)],
        out_specs=(pl.BlockSpec((tm, d), row),
                   pl.BlockSpec((tm, SUBLANES), row), pl.BlockSpec((tm, SUBLANES), row),
                   pl.BlockSpec((tm, SUBLANES), row),
                   pl.BlockSpec((SUBLANES, LANES), lambda i: (0, 0)),
                   pl.BlockSpec(memory_space=pl.ANY)),
        out_shape=(jax.ShapeDtypeStruct((n, d), F32),
                   jax.ShapeDtypeStruct((n, SUBLANES), I32), jax.ShapeDtypeStruct((n, SUBLANES), F32),
                   jax.ShapeDtypeStruct((n, SUBLANES), I32),
                   jax.ShapeDtypeStruct((SUBLANES, LANES), I32),
                   jax.ShapeDtypeStruct((n, SUBLANES, LANES), I32)),
        scratch_shapes=[pltpu.VMEM((1, LANES), F32), pltpu.VMEM((2, tm, d // 2), I32),
                        pltpu.SemaphoreType.DMA((2,))],
        compiler_params=_params(1),
        name="outproj_router",
    )(conv, ml, x2d, mod, mod, mod, g2, w_out, w_router, b_router)


def _sc_workers():
    info = plsc.get_sparse_core_info()
    return info.num_cores, info.num_cores * info.num_subcores


def _sc_dispatch(h_rows, dest_chunks, n_slots):
    n_tok = h_rows.shape[0]
    n_cores, n_workers = _sc_workers()
    per_worker = n_tok // (n_workers * SC_CHUNK)
    assert per_worker * n_workers * SC_CHUNK == n_tok
    mesh = plsc.VectorSubcoreMesh(core_axis_name="c", subcore_axis_name="s")

    @functools.partial(
        pl.kernel, mesh=mesh,
        out_type=jax.ShapeDtypeStruct((n_slots,) + h_rows.shape[1:], h_rows.dtype),
        scratch_types=[pltpu.VMEM((TOP_K, SC_CHUNK), I32),
                       pltpu.VMEM((SC_CHUNK,) + h_rows.shape[1:], h_rows.dtype)],
    )
    def dispatch(h_hbm, dest_hbm, out_hbm, idx_v, rows_v):
        wid = lax.axis_index("s") * n_cores + lax.axis_index("c")

        @pl.loop(0, per_worker)
        def _(i):
            chunk = wid * per_worker + i
            pltpu.sync_copy(dest_hbm.at[chunk], idx_v)
            pltpu.sync_copy(h_hbm.at[pl.ds(chunk * SC_CHUNK, SC_CHUNK)], rows_v)
            for k in range(TOP_K):
                pltpu.sync_copy(rows_v, out_hbm.at[idx_v.at[k]])

    return dispatch(h_rows, dest_chunks)


def _sc_combine(y_sorted, dest_chunks, n_tok):
    n_cores, n_workers = _sc_workers()
    per_worker = n_tok // (n_workers * SC_CHUNK)
    mesh = plsc.VectorSubcoreMesh(core_axis_name="c", subcore_axis_name="s")

    @functools.partial(
        pl.kernel, mesh=mesh,
        out_type=jax.ShapeDtypeStruct((TOP_K, n_tok) + y_sorted.shape[1:], y_sorted.dtype),
        scratch_types=[pltpu.VMEM((TOP_K, SC_CHUNK), I32),
                       pltpu.VMEM((SC_CHUNK,) + y_sorted.shape[1:], y_sorted.dtype)],
    )
    def combine(y_hbm, dest_hbm, out_hbm, idx_v, rows_v):
        wid = lax.axis_index("s") * n_cores + lax.axis_index("c")

        @pl.loop(0, per_worker)
        def _(i):
            chunk = wid * per_worker + i
            pltpu.sync_copy(dest_hbm.at[chunk], idx_v)
            for k in range(TOP_K):
                pltpu.sync_copy(y_hbm.at[idx_v.at[k]], rows_v)
                pltpu.sync_copy(rows_v, out_hbm.at[k, pl.ds(chunk * SC_CHUNK, SC_CHUNK)])

    return combine(y_sorted, dest_chunks)


def _moe_body(ord_ref, order_ref, glo_ref, ghi_ref, tot_ref, nb_ref,
              x_hbm, wg_hbm, wu_hbm, wd_hbm, y_hbm,
              wgu, wd, stage_a, stage_d, xw, yw, wsem, xsem, ysem, *, d_expert):
    b = pl.program_id(0)
    nb = nb_ref[0]
    total = tot_ref[0]
    d_model = wgu.shape[1]
    rows_a = d_model // WEIGHT_PARTS
    rows_d = d_expert // WEIGHT_PARTS

    def part_copies(g):
        e = order_ref[lax.shift_right_logical(g, PART_SHIFT)]
        i = g & (WEIGHT_PARTS - 1)
        s = lax.rem(g, WEIGHT_RING)
        return (pltpu.make_async_copy(wg_hbm.at[e, pl.ds(i * rows_a, rows_a)], stage_a.at[s, 0],
                                      wsem.at[s, 0]),
                pltpu.make_async_copy(wu_hbm.at[e, pl.ds(i * rows_a, rows_a)], stage_a.at[s, 1],
                                      wsem.at[s, 1]),
                pltpu.make_async_copy(wd_hbm.at[e, pl.ds(i * rows_d, rows_d)], stage_d.at[s],
                                      wsem.at[s, 2]))

    def start_part(g):
        for cp in part_copies(g):
            cp.start()

    def finish_part(g):
        for cp in part_copies(g):
            cp.wait()
        i = g & (WEIGHT_PARTS - 1)
        s = lax.rem(g, WEIGHT_RING)
        par = lax.shift_right_logical(g, PART_SHIFT) & 1
        ra = pl.multiple_of(i * rows_a, rows_a)
        rd = pl.multiple_of(i * rows_d, rows_d)
        wgu[par, pl.ds(ra, rows_a), 0:d_expert] = stage_a[s, 0].astype(BF16)
        wgu[par, pl.ds(ra, rows_a), d_expert:2 * d_expert] = stage_a[s, 1].astype(BF16)
        wd[par, pl.ds(rd, rows_d), :] = stage_d[s].astype(BF16)

    def cast_parts(lo, hi):
        def body(g, carry):
            finish_part(g)

            @pl.when(g + WEIGHT_RING < total)
            def _():
                start_part(g + WEIGHT_RING)
            return carry
        lax.fori_loop(lo, hi, body, 0)

    slot = b % 2

    def x_in(blk, s):
        return _row_tile_copies(x_hbm, blk * MOE_BLOCK, xw.at[s], xsem.at[s], to_hbm=False)

    def y_out(blk, s):
        return _row_tile_copies(y_hbm, blk * MOE_BLOCK, yw.at[s], ysem.at[s], to_hbm=True)

    @pl.when(b == 0)
    def _():
        _start_all(x_in(0, 0))
        for g in range(WEIGHT_RING):
            start_part(g)
        cast_parts(0, WEIGHT_PARTS)

    @pl.when(b + 1 < nb)
    def _():
        _start_all(x_in(b + 1, 1 - slot))

    @pl.when(b < nb)
    def _():
        par = ord_ref[b] & 1
        _wait_all(x_in(b, slot))

        @pl.when(b >= 2)
        def _():
            _wait_all(y_out(b - 2, slot))

        x = jnp.concatenate(_unpack_words(xw[slot]), axis=1).astype(BF16)
        gu = jnp.dot(x, wgu[par], preferred_element_type=F32)
        hb = (_silu(gu[:, 0:d_expert]) * gu[:, d_expert:2 * d_expert]).astype(BF16)
        yw[slot] = _pack_words(jnp.dot(hb, wd[par], preferred_element_type=F32))
        _start_all(y_out(b, slot))
        cast_parts(glo_ref[b], ghi_ref[b])

        @pl.when(b == nb - 1)
        def _():
            @pl.when(b >= 1)
            def _():
                _wait_all(y_out(b - 1, 1 - slot))
            _wait_all(y_out(b, slot))


def _moe(x_sorted, we_gate, we_up, we_down, tables):
    d, d_expert = we_gate.shape[1], we_gate.shape[2]
    nb_max = x_sorted.shape[0] // MOE_BLOCK
    any_spec = pl.BlockSpec(memory_space=pl.ANY)
    grid_spec = pltpu.PrefetchScalarGridSpec(
        num_scalar_prefetch=len(tables),
        grid=(nb_max,),
        in_specs=[any_spec, any_spec, any_spec, any_spec],
        out_specs=any_spec,
        scratch_shapes=[pltpu.VMEM((2, d, 2 * d_expert), BF16),
                        pltpu.VMEM((2, d_expert, d), BF16),
                        pltpu.VMEM((WEIGHT_RING, 2, d // WEIGHT_PARTS, d_expert), F32),
                        pltpu.VMEM((WEIGHT_RING, d_expert // WEIGHT_PARTS, d), F32),
                        pltpu.VMEM((2, MOE_BLOCK, d // 2), I32),
                        pltpu.VMEM((2, MOE_BLOCK, d // 2), I32),
                        pltpu.SemaphoreType.DMA((WEIGHT_RING, 3)),
                        pltpu.SemaphoreType.DMA((2,)),
                        pltpu.SemaphoreType.DMA((2,))],
    )
    return pl.pallas_call(
        functools.partial(_moe_body, d_expert=d_expert),
        grid_spec=grid_spec,
        out_shape=jax.ShapeDtypeStruct(x_sorted.shape, x_sorted.dtype),
        compiler_params=pltpu.CompilerParams(
            dimension_semantics=("arbitrary",), vmem_limit_bytes=MOE_VMEM_LIMIT),
        name="moe_routed",
    )(*tables, x_sorted, we_gate, we_up, we_down)


def _final_body(x1_ref, gate_ref, gt2_ref, wsgu_ref, wsd_ref, fg_ref, h_hbm, y_hbm, out_ref,
                hw, yw, sem, *, d_shared):
    tm = x1_ref.shape[0]
    step = pl.program_id(0)
    slot = step % 2

    def rows_in(i, s):
        copies = _row_tile_copies(h_hbm, i * tm, hw.at[s], sem.at[s], to_hbm=False)
        for k in range(TOP_K):
            copies += _row_tile_copies(y_hbm.at[k], i * tm, yw.at[s, k], sem.at[s], to_hbm=False)
        return copies

    @pl.when(step == 0)
    def _():
        _start_all(rows_in(0, 0))

    @pl.when(step + 1 < pl.num_programs(0))
    def _():
        _start_all(rows_in(step + 1, 1 - slot))

    _wait_all(rows_in(step, slot))
    h = jnp.concatenate(_unpack_words(hw[slot]), axis=1).astype(BF16)
    gu = jnp.dot(h, wsgu_ref[...], preferred_element_type=F32)
    hb = (_silu(gu[:, 0:d_shared]) * gu[:, d_shared:2 * d_shared]).astype(BF16)
    acc = jnp.dot(hb, wsd_ref[...], preferred_element_type=F32)
    for k in range(TOP_K):
        acc = acc + gate_ref[:, k:k + 1] * jnp.concatenate(_unpack_words(yw[slot, k]), axis=1)
    x2 = x1_ref[...] + gt2_ref[0] * acc
    out_ref[...] = x2 * lax.rsqrt(jnp.mean(x2 * x2, axis=-1, keepdims=True) + EPS) * fg_ref[...]


def _final(h_rows, x1, y_tok, gates, mod, ws_gu, ws_d, final_g, rows_per_batch):
    n, d = x1.shape
    tm = ROW_TILE
    tiles_per_batch = rows_per_batch // tm
    row = lambda i: (i, 0)
    any_spec = pl.BlockSpec(memory_space=pl.ANY)
    return pl.pallas_call(
        functools.partial(_final_body, d_shared=ws_d.shape[0]),
        grid=(n // tm,),
        in_specs=[pl.BlockSpec((tm, d), row), pl.BlockSpec((tm, SUBLANES), row),
                  pl.BlockSpec((1, 1, d), _mod_spec(5, tiles_per_batch)),
                  _resident(ws_gu.shape), _resident(ws_d.shape), _resident(final_g.shape),
                  any_spec, any_spec],
        out_specs=pl.BlockSpec((tm, d), row),
        out_shape=jax.ShapeDtypeStruct((n, d), F32),
        scratch_shapes=[pltpu.VMEM((2, tm, d // 2), I32), pltpu.VMEM((2, TOP_K, tm, d // 2), I32),
                        pltpu.SemaphoreType.DMA((2,))],
        compiler_params=_params(1),
        name="shared_combine_final",
    )(x1, gates, mod, ws_gu, ws_d, final_g, h_rows, y_tok)


def _routing_tables(idx, rank, counts, n_tok):
    nb_max = -(-(n_tok * TOP_K) // MOE_BLOCK) + N_EXPERTS
    nblk = (counts + MOE_BLOCK - 1) // MOE_BLOCK
    blk_end = jnp.cumsum(nblk)
    blk_start = blk_end - nblk
    dest = (blk_start * MOE_BLOCK)[idx] + rank
    dest_chunks = dest.reshape(n_tok // SC_CHUNK, SC_CHUNK, TOP_K).transpose(0, 2, 1)

    blocks = jnp.arange(nb_max, dtype=I32)
    block_e = jnp.minimum(jnp.searchsorted(blk_end, blocks, side="right"), N_EXPERTS - 1).astype(I32)
    nonempty = nblk > 0
    n_visited = jnp.sum(nonempty.astype(I32))
    ordinal_of = jnp.cumsum(nonempty.astype(I32)) - 1
    order = jnp.argsort(jnp.where(nonempty, 0, 1), stable=True).astype(I32)
    ordinal = ordinal_of[block_e]
    k_in_e = blocks - blk_start[block_e]
    nb_e = jnp.maximum(nblk[block_e], 1)
    live = (ordinal + 1 < n_visited) & (blocks < blk_end[-1])
    first = WEIGHT_PARTS * (ordinal + 1)
    lo = jnp.where(live, first + WEIGHT_PARTS * k_in_e // nb_e, 0)
    hi = jnp.where(live, first + WEIGHT_PARTS * (k_in_e + 1) // nb_e, 0)
    tables = (ordinal.astype(I32), order, lo.astype(I32), hi.astype(I32),
              (WEIGHT_PARTS * n_visited).reshape(1).astype(I32), blk_end[-1:].astype(I32))
    return tables, dest_chunks, nb_max * MOE_BLOCK


def kernel(x, c, ctx, c_ctx, norm1_g, norm2_g, w_ada, b_ada, w_in, conv_w, gate_b, head_g, w_out,
           w_router, b_router, we_gate, we_up, we_down, ws_gate, ws_up, ws_down, final_g):
    assert w_ada.shape[0] == 1, "single-layer block"
    bsz, seq, d = x.shape
    ctx_len = ctx.shape[1]
    n_tok = bsz * seq
    conv_dim = conv_w.shape[2]
    v_all = head_g.shape[1]
    qk_all = (w_in.shape[2] - 3 * conv_dim - 2 * v_all - N_GATES) // 2
    assert seq % ROW_TILE == 0 and ctx_len % ROW_TILE == 0 and ROW_TILE % GRID_W == 0
    assert bsz + 1 <= SUBLANES

    cc = jnp.zeros((SUBLANES, d), F32).at[:bsz].set(c).at[bsz].set(c_ctx)
    mod = _adaln(cc, w_ada[0], b_ada).reshape(SUBLANES * 6, 1, d)

    n_main = 3 * conv_dim + 2 * qk_all + 2 * v_all
    w_main = w_in[0, :, :n_main].astype(BF16)
    k_lo = 3 * conv_dim + qk_all
    w_kt = w_in[0, :, k_lo:k_lo + qk_all].T.astype(BF16)
    w_v = w_in[0, :, k_lo + qk_all:k_lo + qk_all + v_all].astype(BF16)
    w_gate = jnp.zeros((d, LANES), BF16).at[:, :N_GATES].set(w_in[0, :, n_main:].astype(BF16))
    gate_bias = jnp.zeros((1, LANES), F32).at[0, :N_GATES].set(gate_b[0].reshape(-1))

    x2d = x.reshape(n_tok, d)
    conv, q, kt, v, og, g, gt = _inproj(x2d, mod, norm1_g, w_main, w_kt, w_gate, gate_bias, conv_w[0],
                                       seq, conv_dim, qk_all, v_all)
    ktc, vc, _, gtc = _inproj_ctx(ctx.reshape(bsz * ctx_len, d), mod, norm1_g, w_v, w_kt, w_gate,
                                  gate_bias, bsz)

    h_bwd = _mlstm(1, q, kt, v, g, gt, ktc, vc, gtc, None, bsz)
    ml = _mlstm(0, q, kt, v, g, gt, ktc, vc, gtc, (h_bwd, og), bsz, head_g)

    assert 2 * N_EXPERTS == LANES
    w_r_hi = w_router[0].astype(BF16)
    w_r = jnp.concatenate([w_r_hi, (w_router[0] - w_r_hi.astype(F32)).astype(BF16)], axis=1)
    b_r = jnp.zeros((1, LANES), F32).at[0, :N_EXPERTS].set(b_router[0])
    x1, idx, gates, rank, cnt, h_rows = _outproj(conv, ml, x2d, mod, norm2_g, w_out[0].astype(BF16),
                                             w_r, b_r, seq)

    tables, dest_chunks, n_slots = _routing_tables(idx[:, :TOP_K], rank[:, :TOP_K], cnt[0, :N_EXPERTS],
                                                   n_tok)
    x_sorted = _sc_dispatch(h_rows, dest_chunks, n_slots)
    y_sorted = _moe(x_sorted, we_gate[0], we_up[0], we_down[0], tables)
    y_tok = _sc_combine(y_sorted, dest_chunks, n_tok)

    ws_gu = jnp.concatenate([ws_gate[0], ws_up[0]], axis=1).astype(BF16)
    out = _final(h_rows, x1, y_tok, gates, mod, ws_gu, ws_down[0].astype(BF16),
                 final_g.reshape(1, d), seq)
    return out.reshape(bsz, seq, d)
```

```python
import functools

import jax
import jax.numpy as jnp
from jax import lax
from jax.experimental import pallas as pl
from jax.experimental.pallas import tpu as pltpu
from jax.experimental.pallas import tpu_sc as plsc

F32 = jnp.float32
BF16 = jnp.bfloat16
I32 = jnp.int32

N_HEADS = 4
GRID_W = 64
CHUNK = 128
TOP_K = 6
N_EXPERTS = 64
ROUTED_SCALE = 2.446
EPS = 1e-6
N_GATES = 4 * N_HEADS

LANES = 128
SUBLANES = 8
MOE_BLOCK = 256
ROW_TILE = 256
ADALN_TILE = 1024
WEIGHT_PARTS = 8
PART_SHIFT = 3
WEIGHT_RING = 3
SC_CHUNK = 64
HIGH_HALF = -65536
VMEM_LIMIT = 56 * 1024 * 1024
MOE_VMEM_LIMIT = 62 * 1024 * 1024

_HIGHEST = lax.Precision.HIGHEST
_NEG_INF = float("-inf")


def _resident(shape):
    nd = len(shape)
    return pl.BlockSpec(shape, lambda *_: (0,) * nd, pipeline_mode=pl.Buffered(1))


def _params(n_axes):
    return pltpu.CompilerParams(
        dimension_semantics=("arbitrary",) * n_axes, vmem_limit_bytes=VMEM_LIMIT)


def _log_sigmoid(x):
    return jnp.minimum(x, 0.0) - jnp.log1p(jnp.exp(-jnp.abs(x)))


def _silu(x):
    return x * jax.nn.sigmoid(x)


def _pack_words(val):
    half = val.shape[1] // 2
    lo = lax.bitcast_convert_type(val[:, :half].astype(BF16).astype(F32), I32)
    hi = lax.bitcast_convert_type(val[:, half:].astype(BF16).astype(F32), I32)
    return (hi & HIGH_HALF) | lax.shift_right_logical(lo, 16)


def _unpack_words(word):
    lo = lax.bitcast_convert_type(lax.shift_left(word, 16), F32)
    hi = lax.bitcast_convert_type(word & HIGH_HALF, F32)
    return lo, hi


def _row_tile_copies(hbm_rows, row0, tile, sem, to_hbm):
    n = tile.shape[0]
    copies = []
    for c in range(SUBLANES):
        hbm = hbm_rows.at[pl.ds(row0, n), c, :]
        vmem = tile.at[:, pl.ds(c * LANES, LANES)]
        copies.append(pltpu.make_async_copy(vmem, hbm, sem) if to_hbm
                      else pltpu.make_async_copy(hbm, vmem, sem))
    return copies


def _start_all(copies):
    for cp in copies:
        cp.start()


def _wait_all(copies):
    for cp in copies:
        cp.wait()


def _adaln_body(c_ref, w_ref, b_ref, o_ref):
    s = _silu(c_ref[...])
    o_ref[...] = jnp.dot(s.astype(BF16), w_ref[...].astype(BF16),
                         preferred_element_type=F32) + b_ref[...]


def _adaln(cc, w, b):
    d, n6 = w.shape
    return pl.pallas_call(
        _adaln_body,
        grid=(n6 // ADALN_TILE,),
        in_specs=[pl.BlockSpec((SUBLANES, d), lambda j: (0, 0)),
                  pl.BlockSpec((d, ADALN_TILE), lambda j: (0, j)),
                  pl.BlockSpec((1, ADALN_TILE), lambda j: (0, j))],
        out_specs=pl.BlockSpec((SUBLANES, ADALN_TILE), lambda j: (0, j)),
        out_shape=jax.ShapeDtypeStruct((SUBLANES, n6), F32),
        compiler_params=_params(1),
        name="adaln",
    )(cc, w, b)


def _norm_mod(x, g, shift, scale):
    y = x * lax.rsqrt(jnp.mean(x * x, axis=-1, keepdims=True) + EPS) * g
    return y * (1.0 + scale) + shift


def _gate_prep(xb, wg_ref, gb_ref, g_ref, gt_ref):
    tm = xb.shape[0]
    gg = jnp.dot(xb, wg_ref[...], preferred_element_type=F32) + gb_ref[...]
    lane = lax.broadcasted_iota(I32, (tm, LANES), 1)
    is_f = (lane & N_HEADS) != 0
    is_bwd = (lane & (2 * N_HEADS)) != 0
    lf = jnp.where(is_f, _log_sigmoid(gg), 0.0)
    r = lax.broadcasted_iota(I32, (tm, tm), 0)
    c = lax.broadcasted_iota(I32, (tm, tm), 1)
    same = (r // CHUNK) == (c // CHUNK)
    tri_l = jnp.where(same & (c <= r), 1.0, 0.0).astype(F32)
    tri_u = jnp.where(same & (c >= r), 1.0, 0.0).astype(F32)
    pre = jnp.dot(tri_l, lf, precision=_HIGHEST, preferred_element_type=F32)
    suf = jnp.dot(tri_u, lf, precision=_HIGHEST, preferred_element_type=F32)
    out = jnp.where(is_f, jnp.where(is_bwd, suf, pre), gg)
    g_ref[...] = out[:, :N_GATES]
    gt_ref[...] = out.T[:N_GATES, :]


def _project_transposed(wt_ref, xb):
    return lax.dot_general(wt_ref[...], xb, (((1,), (1,)), ((), ())),
                           preferred_element_type=F32).astype(BF16)


def _inproj_body(x_ref, sh_ref, sc_ref, g1_ref, w_ref, wkt_ref, wg_ref, gb_ref, cw_ref,
                 conv_ref, q_ref, k_ref, v_ref, o_ref, g_ref, gt_ref, *, conv_dim, qk_all, v_all):
    tm = x_ref.shape[0]
    xb = _norm_mod(x_ref[...], g1_ref[...], sh_ref[0], sc_ref[0]).astype(BF16)

    def proj(lo, width):
        return jnp.dot(xb, w_ref[:, lo:lo + width], preferred_element_type=F32)

    u = proj(conv_dim, conv_dim) * proj(2 * conv_dim, conv_dim)
    pos = lax.broadcasted_iota(I32, (tm, 1), 0) % GRID_W
    um = jnp.where(pos == 0, 0.0, pltpu.roll(u, 1, axis=0))
    up = jnp.where(pos == GRID_W - 1, 0.0, pltpu.roll(u, tm - 1, axis=0))
    y = um * cw_ref[0:1, :] + u * cw_ref[1:2, :] + up * cw_ref[2:3, :]
    conv_ref[...] = (proj(0, conv_dim) * y).astype(BF16)

    off = 3 * conv_dim
    qscale = (qk_all // N_HEADS) ** -0.5
    q_ref[...] = (proj(off, qk_all) * qscale).astype(BF16)
    k_ref[...] = _project_transposed(wkt_ref, xb)
    v_ref[...] = proj(off + 2 * qk_all, v_all).astype(BF16)
    o_ref[...] = jax.nn.sigmoid(proj(off + 2 * qk_all + v_all, v_all)).astype(BF16)
    _gate_prep(xb, wg_ref, gb_ref, g_ref, gt_ref)


def _inproj_ctx_body(x_ref, sh_ref, sc_ref, g1_ref, w_ref, wkt_ref, wg_ref, gb_ref,
                     k_ref, v_ref, g_ref, gt_ref):
    xb = _norm_mod(x_ref[...], g1_ref[...], sh_ref[0], sc_ref[0]).astype(BF16)
    k_ref[...] = _project_transposed(wkt_ref, xb)
    v_ref[...] = jnp.dot(xb, w_ref[...], preferred_element_type=F32).astype(BF16)
    _gate_prep(xb, wg_ref, gb_ref, g_ref, gt_ref)


def _mod_spec(part, tiles_per_row, fixed_row=None):
    def index(i):
        row = fixed_row if fixed_row is not None else i // tiles_per_row
        return (row * 6 + part, 0, 0)

    return index


def _inproj(x2d, mod, g1, w_main, w_kt, w_gate, gate_b, conv_w, rows_per_batch, conv_dim, qk_all, v_all):
    n, d = x2d.shape
    tm = ROW_TILE
    tiles_per_batch = rows_per_batch // tm
    row = lambda i: (i, 0)
    mod_block = (1, 1, d)
    out_shapes = (
        jax.ShapeDtypeStruct((n, conv_dim), BF16),
        jax.ShapeDtypeStruct((n, qk_all), BF16),
        jax.ShapeDtypeStruct((qk_all, n), BF16),
        jax.ShapeDtypeStruct((n, v_all), BF16),
        jax.ShapeDtypeStruct((n, v_all), BF16),
        jax.ShapeDtypeStruct((n, N_GATES), F32),
        jax.ShapeDtypeStruct((N_GATES, n), F32),
    )
    out_specs = (
        pl.BlockSpec((tm, conv_dim), row),
        pl.BlockSpec((tm, qk_all), row),
        pl.BlockSpec((qk_all, tm), lambda i: (0, i)),
        pl.BlockSpec((tm, v_all), row),
        pl.BlockSpec((tm, v_all), row),
        pl.BlockSpec((tm, N_GATES), row),
        pl.BlockSpec((N_GATES, tm), lambda i: (0, i)),
    )
    return pl.pallas_call(
        functools.partial(_inproj_body, conv_dim=conv_dim, qk_all=qk_all, v_all=v_all),
        grid=(n // tm,),
        in_specs=[pl.BlockSpec((tm, d), row),
                  pl.BlockSpec(mod_block, _mod_spec(0, tiles_per_batch)),
                  pl.BlockSpec(mod_block, _mod_spec(1, tiles_per_batch)),
                  _resident(g1.shape), _resident(w_main.shape), _resident(w_kt.shape),
                  _resident(w_gate.shape), _resident(gate_b.shape), _resident(conv_w.shape)],
        out_specs=out_specs,
        out_shape=out_shapes,
        compiler_params=_params(1),
        name="inproj",
    )(x2d, mod, mod, g1, w_main, w_kt, w_gate, gate_b, conv_w)


def _inproj_ctx(c2d, mod, g1, w_v, w_kt, w_gate, gate_b, ctx_mod_row):
    n, d = c2d.shape
    tm = ROW_TILE
    row = lambda i: (i, 0)
    mod_block = (1, 1, d)
    qk_all, v_all = w_kt.shape[0], w_v.shape[1]
    return pl.pallas_call(
        _inproj_ctx_body,
        grid=(n // tm,),
        in_specs=[pl.BlockSpec((tm, d), row),
                  pl.BlockSpec(mod_block, _mod_spec(0, 1, ctx_mod_row)),
                  pl.BlockSpec(mod_block, _mod_spec(1, 1, ctx_mod_row)),
                  _resident(g1.shape), _resident(w_v.shape), _resident(w_kt.shape),
                  _resident(w_gate.shape), _resident(gate_b.shape)],
        out_specs=(pl.BlockSpec((qk_all, tm), lambda i: (0, i)), pl.BlockSpec((tm, v_all), row),
                   pl.BlockSpec((tm, N_GATES), row), pl.BlockSpec((N_GATES, tm), lambda i: (0, i))),
        out_shape=(jax.ShapeDtypeStruct((qk_all, n), BF16), jax.ShapeDtypeStruct((n, v_all), BF16),
                   jax.ShapeDtypeStruct((n, N_GATES), F32), jax.ShapeDtypeStruct((N_GATES, n), F32)),
        compiler_params=_params(1),
        name="inproj_ctx",
    )(c2d, mod, mod, g1, w_v, w_kt, w_gate, gate_b)


def _with_ones(v):
    return jnp.concatenate([v, jnp.ones((v.shape[0], LANES), v.dtype)], axis=1)


def _mlstm_state_update(h, direction, kt_ref, v_ref, gt_ref, s_ref, m_ref, qk, vh):
    ci = direction * 2 * N_HEADS + h
    cb = ci + N_HEADS
    last = 0 if direction else CHUNK - 1
    kt = kt_ref[h * qk:(h + 1) * qk, :].astype(F32)
    va = _with_ones(v_ref[:, h * vh:(h + 1) * vh])
    b_last = gt_ref[cb:cb + 1, last:last + 1]
    m_prev = m_ref[h][0:1, 0:1]
    g_r = b_last - gt_ref[cb:cb + 1, :] + gt_ref[ci:ci + 1, :]
    m_new = jnp.maximum(b_last + m_prev, jnp.max(g_r, axis=1, keepdims=True))
    a = jnp.exp(b_last + m_prev - m_new)
    kw = (kt * jnp.exp(g_r - m_new)).astype(BF16)
    s_ref[h] = a * s_ref[h] + jnp.dot(kw, va, preferred_element_type=F32)
    m_ref[h] = jnp.broadcast_to(m_new, m_ref.shape[1:])


def _mlstm_head_output(h, direction, q_ref, kt_ref, v_ref, g_ref, gt_ref, s_ref, m_ref, qk, vh):
    ci = direction * 2 * N_HEADS + h
    cb = ci + N_HEADS
    q = q_ref[:, h * qk:(h + 1) * qk]
    kt = kt_ref[h * qk:(h + 1) * qk, :]
    va = _with_ones(v_ref[:, h * vh:(h + 1) * vh])
    ig_r = gt_ref[ci:ci + 1, :]
    b_r = gt_ref[cb:cb + 1, :]
    b_c = g_ref[:, cb:cb + 1]
    m_prev = m_ref[h][0:1, 0:1]
    row = lax.broadcasted_iota(I32, (CHUNK, CHUNK), 0)
    col = lax.broadcasted_iota(I32, (CHUNK, CHUNK), 1)
    mask = (col >= row) if direction else (col <= row)
    dm = jnp.where(mask, b_c + (ig_r - b_r), _NEG_INF)
    inter = b_c + m_prev
    m_t = jnp.maximum(inter, jnp.max(dm, axis=1, keepdims=True))
    w_inter = jnp.exp(inter - m_t)
    s = jnp.dot(q, kt, preferred_element_type=F32) * jnp.exp(dm - m_t)
    intra = jnp.dot(s.astype(BF16), va, preferred_element_type=F32)
    carried = jnp.dot(q, s_ref[h].astype(BF16), preferred_element_type=F32)
    num = intra[:, 0:vh] + w_inter * carried[:, 0:vh]
    den = intra[:, vh:vh + 1] + w_inter * carried[:, vh:vh + 1]
    return num / jnp.maximum(jnp.abs(den), jnp.exp(-m_t))


def _mlstm_body(*refs, direction, n_ctx_chunks, qk, vh):
    if direction:
        (q_ref, kt_ref, v_ref, g_ref, gt_ref, ktc_ref, vc_ref, gtc_ref,
         out_ref, s_ref, m_ref) = refs
    else:
        (q_ref, kt_ref, v_ref, g_ref, gt_ref, ktc_ref, vc_ref, gtc_ref,
         hb_ref, og_ref, hg_ref, out_ref, s_ref, m_ref) = refs
    step = pl.program_id(1)

    @pl.when(step == 0)
    def _():
        s_ref[...] = jnp.zeros_like(s_ref)
        m_ref[...] = jnp.full_like(m_ref, _NEG_INF)

    @pl.when(step < n_ctx_chunks)
    def _():
        for h in range(N_HEADS):
            _mlstm_state_update(h, direction, ktc_ref, vc_ref, gtc_ref, s_ref, m_ref, qk, vh)

    @pl.when(step >= n_ctx_chunks)
    def _():
        for h in range(N_HEADS):
            hh = _mlstm_head_output(h, direction, q_ref, kt_ref, v_ref, g_ref, gt_ref, s_ref, m_ref, qk, vh)
            cols = slice(h * vh, (h + 1) * vh)
            if direction:
                out_ref[:, cols] = hh
            else:
                hs = hh + hb_ref[:, cols]
                hs = hs * lax.rsqrt(jnp.mean(hs * hs, axis=-1, keepdims=True) + EPS)
                out_ref[:, cols] = (hs * hg_ref[:, cols] * og_ref[:, cols].astype(F32)).astype(BF16)
            _mlstm_state_update(h, direction, kt_ref, v_ref, gt_ref, s_ref, m_ref, qk, vh)


def _mlstm(direction, q, kt, v, g, gt, ktc, vc, gtc, extra, bsz, head_g=None):
    n, qk_all = q.shape
    v_all = v.shape[1]
    qk, vh = qk_all // N_HEADS, v_all // N_HEADS
    nc = n // bsz // CHUNK
    ncc = vc.shape[0] // bsz // CHUNK

    def lat(b, s):
        j = jnp.clip(s - ncc, 0, nc - 1)
        return b * nc + (nc - 1 - j if direction else j)

    def ctx(b, s):
        j = jnp.clip(s, 0, ncc - 1)
        return b * ncc + (ncc - 1 - j if direction else j)

    lat_row = lambda b, s: (lat(b, s), 0)
    lat_col = lambda b, s: (0, lat(b, s))
    ctx_row = lambda b, s: (ctx(b, s), 0)
    ctx_col = lambda b, s: (0, ctx(b, s))
    in_specs = [pl.BlockSpec((CHUNK, qk_all), lat_row), pl.BlockSpec((qk_all, CHUNK), lat_col),
                pl.BlockSpec((CHUNK, v_all), lat_row), pl.BlockSpec((CHUNK, N_GATES), lat_row),
                pl.BlockSpec((N_GATES, CHUNK), lat_col),
                pl.BlockSpec((qk_all, CHUNK), ctx_col), pl.BlockSpec((CHUNK, v_all), ctx_row),
                pl.BlockSpec((N_GATES, CHUNK), ctx_col)]
    args = [q, kt, v, g, gt, ktc, vc, gtc]
    if direction:
        out_dtype = F32
    else:
        hb, og = extra
        in_specs += [pl.BlockSpec((CHUNK, v_all), lat_row), pl.BlockSpec((CHUNK, v_all), lat_row),
                     pl.BlockSpec((1, v_all), lambda b, s: (0, 0))]
        args += [hb, og, head_g]
        out_dtype = BF16
    return pl.pallas_call(
        functools.partial(_mlstm_body, direction=direction, n_ctx_chunks=ncc, qk=qk, vh=vh),
        grid=(bsz, ncc + nc),
        in_specs=in_specs,
        out_specs=pl.BlockSpec((CHUNK, v_all), lat_row),
        out_shape=jax.ShapeDtypeStruct((n, v_all), out_dtype),
        scratch_shapes=[pltpu.VMEM((N_HEADS, qk, vh + LANES), F32),
                        pltpu.VMEM((N_HEADS, SUBLANES, LANES), F32)],
        compiler_params=_params(2),
        name="mlstm_bwd" if direction else "mlstm_fwd",
    )(*args)


def _outproj_body(conv_ref, ml_ref, x_ref, gt1_ref, sh2_ref, sc2_ref, g2_ref, wo_ref, wr_ref, br_ref,
                  x1_ref, idx_ref, gate_ref, rank_ref, cnt_ref, h_hbm, carry_ref, hw, hsem):
    tm = x_ref.shape[0]
    half = conv_ref.shape[1]
    step = pl.program_id(0)
    buf = step % 2

    def h_out(i, s):
        return _row_tile_copies(h_hbm, i * tm, hw.at[s], hsem.at[s], to_hbm=True)

    @pl.when(step == 0)
    def _():
        carry_ref[...] = jnp.zeros_like(carry_ref)

    @pl.when(step >= 2)
    def _():
        _wait_all(h_out(step - 2, buf))

    y = (jnp.dot(conv_ref[...], wo_ref[0:half, :], preferred_element_type=F32)
         + jnp.dot(ml_ref[...], wo_ref[half:2 * half, :], preferred_element_type=F32))
    x1 = x_ref[...] + gt1_ref[0] * y
    x1_ref[...] = x1
    hn = _norm_mod(x1, g2_ref[...], sh2_ref[0], sc2_ref[0])
    hw[buf] = _pack_words(hn)
    _start_all(h_out(step, buf))

    h_hi = hn.astype(BF16)
    h_lo = (hn - h_hi.astype(F32)).astype(BF16)
    parts = (jnp.dot(h_hi, wr_ref[...], preferred_element_type=F32)
             + jnp.dot(h_lo, wr_ref[...], preferred_element_type=F32))
    scores = jax.nn.sigmoid(parts + pltpu.roll(parts, N_EXPERTS, axis=1))
    lane = lax.broadcasted_iota(I32, (tm, LANES), 1).astype(F32)
    biased = jnp.where(lane < N_EXPERTS, scores + br_ref[...], _NEG_INF)
    onehot = jnp.zeros((tm, LANES), F32)
    picks, sels = [], []
    for _ in range(TOP_K):
        mx = jnp.max(biased, axis=1, keepdims=True)
        pick = jnp.min(jnp.where(biased == mx, lane, float(LANES)), axis=1, keepdims=True)
        hit = lane == pick
        sels.append(jnp.sum(jnp.where(hit, scores, 0.0), axis=1, keepdims=True))
        picks.append(pick)
        biased = jnp.where(hit, _NEG_INF, biased)
        onehot = onehot + hit.astype(F32)
    total = sels[0]
    for s in sels[1:]:
        total = total + s

    r = lax.broadcasted_iota(I32, (tm, tm), 0)
    c = lax.broadcasted_iota(I32, (tm, tm), 1)
    strict = jnp.where(c < r, 1.0, 0.0).astype(BF16)
    before = jnp.dot(strict, onehot.astype(BF16), preferred_element_type=F32) + carry_ref[...]
    slot = lax.broadcasted_iota(I32, (tm, SUBLANES), 1)
    idx_out = jnp.zeros((tm, LANES), F32)
    rank_out = jnp.zeros((tm, LANES), F32)
    gate_out = jnp.zeros((tm, SUBLANES), F32)
    for j in range(TOP_K):
        rank = jnp.sum(jnp.where(lane == picks[j], before, 0.0), axis=1, keepdims=True)
        idx_out = jnp.where(lane == float(j), picks[j], idx_out)
        rank_out = jnp.where(lane == float(j), rank, rank_out)
        gate_out = jnp.where(slot == j, sels[j] / total * ROUTED_SCALE, gate_out)
    idx_ref[...] = idx_out.T[:SUBLANES, :].astype(I32)
    rank_ref[...] = rank_out.T[:SUBLANES, :].astype(I32)
    gate_ref[...] = gate_out
    carry_ref[...] = carry_ref[...] + jnp.sum(onehot, axis=0, keepdims=True)
    cnt_ref[...] = jnp.broadcast_to(carry_ref[...], cnt_ref.shape).astype(I32)

    @pl.when(step == pl.num_programs(0) - 1)
    def _():
        @pl.when(step >= 1)
        def _():
            _wait_all(h_out(step - 1, 1 - buf))
        _wait_all(h_out(step, buf))


def _outproj(conv, ml, x2d, mod, g2, w_out, w_router, b_router, rows_per_batch):
    n, d = x2d.shape
    tm = ROW_TILE
    tiles_per_batch = rows_per_batch // tm
    row = lambda i: (i, 0)
    mod_block = (1, 1, d)
    half = conv.shape[1]
    return pl.pallas_call(
        _outproj_body,
        grid=(n // tm,),
        in_specs=[pl.BlockSpec((tm, half), row), pl.BlockSpec((tm, half), row), pl.BlockSpec((tm, d), row),
                  pl.BlockSpec(mod_block, _mod_spec(2, tiles_per_batch)),
                  pl.BlockSpec(mod_block, _mod_spec(3, tiles_per_batch)),
                  pl.BlockSpec(mod_block, _mod_spec(4, tiles_per_batch)),
                  _resident(g2.shape), _resident(w_out.shape), _resident(w_router.shape),
                  _resident(b_router.shape)],
        out_specs=(pl.BlockSpec((tm, d), row),
                   pl.BlockSpec((SUBLANES, tm), lambda i: (0, i)), pl.BlockSpec((tm, SUBLANES), row),
                   pl.BlockSpec((SUBLANES, tm), lambda i: (0, i)),
                   pl.BlockSpec((SUBLANES, LANES), lambda i: (0, 0)),
                   pl.BlockSpec(memory_space=pl.ANY)),
        out_shape=(jax.ShapeDtypeStruct((n, d), F32),
                   jax.ShapeDtypeStruct((SUBLANES, n), I32), jax.ShapeDtypeStruct((n, SUBLANES), F32),
                   jax.ShapeDtypeStruct((SUBLANES, n), I32),
                   jax.ShapeDtypeStruct((SUBLANES, LANES), I32),
                   jax.ShapeDtypeStruct((n, SUBLANES, LANES), I32)),
        scratch_shapes=[pltpu.VMEM((1, LANES), F32), pltpu.VMEM((2, tm, d // 2), I32),
                        pltpu.SemaphoreType.DMA((2,))],
        compiler_params=_params(1),
        name="outproj_router",
    )(conv, ml, x2d, mod, mod, mod, g2, w_out, w_router, b_router)


def _sc_workers():
    info = plsc.get_sparse_core_info()
    return info.num_cores, info.num_cores * info.num_subcores


def _sc_dispatch(h_rows, dest_chunks, n_slots):
    n_tok = h_rows.shape[0]
    n_cores, n_workers = _sc_workers()
    per_worker = n_tok // (n_workers * SC_CHUNK)
    assert per_worker * n_workers * SC_CHUNK == n_tok
    mesh = plsc.VectorSubcoreMesh(core_axis_name="c", subcore_axis_name="s")

    @functools.partial(
        pl.kernel, mesh=mesh,
        out_type=jax.ShapeDtypeStruct((n_slots,) + h_rows.shape[1:], h_rows.dtype),
        scratch_types=[pltpu.VMEM((TOP_K, SC_CHUNK), I32),
                       pltpu.VMEM((SC_CHUNK,) + h_rows.shape[1:], h_rows.dtype)],
    )
    def dispatch(h_hbm, dest_hbm, out_hbm, idx_v, rows_v):
        wid = lax.axis_index("s") * n_cores + lax.axis_index("c")

        @pl.loop(0, per_worker)
        def _(i):
            chunk = wid * per_worker + i
            pltpu.sync_copy(dest_hbm.at[chunk], idx_v)
            pltpu.sync_copy(h_hbm.at[pl.ds(chunk * SC_CHUNK, SC_CHUNK)], rows_v)
            for k in range(TOP_K):
                pltpu.sync_copy(rows_v, out_hbm.at[idx_v.at[k]])

    return dispatch(h_rows, dest_chunks)


def _sc_combine(y_sorted, dest_chunks, n_tok):
    n_cores, n_workers = _sc_workers()
    per_worker = n_tok // (n_workers * SC_CHUNK)
    mesh = plsc.VectorSubcoreMesh(core_axis_name="c", subcore_axis_name="s")

    @functools.partial(
        pl.kernel, mesh=mesh,
        out_type=jax.ShapeDtypeStruct((TOP_K, n_tok) + y_sorted.shape[1:], y_sorted.dtype),
        scratch_types=[pltpu.VMEM((TOP_K, SC_CHUNK), I32),
                       pltpu.VMEM((SC_CHUNK,) + y_sorted.shape[1:], y_sorted.dtype)],
    )
    def combine(y_hbm, dest_hbm, out_hbm, idx_v, rows_v):
        wid = lax.axis_index("s") * n_cores + lax.axis_index("c")

        @pl.loop(0, per_worker)
        def _(i):
            chunk = wid * per_worker + i
            pltpu.sync_copy(dest_hbm.at[chunk], idx_v)
            for k in range(TOP_K):
                pltpu.sync_copy(y_hbm.at[idx_v.at[k]], rows_v)
                pltpu.sync_copy(rows_v, out_hbm.at[k, pl.ds(chunk * SC_CHUNK, SC_CHUNK)])

    return combine(y_sorted, dest_chunks)


def _moe_body(ord_ref, order_ref, glo_ref, ghi_ref, tot_ref, nb_ref,
              x_hbm, wg_hbm, wu_hbm, wd_hbm, y_hbm,
              wgu, wd, stage_a, stage_d, xw, yw, wsem, xsem, ysem, *, d_expert):
    b = pl.program_id(0)
    nb = nb_ref[0]
    total = tot_ref[0]
    d_model = wgu.shape[1]
    rows_a = d_model // WEIGHT_PARTS
    rows_d = d_expert // WEIGHT_PARTS

    def part_copies(g):
        e = order_ref[lax.shift_right_logical(g, PART_SHIFT)]
        i = g & (WEIGHT_PARTS - 1)
        s = lax.rem(g, WEIGHT_RING)
        return (pltpu.make_async_copy(wg_hbm.at[e, pl.ds(i * rows_a, rows_a)], stage_a.at[s, 0],
                                      wsem.at[s, 0]),
                pltpu.make_async_copy(wu_hbm.at[e, pl.ds(i * rows_a, rows_a)], stage_a.at[s, 1],
                                      wsem.at[s, 1]),
                pltpu.make_async_copy(wd_hbm.at[e, pl.ds(i * rows_d, rows_d)], stage_d.at[s],
                                      wsem.at[s, 2]))

    def start_part(g):
        for cp in part_copies(g):
            cp.start()

    def finish_part(g):
        for cp in part_copies(g):
            cp.wait()
        i = g & (WEIGHT_PARTS - 1)
        s = lax.rem(g, WEIGHT_RING)
        par = lax.shift_right_logical(g, PART_SHIFT) & 1
        ra = pl.multiple_of(i * rows_a, rows_a)
        rd = pl.multiple_of(i * rows_d, rows_d)
        wgu[par, pl.ds(ra, rows_a), 0:d_expert] = stage_a[s, 0].astype(BF16)
        wgu[par, pl.ds(ra, rows_a), d_expert:2 * d_expert] = stage_a[s, 1].astype(BF16)
        wd[par, pl.ds(rd, rows_d), :] = stage_d[s].astype(BF16)

    def cast_parts(lo, hi):
        def body(g, carry):
            finish_part(g)

            @pl.when(g + WEIGHT_RING < total)
            def _():
                start_part(g + WEIGHT_RING)
            return carry
        lax.fori_loop(lo, hi, body, 0)

    slot = b % 2

    def x_in(blk, s):
        return _row_tile_copies(x_hbm, blk * MOE_BLOCK, xw.at[s], xsem.at[s], to_hbm=False)

    def y_out(blk, s):
        return _row_tile_copies(y_hbm, blk * MOE_BLOCK, yw.at[s], ysem.at[s], to_hbm=True)

    @pl.when(b == 0)
    def _():
        _start_all(x_in(0, 0))
        for g in range(WEIGHT_RING):
            start_part(g)
        cast_parts(0, WEIGHT_PARTS)

    @pl.when(b + 1 < nb)
    def _():
        _start_all(x_in(b + 1, 1 - slot))

    @pl.when(b < nb)
    def _():
        par = ord_ref[b] & 1
        _wait_all(x_in(b, slot))

        @pl.when(b >= 2)
        def _():
            _wait_all(y_out(b - 2, slot))

        x = jnp.concatenate(_unpack_words(xw[slot]), axis=1).astype(BF16)
        gu = jnp.dot(x, wgu[par], preferred_element_type=F32)
        hb = (_silu(gu[:, 0:d_expert]) * gu[:, d_expert:2 * d_expert]).astype(BF16)
        yw[slot] = _pack_words(jnp.dot(hb, wd[par], preferred_element_type=F32))
        _start_all(y_out(b, slot))
        cast_parts(glo_ref[b], ghi_ref[b])

        @pl.when(b == nb - 1)
        def _():
            @pl.when(b >= 1)
            def _():
                _wait_all(y_out(b - 1, 1 - slot))
            _wait_all(y_out(b, slot))


def _moe(x_sorted, we_gate, we_up, we_down, tables):
    d, d_expert = we_gate.shape[1], we_gate.shape[2]
    nb_max = x_sorted.shape[0] // MOE_BLOCK
    any_spec = pl.BlockSpec(memory_space=pl.ANY)
    grid_spec = pltpu.PrefetchScalarGridSpec(
        num_scalar_prefetch=len(tables),
        grid=(nb_max,),
        in_specs=[any_spec, any_spec, any_spec, any_spec],
        out_specs=any_spec,
        scratch_shapes=[pltpu.VMEM((2, d, 2 * d_expert), BF16),
                        pltpu.VMEM((2, d_expert, d), BF16),
                        pltpu.VMEM((WEIGHT_RING, 2, d // WEIGHT_PARTS, d_expert), F32),
                        pltpu.VMEM((WEIGHT_RING, d_expert // WEIGHT_PARTS, d), F32),
                        pltpu.VMEM((2, MOE_BLOCK, d // 2), I32),
                        pltpu.VMEM((2, MOE_BLOCK, d // 2), I32),
                        pltpu.SemaphoreType.DMA((WEIGHT_RING, 3)),
                        pltpu.SemaphoreType.DMA((2,)),
                        pltpu.SemaphoreType.DMA((2,))],
    )
    return pl.pallas_call(
        functools.partial(_moe_body, d_expert=d_expert),
        grid_spec=grid_spec,
        out_shape=jax.ShapeDtypeStruct(x_sorted.shape, x_sorted.dtype),
        compiler_params=pltpu.CompilerParams(
            dimension_semantics=("arbitrary",), vmem_limit_bytes=MOE_VMEM_LIMIT),
        name="moe_routed",
    )(*tables, x_sorted, we_gate, we_up, we_down)


def _shared_body(wsgu_ref, wsd_ref, h_hbm, *rest, d_shared, first_tile):
    out_ref, hw, sem = rest[-3:]
    tm = out_ref.shape[0]
    step = pl.program_id(0)
    slot = step % 2

    def rows_in(i, s):
        return _row_tile_copies(h_hbm, (first_tile + i) * tm, hw.at[s], sem.at[s], to_hbm=False)

    @pl.when(step == 0)
    def _():
        _start_all(rows_in(0, 0))

    @pl.when(step + 1 < pl.num_programs(0))
    def _():
        _start_all(rows_in(step + 1, 1 - slot))

    _wait_all(rows_in(step, slot))
    h = jnp.concatenate(_unpack_words(hw[slot]), axis=1).astype(BF16)
    gu = jnp.dot(h, wsgu_ref[...], preferred_element_type=F32)
    hb = (_silu(gu[:, 0:d_shared]) * gu[:, d_shared:2 * d_shared]).astype(BF16)
    out_ref[...] = _pack_words(jnp.dot(hb, wsd_ref[...], preferred_element_type=F32))


def _shared(h_rows, ws_gu, ws_d, prev, first_tile, n_tiles):
    n = h_rows.shape[0]
    d = ws_d.shape[1]
    tm = ROW_TILE
    any_spec = pl.BlockSpec(memory_space=pl.ANY)
    carried = () if prev is None else (prev,)
    return pl.pallas_call(
        functools.partial(_shared_body, d_shared=ws_d.shape[0], first_tile=first_tile),
        grid=(n_tiles,),
        in_specs=[_resident(ws_gu.shape), _resident(ws_d.shape), any_spec] + [any_spec] * len(carried),
        out_specs=pl.BlockSpec((tm, d // 2), lambda i: (first_tile + i, 0)),
        out_shape=jax.ShapeDtypeStruct((n, d // 2), I32),
        scratch_shapes=[pltpu.VMEM((2, tm, d // 2), I32), pltpu.SemaphoreType.DMA((2,))],
        input_output_aliases={3: 0} if carried else {},
        compiler_params=_params(1),
        name="shared_expert",
    )(ws_gu, ws_d, h_rows, *carried)


def _final_body(x1_ref, sh_ref, gate_ref, gt2_ref, fg_ref, y_hbm, out_ref, yw, sem):
    tm = x1_ref.shape[0]
    step = pl.program_id(0)
    slot = step % 2

    def rows_in(i, s):
        copies = []
        for k in range(TOP_K):
            copies += _row_tile_copies(y_hbm.at[k], i * tm, yw.at[s, k], sem.at[s], to_hbm=False)
        return copies

    @pl.when(step == 0)
    def _():
        _start_all(rows_in(0, 0))

    @pl.when(step + 1 < pl.num_programs(0))
    def _():
        _start_all(rows_in(step + 1, 1 - slot))

    _wait_all(rows_in(step, slot))
    acc = jnp.concatenate(_unpack_words(sh_ref[...]), axis=1)
    for k in range(TOP_K):
        acc = acc + gate_ref[:, k:k + 1] * jnp.concatenate(_unpack_words(yw[slot, k]), axis=1)
    x2 = x1_ref[...] + gt2_ref[0] * acc
    out_ref[...] = x2 * lax.rsqrt(jnp.mean(x2 * x2, axis=-1, keepdims=True) + EPS) * fg_ref[...]


def _final(x1, shared, y_tok, gates, mod, final_g, rows_per_batch):
    n, d = x1.shape
    tm = ROW_TILE
    tiles_per_batch = rows_per_batch // tm
    row = lambda i: (i, 0)
    return pl.pallas_call(
        _final_body,
        grid=(n // tm,),
        in_specs=[pl.BlockSpec((tm, d), row), pl.BlockSpec((tm, d // 2), row),
                  pl.BlockSpec((tm, SUBLANES), row),
                  pl.BlockSpec((1, 1, d), _mod_spec(5, tiles_per_batch)),
                  _resident(final_g.shape), pl.BlockSpec(memory_space=pl.ANY)],
        out_specs=pl.BlockSpec((tm, d), row),
        out_shape=jax.ShapeDtypeStruct((n, d), F32),
        scratch_shapes=[pltpu.VMEM((2, TOP_K, tm, d // 2), I32), pltpu.SemaphoreType.DMA((2,))],
        compiler_params=_params(1),
        name="combine_final",
    )(x1, shared, gates, mod, final_g, y_tok)


def _routing_tables(idx, rank, counts, n_tok):
    nb_max = -(-(n_tok * TOP_K) // MOE_BLOCK) + N_EXPERTS
    nblk = (counts + MOE_BLOCK - 1) // MOE_BLOCK
    blk_end = jnp.cumsum(nblk)
    blk_start = blk_end - nblk
    dest = (blk_start * MOE_BLOCK)[idx] + rank
    dest_chunks = dest.reshape(TOP_K, n_tok // SC_CHUNK, SC_CHUNK).transpose(1, 0, 2)

    blocks = jnp.arange(nb_max, dtype=I32)
    block_e = jnp.minimum(jnp.searchsorted(blk_end, blocks, side="right"), N_EXPERTS - 1).astype(I32)
    nonempty = nblk > 0
    n_visited = jnp.sum(nonempty.astype(I32))
    ordinal_of = jnp.cumsum(nonempty.astype(I32)) - 1
    order = jnp.argsort(jnp.where(nonempty, 0, 1), stable=True).astype(I32)
    ordinal = ordinal_of[block_e]
    k_in_e = blocks - blk_start[block_e]
    nb_e = jnp.maximum(nblk[block_e], 1)
    live = (ordinal + 1 < n_visited) & (blocks < blk_end[-1])
    first = WEIGHT_PARTS * (ordinal + 1)
    lo = jnp.where(live, first + WEIGHT_PARTS * k_in_e // nb_e, 0)
    hi = jnp.where(live, first + WEIGHT_PARTS * (k_in_e + 1) // nb_e, 0)
    tables = (ordinal.astype(I32), order, lo.astype(I32), hi.astype(I32),
              (WEIGHT_PARTS * n_visited).reshape(1).astype(I32), blk_end[-1:].astype(I32))
    return tables, dest_chunks, nb_max * MOE_BLOCK


def kernel(x, c, ctx, c_ctx, norm1_g, norm2_g, w_ada, b_ada, w_in, conv_w, gate_b, head_g, w_out,
           w_router, b_router, we_gate, we_up, we_down, ws_gate, ws_up, ws_down, final_g):
    assert w_ada.shape[0] == 1, "single-layer block"
    bsz, seq, d = x.shape
    ctx_len = ctx.shape[1]
    n_tok = bsz * seq
    conv_dim = conv_w.shape[2]
    v_all = head_g.shape[1]
    qk_all = (w_in.shape[2] - 3 * conv_dim - 2 * v_all - N_GATES) // 2
    assert seq % ROW_TILE == 0 and ctx_len % ROW_TILE == 0 and ROW_TILE % GRID_W == 0
    assert bsz + 1 <= SUBLANES

    cc = jnp.zeros((SUBLANES, d), F32).at[:bsz].set(c).at[bsz].set(c_ctx)
    mod = _adaln(cc, w_ada[0], b_ada).reshape(SUBLANES * 6, 1, d)

    n_main = 3 * conv_dim + 2 * qk_all + 2 * v_all
    w_main = w_in[0, :, :n_main].astype(BF16)
    k_lo = 3 * conv_dim + qk_all
    w_kt = w_in[0, :, k_lo:k_lo + qk_all].T.astype(BF16)
    w_v = w_in[0, :, k_lo + qk_all:k_lo + qk_all + v_all].astype(BF16)
    w_gate = jnp.zeros((d, LANES), BF16).at[:, :N_GATES].set(w_in[0, :, n_main:].astype(BF16))
    gate_bias = jnp.zeros((1, LANES), F32).at[0, :N_GATES].set(gate_b[0].reshape(-1))

    x2d = x.reshape(n_tok, d)
    conv, q, kt, v, og, g, gt = _inproj(x2d, mod, norm1_g, w_main, w_kt, w_gate, gate_bias, conv_w[0],
                                       seq, conv_dim, qk_all, v_all)
    ktc, vc, _, gtc = _inproj_ctx(ctx.reshape(bsz * ctx_len, d), mod, norm1_g, w_v, w_kt, w_gate,
                                  gate_bias, bsz)

    h_bwd = _mlstm(1, q, kt, v, g, gt, ktc, vc, gtc, None, bsz)
    ml = _mlstm(0, q, kt, v, g, gt, ktc, vc, gtc, (h_bwd, og), bsz, head_g)

    assert 2 * N_EXPERTS == LANES
    w_r_hi = w_router[0].astype(BF16)
    w_r = jnp.concatenate([w_r_hi, (w_router[0] - w_r_hi.astype(F32)).astype(BF16)], axis=1)
    b_r = jnp.zeros((1, LANES), F32).at[0, :N_EXPERTS].set(b_router[0])
    x1, idx, gates, rank, cnt, h_rows = _outproj(conv, ml, x2d, mod, norm2_g, w_out[0].astype(BF16),
                                             w_r, b_r, seq)

    tables, dest_chunks, n_slots = _routing_tables(idx[:TOP_K], rank[:TOP_K], cnt[0, :N_EXPERTS], n_tok)
    ws_gu = jnp.concatenate([ws_gate[0], ws_up[0]], axis=1).astype(BF16)
    ws_d = ws_down[0].astype(BF16)
    half_tiles = n_tok // ROW_TILE // 2
    x_sorted = _sc_dispatch(h_rows, dest_chunks, n_slots)
    shared = _shared(h_rows, ws_gu, ws_d, None, 0, half_tiles)
    y_sorted = _moe(x_sorted, we_gate[0], we_up[0], we_down[0], tables)
    y_tok = _sc_combine(y_sorted, dest_chunks, n_tok)
    shared = _shared(h_rows, ws_gu, ws_d, shared, half_tiles, n_tok // ROW_TILE - half_tiles)
    out = _final(x1, shared, y_tok, gates, mod, final_g.reshape(1, d), seq)
    return out.reshape(bsz, seq, d)
```

```python
import functools

import jax
import jax.numpy as jnp
from jax import lax
from jax.experimental import pallas as pl
from jax.experimental.pallas import tpu as pltpu
from jax.experimental.pallas import tpu_sc as plsc

F32 = jnp.float32
BF16 = jnp.bfloat16
I32 = jnp.int32

N_HEADS = 4
GRID_W = 64
CHUNK = 128
TOP_K = 6
N_EXPERTS = 64
ROUTED_SCALE = 2.446
EPS = 1e-6
N_GATES = 4 * N_HEADS

LANES = 128
SUBLANES = 8
MOE_BLOCK = 256
ROW_TILE = 256
ADALN_TILE = 1024
WEIGHT_PARTS = 8
PART_SHIFT = 3
WEIGHT_RING = 3
SC_CHUNK = 64
HIGH_HALF = -65536
VMEM_LIMIT = 56 * 1024 * 1024
MOE_VMEM_LIMIT = 62 * 1024 * 1024

_HIGHEST = lax.Precision.HIGHEST
_NEG_INF = float("-inf")


def _resident(shape):
    nd = len(shape)
    return pl.BlockSpec(shape, lambda *_: (0,) * nd, pipeline_mode=pl.Buffered(1))


def _params(n_axes):
    return pltpu.CompilerParams(
        dimension_semantics=("arbitrary",) * n_axes, vmem_limit_bytes=VMEM_LIMIT)


def _log_sigmoid(x):
    return jnp.minimum(x, 0.0) - jnp.log1p(jnp.exp(-jnp.abs(x)))


def _silu(x):
    return x * jax.nn.sigmoid(x)


def _pack_words(val):
    half = val.shape[1] // 2
    lo = lax.bitcast_convert_type(val[:, :half].astype(BF16).astype(F32), I32)
    hi = lax.bitcast_convert_type(val[:, half:].astype(BF16).astype(F32), I32)
    return (hi & HIGH_HALF) | lax.shift_right_logical(lo, 16)


def _unpack_words(word):
    lo = lax.bitcast_convert_type(lax.shift_left(word, 16), F32)
    hi = lax.bitcast_convert_type(word & HIGH_HALF, F32)
    return lo, hi


def _row_tile_copies(hbm_rows, row0, tile, sem, to_hbm):
    n = tile.shape[0]
    copies = []
    for c in range(SUBLANES):
        hbm = hbm_rows.at[pl.ds(row0, n), c, :]
        vmem = tile.at[:, pl.ds(c * LANES, LANES)]
        copies.append(pltpu.make_async_copy(vmem, hbm, sem) if to_hbm
                      else pltpu.make_async_copy(hbm, vmem, sem))
    return copies


def _start_all(copies):
    for cp in copies:
        cp.start()


def _wait_all(copies):
    for cp in copies:
        cp.wait()


def _adaln_body(c_ref, w_ref, b_ref, o_ref):
    s = _silu(c_ref[...])
    o_ref[...] = jnp.dot(s.astype(BF16), w_ref[...].astype(BF16),
                         preferred_element_type=F32) + b_ref[...]


def _adaln(cc, w, b):
    d, n6 = w.shape
    return pl.pallas_call(
        _adaln_body,
        grid=(n6 // ADALN_TILE,),
        in_specs=[pl.BlockSpec((SUBLANES, d), lambda j: (0, 0)),
                  pl.BlockSpec((d, ADALN_TILE), lambda j: (0, j)),
                  pl.BlockSpec((1, ADALN_TILE), lambda j: (0, j))],
        out_specs=pl.BlockSpec((SUBLANES, ADALN_TILE), lambda j: (0, j)),
        out_shape=jax.ShapeDtypeStruct((SUBLANES, n6), F32),
        compiler_params=_params(1),
        name="adaln",
    )(cc, w, b)


def _norm_mod(x, g, shift, scale):
    y = x * lax.rsqrt(jnp.mean(x * x, axis=-1, keepdims=True) + EPS) * g
    return y * (1.0 + scale) + shift


def _gate_prep(xb, wg_ref, gb_ref, g_ref, gt_ref):
    tm = xb.shape[0]
    gg = jnp.dot(xb, wg_ref[...], preferred_element_type=F32) + gb_ref[...]
    lane = lax.broadcasted_iota(I32, (tm, LANES), 1)
    is_f = (lane & N_HEADS) != 0
    is_bwd = (lane & (2 * N_HEADS)) != 0
    lf = jnp.where(is_f, _log_sigmoid(gg), 0.0)
    r = lax.broadcasted_iota(I32, (tm, tm), 0)
    c = lax.broadcasted_iota(I32, (tm, tm), 1)
    same = (r // CHUNK) == (c // CHUNK)
    tri_l = jnp.where(same & (c <= r), 1.0, 0.0).astype(F32)
    tri_u = jnp.where(same & (c >= r), 1.0, 0.0).astype(F32)
    pre = jnp.dot(tri_l, lf, precision=_HIGHEST, preferred_element_type=F32)
    suf = jnp.dot(tri_u, lf, precision=_HIGHEST, preferred_element_type=F32)
    out = jnp.where(is_f, jnp.where(is_bwd, suf, pre), gg)
    g_ref[...] = out[:, :N_GATES]
    gt_ref[...] = out.T[:N_GATES, :]


def _project_transposed(wt_ref, xb):
    return lax.dot_general(wt_ref[...], xb, (((1,), (1,)), ((), ())),
                           preferred_element_type=F32).astype(BF16)


def _inproj_body(x_ref, sh_ref, sc_ref, g1_ref, w_ref, wkt_ref, wg_ref, gb_ref, cw_ref,
                 conv_ref, q_ref, k_ref, v_ref, o_ref, g_ref, gt_ref, *, conv_dim, qk_all, v_all):
    tm = x_ref.shape[0]
    xb = _norm_mod(x_ref[...], g1_ref[...], sh_ref[0], sc_ref[0]).astype(BF16)

    def proj(lo, width):
        return jnp.dot(xb, w_ref[:, lo:lo + width], preferred_element_type=F32)

    u = proj(conv_dim, conv_dim) * proj(2 * conv_dim, conv_dim)
    pos = lax.broadcasted_iota(I32, (tm, 1), 0) % GRID_W
    um = jnp.where(pos == 0, 0.0, pltpu.roll(u, 1, axis=0))
    up = jnp.where(pos == GRID_W - 1, 0.0, pltpu.roll(u, tm - 1, axis=0))
    y = um * cw_ref[0:1, :] + u * cw_ref[1:2, :] + up * cw_ref[2:3, :]
    conv_ref[...] = (proj(0, conv_dim) * y).astype(BF16)

    off = 3 * conv_dim
    qscale = (qk_all // N_HEADS) ** -0.5
    q_ref[...] = (proj(off, qk_all) * qscale).astype(BF16)
    k_ref[...] = _project_transposed(wkt_ref, xb)
    v_ref[...] = proj(off + 2 * qk_all, v_all).astype(BF16)
    o_ref[...] = jax.nn.sigmoid(proj(off + 2 * qk_all + v_all, v_all)).astype(BF16)
    _gate_prep(xb, wg_ref, gb_ref, g_ref, gt_ref)


def _inproj_ctx_body(x_ref, sh_ref, sc_ref, g1_ref, w_ref, wkt_ref, wg_ref, gb_ref,
                     k_ref, v_ref, g_ref, gt_ref):
    xb = _norm_mod(x_ref[...], g1_ref[...], sh_ref[0], sc_ref[0]).astype(BF16)
    k_ref[...] = _project_transposed(wkt_ref, xb)
    v_ref[...] = jnp.dot(xb, w_ref[...], preferred_element_type=F32).astype(BF16)
    _gate_prep(xb, wg_ref, gb_ref, g_ref, gt_ref)


def _mod_spec(part, tiles_per_row, fixed_row=None):
    def index(i):
        row = fixed_row if fixed_row is not None else i // tiles_per_row
        return (row * 6 + part, 0, 0)

    return index


def _inproj(x2d, mod, g1, w_main, w_kt, w_gate, gate_b, conv_w, rows_per_batch, conv_dim, qk_all, v_all):
    n, d = x2d.shape
    tm = ROW_TILE
    tiles_per_batch = rows_per_batch // tm
    row = lambda i: (i, 0)
    mod_block = (1, 1, d)
    out_shapes = (
        jax.ShapeDtypeStruct((n, conv_dim), BF16),
        jax.ShapeDtypeStruct((n, qk_all), BF16),
        jax.ShapeDtypeStruct((qk_all, n), BF16),
        jax.ShapeDtypeStruct((n, v_all), BF16),
        jax.ShapeDtypeStruct((n, v_all), BF16),
        jax.ShapeDtypeStruct((n, N_GATES), F32),
        jax.ShapeDtypeStruct((N_GATES, n), F32),
    )
    out_specs = (
        pl.BlockSpec((tm, conv_dim), row),
        pl.BlockSpec((tm, qk_all), row),
        pl.BlockSpec((qk_all, tm), lambda i: (0, i)),
        pl.BlockSpec((tm, v_all), row),
        pl.BlockSpec((tm, v_all), row),
        pl.BlockSpec((tm, N_GATES), row),
        pl.BlockSpec((N_GATES, tm), lambda i: (0, i)),
    )
    return pl.pallas_call(
        functools.partial(_inproj_body, conv_dim=conv_dim, qk_all=qk_all, v_all=v_all),
        grid=(n // tm,),
        in_specs=[pl.BlockSpec((tm, d), row),
                  pl.BlockSpec(mod_block, _mod_spec(0, tiles_per_batch)),
                  pl.BlockSpec(mod_block, _mod_spec(1, tiles_per_batch)),
                  _resident(g1.shape), _resident(w_main.shape), _resident(w_kt.shape),
                  _resident(w_gate.shape), _resident(gate_b.shape), _resident(conv_w.shape)],
        out_specs=out_specs,
        out_shape=out_shapes,
        compiler_params=_params(1),
        name="inproj",
    )(x2d, mod, mod, g1, w_main, w_kt, w_gate, gate_b, conv_w)


def _inproj_ctx(c2d, mod, g1, w_v, w_kt, w_gate, gate_b, ctx_mod_row):
    n, d = c2d.shape
    tm = ROW_TILE
    row = lambda i: (i, 0)
    mod_block = (1, 1, d)
    qk_all, v_all = w_kt.shape[0], w_v.shape[1]
    return pl.pallas_call(
        _inproj_ctx_body,
        grid=(n // tm,),
        in_specs=[pl.BlockSpec((tm, d), row),
                  pl.BlockSpec(mod_block, _mod_spec(0, 1, ctx_mod_row)),
                  pl.BlockSpec(mod_block, _mod_spec(1, 1, ctx_mod_row)),
                  _resident(g1.shape), _resident(w_v.shape), _resident(w_kt.shape),
                  _resident(w_gate.shape), _resident(gate_b.shape)],
        out_specs=(pl.BlockSpec((qk_all, tm), lambda i: (0, i)), pl.BlockSpec((tm, v_all), row),
                   pl.BlockSpec((tm, N_GATES), row), pl.BlockSpec((N_GATES, tm), lambda i: (0, i))),
        out_shape=(jax.ShapeDtypeStruct((qk_all, n), BF16), jax.ShapeDtypeStruct((n, v_all), BF16),
                   jax.ShapeDtypeStruct((n, N_GATES), F32), jax.ShapeDtypeStruct((N_GATES, n), F32)),
        compiler_params=_params(1),
        name="inproj_ctx",
    )(c2d, mod, mod, g1, w_v, w_kt, w_gate, gate_b)


def _with_ones(v):
    return jnp.concatenate([v, jnp.ones((v.shape[0], LANES), v.dtype)], axis=1)


def _mlstm_state_update(h, direction, kt_ref, v_ref, gt_ref, s_ref, m_ref, qk, vh):
    ci = direction * 2 * N_HEADS + h
    cb = ci + N_HEADS
    last = 0 if direction else CHUNK - 1
    kt = kt_ref[h * qk:(h + 1) * qk, :].astype(F32)
    va = _with_ones(v_ref[:, h * vh:(h + 1) * vh])
    b_last = gt_ref[cb:cb + 1, last:last + 1]
    m_prev = m_ref[h][0:1, 0:1]
    g_r = b_last - gt_ref[cb:cb + 1, :] + gt_ref[ci:ci + 1, :]
    m_new = jnp.maximum(b_last + m_prev, jnp.max(g_r, axis=1, keepdims=True))
    a = jnp.exp(b_last + m_prev - m_new)
    kw = (kt * jnp.exp(g_r - m_new)).astype(BF16)
    s_ref[h] = a * s_ref[h] + jnp.dot(kw, va, preferred_element_type=F32)
    m_ref[h] = jnp.broadcast_to(m_new, m_ref.shape[1:])


def _mlstm_head_output(h, direction, q_ref, kt_ref, v_ref, g_ref, gt_ref, s_ref, m_ref, qk, vh):
    ci = direction * 2 * N_HEADS + h
    cb = ci + N_HEADS
    q = q_ref[:, h * qk:(h + 1) * qk]
    kt = kt_ref[h * qk:(h + 1) * qk, :]
    va = _with_ones(v_ref[:, h * vh:(h + 1) * vh])
    ig_r = gt_ref[ci:ci + 1, :]
    b_r = gt_ref[cb:cb + 1, :]
    b_c = g_ref[:, cb:cb + 1]
    m_prev = m_ref[h][0:1, 0:1]
    row = lax.broadcasted_iota(I32, (CHUNK, CHUNK), 0)
    col = lax.broadcasted_iota(I32, (CHUNK, CHUNK), 1)
    mask = (col >= row) if direction else (col <= row)
    dm = jnp.where(mask, b_c + (ig_r - b_r), _NEG_INF)
    inter = b_c + m_prev
    m_t = jnp.maximum(inter, jnp.max(dm, axis=1, keepdims=True))
    w_inter = jnp.exp(inter - m_t)
    s = jnp.dot(q, kt, preferred_element_type=F32) * jnp.exp(dm - m_t)
    intra = jnp.dot(s.astype(BF16), va, preferred_element_type=F32)
    carried = jnp.dot(q, s_ref[h].astype(BF16), preferred_element_type=F32)
    num = intra[:, 0:vh] + w_inter * carried[:, 0:vh]
    den = intra[:, vh:vh + 1] + w_inter * carried[:, vh:vh + 1]
    return num / jnp.maximum(jnp.abs(den), jnp.exp(-m_t))


def _mlstm_body(*refs, direction, n_ctx_chunks, qk, vh):
    if direction:
        (q_ref, kt_ref, v_ref, g_ref, gt_ref, ktc_ref, vc_ref, gtc_ref,
         out_ref, s_ref, m_ref) = refs
    else:
        (q_ref, kt_ref, v_ref, g_ref, gt_ref, ktc_ref, vc_ref, gtc_ref,
         hb_ref, og_ref, hg_ref, out_ref, s_ref, m_ref) = refs
    step = pl.program_id(1)

    @pl.when(step == 0)
    def _():
        s_ref[...] = jnp.zeros_like(s_ref)
        m_ref[...] = jnp.full_like(m_ref, _NEG_INF)

    @pl.when(step < n_ctx_chunks)
    def _():
        for h in range(N_HEADS):
            _mlstm_state_update(h, direction, ktc_ref, vc_ref, gtc_ref, s_ref, m_ref, qk, vh)

    @pl.when(step >= n_ctx_chunks)
    def _():
        for h in range(N_HEADS):
            hh = _mlstm_head_output(h, direction, q_ref, kt_ref, v_ref, g_ref, gt_ref, s_ref, m_ref, qk, vh)
            cols = slice(h * vh, (h + 1) * vh)
            if direction:
                out_ref[:, cols] = hh
            else:
                hs = hh + hb_ref[:, cols]
                hs = hs * lax.rsqrt(jnp.mean(hs * hs, axis=-1, keepdims=True) + EPS)
                out_ref[:, cols] = (hs * hg_ref[:, cols] * og_ref[:, cols].astype(F32)).astype(BF16)
            _mlstm_state_update(h, direction, kt_ref, v_ref, gt_ref, s_ref, m_ref, qk, vh)


def _mlstm(direction, q, kt, v, g, gt, ktc, vc, gtc, extra, bsz, head_g=None):
    n, qk_all = q.shape
    v_all = v.shape[1]
    qk, vh = qk_all // N_HEADS, v_all // N_HEADS
    nc = n // bsz // CHUNK
    ncc = vc.shape[0] // bsz // CHUNK

    def lat(b, s):
        j = jnp.clip(s - ncc, 0, nc - 1)
        return b * nc + (nc - 1 - j if direction else j)

    def ctx(b, s):
        j = jnp.clip(s, 0, ncc - 1)
        return b * ncc + (ncc - 1 - j if direction else j)

    lat_row = lambda b, s: (lat(b, s), 0)
    lat_col = lambda b, s: (0, lat(b, s))
    ctx_row = lambda b, s: (ctx(b, s), 0)
    ctx_col = lambda b, s: (0, ctx(b, s))
    in_specs = [pl.BlockSpec((CHUNK, qk_all), lat_row), pl.BlockSpec((qk_all, CHUNK), lat_col),
                pl.BlockSpec((CHUNK, v_all), lat_row), pl.BlockSpec((CHUNK, N_GATES), lat_row),
                pl.BlockSpec((N_GATES, CHUNK), lat_col),
                pl.BlockSpec((qk_all, CHUNK), ctx_col), pl.BlockSpec((CHUNK, v_all), ctx_row),
                pl.BlockSpec((N_GATES, CHUNK), ctx_col)]
    args = [q, kt, v, g, gt, ktc, vc, gtc]
    if direction:
        out_dtype = F32
    else:
        hb, og = extra
        in_specs += [pl.BlockSpec((CHUNK, v_all), lat_row), pl.BlockSpec((CHUNK, v_all), lat_row),
                     pl.BlockSpec((1, v_all), lambda b, s: (0, 0))]
        args += [hb, og, head_g]
        out_dtype = BF16
    return pl.pallas_call(
        functools.partial(_mlstm_body, direction=direction, n_ctx_chunks=ncc, qk=qk, vh=vh),
        grid=(bsz, ncc + nc),
        in_specs=in_specs,
        out_specs=pl.BlockSpec((CHUNK, v_all), lat_row),
        out_shape=jax.ShapeDtypeStruct((n, v_all), out_dtype),
        scratch_shapes=[pltpu.VMEM((N_HEADS, qk, vh + LANES), F32),
                        pltpu.VMEM((N_HEADS, SUBLANES, LANES), F32)],
        compiler_params=_params(2),
        name="mlstm_bwd" if direction else "mlstm_fwd",
    )(*args)


def _outproj_body(conv_ref, ml_ref, x_ref, gt1_ref, sh2_ref, sc2_ref, g2_ref, wo_ref, wr_ref, br_ref,
                  x1_ref, idx_ref, gate_ref, rank_ref, cnt_ref, h_hbm, carry_ref, hw, hsem):
    tm = x_ref.shape[0]
    half = conv_ref.shape[1]
    step = pl.program_id(0)
    buf = step % 2

    def h_out(i, s):
        return _row_tile_copies(h_hbm, i * tm, hw.at[s], hsem.at[s], to_hbm=True)

    @pl.when(step == 0)
    def _():
        carry_ref[...] = jnp.zeros_like(carry_ref)

    @pl.when(step >= 2)
    def _():
        _wait_all(h_out(step - 2, buf))

    y = (jnp.dot(conv_ref[...], wo_ref[0:half, :], preferred_element_type=F32)
         + jnp.dot(ml_ref[...], wo_ref[half:2 * half, :], preferred_element_type=F32))
    x1 = x_ref[...] + gt1_ref[0] * y
    x1_ref[...] = x1
    hn = _norm_mod(x1, g2_ref[...], sh2_ref[0], sc2_ref[0])
    hw[buf] = _pack_words(hn)
    _start_all(h_out(step, buf))

    h_hi = hn.astype(BF16)
    h_lo = (hn - h_hi.astype(F32)).astype(BF16)
    parts = (jnp.dot(h_hi, wr_ref[...], preferred_element_type=F32)
             + jnp.dot(h_lo, wr_ref[...], preferred_element_type=F32))
    scores = jax.nn.sigmoid(parts + pltpu.roll(parts, N_EXPERTS, axis=1))
    lane = lax.broadcasted_iota(I32, (tm, LANES), 1).astype(F32)
    biased = jnp.where(lane < N_EXPERTS, scores + br_ref[...], _NEG_INF)
    onehot = jnp.zeros((tm, LANES), F32)
    picks, sels = [], []
    for _ in range(TOP_K):
        mx = jnp.max(biased, axis=1, keepdims=True)
        pick = jnp.min(jnp.where(biased == mx, lane, float(LANES)), axis=1, keepdims=True)
        hit = lane == pick
        sels.append(jnp.sum(jnp.where(hit, scores, 0.0), axis=1, keepdims=True))
        picks.append(pick)
        biased = jnp.where(hit, _NEG_INF, biased)
        onehot = onehot + hit.astype(F32)
    total = sels[0]
    for s in sels[1:]:
        total = total + s

    r = lax.broadcasted_iota(I32, (tm, tm), 0)
    c = lax.broadcasted_iota(I32, (tm, tm), 1)
    strict = jnp.where(c < r, 1.0, 0.0).astype(BF16)
    before = jnp.dot(strict, onehot.astype(BF16), preferred_element_type=F32) + carry_ref[...]
    slot = lax.broadcasted_iota(I32, (tm, SUBLANES), 1)
    idx_out = jnp.zeros((tm, LANES), F32)
    rank_out = jnp.zeros((tm, LANES), F32)
    gate_out = jnp.zeros((tm, SUBLANES), F32)
    for j in range(TOP_K):
        rank = jnp.sum(jnp.where(lane == picks[j], before, 0.0), axis=1, keepdims=True)
        idx_out = jnp.where(lane == float(j), picks[j], idx_out)
        rank_out = jnp.where(lane == float(j), rank, rank_out)
        gate_out = jnp.where(slot == j, sels[j] / total * ROUTED_SCALE, gate_out)
    idx_ref[...] = idx_out.T[:SUBLANES, :].astype(I32)
    rank_ref[...] = rank_out.T[:SUBLANES, :].astype(I32)
    gate_ref[...] = gate_out
    carry_ref[...] = carry_ref[...] + jnp.sum(onehot, axis=0, keepdims=True)
    cnt_ref[...] = jnp.broadcast_to(carry_ref[...], cnt_ref.shape).astype(I32)

    @pl.when(step == pl.num_programs(0) - 1)
    def _():
        @pl.when(step >= 1)
        def _():
            _wait_all(h_out(step - 1, 1 - buf))
        _wait_all(h_out(step, buf))


def _outproj(conv, ml, x2d, mod, g2, w_out, w_router, b_router, rows_per_batch):
    n, d = x2d.shape
    tm = ROW_TILE
    tiles_per_batch = rows_per_batch // tm
    row = lambda i: (i, 0)
    mod_block = (1, 1, d)
    half = conv.shape[1]
    return pl.pallas_call(
        _outproj_body,
        grid=(n // tm,),
        in_specs=[pl.BlockSpec((tm, half), row), pl.BlockSpec((tm, half), row), pl.BlockSpec((tm, d), row),
                  pl.BlockSpec(mod_block, _mod_spec(2, tiles_per_batch)),
                  pl.BlockSpec(mod_block, _mod_spec(3, tiles_per_batch)),
                  pl.BlockSpec(mod_block, _mod_spec(4, tiles_per_batch)),
                  _resident(g2.shape), _resident(w_out.shape), _resident(w_router.shape),
                  _resident(b_router.shape)],
        out_specs=(pl.BlockSpec((tm, d), row),
                   pl.BlockSpec((SUBLANES, tm), lambda i: (0, i)), pl.BlockSpec((tm, SUBLANES), row),
                   pl.BlockSpec((SUBLANES, tm), lambda i: (0, i)),
                   pl.BlockSpec((SUBLANES, LANES), lambda i: (0, 0)),
                   pl.BlockSpec(memory_space=pl.ANY)),
        out_shape=(jax.ShapeDtypeStruct((n, d), F32),
                   jax.ShapeDtypeStruct((SUBLANES, n), I32), jax.ShapeDtypeStruct((n, SUBLANES), F32),
                   jax.ShapeDtypeStruct((SUBLANES, n), I32),
                   jax.ShapeDtypeStruct((SUBLANES, LANES), I32),
                   jax.ShapeDtypeStruct((n, SUBLANES, LANES), I32)),
        scratch_shapes=[pltpu.VMEM((1, LANES), F32), pltpu.VMEM((2, tm, d // 2), I32),
                        pltpu.SemaphoreType.DMA((2,))],
        compiler_params=_params(1),
        name="outproj_router",
    )(conv, ml, x2d, mod, mod, mod, g2, w_out, w_router, b_router)


def _sc_workers():
    info = plsc.get_sparse_core_info()
    return info.num_cores, info.num_cores * info.num_subcores


def _sc_dispatch(h_rows, dest_chunks, n_slots):
    n_tok = h_rows.shape[0]
    n_cores, n_workers = _sc_workers()
    per_worker = n_tok // (n_workers * SC_CHUNK)
    assert per_worker * n_workers * SC_CHUNK == n_tok
    mesh = plsc.VectorSubcoreMesh(core_axis_name="c", subcore_axis_name="s")

    @functools.partial(
        pl.kernel, mesh=mesh,
        out_type=jax.ShapeDtypeStruct((n_slots,) + h_rows.shape[1:], h_rows.dtype),
        scratch_types=[pltpu.VMEM((TOP_K, SC_CHUNK), I32),
                       pltpu.VMEM((SC_CHUNK,) + h_rows.shape[1:], h_rows.dtype)],
    )
    def dispatch(h_hbm, dest_hbm, out_hbm, idx_v, rows_v):
        wid = lax.axis_index("s") * n_cores + lax.axis_index("c")

        @pl.loop(0, per_worker)
        def _(i):
            chunk = wid * per_worker + i
            pltpu.sync_copy(dest_hbm.at[chunk], idx_v)
            pltpu.sync_copy(h_hbm.at[pl.ds(chunk * SC_CHUNK, SC_CHUNK)], rows_v)
            for k in range(TOP_K):
                pltpu.sync_copy(rows_v, out_hbm.at[idx_v.at[k]])

    return dispatch(h_rows, dest_chunks)


def _sc_combine(y_sorted, dest_chunks, n_tok):
    n_cores, n_workers = _sc_workers()
    per_worker = n_tok // (n_workers * SC_CHUNK)
    mesh = plsc.VectorSubcoreMesh(core_axis_name="c", subcore_axis_name="s")

    @functools.partial(
        pl.kernel, mesh=mesh,
        out_type=jax.ShapeDtypeStruct((TOP_K, n_tok) + y_sorted.shape[1:], y_sorted.dtype),
        scratch_types=[pltpu.VMEM((TOP_K, SC_CHUNK), I32),
                       pltpu.VMEM((SC_CHUNK,) + y_sorted.shape[1:], y_sorted.dtype)],
    )
    def combine(y_hbm, dest_hbm, out_hbm, idx_v, rows_v):
        wid = lax.axis_index("s") * n_cores + lax.axis_index("c")

        @pl.loop(0, per_worker)
        def _(i):
            chunk = wid * per_worker + i
            pltpu.sync_copy(dest_hbm.at[chunk], idx_v)
            for k in range(TOP_K):
                pltpu.sync_copy(y_hbm.at[idx_v.at[k]], rows_v)
                pltpu.sync_copy(rows_v, out_hbm.at[k, pl.ds(chunk * SC_CHUNK, SC_CHUNK)])

    return combine(y_sorted, dest_chunks)


def _moe_body(ord_ref, order_ref, glo_ref, ghi_ref, tot_ref, nb_ref,
              x_hbm, wg_hbm, wu_hbm, wd_hbm, y_hbm,
              wgu, wd, stage_a, stage_d, xw, yw, wsem, xsem, ysem, *, d_expert):
    b = pl.program_id(0)
    nb = nb_ref[0]
    total = tot_ref[0]
    d_model = wgu.shape[1]
    rows_a = d_model // WEIGHT_PARTS
    rows_d = d_expert // WEIGHT_PARTS

    def part_copies(g):
        e = order_ref[lax.shift_right_logical(g, PART_SHIFT)]
        i = g & (WEIGHT_PARTS - 1)
        s = lax.rem(g, WEIGHT_RING)
        return (pltpu.make_async_copy(wg_hbm.at[e, pl.ds(i * rows_a, rows_a)], stage_a.at[s, 0],
                                      wsem.at[s, 0]),
                pltpu.make_async_copy(wu_hbm.at[e, pl.ds(i * rows_a, rows_a)], stage_a.at[s, 1],
                                      wsem.at[s, 1]),
                pltpu.make_async_copy(wd_hbm.at[e, pl.ds(i * rows_d, rows_d)], stage_d.at[s],
                                      wsem.at[s, 2]))

    def start_part(g):
        for cp in part_copies(g):
            cp.start()

    def wait_part(g):
        for cp in part_copies(g):
            cp.wait()

    def cast_part(g):
        i = g & (WEIGHT_PARTS - 1)
        s = lax.rem(g, WEIGHT_RING)
        par = lax.shift_right_logical(g, PART_SHIFT) & 1
        ra = pl.multiple_of(i * rows_a, rows_a)
        rd = pl.multiple_of(i * rows_d, rows_d)
        wgu[par, pl.ds(ra, rows_a), 0:d_expert] = stage_a[s, 0].astype(BF16)
        wgu[par, pl.ds(ra, rows_a), d_expert:2 * d_expert] = stage_a[s, 1].astype(BF16)
        wd[par, pl.ds(rd, rows_d), :] = stage_d[s].astype(BF16)

    def refill(g):
        @pl.when(g + WEIGHT_RING < total)
        def _():
            start_part(g + WEIGHT_RING)

    def cast_parts(lo, hi):
        def body(g, carry):
            wait_part(g)
            cast_part(g)
            refill(g)
            return carry
        lax.fori_loop(lo, hi, body, 0)

    slot = b % 2

    def x_in(blk, s):
        return _row_tile_copies(x_hbm, blk * MOE_BLOCK, xw.at[s], xsem.at[s], to_hbm=False)

    def y_out(blk, s):
        return _row_tile_copies(y_hbm, blk * MOE_BLOCK, yw.at[s], ysem.at[s], to_hbm=True)

    @pl.when(b == 0)
    def _():
        _start_all(x_in(0, 0))
        for g in range(WEIGHT_RING):
            start_part(g)
        cast_parts(0, WEIGHT_PARTS)

    @pl.when(b + 1 < nb)
    def _():
        _start_all(x_in(b + 1, 1 - slot))

    @pl.when(b < nb)
    def _():
        par = ord_ref[b] & 1
        _wait_all(x_in(b, slot))

        @pl.when(b >= 2)
        def _():
            _wait_all(y_out(b - 2, slot))

        x = jnp.concatenate(_unpack_words(xw[slot]), axis=1).astype(BF16)
        gu = jnp.dot(x, wgu[par], preferred_element_type=F32)
        hb = (_silu(gu[:, 0:d_expert]) * gu[:, d_expert:2 * d_expert]).astype(BF16)
        yw[slot] = _pack_words(jnp.dot(hb, wd[par], preferred_element_type=F32))
        _start_all(y_out(b, slot))
        cast_parts(glo_ref[b], ghi_ref[b])

        @pl.when(b == nb - 1)
        def _():
            @pl.when(b >= 1)
            def _():
                _wait_all(y_out(b - 1, 1 - slot))
            _wait_all(y_out(b, slot))


def _moe(x_sorted, we_gate, we_up, we_down, tables):
    d, d_expert = we_gate.shape[1], we_gate.shape[2]
    nb_max = x_sorted.shape[0] // MOE_BLOCK
    any_spec = pl.BlockSpec(memory_space=pl.ANY)
    grid_spec = pltpu.PrefetchScalarGridSpec(
        num_scalar_prefetch=len(tables),
        grid=(nb_max,),
        in_specs=[any_spec, any_spec, any_spec, any_spec],
        out_specs=any_spec,
        scratch_shapes=[pltpu.VMEM((2, d, 2 * d_expert), BF16),
                        pltpu.VMEM((2, d_expert, d), BF16),
                        pltpu.VMEM((WEIGHT_RING, 2, d // WEIGHT_PARTS, d_expert), F32),
                        pltpu.VMEM((WEIGHT_RING, d_expert // WEIGHT_PARTS, d), F32),
                        pltpu.VMEM((2, MOE_BLOCK, d // 2), I32),
                        pltpu.VMEM((2, MOE_BLOCK, d // 2), I32),
                        pltpu.SemaphoreType.DMA((WEIGHT_RING, 3)),
                        pltpu.SemaphoreType.DMA((2,)),
                        pltpu.SemaphoreType.DMA((2,))],
    )
    return pl.pallas_call(
        functools.partial(_moe_body, d_expert=d_expert),
        grid_spec=grid_spec,
        out_shape=jax.ShapeDtypeStruct(x_sorted.shape, x_sorted.dtype),
        compiler_params=pltpu.CompilerParams(
            dimension_semantics=("arbitrary",), vmem_limit_bytes=MOE_VMEM_LIMIT),
        name="moe_routed",
    )(*tables, x_sorted, we_gate, we_up, we_down)


def _final_body(x1_ref, gate_ref, gt2_ref, wsgu_ref, wsd_ref, fg_ref, h_hbm, y_hbm, out_ref,
                hw, yw, sem, *, d_shared):
    tm = x1_ref.shape[0]
    step = pl.program_id(0)
    slot = step % 2

    def rows_in(i, s):
        copies = _row_tile_copies(h_hbm, i * tm, hw.at[s], sem.at[s], to_hbm=False)
        for k in range(TOP_K):
            copies += _row_tile_copies(y_hbm.at[k], i * tm, yw.at[s, k], sem.at[s], to_hbm=False)
        return copies

    @pl.when(step == 0)
    def _():
        _start_all(rows_in(0, 0))

    @pl.when(step + 1 < pl.num_programs(0))
    def _():
        _start_all(rows_in(step + 1, 1 - slot))

    _wait_all(rows_in(step, slot))
    h = jnp.concatenate(_unpack_words(hw[slot]), axis=1).astype(BF16)
    gu = jnp.dot(h, wsgu_ref[...], preferred_element_type=F32)
    hb = (_silu(gu[:, 0:d_shared]) * gu[:, d_shared:2 * d_shared]).astype(BF16)
    acc = jnp.dot(hb, wsd_ref[...], preferred_element_type=F32)
    for k in range(TOP_K):
        acc = acc + gate_ref[:, k:k + 1] * jnp.concatenate(_unpack_words(yw[slot, k]), axis=1)
    x2 = x1_ref[...] + gt2_ref[0] * acc
    out_ref[...] = x2 * lax.rsqrt(jnp.mean(x2 * x2, axis=-1, keepdims=True) + EPS) * fg_ref[...]


def _final(h_rows, x1, y_tok, gates, mod, ws_gu, ws_d, final_g, rows_per_batch):
    n, d = x1.shape
    tm = ROW_TILE
    tiles_per_batch = rows_per_batch // tm
    row = lambda i: (i, 0)
    any_spec = pl.BlockSpec(memory_space=pl.ANY)
    return pl.pallas_call(
        functools.partial(_final_body, d_shared=ws_d.shape[0]),
        grid=(n // tm,),
        in_specs=[pl.BlockSpec((tm, d), row), pl.BlockSpec((tm, SUBLANES), row),
                  pl.BlockSpec((1, 1, d), _mod_spec(5, tiles_per_batch)),
                  _resident(ws_gu.shape), _resident(ws_d.shape), _resident(final_g.shape),
                  any_spec, any_spec],
        out_specs=pl.BlockSpec((tm, d), row),
        out_shape=jax.ShapeDtypeStruct((n, d), F32),
        scratch_shapes=[pltpu.VMEM((2, tm, d // 2), I32), pltpu.VMEM((2, TOP_K, tm, d // 2), I32),
                        pltpu.SemaphoreType.DMA((2,))],
        compiler_params=_params(1),
        name="shared_combine_final",
    )(x1, gates, mod, ws_gu, ws_d, final_g, h_rows, y_tok)


def _routing_tables(idx, rank, counts, n_tok):
    nb_max = -(-(n_tok * TOP_K) // MOE_BLOCK) + N_EXPERTS
    nblk = (counts + MOE_BLOCK - 1) // MOE_BLOCK
    blk_end = jnp.cumsum(nblk)
    blk_start = blk_end - nblk
    experts = jnp.arange(N_EXPERTS, dtype=I32)[:, None, None]
    first_slot = (blk_start * MOE_BLOCK)[:, None, None]
    dest = jnp.sum(jnp.where(idx[None] == experts, first_slot, 0), axis=0) + rank
    dest_chunks = dest.reshape(TOP_K, n_tok // SC_CHUNK, SC_CHUNK).transpose(1, 0, 2)

    blocks = jnp.arange(nb_max, dtype=I32)
    block_e = jnp.minimum(jnp.searchsorted(blk_end, blocks, side="right"), N_EXPERTS - 1).astype(I32)
    nonempty = nblk > 0
    n_visited = jnp.sum(nonempty.astype(I32))
    ordinal_of = jnp.cumsum(nonempty.astype(I32)) - 1
    order = jnp.argsort(jnp.where(nonempty, 0, 1), stable=True).astype(I32)
    ordinal = ordinal_of[block_e]
    k_in_e = blocks - blk_start[block_e]
    nb_e = jnp.maximum(nblk[block_e], 1)
    live = (ordinal + 1 < n_visited) & (blocks < blk_end[-1])
    first = WEIGHT_PARTS * (ordinal + 1)
    lo = jnp.where(live, first + WEIGHT_PARTS * k_in_e // nb_e, 0)
    hi = jnp.where(live, first + WEIGHT_PARTS * (k_in_e + 1) // nb_e, 0)
    tables = (ordinal.astype(I32), order, lo.astype(I32), hi.astype(I32),
              (WEIGHT_PARTS * n_visited).reshape(1).astype(I32), blk_end[-1:].astype(I32))
    return tables, dest_chunks, nb_max * MOE_BLOCK


def kernel(x, c, ctx, c_ctx, norm1_g, norm2_g, w_ada, b_ada, w_in, conv_w, gate_b, head_g, w_out,
           w_router, b_router, we_gate, we_up, we_down, ws_gate, ws_up, ws_down, final_g):
    assert w_ada.shape[0] == 1, "single-layer block"
    bsz, seq, d = x.shape
    ctx_len = ctx.shape[1]
    n_tok = bsz * seq
    conv_dim = conv_w.shape[2]
    v_all = head_g.shape[1]
    qk_all = (w_in.shape[2] - 3 * conv_dim - 2 * v_all - N_GATES) // 2
    assert seq % ROW_TILE == 0 and ctx_len % ROW_TILE == 0 and ROW_TILE % GRID_W == 0
    assert bsz + 1 <= SUBLANES

    cc = jnp.zeros((SUBLANES, d), F32).at[:bsz].set(c).at[bsz].set(c_ctx)
    mod = _adaln(cc, w_ada[0], b_ada).reshape(SUBLANES * 6, 1, d)

    n_main = 3 * conv_dim + 2 * qk_all + 2 * v_all
    w_main = w_in[0, :, :n_main].astype(BF16)
    k_lo = 3 * conv_dim + qk_all
    w_kt = w_in[0, :, k_lo:k_lo + qk_all].T.astype(BF16)
    w_v = w_in[0, :, k_lo + qk_all:k_lo + qk_all + v_all].astype(BF16)
    w_gate = jnp.zeros((d, LANES), BF16).at[:, :N_GATES].set(w_in[0, :, n_main:].astype(BF16))
    gate_bias = jnp.zeros((1, LANES), F32).at[0, :N_GATES].set(gate_b[0].reshape(-1))

    x2d = x.reshape(n_tok, d)
    conv, q, kt, v, og, g, gt = _inproj(x2d, mod, norm1_g, w_main, w_kt, w_gate, gate_bias, conv_w[0],
                                       seq, conv_dim, qk_all, v_all)
    ktc, vc, _, gtc = _inproj_ctx(ctx.reshape(bsz * ctx_len, d), mod, norm1_g, w_v, w_kt, w_gate,
                                  gate_bias, bsz)

    h_bwd = _mlstm(1, q, kt, v, g, gt, ktc, vc, gtc, None, bsz)
    ml = _mlstm(0, q, kt, v, g, gt, ktc, vc, gtc, (h_bwd, og), bsz, head_g)

    assert 2 * N_EXPERTS == LANES
    w_r_hi = w_router[0].astype(BF16)
    w_r = jnp.concatenate([w_r_hi, (w_router[0] - w_r_hi.astype(F32)).astype(BF16)], axis=1)
    b_r = jnp.zeros((1, LANES), F32).at[0, :N_EXPERTS].set(b_router[0])
    x1, idx, gates, rank, cnt, h_rows = _outproj(conv, ml, x2d, mod, norm2_g, w_out[0].astype(BF16),
                                             w_r, b_r, seq)

    tables, dest_chunks, n_slots = _routing_tables(idx[:TOP_K], rank[:TOP_K], cnt[0, :N_EXPERTS], n_tok)
    x_sorted = _sc_dispatch(h_rows, dest_chunks, n_slots)
    y_sorted = _moe(x_sorted, we_gate[0], we_up[0], we_down[0], tables)
    y_tok = _sc_combine(y_sorted, dest_chunks, n_tok)

    ws_gu = jnp.concatenate([ws_gate[0], ws_up[0]], axis=1).astype(BF16)
    out = _final(h_rows, x1, y_tok, gates, mod, ws_gu, ws_down[0].astype(BF16),
                 final_g.reshape(1, d), seq)
    return out.reshape(bsz, seq, d)
```

```python
import functools

import jax
import jax.numpy as jnp
from jax import lax
from jax.experimental import pallas as pl
from jax.experimental.pallas import tpu as pltpu
from jax.experimental.pallas import tpu_sc as plsc

F32 = jnp.float32
BF16 = jnp.bfloat16
I32 = jnp.int32

N_HEADS = 4
GRID_W = 64
CHUNK = 128
TOP_K = 6
N_EXPERTS = 64
ROUTED_SCALE = 2.446
EPS = 1e-6
N_GATES = 4 * N_HEADS

LANES = 128
SUBLANES = 8
MOE_BLOCK = 256
ROW_TILE = 256
ADALN_TILE = 1024
WEIGHT_PARTS = 8
PART_SHIFT = 3
WEIGHT_RING = 3
SC_CHUNK = 64
HIGH_HALF = -65536
VMEM_LIMIT = 56 * 1024 * 1024
MOE_VMEM_LIMIT = 62 * 1024 * 1024

_HIGHEST = lax.Precision.HIGHEST
_NEG_INF = float("-inf")


def _resident(shape):
    nd = len(shape)
    return pl.BlockSpec(shape, lambda *_: (0,) * nd, pipeline_mode=pl.Buffered(1))


def _params(n_axes):
    return pltpu.CompilerParams(
        dimension_semantics=("arbitrary",) * n_axes, vmem_limit_bytes=VMEM_LIMIT)


def _log_sigmoid(x):
    return jnp.minimum(x, 0.0) - jnp.log1p(jnp.exp(-jnp.abs(x)))


def _silu(x):
    return x * jax.nn.sigmoid(x)


def _pack_words(val):
    half = val.shape[1] // 2
    lo = lax.bitcast_convert_type(val[:, :half].astype(BF16).astype(F32), I32)
    hi = lax.bitcast_convert_type(val[:, half:].astype(BF16).astype(F32), I32)
    return (hi & HIGH_HALF) | lax.shift_right_logical(lo, 16)


def _unpack_words(word):
    lo = lax.bitcast_convert_type(lax.shift_left(word, 16), F32)
    hi = lax.bitcast_convert_type(word & HIGH_HALF, F32)
    return lo, hi


def _row_tile_copies(hbm_rows, row0, tile, sem, to_hbm):
    n = tile.shape[0]
    copies = []
    for c in range(SUBLANES):
        hbm = hbm_rows.at[pl.ds(row0, n), c, :]
        vmem = tile.at[:, pl.ds(c * LANES, LANES)]
        copies.append(pltpu.make_async_copy(vmem, hbm, sem) if to_hbm
                      else pltpu.make_async_copy(hbm, vmem, sem))
    return copies


def _start_all(copies):
    for cp in copies:
        cp.start()


def _wait_all(copies):
    for cp in copies:
        cp.wait()


def _adaln_body(c_ref, w_ref, b_ref, o_ref):
    s = _silu(c_ref[...])
    o_ref[...] = jnp.dot(s.astype(BF16), w_ref[...].astype(BF16),
                         preferred_element_type=F32) + b_ref[...]


def _adaln(cc, w, b):
    d, n6 = w.shape
    return pl.pallas_call(
        _adaln_body,
        grid=(n6 // ADALN_TILE,),
        in_specs=[pl.BlockSpec((SUBLANES, d), lambda j: (0, 0)),
                  pl.BlockSpec((d, ADALN_TILE), lambda j: (0, j)),
                  pl.BlockSpec((1, ADALN_TILE), lambda j: (0, j))],
        out_specs=pl.BlockSpec((SUBLANES, ADALN_TILE), lambda j: (0, j)),
        out_shape=jax.ShapeDtypeStruct((SUBLANES, n6), F32),
        compiler_params=_params(1),
        name="adaln",
    )(cc, w, b)


def _norm_mod(x, g, shift, scale):
    y = x * lax.rsqrt(jnp.mean(x * x, axis=-1, keepdims=True) + EPS) * g
    return y * (1.0 + scale) + shift


def _gate_prep(xb, wg_ref, gb_ref, g_ref, gt_ref):
    tm = xb.shape[0]
    gg = jnp.dot(xb, wg_ref[...], preferred_element_type=F32) + gb_ref[...]
    lane = lax.broadcasted_iota(I32, (tm, LANES), 1)
    is_f = (lane & N_HEADS) != 0
    is_bwd = (lane & (2 * N_HEADS)) != 0
    lf = jnp.where(is_f, _log_sigmoid(gg), 0.0)
    r = lax.broadcasted_iota(I32, (tm, tm), 0)
    c = lax.broadcasted_iota(I32, (tm, tm), 1)
    same = (r // CHUNK) == (c // CHUNK)
    tri_l = jnp.where(same & (c <= r), 1.0, 0.0).astype(F32)
    tri_u = jnp.where(same & (c >= r), 1.0, 0.0).astype(F32)
    pre = jnp.dot(tri_l, lf, precision=_HIGHEST, preferred_element_type=F32)
    suf = jnp.dot(tri_u, lf, precision=_HIGHEST, preferred_element_type=F32)
    out = jnp.where(is_f, jnp.where(is_bwd, suf, pre), gg)
    g_ref[...] = out[:, :N_GATES]
    gt_ref[...] = out.T[:N_GATES, :]


def _project_transposed(wt_ref, xb):
    return lax.dot_general(wt_ref[...], xb, (((1,), (1,)), ((), ())),
                           preferred_element_type=F32).astype(BF16)


def _inproj_body(x_ref, sh_ref, sc_ref, g1_ref, w_ref, wkt_ref, wg_ref, gb_ref, cw_ref,
                 conv_ref, q_ref, k_ref, v_ref, o_ref, g_ref, gt_ref, *, conv_dim, qk_all, v_all):
    tm = x_ref.shape[0]
    xb = _norm_mod(x_ref[...], g1_ref[...], sh_ref[0], sc_ref[0]).astype(BF16)

    def proj(lo, width):
        return jnp.dot(xb, w_ref[:, lo:lo + width], preferred_element_type=F32)

    u = proj(conv_dim, conv_dim) * proj(2 * conv_dim, conv_dim)
    pos = lax.broadcasted_iota(I32, (tm, 1), 0) % GRID_W
    um = jnp.where(pos == 0, 0.0, pltpu.roll(u, 1, axis=0))
    up = jnp.where(pos == GRID_W - 1, 0.0, pltpu.roll(u, tm - 1, axis=0))
    y = um * cw_ref[0:1, :] + u * cw_ref[1:2, :] + up * cw_ref[2:3, :]
    conv_ref[...] = (proj(0, conv_dim) * y).astype(BF16)

    off = 3 * conv_dim
    qscale = (qk_all // N_HEADS) ** -0.5
    q_ref[...] = (proj(off, qk_all) * qscale).astype(BF16)
    k_ref[...] = _project_transposed(wkt_ref, xb)
    v_ref[...] = proj(off + 2 * qk_all, v_all).astype(BF16)
    o_ref[...] = jax.nn.sigmoid(proj(off + 2 * qk_all + v_all, v_all)).astype(BF16)
    _gate_prep(xb, wg_ref, gb_ref, g_ref, gt_ref)


def _inproj_ctx_body(x_ref, sh_ref, sc_ref, g1_ref, w_ref, wkt_ref, wg_ref, gb_ref,
                     k_ref, v_ref, g_ref, gt_ref):
    xb = _norm_mod(x_ref[...], g1_ref[...], sh_ref[0], sc_ref[0]).astype(BF16)
    k_ref[...] = _project_transposed(wkt_ref, xb)
    v_ref[...] = jnp.dot(xb, w_ref[...], preferred_element_type=F32).astype(BF16)
    _gate_prep(xb, wg_ref, gb_ref, g_ref, gt_ref)


def _mod_spec(part, tiles_per_row, fixed_row=None):
    def index(i):
        row = fixed_row if fixed_row is not None else i // tiles_per_row
        return (row * 6 + part, 0, 0)

    return index


def _inproj(x2d, mod, g1, w_main, w_kt, w_gate, gate_b, conv_w, rows_per_batch, conv_dim, qk_all, v_all):
    n, d = x2d.shape
    tm = ROW_TILE
    tiles_per_batch = rows_per_batch // tm
    row = lambda i: (i, 0)
    mod_block = (1, 1, d)
    out_shapes = (
        jax.ShapeDtypeStruct((n, conv_dim), BF16),
        jax.ShapeDtypeStruct((n, qk_all), BF16),
        jax.ShapeDtypeStruct((qk_all, n), BF16),
        jax.ShapeDtypeStruct((n, v_all), BF16),
        jax.ShapeDtypeStruct((n, v_all), BF16),
        jax.ShapeDtypeStruct((n, N_GATES), F32),
        jax.ShapeDtypeStruct((N_GATES, n), F32),
    )
    out_specs = (
        pl.BlockSpec((tm, conv_dim), row),
        pl.BlockSpec((tm, qk_all), row),
        pl.BlockSpec((qk_all, tm), lambda i: (0, i)),
        pl.BlockSpec((tm, v_all), row),
        pl.BlockSpec((tm, v_all), row),
        pl.BlockSpec((tm, N_GATES), row),
        pl.BlockSpec((N_GATES, tm), lambda i: (0, i)),
    )
    return pl.pallas_call(
        functools.partial(_inproj_body, conv_dim=conv_dim, qk_all=qk_all, v_all=v_all),
        grid=(n // tm,),
        in_specs=[pl.BlockSpec((tm, d), row),
                  pl.BlockSpec(mod_block, _mod_spec(0, tiles_per_batch)),
                  pl.BlockSpec(mod_block, _mod_spec(1, tiles_per_batch)),
                  _resident(g1.shape), _resident(w_main.shape), _resident(w_kt.shape),
                  _resident(w_gate.shape), _resident(gate_b.shape), _resident(conv_w.shape)],
        out_specs=out_specs,
        out_shape=out_shapes,
        compiler_params=_params(1),
        name="inproj",
    )(x2d, mod, mod, g1, w_main, w_kt, w_gate, gate_b, conv_w)


def _inproj_ctx(c2d, mod, g1, w_v, w_kt, w_gate, gate_b, ctx_mod_row):
    n, d = c2d.shape
    tm = ROW_TILE
    row = lambda i: (i, 0)
    mod_block = (1, 1, d)
    qk_all, v_all = w_kt.shape[0], w_v.shape[1]
    return pl.pallas_call(
        _inproj_ctx_body,
        grid=(n // tm,),
        in_specs=[pl.BlockSpec((tm, d), row),
                  pl.BlockSpec(mod_block, _mod_spec(0, 1, ctx_mod_row)),
                  pl.BlockSpec(mod_block, _mod_spec(1, 1, ctx_mod_row)),
                  _resident(g1.shape), _resident(w_v.shape), _resident(w_kt.shape),
                  _resident(w_gate.shape), _resident(gate_b.shape)],
        out_specs=(pl.BlockSpec((qk_all, tm), lambda i: (0, i)), pl.BlockSpec((tm, v_all), row),
                   pl.BlockSpec((tm, N_GATES), row), pl.BlockSpec((N_GATES, tm), lambda i: (0, i))),
        out_shape=(jax.ShapeDtypeStruct((qk_all, n), BF16), jax.ShapeDtypeStruct((n, v_all), BF16),
                   jax.ShapeDtypeStruct((n, N_GATES), F32), jax.ShapeDtypeStruct((N_GATES, n), F32)),
        compiler_params=_params(1),
        name="inproj_ctx",
    )(c2d, mod, mod, g1, w_v, w_kt, w_gate, gate_b)


def _with_ones(v):
    return jnp.concatenate([v, jnp.ones((v.shape[0], LANES), v.dtype)], axis=1)


def _mlstm_state_update(h, direction, kt_ref, v_ref, gt_ref, s_ref, m_ref, qk, vh):
    ci = direction * 2 * N_HEADS + h
    cb = ci + N_HEADS
    last = 0 if direction else CHUNK - 1
    kt = kt_ref[h * qk:(h + 1) * qk, :].astype(F32)
    va = _with_ones(v_ref[:, h * vh:(h + 1) * vh])
    b_last = gt_ref[cb:cb + 1, last:last + 1]
    m_prev = m_ref[h][0:1, 0:1]
    g_r = b_last - gt_ref[cb:cb + 1, :] + gt_ref[ci:ci + 1, :]
    m_new = jnp.maximum(b_last + m_prev, jnp.max(g_r, axis=1, keepdims=True))
    a = jnp.exp(b_last + m_prev - m_new)
    kw = (kt * jnp.exp(g_r - m_new)).astype(BF16)
    s_ref[h] = a * s_ref[h] + jnp.dot(kw, va, preferred_element_type=F32)
    m_ref[h] = jnp.broadcast_to(m_new, m_ref.shape[1:])


def _mlstm_head_output(h, direction, q_ref, kt_ref, v_ref, g_ref, gt_ref, s_ref, m_ref, qk, vh):
    ci = direction * 2 * N_HEADS + h
    cb = ci + N_HEADS
    q = q_ref[:, h * qk:(h + 1) * qk]
    kt = kt_ref[h * qk:(h + 1) * qk, :]
    va = _with_ones(v_ref[:, h * vh:(h + 1) * vh])
    ig_r = gt_ref[ci:ci + 1, :]
    b_r = gt_ref[cb:cb + 1, :]
    b_c = g_ref[:, cb:cb + 1]
    m_prev = m_ref[h][0:1, 0:1]
    row = lax.broadcasted_iota(I32, (CHUNK, CHUNK), 0)
    col = lax.broadcasted_iota(I32, (CHUNK, CHUNK), 1)
    mask = (col >= row) if direction else (col <= row)
    dm = jnp.where(mask, b_c + (ig_r - b_r), _NEG_INF)
    inter = b_c + m_prev
    m_t = jnp.maximum(inter, jnp.max(dm, axis=1, keepdims=True))
    w_inter = jnp.exp(inter - m_t)
    s = jnp.dot(q, kt, preferred_element_type=F32) * jnp.exp(dm - m_t)
    intra = jnp.dot(s.astype(BF16), va, preferred_element_type=F32)
    carried = jnp.dot(q, s_ref[h].astype(BF16), preferred_element_type=F32)
    num = intra[:, 0:vh] + w_inter * carried[:, 0:vh]
    den = intra[:, vh:vh + 1] + w_inter * carried[:, vh:vh + 1]
    return num / jnp.maximum(jnp.abs(den), jnp.exp(-m_t))


def _mlstm_body(*refs, direction, bsz, n_ctx_chunks, qk, vh):
    q_ref, v_ref, g_ref, vc_ref = refs[0:4]
    kt_refs, gt_refs = refs[4:4 + bsz], refs[4 + bsz:4 + 2 * bsz]
    ktc_refs, gtc_refs = refs[4 + 2 * bsz:4 + 3 * bsz], refs[4 + 3 * bsz:4 + 4 * bsz]
    rest = refs[4 + 4 * bsz:]
    if direction:
        out_ref, s_ref, m_ref = rest
    else:
        hb_ref, og_ref, hg_ref, out_ref, s_ref, m_ref = rest
    step = pl.program_id(0)

    @pl.when(step == 0)
    def _():
        s_ref[...] = jnp.zeros_like(s_ref)
        m_ref[...] = jnp.full_like(m_ref, _NEG_INF)

    @pl.when(step < n_ctx_chunks)
    def _():
        for b in range(bsz):
            for h in range(N_HEADS):
                _mlstm_state_update(h, direction, ktc_refs[b], vc_ref.at[b], gtc_refs[b],
                                    s_ref.at[b], m_ref.at[b], qk, vh)

    @pl.when(step >= n_ctx_chunks)
    def _():
        for b in range(bsz):
            for h in range(N_HEADS):
                hh = _mlstm_head_output(h, direction, q_ref.at[b], kt_refs[b], v_ref.at[b], g_ref.at[b],
                                        gt_refs[b], s_ref.at[b], m_ref.at[b], qk, vh)
                cols = slice(h * vh, (h + 1) * vh)
                if direction:
                    out_ref[b, :, cols] = hh
                else:
                    hs = hh + hb_ref[b, :, cols]
                    hs = hs * lax.rsqrt(jnp.mean(hs * hs, axis=-1, keepdims=True) + EPS)
                    out_ref[b, :, cols] = (hs * hg_ref[:, cols]
                                           * og_ref[b, :, cols].astype(F32)).astype(BF16)
                _mlstm_state_update(h, direction, kt_refs[b], v_ref.at[b], gt_refs[b],
                                    s_ref.at[b], m_ref.at[b], qk, vh)


def _mlstm(direction, q, kt, v, g, gt, ktc, vc, gtc, extra, bsz, head_g=None):
    n, qk_all = q.shape
    v_all = v.shape[1]
    qk, vh = qk_all // N_HEADS, v_all // N_HEADS
    seq = n // bsz
    nc = seq // CHUNK
    ncc = vc.shape[0] // bsz // CHUNK

    def lat(s):
        j = jnp.clip(s - ncc, 0, nc - 1)
        return nc - 1 - j if direction else j

    def ctx(s):
        j = jnp.clip(s, 0, ncc - 1)
        return ncc - 1 - j if direction else j

    def per_batch(a):
        return a.reshape(bsz, a.shape[0] // bsz, a.shape[1])

    lat_blk = lambda c: pl.BlockSpec((bsz, CHUNK, c), lambda s: (0, lat(s), 0))
    in_specs = [lat_blk(qk_all), lat_blk(v_all), lat_blk(N_GATES),
                pl.BlockSpec((bsz, CHUNK, v_all), lambda s: (0, ctx(s), 0))]
    args = [per_batch(q), per_batch(v), per_batch(g), per_batch(vc)]
    for arr, rows, n_chunks, pos in ((kt, qk_all, nc, lat), (gt, N_GATES, nc, lat),
                                     (ktc, qk_all, ncc, ctx), (gtc, N_GATES, ncc, ctx)):
        for b in range(bsz):
            in_specs.append(pl.BlockSpec((rows, CHUNK), lambda s, b=b, n_chunks=n_chunks, pos=pos:
                                         (0, b * n_chunks + pos(s))))
            args.append(arr)
    if direction:
        out_dtype = F32
    else:
        hb, og = extra
        in_specs += [lat_blk(v_all), lat_blk(v_all), pl.BlockSpec((1, v_all), lambda s: (0, 0))]
        args += [per_batch(hb), per_batch(og), head_g]
        out_dtype = BF16
    out = pl.pallas_call(
        functools.partial(_mlstm_body, direction=direction, bsz=bsz, n_ctx_chunks=ncc, qk=qk, vh=vh),
        grid=(ncc + nc,),
        in_specs=in_specs,
        out_specs=lat_blk(v_all),
        out_shape=jax.ShapeDtypeStruct((bsz, seq, v_all), out_dtype),
        scratch_shapes=[pltpu.VMEM((bsz, N_HEADS, qk, vh + LANES), F32),
                        pltpu.VMEM((bsz, N_HEADS, SUBLANES, LANES), F32)],
        compiler_params=_params(1),
        name="mlstm_bwd" if direction else "mlstm_fwd",
    )(*args)
    return out.reshape(n, v_all)


def _outproj_body(conv_ref, ml_ref, x_ref, gt1_ref, sh2_ref, sc2_ref, g2_ref, wo_ref, wr_ref, br_ref,
                  x1_ref, idx_ref, gate_ref, rank_ref, cnt_ref, h_hbm, carry_ref, hw, hsem):
    tm = x_ref.shape[0]
    half = conv_ref.shape[1]
    step = pl.program_id(0)
    buf = step % 2

    def h_out(i, s):
        return _row_tile_copies(h_hbm, i * tm, hw.at[s], hsem.at[s], to_hbm=True)

    @pl.when(step == 0)
    def _():
        carry_ref[...] = jnp.zeros_like(carry_ref)

    @pl.when(step >= 2)
    def _():
        _wait_all(h_out(step - 2, buf))

    y = (jnp.dot(conv_ref[...], wo_ref[0:half, :], preferred_element_type=F32)
         + jnp.dot(ml_ref[...], wo_ref[half:2 * half, :], preferred_element_type=F32))
    x1 = x_ref[...] + gt1_ref[0] * y
    x1_ref[...] = x1
    hn = _norm_mod(x1, g2_ref[...], sh2_ref[0], sc2_ref[0])
    hw[buf] = _pack_words(hn)
    _start_all(h_out(step, buf))

    h_hi = hn.astype(BF16)
    h_lo = (hn - h_hi.astype(F32)).astype(BF16)
    parts = (jnp.dot(h_hi, wr_ref[...], preferred_element_type=F32)
             + jnp.dot(h_lo, wr_ref[...], preferred_element_type=F32))
    scores = jax.nn.sigmoid(parts + pltpu.roll(parts, N_EXPERTS, axis=1))
    lane = lax.broadcasted_iota(I32, (tm, LANES), 1).astype(F32)
    biased = jnp.where(lane < N_EXPERTS, scores + br_ref[...], _NEG_INF)
    onehot = jnp.zeros((tm, LANES), F32)
    picks, sels = [], []
    for _ in range(TOP_K):
        mx = jnp.max(biased, axis=1, keepdims=True)
        pick = jnp.min(jnp.where(biased == mx, lane, float(LANES)), axis=1, keepdims=True)
        hit = lane == pick
        sels.append(jnp.sum(jnp.where(hit, scores, 0.0), axis=1, keepdims=True))
        picks.append(pick)
        biased = jnp.where(hit, _NEG_INF, biased)
        onehot = onehot + hit.astype(F32)
    total = sels[0]
    for s in sels[1:]:
        total = total + s

    r = lax.broadcasted_iota(I32, (tm, tm), 0)
    c = lax.broadcasted_iota(I32, (tm, tm), 1)
    strict = jnp.where(c < r, 1.0, 0.0).astype(BF16)
    before = jnp.dot(strict, onehot.astype(BF16), preferred_element_type=F32) + carry_ref[...]
    slot = lax.broadcasted_iota(I32, (tm, SUBLANES), 1)
    idx_out = jnp.zeros((tm, LANES), F32)
    rank_out = jnp.zeros((tm, LANES), F32)
    gate_out = jnp.zeros((tm, SUBLANES), F32)
    for j in range(TOP_K):
        rank = jnp.sum(jnp.where(lane == picks[j], before, 0.0), axis=1, keepdims=True)
        idx_out = jnp.where(lane == float(j), picks[j], idx_out)
        rank_out = jnp.where(lane == float(j), rank, rank_out)
        gate_out = jnp.where(slot == j, sels[j] / total * ROUTED_SCALE, gate_out)
    idx_ref[...] = idx_out.T[:SUBLANES, :].astype(I32)
    rank_ref[...] = rank_out.T[:SUBLANES, :].astype(I32)
    gate_ref[...] = gate_out
    carry_ref[...] = carry_ref[...] + jnp.sum(onehot, axis=0, keepdims=True)
    cnt_ref[...] = jnp.broadcast_to(carry_ref[...], cnt_ref.shape).astype(I32)

    @pl.when(step == pl.num_programs(0) - 1)
    def _():
        @pl.when(step >= 1)
        def _():
            _wait_all(h_out(step - 1, 1 - buf))
        _wait_all(h_out(step, buf))


def _outproj(conv, ml, x2d, mod, g2, w_out, w_router, b_router, rows_per_batch):
    n, d = x2d.shape
    tm = ROW_TILE
    tiles_per_batch = rows_per_batch // tm
    row = lambda i: (i, 0)
    mod_block = (1, 1, d)
    half = conv.shape[1]
    return pl.pallas_call(
        _outproj_body,
        grid=(n // tm,),
        in_specs=[pl.BlockSpec((tm, half), row), pl.BlockSpec((tm, half), row), pl.BlockSpec((tm, d), row),
                  pl.BlockSpec(mod_block, _mod_spec(2, tiles_per_batch)),
                  pl.BlockSpec(mod_block, _mod_spec(3, tiles_per_batch)),
                  pl.BlockSpec(mod_block, _mod_spec(4, tiles_per_batch)),
                  _resident(g2.shape), _resident(w_out.shape), _resident(w_router.shape),
                  _resident(b_router.shape)],
        out_specs=(pl.BlockSpec((tm, d), row),
                   pl.BlockSpec((SUBLANES, tm), lambda i: (0, i)), pl.BlockSpec((tm, SUBLANES), row),
                   pl.BlockSpec((SUBLANES, tm), lambda i: (0, i)),
                   pl.BlockSpec((SUBLANES, LANES), lambda i: (0, 0)),
                   pl.BlockSpec(memory_space=pl.ANY)),
        out_shape=(jax.ShapeDtypeStruct((n, d), F32),
                   jax.ShapeDtypeStruct((SUBLANES, n), I32), jax.ShapeDtypeStruct((n, SUBLANES), F32),
                   jax.ShapeDtypeStruct((SUBLANES, n), I32),
                   jax.ShapeDtypeStruct((SUBLANES, LANES), I32),
                   jax.ShapeDtypeStruct((n, SUBLANES, LANES), I32)),
        scratch_shapes=[pltpu.VMEM((1, LANES), F32), pltpu.VMEM((2, tm, d // 2), I32),
                        pltpu.SemaphoreType.DMA((2,))],
        compiler_params=_params(1),
        name="outproj_router",
    )(conv, ml, x2d, mod, mod, mod, g2, w_out, w_router, b_router)


def _sc_workers():
    info = plsc.get_sparse_core_info()
    return info.num_cores, info.num_cores * info.num_subcores


def _sc_dispatch(h_rows, dest_chunks, n_slots):
    n_tok = h_rows.shape[0]
    n_cores, n_workers = _sc_workers()
    per_worker = n_tok // (n_workers * SC_CHUNK)
    assert per_worker * n_workers * SC_CHUNK == n_tok
    mesh = plsc.VectorSubcoreMesh(core_axis_name="c", subcore_axis_name="s")

    @functools.partial(
        pl.kernel, mesh=mesh,
        out_type=jax.ShapeDtypeStruct((n_slots,) + h_rows.shape[1:], h_rows.dtype),
        scratch_types=[pltpu.VMEM((TOP_K, SC_CHUNK), I32),
                       pltpu.VMEM((SC_CHUNK,) + h_rows.shape[1:], h_rows.dtype)],
    )
    def dispatch(h_hbm, dest_hbm, out_hbm, idx_v, rows_v):
        wid = lax.axis_index("s") * n_cores + lax.axis_index("c")

        @pl.loop(0, per_worker)
        def _(i):
            chunk = wid * per_worker + i
            pltpu.sync_copy(dest_hbm.at[chunk], idx_v)
            pltpu.sync_copy(h_hbm.at[pl.ds(chunk * SC_CHUNK, SC_CHUNK)], rows_v)
            for k in range(TOP_K):
                pltpu.sync_copy(rows_v, out_hbm.at[idx_v.at[k]])

    return dispatch(h_rows, dest_chunks)


def _sc_combine(y_sorted, dest_chunks, n_tok):
    n_cores, n_workers = _sc_workers()
    per_worker = n_tok // (n_workers * SC_CHUNK)
    mesh = plsc.VectorSubcoreMesh(core_axis_name="c", subcore_axis_name="s")

    @functools.partial(
        pl.kernel, mesh=mesh,
        out_type=jax.ShapeDtypeStruct((TOP_K, n_tok) + y_sorted.shape[1:], y_sorted.dtype),
        scratch_types=[pltpu.VMEM((TOP_K, SC_CHUNK), I32),
                       pltpu.VMEM((SC_CHUNK,) + y_sorted.shape[1:], y_sorted.dtype)],
    )
    def combine(y_hbm, dest_hbm, out_hbm, idx_v, rows_v):
        wid = lax.axis_index("s") * n_cores + lax.axis_index("c")

        @pl.loop(0, per_worker)
        def _(i):
            chunk = wid * per_worker + i
            pltpu.sync_copy(dest_hbm.at[chunk], idx_v)
            for k in range(TOP_K):
                pltpu.sync_copy(y_hbm.at[idx_v.at[k]], rows_v)
                pltpu.sync_copy(rows_v, out_hbm.at[k, pl.ds(chunk * SC_CHUNK, SC_CHUNK)])

    return combine(y_sorted, dest_chunks)


def _moe_body(ord_ref, order_ref, glo_ref, ghi_ref, tot_ref, nb_ref,
              x_hbm, wg_hbm, wu_hbm, wd_hbm, y_hbm,
              wgu, wd, stage_a, stage_d, xw, yw, wsem, xsem, ysem, *, d_expert):
    b = pl.program_id(0)
    nb = nb_ref[0]
    total = tot_ref[0]
    d_model = wgu.shape[1]
    rows_a = d_model // WEIGHT_PARTS
    rows_d = d_expert // WEIGHT_PARTS

    def part_copies(g):
        e = order_ref[lax.shift_right_logical(g, PART_SHIFT)]
        i = g & (WEIGHT_PARTS - 1)
        s = lax.rem(g, WEIGHT_RING)
        return (pltpu.make_async_copy(wg_hbm.at[e, pl.ds(i * rows_a, rows_a)], stage_a.at[s, 0],
                                      wsem.at[s, 0]),
                pltpu.make_async_copy(wu_hbm.at[e, pl.ds(i * rows_a, rows_a)], stage_a.at[s, 1],
                                      wsem.at[s, 1]),
                pltpu.make_async_copy(wd_hbm.at[e, pl.ds(i * rows_d, rows_d)], stage_d.at[s],
                                      wsem.at[s, 2]))

    def start_part(g):
        for cp in part_copies(g):
            cp.start()

    def wait_part(g):
        for cp in part_copies(g):
            cp.wait()

    def cast_part(g):
        i = g & (WEIGHT_PARTS - 1)
        s = lax.rem(g, WEIGHT_RING)
        par = lax.shift_right_logical(g, PART_SHIFT) & 1
        ra = pl.multiple_of(i * rows_a, rows_a)
        rd = pl.multiple_of(i * rows_d, rows_d)
        wgu[par, pl.ds(ra, rows_a), 0:d_expert] = stage_a[s, 0].astype(BF16)
        wgu[par, pl.ds(ra, rows_a), d_expert:2 * d_expert] = stage_a[s, 1].astype(BF16)
        wd[par, pl.ds(rd, rows_d), :] = stage_d[s].astype(BF16)

    def refill(g):
        @pl.when(g + WEIGHT_RING < total)
        def _():
            start_part(g + WEIGHT_RING)

    def cast_parts(lo, hi):
        def body(g, carry):
            wait_part(g)
            cast_part(g)
            refill(g)
            return carry
        lax.fori_loop(lo, hi, body, 0)

    slot = b % 2

    def x_in(blk, s):
        return _row_tile_copies(x_hbm, blk * MOE_BLOCK, xw.at[s], xsem.at[s], to_hbm=False)

    def y_out(blk, s):
        return _row_tile_copies(y_hbm, blk * MOE_BLOCK, yw.at[s], ysem.at[s], to_hbm=True)

    @pl.when(b == 0)
    def _():
        _start_all(x_in(0, 0))
        for g in range(WEIGHT_RING):
            start_part(g)
        cast_parts(0, WEIGHT_PARTS)

    @pl.when(b + 1 < nb)
    def _():
        _start_all(x_in(b + 1, 1 - slot))

    @pl.when(b < nb)
    def _():
        par = ord_ref[b] & 1
        _wait_all(x_in(b, slot))

        @pl.when(b >= 2)
        def _():
            _wait_all(y_out(b - 2, slot))

        x = jnp.concatenate(_unpack_words(xw[slot]), axis=1).astype(BF16)
        gu = jnp.dot(x, wgu[par], preferred_element_type=F32)
        hb = (_silu(gu[:, 0:d_expert]) * gu[:, d_expert:2 * d_expert]).astype(BF16)
        yw[slot] = _pack_words(jnp.dot(hb, wd[par], preferred_element_type=F32))
        _start_all(y_out(b, slot))
        cast_parts(glo_ref[b], ghi_ref[b])

        @pl.when(b == nb - 1)
        def _():
            @pl.when(b >= 1)
            def _():
                _wait_all(y_out(b - 1, 1 - slot))
            _wait_all(y_out(b, slot))


def _moe(x_sorted, we_gate, we_up, we_down, tables):
    d, d_expert = we_gate.shape[1], we_gate.shape[2]
    nb_max = x_sorted.shape[0] // MOE_BLOCK
    any_spec = pl.BlockSpec(memory_space=pl.ANY)
    grid_spec = pltpu.PrefetchScalarGridSpec(
        num_scalar_prefetch=len(tables),
        grid=(nb_max,),
        in_specs=[any_spec, any_spec, any_spec, any_spec],
        out_specs=any_spec,
        scratch_shapes=[pltpu.VMEM((2, d, 2 * d_expert), BF16),
                        pltpu.VMEM((2, d_expert, d), BF16),
                        pltpu.VMEM((WEIGHT_RING, 2, d // WEIGHT_PARTS, d_expert), F32),
                        pltpu.VMEM((WEIGHT_RING, d_expert // WEIGHT_PARTS, d), F32),
                        pltpu.VMEM((2, MOE_BLOCK, d // 2), I32),
                        pltpu.VMEM((2, MOE_BLOCK, d // 2), I32),
                        pltpu.SemaphoreType.DMA((WEIGHT_RING, 3)),
                        pltpu.SemaphoreType.DMA((2,)),
                        pltpu.SemaphoreType.DMA((2,))],
    )
    return pl.pallas_call(
        functools.partial(_moe_body, d_expert=d_expert),
        grid_spec=grid_spec,
        out_shape=jax.ShapeDtypeStruct(x_sorted.shape, x_sorted.dtype),
        compiler_params=pltpu.CompilerParams(
            dimension_semantics=("arbitrary",), vmem_limit_bytes=MOE_VMEM_LIMIT),
        name="moe_routed",
    )(*tables, x_sorted, we_gate, we_up, we_down)


def _final_body(x1_ref, gate_ref, gt2_ref, wsgu_ref, wsd_ref, fg_ref, h_hbm, y_hbm, out_ref,
                hw, yw, sem, *, d_shared):
    tm = x1_ref.shape[0]
    step = pl.program_id(0)
    slot = step % 2

    def rows_in(i, s):
        copies = _row_tile_copies(h_hbm, i * tm, hw.at[s], sem.at[s], to_hbm=False)
        for k in range(TOP_K):
            copies += _row_tile_copies(y_hbm.at[k], i * tm, yw.at[s, k], sem.at[s], to_hbm=False)
        return copies

    @pl.when(step == 0)
    def _():
        _start_all(rows_in(0, 0))

    @pl.when(step + 1 < pl.num_programs(0))
    def _():
        _start_all(rows_in(step + 1, 1 - slot))

    _wait_all(rows_in(step, slot))
    h = jnp.concatenate(_unpack_words(hw[slot]), axis=1).astype(BF16)
    gu = jnp.dot(h, wsgu_ref[...], preferred_element_type=F32)
    hb = (_silu(gu[:, 0:d_shared]) * gu[:, d_shared:2 * d_shared]).astype(BF16)
    acc = jnp.dot(hb, wsd_ref[...], preferred_element_type=F32)
    for k in range(TOP_K):
        acc = acc + gate_ref[:, k:k + 1] * jnp.concatenate(_unpack_words(yw[slot, k]), axis=1)
    x2 = x1_ref[...] + gt2_ref[0] * acc
    out_ref[...] = x2 * lax.rsqrt(jnp.mean(x2 * x2, axis=-1, keepdims=True) + EPS) * fg_ref[...]


def _final(h_rows, x1, y_tok, gates, mod, ws_gu, ws_d, final_g, rows_per_batch):
    n, d = x1.shape
    tm = ROW_TILE
    tiles_per_batch = rows_per_batch // tm
    row = lambda i: (i, 0)
    any_spec = pl.BlockSpec(memory_space=pl.ANY)
    return pl.pallas_call(
        functools.partial(_final_body, d_shared=ws_d.shape[0]),
        grid=(n // tm,),
        in_specs=[pl.BlockSpec((tm, d), row), pl.BlockSpec((tm, SUBLANES), row),
                  pl.BlockSpec((1, 1, d), _mod_spec(5, tiles_per_batch)),
                  _resident(ws_gu.shape), _resident(ws_d.shape), _resident(final_g.shape),
                  any_spec, any_spec],
        out_specs=pl.BlockSpec((tm, d), row),
        out_shape=jax.ShapeDtypeStruct((n, d), F32),
        scratch_shapes=[pltpu.VMEM((2, tm, d // 2), I32), pltpu.VMEM((2, TOP_K, tm, d // 2), I32),
                        pltpu.SemaphoreType.DMA((2,))],
        compiler_params=_params(1),
        name="shared_combine_final",
    )(x1, gates, mod, ws_gu, ws_d, final_g, h_rows, y_tok)


def _routing_tables(idx, rank, counts, n_tok):
    nb_max = -(-(n_tok * TOP_K) // MOE_BLOCK) + N_EXPERTS
    nblk = (counts + MOE_BLOCK - 1) // MOE_BLOCK
    blk_end = jnp.cumsum(nblk)
    blk_start = blk_end - nblk
    experts = jnp.arange(N_EXPERTS, dtype=I32)[:, None, None]
    first_slot = (blk_start * MOE_BLOCK)[:, None, None]
    dest = jnp.sum(jnp.where(idx[None] == experts, first_slot, 0), axis=0) + rank
    dest_chunks = dest.reshape(TOP_K, n_tok // SC_CHUNK, SC_CHUNK).transpose(1, 0, 2)

    blocks = jnp.arange(nb_max, dtype=I32)
    block_e = jnp.minimum(jnp.searchsorted(blk_end, blocks, side="right"), N_EXPERTS - 1).astype(I32)
    nonempty = nblk > 0
    n_visited = jnp.sum(nonempty.astype(I32))
    ordinal_of = jnp.cumsum(nonempty.astype(I32)) - 1
    order = jnp.argsort(jnp.where(nonempty, 0, 1), stable=True).astype(I32)
    ordinal = ordinal_of[block_e]
    k_in_e = blocks - blk_start[block_e]
    nb_e = jnp.maximum(nblk[block_e], 1)
    live = (ordinal + 1 < n_visited) & (blocks < blk_end[-1])
    first = WEIGHT_PARTS * (ordinal + 1)
    lo = jnp.where(live, first + WEIGHT_PARTS * k_in_e // nb_e, 0)
    hi = jnp.where(live, first + WEIGHT_PARTS * (k_in_e + 1) // nb_e, 0)
    tables = (ordinal.astype(I32), order, lo.astype(I32), hi.astype(I32),
              (WEIGHT_PARTS * n_visited).reshape(1).astype(I32), blk_end[-1:].astype(I32))
    return tables, dest_chunks, nb_max * MOE_BLOCK


def kernel(x, c, ctx, c_ctx, norm1_g, norm2_g, w_ada, b_ada, w_in, conv_w, gate_b, head_g, w_out,
           w_router, b_router, we_gate, we_up, we_down, ws_gate, ws_up, ws_down, final_g):
    assert w_ada.shape[0] == 1, "single-layer block"
    bsz, seq, d = x.shape
    ctx_len = ctx.shape[1]
    n_tok = bsz * seq
    conv_dim = conv_w.shape[2]
    v_all = head_g.shape[1]
    qk_all = (w_in.shape[2] - 3 * conv_dim - 2 * v_all - N_GATES) // 2
    assert seq % ROW_TILE == 0 and ctx_len % ROW_TILE == 0 and ROW_TILE % GRID_W == 0
    assert bsz + 1 <= SUBLANES

    cc = jnp.zeros((SUBLANES, d), F32).at[:bsz].set(c).at[bsz].set(c_ctx)
    mod = _adaln(cc, w_ada[0], b_ada).reshape(SUBLANES * 6, 1, d)

    n_main = 3 * conv_dim + 2 * qk_all + 2 * v_all
    w_main = w_in[0, :, :n_main].astype(BF16)
    k_lo = 3 * conv_dim + qk_all
    w_kt = w_in[0, :, k_lo:k_lo + qk_all].T.astype(BF16)
    w_v = w_in[0, :, k_lo + qk_all:k_lo + qk_all + v_all].astype(BF16)
    w_gate = jnp.zeros((d, LANES), BF16).at[:, :N_GATES].set(w_in[0, :, n_main:].astype(BF16))
    gate_bias = jnp.zeros((1, LANES), F32).at[0, :N_GATES].set(gate_b[0].reshape(-1))

    x2d = x.reshape(n_tok, d)
    conv, q, kt, v, og, g, gt = _inproj(x2d, mod, norm1_g, w_main, w_kt, w_gate, gate_bias, conv_w[0],
                                       seq, conv_dim, qk_all, v_all)
    ktc, vc, _, gtc = _inproj_ctx(ctx.reshape(bsz * ctx_len, d), mod, norm1_g, w_v, w_kt, w_gate,
                                  gate_bias, bsz)

    h_bwd = _mlstm(1, q, kt, v, g, gt, ktc, vc, gtc, None, bsz)
    ml = _mlstm(0, q, kt, v, g, gt, ktc, vc, gtc, (h_bwd, og), bsz, head_g)

    assert 2 * N_EXPERTS == LANES
    w_r_hi = w_router[0].astype(BF16)
    w_r = jnp.concatenate([w_r_hi, (w_router[0] - w_r_hi.astype(F32)).astype(BF16)], axis=1)
    b_r = jnp.zeros((1, LANES), F32).at[0, :N_EXPERTS].set(b_router[0])
    x1, idx, gates, rank, cnt, h_rows = _outproj(conv, ml, x2d, mod, norm2_g, w_out[0].astype(BF16),
                                             w_r, b_r, seq)

    tables, dest_chunks, n_slots = _routing_tables(idx[:TOP_K], rank[:TOP_K], cnt[0, :N_EXPERTS], n_tok)
    x_sorted = _sc_dispatch(h_rows, dest_chunks, n_slots)
    y_sorted = _moe(x_sorted, we_gate[0], we_up[0], we_down[0], tables)
    y_tok = _sc_combine(y_sorted, dest_chunks, n_tok)

    ws_gu = jnp.concatenate([ws_gate[0], ws_up[0]], axis=1).astype(BF16)
    out = _final(h_rows, x1, y_tok, gates, mod, ws_gu, ws_down[0].astype(BF16),
                 final_g.reshape(1, d), seq)
    return out.reshape(bsz, seq, d)
```

```python
import functools

import jax
import jax.numpy as jnp
from jax import lax
from jax.experimental import pallas as pl
from jax.experimental.pallas import tpu as pltpu
from jax.experimental.pallas import tpu_sc as plsc

F32 = jnp.float32
BF16 = jnp.bfloat16
I32 = jnp.int32

N_HEADS = 4
GRID_W = 64
CHUNK = 128
TOP_K = 6
N_EXPERTS = 64
ROUTED_SCALE = 2.446
EPS = 1e-6
N_GATES = 4 * N_HEADS

LANES = 128
SUBLANES = 8
MOE_BLOCK = 256
ROW_TILE = 256
ADALN_TILE = 1024
CAST_ROWS = 128
WEIGHT_PARTS = 8
PART_SHIFT = 3
WEIGHT_RING = 3
SC_CHUNK = 64
HIGH_HALF = -65536
VMEM_LIMIT = 56 * 1024 * 1024
MOE_VMEM_LIMIT = 62 * 1024 * 1024

_HIGHEST = lax.Precision.HIGHEST
_NEG_INF = float("-inf")


def _resident(shape):
    nd = len(shape)
    return pl.BlockSpec(shape, lambda *_: (0,) * nd, pipeline_mode=pl.Buffered(1))


def _params(n_axes):
    return pltpu.CompilerParams(
        dimension_semantics=("arbitrary",) * n_axes, vmem_limit_bytes=VMEM_LIMIT)


def _log_sigmoid(x):
    return jnp.minimum(x, 0.0) - jnp.log1p(jnp.exp(-jnp.abs(x)))


def _silu(x):
    return x * jax.nn.sigmoid(x)


def _pack_words(val):
    half = val.shape[1] // 2
    lo = lax.bitcast_convert_type(val[:, :half].astype(BF16).astype(F32), I32)
    hi = lax.bitcast_convert_type(val[:, half:].astype(BF16).astype(F32), I32)
    return (hi & HIGH_HALF) | lax.shift_right_logical(lo, 16)


def _unpack_words(word):
    lo = lax.bitcast_convert_type(lax.shift_left(word, 16), F32)
    hi = lax.bitcast_convert_type(word & HIGH_HALF, F32)
    return lo, hi


def _row_tile_copies(hbm_rows, row0, tile, sem, to_hbm):
    n = tile.shape[0]
    copies = []
    for c in range(SUBLANES):
        hbm = hbm_rows.at[pl.ds(row0, n), c, :]
        vmem = tile.at[:, pl.ds(c * LANES, LANES)]
        copies.append(pltpu.make_async_copy(vmem, hbm, sem) if to_hbm
                      else pltpu.make_async_copy(hbm, vmem, sem))
    return copies


def _load_cast(src, dst, stage, sem, dst_col0=0):
    _, rows, cols = stage.shape
    n_chunks = dst.shape[0] // rows
    assert n_chunks * rows == dst.shape[0]

    def chunk(c, s):
        return pltpu.make_async_copy(src.at[pl.ds(c * rows, rows), pl.ds(0, cols)], stage.at[s], sem.at[s])

    chunk(0, 0).start()

    def body(c, carry):
        s = c % 2

        @pl.when(c + 1 < n_chunks)
        def _():
            chunk(c + 1, 1 - s).start()

        chunk(c, s).wait()
        dst[pl.ds(pl.multiple_of(c * rows, rows), rows), dst_col0:dst_col0 + cols] = stage[s].astype(BF16)
        return carry

    lax.fori_loop(0, n_chunks, body, 0)


def _start_all(copies):
    for cp in copies:
        cp.start()


def _wait_all(copies):
    for cp in copies:
        cp.wait()


def _adaln_body(c_ref, w_ref, b_ref, o_ref):
    s = _silu(c_ref[...])
    o_ref[...] = jnp.dot(s.astype(BF16), w_ref[...].astype(BF16),
                         preferred_element_type=F32) + b_ref[...]


def _adaln(cc, w, b):
    d, n6 = w.shape
    return pl.pallas_call(
        _adaln_body,
        grid=(n6 // ADALN_TILE,),
        in_specs=[pl.BlockSpec((SUBLANES, d), lambda j: (0, 0)),
                  pl.BlockSpec((d, ADALN_TILE), lambda j: (0, j)),
                  pl.BlockSpec((1, ADALN_TILE), lambda j: (0, j))],
        out_specs=pl.BlockSpec((SUBLANES, ADALN_TILE), lambda j: (0, j)),
        out_shape=jax.ShapeDtypeStruct((SUBLANES, n6), F32),
        compiler_params=_params(1),
        name="adaln",
    )(cc, w, b)


def _norm_mod(x, g, shift, scale):
    y = x * lax.rsqrt(jnp.mean(x * x, axis=-1, keepdims=True) + EPS) * g
    return y * (1.0 + scale) + shift


def _gate_prep(xb, wg_ref, gb_ref, g_ref, gt_ref):
    tm = xb.shape[0]
    gg = jnp.dot(xb, wg_ref[...], preferred_element_type=F32) + gb_ref[...]
    lane = lax.broadcasted_iota(I32, (tm, LANES), 1)
    is_f = (lane & N_HEADS) != 0
    is_bwd = (lane & (2 * N_HEADS)) != 0
    lf = jnp.where(is_f, _log_sigmoid(gg), 0.0)
    r = lax.broadcasted_iota(I32, (tm, tm), 0)
    c = lax.broadcasted_iota(I32, (tm, tm), 1)
    same = (r // CHUNK) == (c // CHUNK)
    tri_l = jnp.where(same & (c <= r), 1.0, 0.0).astype(F32)
    tri_u = jnp.where(same & (c >= r), 1.0, 0.0).astype(F32)
    pre = jnp.dot(tri_l, lf, precision=_HIGHEST, preferred_element_type=F32)
    suf = jnp.dot(tri_u, lf, precision=_HIGHEST, preferred_element_type=F32)
    out = jnp.where(is_f, jnp.where(is_bwd, suf, pre), gg)
    g_ref[...] = out[:, :N_GATES]
    gt_ref[...] = out.T[:N_GATES, :]


def _project_transposed(wt_ref, xb):
    return lax.dot_general(wt_ref[...], xb, (((1,), (1,)), ((), ())),
                           preferred_element_type=F32).astype(BF16)


def _inproj_body(x_ref, sh_ref, sc_ref, g1_ref, wkt_ref, wg_ref, gb_ref, cw_ref, w_hbm,
                 conv_ref, q_ref, k_ref, v_ref, o_ref, g_ref, gt_ref, w_ref, stage, sem,
                 *, conv_dim, qk_all, v_all):
    tm = x_ref.shape[0]

    @pl.when(pl.program_id(0) == 0)
    def _():
        _load_cast(w_hbm, w_ref, stage, sem)

    xb = _norm_mod(x_ref[...], g1_ref[...], sh_ref[0], sc_ref[0]).astype(BF16)

    def proj(lo, width):
        return jnp.dot(xb, w_ref[:, lo:lo + width], preferred_element_type=F32)

    u = proj(conv_dim, conv_dim) * proj(2 * conv_dim, conv_dim)
    pos = lax.broadcasted_iota(I32, (tm, 1), 0) % GRID_W
    um = jnp.where(pos == 0, 0.0, pltpu.roll(u, 1, axis=0))
    up = jnp.where(pos == GRID_W - 1, 0.0, pltpu.roll(u, tm - 1, axis=0))
    y = um * cw_ref[0:1, :] + u * cw_ref[1:2, :] + up * cw_ref[2:3, :]
    conv_ref[...] = (proj(0, conv_dim) * y).astype(BF16)

    off = 3 * conv_dim
    qscale = (qk_all // N_HEADS) ** -0.5
    q_ref[...] = (proj(off, qk_all) * qscale).astype(BF16)
    k_ref[...] = _project_transposed(wkt_ref, xb)
    v_ref[...] = proj(off + 2 * qk_all, v_all).astype(BF16)
    o_ref[...] = jax.nn.sigmoid(proj(off + 2 * qk_all + v_all, v_all)).astype(BF16)
    _gate_prep(xb, wg_ref, gb_ref, g_ref, gt_ref)


def _inproj_ctx_body(x_ref, sh_ref, sc_ref, g1_ref, w_ref, wkt_ref, wg_ref, gb_ref,
                     k_ref, v_ref, g_ref, gt_ref):
    xb = _norm_mod(x_ref[...], g1_ref[...], sh_ref[0], sc_ref[0]).astype(BF16)
    k_ref[...] = _project_transposed(wkt_ref, xb)
    v_ref[...] = jnp.dot(xb, w_ref[...], preferred_element_type=F32).astype(BF16)
    _gate_prep(xb, wg_ref, gb_ref, g_ref, gt_ref)


def _mod_spec(part, tiles_per_row, fixed_row=None):
    def index(i):
        row = fixed_row if fixed_row is not None else i // tiles_per_row
        return (row * 6 + part, 0, 0)

    return index


def _inproj(x2d, mod, g1, w_in, w_kt, w_gate, gate_b, conv_w, rows_per_batch, conv_dim, qk_all, v_all):
    n, d = x2d.shape
    tm = ROW_TILE
    tiles_per_batch = rows_per_batch // tm
    row = lambda i: (i, 0)
    mod_block = (1, 1, d)
    n_main = 3 * conv_dim + 2 * qk_all + 2 * v_all
    out_shapes = (
        jax.ShapeDtypeStruct((n, conv_dim), BF16),
        jax.ShapeDtypeStruct((n, qk_all), BF16),
        jax.ShapeDtypeStruct((qk_all, n), BF16),
        jax.ShapeDtypeStruct((n, v_all), BF16),
        jax.ShapeDtypeStruct((n, v_all), BF16),
        jax.ShapeDtypeStruct((n, N_GATES), F32),
        jax.ShapeDtypeStruct((N_GATES, n), F32),
    )
    out_specs = (
        pl.BlockSpec((tm, conv_dim), row),
        pl.BlockSpec((tm, qk_all), row),
        pl.BlockSpec((qk_all, tm), lambda i: (0, i)),
        pl.BlockSpec((tm, v_all), row),
        pl.BlockSpec((tm, v_all), row),
        pl.BlockSpec((tm, N_GATES), row),
        pl.BlockSpec((N_GATES, tm), lambda i: (0, i)),
    )
    return pl.pallas_call(
        functools.partial(_inproj_body, conv_dim=conv_dim, qk_all=qk_all, v_all=v_all),
        grid=(n // tm,),
        in_specs=[pl.BlockSpec((tm, d), row),
                  pl.BlockSpec(mod_block, _mod_spec(0, tiles_per_batch)),
                  pl.BlockSpec(mod_block, _mod_spec(1, tiles_per_batch)),
                  _resident(g1.shape), _resident(w_kt.shape),
                  _resident(w_gate.shape), _resident(gate_b.shape), _resident(conv_w.shape),
                  pl.BlockSpec(memory_space=pl.ANY)],
        out_specs=out_specs,
        out_shape=out_shapes,
        scratch_shapes=[pltpu.VMEM((d, n_main), BF16), pltpu.VMEM((2, CAST_ROWS, n_main), F32),
                        pltpu.SemaphoreType.DMA((2,))],
        compiler_params=_params(1),
        name="inproj",
    )(x2d, mod, mod, g1, w_kt, w_gate, gate_b, conv_w, w_in)


def _inproj_ctx(c2d, mod, g1, w_v, w_kt, w_gate, gate_b, ctx_mod_row):
    n, d = c2d.shape
    tm = ROW_TILE
    row = lambda i: (i, 0)
    mod_block = (1, 1, d)
    qk_all, v_all = w_kt.shape[0], w_v.shape[1]
    return pl.pallas_call(
        _inproj_ctx_body,
        grid=(n // tm,),
        in_specs=[pl.BlockSpec((tm, d), row),
                  pl.BlockSpec(mod_block, _mod_spec(0, 1, ctx_mod_row)),
                  pl.BlockSpec(mod_block, _mod_spec(1, 1, ctx_mod_row)),
                  _resident(g1.shape), _resident(w_v.shape), _resident(w_kt.shape),
                  _resident(w_gate.shape), _resident(gate_b.shape)],
        out_specs=(pl.BlockSpec((qk_all, tm), lambda i: (0, i)), pl.BlockSpec((tm, v_all), row),
                   pl.BlockSpec((tm, N_GATES), row), pl.BlockSpec((N_GATES, tm), lambda i: (0, i))),
        out_shape=(jax.ShapeDtypeStruct((qk_all, n), BF16), jax.ShapeDtypeStruct((n, v_all), BF16),
                   jax.ShapeDtypeStruct((n, N_GATES), F32), jax.ShapeDtypeStruct((N_GATES, n), F32)),
        compiler_params=_params(1),
        name="inproj_ctx",
    )(c2d, mod, mod, g1, w_v, w_kt, w_gate, gate_b)


def _with_ones(v):
    return jnp.concatenate([v, jnp.ones((v.shape[0], LANES), v.dtype)], axis=1)


def _mlstm_state_update(h, direction, kt_ref, v_ref, gt_ref, s_ref, m_ref, qk, vh):
    ci = direction * 2 * N_HEADS + h
    cb = ci + N_HEADS
    last = 0 if direction else CHUNK - 1
    kt = kt_ref[h * qk:(h + 1) * qk, :].astype(F32)
    va = _with_ones(v_ref[:, h * vh:(h + 1) * vh])
    b_last = gt_ref[cb:cb + 1, last:last + 1]
    m_prev = m_ref[h][0:1, 0:1]
    g_r = b_last - gt_ref[cb:cb + 1, :] + gt_ref[ci:ci + 1, :]
    m_new = jnp.maximum(b_last + m_prev, jnp.max(g_r, axis=1, keepdims=True))
    a = jnp.exp(b_last + m_prev - m_new)
    kw = (kt * jnp.exp(g_r - m_new)).astype(BF16)
    s_ref[h] = a * s_ref[h] + jnp.dot(kw, va, preferred_element_type=F32)
    m_ref[h] = jnp.broadcast_to(m_new, m_ref.shape[1:])


def _mlstm_head_output(h, direction, q_ref, kt_ref, v_ref, g_ref, gt_ref, s_ref, m_ref, qk, vh):
    ci = direction * 2 * N_HEADS + h
    cb = ci + N_HEADS
    q = q_ref[:, h * qk:(h + 1) * qk]
    kt = kt_ref[h * qk:(h + 1) * qk, :]
    va = _with_ones(v_ref[:, h * vh:(h + 1) * vh])
    ig_r = gt_ref[ci:ci + 1, :]
    b_r = gt_ref[cb:cb + 1, :]
    b_c = g_ref[:, cb:cb + 1]
    m_prev = m_ref[h][0:1, 0:1]
    row = lax.broadcasted_iota(I32, (CHUNK, CHUNK), 0)
    col = lax.broadcasted_iota(I32, (CHUNK, CHUNK), 1)
    mask = (col >= row) if direction else (col <= row)
    dm = jnp.where(mask, b_c + (ig_r - b_r), _NEG_INF)
    inter = b_c + m_prev
    m_t = jnp.maximum(inter, jnp.max(dm, axis=1, keepdims=True))
    w_inter = jnp.exp(inter - m_t)
    s = jnp.dot(q, kt, preferred_element_type=F32) * jnp.exp(dm - m_t)
    intra = jnp.dot(s.astype(BF16), va, preferred_element_type=F32)
    carried = jnp.dot(q, s_ref[h].astype(BF16), preferred_element_type=F32)
    num = intra[:, 0:vh] + w_inter * carried[:, 0:vh]
    den = intra[:, vh:vh + 1] + w_inter * carried[:, vh:vh + 1]
    return num / jnp.maximum(jnp.abs(den), jnp.exp(-m_t))


def _mlstm_body(*refs, direction, bsz, n_ctx_chunks, qk, vh):
    q_ref, v_ref, g_ref, vc_ref = refs[0:4]
    kt_refs, gt_refs = refs[4:4 + bsz], refs[4 + bsz:4 + 2 * bsz]
    ktc_refs, gtc_refs = refs[4 + 2 * bsz:4 + 3 * bsz], refs[4 + 3 * bsz:4 + 4 * bsz]
    rest = refs[4 + 4 * bsz:]
    if direction:
        out_ref, s_ref, m_ref = rest
    else:
        hb_ref, og_ref, hg_ref, out_ref, s_ref, m_ref = rest
    step = pl.program_id(0)

    @pl.when(step == 0)
    def _():
        s_ref[...] = jnp.zeros_like(s_ref)
        m_ref[...] = jnp.full_like(m_ref, _NEG_INF)

    @pl.when(step < n_ctx_chunks)
    def _():
        for b in range(bsz):
            for h in range(N_HEADS):
                _mlstm_state_update(h, direction, ktc_refs[b], vc_ref.at[b], gtc_refs[b],
                                    s_ref.at[b], m_ref.at[b], qk, vh)

    @pl.when(step >= n_ctx_chunks)
    def _():
        for b in range(bsz):
            for h in range(N_HEADS):
                hh = _mlstm_head_output(h, direction, q_ref.at[b], kt_refs[b], v_ref.at[b], g_ref.at[b],
                                        gt_refs[b], s_ref.at[b], m_ref.at[b], qk, vh)
                cols = slice(h * vh, (h + 1) * vh)
                if direction:
                    out_ref[b, :, cols] = hh
                else:
                    hs = hh + hb_ref[b, :, cols]
                    hs = hs * lax.rsqrt(jnp.mean(hs * hs, axis=-1, keepdims=True) + EPS)
                    out_ref[b, :, cols] = (hs * hg_ref[:, cols]
                                           * og_ref[b, :, cols].astype(F32)).astype(BF16)
                _mlstm_state_update(h, direction, kt_refs[b], v_ref.at[b], gt_refs[b],
                                    s_ref.at[b], m_ref.at[b], qk, vh)


def _mlstm(direction, q, kt, v, g, gt, ktc, vc, gtc, extra, bsz, head_g=None):
    n, qk_all = q.shape
    v_all = v.shape[1]
    qk, vh = qk_all // N_HEADS, v_all // N_HEADS
    seq = n // bsz
    nc = seq // CHUNK
    ncc = vc.shape[0] // bsz // CHUNK

    def lat(s):
        j = jnp.clip(s - ncc, 0, nc - 1)
        return nc - 1 - j if direction else j

    def ctx(s):
        j = jnp.clip(s, 0, ncc - 1)
        return ncc - 1 - j if direction else j

    def per_batch(a):
        return a.reshape(bsz, a.shape[0] // bsz, a.shape[1])

    lat_blk = lambda c: pl.BlockSpec((bsz, CHUNK, c), lambda s: (0, lat(s), 0))
    in_specs = [lat_blk(qk_all), lat_blk(v_all), lat_blk(N_GATES),
                pl.BlockSpec((bsz, CHUNK, v_all), lambda s: (0, ctx(s), 0))]
    args = [per_batch(q), per_batch(v), per_batch(g), per_batch(vc)]
    for arr, rows, n_chunks, pos in ((kt, qk_all, nc, lat), (gt, N_GATES, nc, lat),
                                     (ktc, qk_all, ncc, ctx), (gtc, N_GATES, ncc, ctx)):
        for b in range(bsz):
            in_specs.append(pl.BlockSpec((rows, CHUNK), lambda s, b=b, n_chunks=n_chunks, pos=pos:
                                         (0, b * n_chunks + pos(s))))
            args.append(arr)
    if direction:
        out_dtype = F32
    else:
        hb, og = extra
        in_specs += [lat_blk(v_all), lat_blk(v_all), pl.BlockSpec((1, v_all), lambda s: (0, 0))]
        args += [per_batch(hb), per_batch(og), head_g]
        out_dtype = BF16
    out = pl.pallas_call(
        functools.partial(_mlstm_body, direction=direction, bsz=bsz, n_ctx_chunks=ncc, qk=qk, vh=vh),
        grid=(ncc + nc,),
        in_specs=in_specs,
        out_specs=lat_blk(v_all),
        out_shape=jax.ShapeDtypeStruct((bsz, seq, v_all), out_dtype),
        scratch_shapes=[pltpu.VMEM((bsz, N_HEADS, qk, vh + LANES), F32),
                        pltpu.VMEM((bsz, N_HEADS, SUBLANES, LANES), F32)],
        compiler_params=_params(1),
        name="mlstm_bwd" if direction else "mlstm_fwd",
    )(*args)
    return out.reshape(n, v_all)


def _outproj_body(conv_ref, ml_ref, x_ref, gt1_ref, sh2_ref, sc2_ref, g2_ref, wr_ref, br_ref, wo_hbm,
                  x1_ref, idx_ref, gate_ref, rank_ref, cnt_ref, h_hbm, carry_ref, hw, hsem,
                  wo_ref, stage, wsem):
    tm = x_ref.shape[0]
    half = conv_ref.shape[1]
    step = pl.program_id(0)
    buf = step % 2

    def h_out(i, s):
        return _row_tile_copies(h_hbm, i * tm, hw.at[s], hsem.at[s], to_hbm=True)

    @pl.when(step == 0)
    def _():
        carry_ref[...] = jnp.zeros_like(carry_ref)
        _load_cast(wo_hbm, wo_ref, stage, wsem)

    @pl.when(step >= 2)
    def _():
        _wait_all(h_out(step - 2, buf))

    y = (jnp.dot(conv_ref[...], wo_ref[0:half, :], preferred_element_type=F32)
         + jnp.dot(ml_ref[...], wo_ref[half:2 * half, :], preferred_element_type=F32))
    x1 = x_ref[...] + gt1_ref[0] * y
    x1_ref[...] = x1
    hn = _norm_mod(x1, g2_ref[...], sh2_ref[0], sc2_ref[0])
    hw[buf] = _pack_words(hn)
    _start_all(h_out(step, buf))

    h_hi = hn.astype(BF16)
    h_lo = (hn - h_hi.astype(F32)).astype(BF16)
    parts = (jnp.dot(h_hi, wr_ref[...], preferred_element_type=F32)
             + jnp.dot(h_lo, wr_ref[...], preferred_element_type=F32))
    scores = jax.nn.sigmoid(parts + pltpu.roll(parts, N_EXPERTS, axis=1))
    lane = lax.broadcasted_iota(I32, (tm, LANES), 1).astype(F32)
    biased = jnp.where(lane < N_EXPERTS, scores + br_ref[...], _NEG_INF)
    onehot = jnp.zeros((tm, LANES), F32)
    picks, sels = [], []
    for _ in range(TOP_K):
        mx = jnp.max(biased, axis=1, keepdims=True)
        pick = jnp.min(jnp.where(biased == mx, lane, float(LANES)), axis=1, keepdims=True)
        hit = lane == pick
        sels.append(jnp.sum(jnp.where(hit, scores, 0.0), axis=1, keepdims=True))
        picks.append(pick)
        biased = jnp.where(hit, _NEG_INF, biased)
        onehot = onehot + hit.astype(F32)
    total = sels[0]
    for s in sels[1:]:
        total = total + s

    r = lax.broadcasted_iota(I32, (tm, tm), 0)
    c = lax.broadcasted_iota(I32, (tm, tm), 1)
    strict = jnp.where(c < r, 1.0, 0.0).astype(BF16)
    before = jnp.dot(strict, onehot.astype(BF16), preferred_element_type=F32) + carry_ref[...]
    slot = lax.broadcasted_iota(I32, (tm, SUBLANES), 1)
    idx_out = jnp.zeros((tm, LANES), F32)
    rank_out = jnp.zeros((tm, LANES), F32)
    gate_out = jnp.zeros((tm, SUBLANES), F32)
    for j in range(TOP_K):
        rank = jnp.sum(jnp.where(lane == picks[j], before, 0.0), axis=1, keepdims=True)
        idx_out = jnp.where(lane == float(j), picks[j], idx_out)
        rank_out = jnp.where(lane == float(j), rank, rank_out)
        gate_out = jnp.where(slot == j, sels[j] / total * ROUTED_SCALE, gate_out)
    idx_ref[...] = idx_out.T[:SUBLANES, :].astype(I32)
    rank_ref[...] = rank_out.T[:SUBLANES, :].astype(I32)
    gate_ref[...] = gate_out
    carry_ref[...] = carry_ref[...] + jnp.sum(onehot, axis=0, keepdims=True)
    cnt_ref[...] = jnp.broadcast_to(carry_ref[...], cnt_ref.shape).astype(I32)

    @pl.when(step == pl.num_programs(0) - 1)
    def _():
        @pl.when(step >= 1)
        def _():
            _wait_all(h_out(step - 1, 1 - buf))
        _wait_all(h_out(step, buf))


def _outproj(conv, ml, x2d, mod, g2, w_out, w_router, b_router, rows_per_batch):
    n, d = x2d.shape
    tm = ROW_TILE
    tiles_per_batch = rows_per_batch // tm
    row = lambda i: (i, 0)
    mod_block = (1, 1, d)
    half = conv.shape[1]
    return pl.pallas_call(
        _outproj_body,
        grid=(n // tm,),
        in_specs=[pl.BlockSpec((tm, half), row), pl.BlockSpec((tm, half), row), pl.BlockSpec((tm, d), row),
                  pl.BlockSpec(mod_block, _mod_spec(2, tiles_per_batch)),
                  pl.BlockSpec(mod_block, _mod_spec(3, tiles_per_batch)),
                  pl.BlockSpec(mod_block, _mod_spec(4, tiles_per_batch)),
                  _resident(g2.shape), _resident(w_router.shape), _resident(b_router.shape),
                  pl.BlockSpec(memory_space=pl.ANY)],
        out_specs=(pl.BlockSpec((tm, d), row),
                   pl.BlockSpec((SUBLANES, tm), lambda i: (0, i)), pl.BlockSpec((tm, SUBLANES), row),
                   pl.BlockSpec((SUBLANES, tm), lambda i: (0, i)),
                   pl.BlockSpec((SUBLANES, LANES), lambda i: (0, 0)),
                   pl.BlockSpec(memory_space=pl.ANY)),
        out_shape=(jax.ShapeDtypeStruct((n, d), F32),
                   jax.ShapeDtypeStruct((SUBLANES, n), I32), jax.ShapeDtypeStruct((n, SUBLANES), F32),
                   jax.ShapeDtypeStruct((SUBLANES, n), I32),
                   jax.ShapeDtypeStruct((SUBLANES, LANES), I32),
                   jax.ShapeDtypeStruct((n, SUBLANES, LANES), I32)),
        scratch_shapes=[pltpu.VMEM((1, LANES), F32), pltpu.VMEM((2, tm, d // 2), I32),
                        pltpu.SemaphoreType.DMA((2,)),
                        pltpu.VMEM(w_out.shape, BF16), pltpu.VMEM((2, CAST_ROWS, w_out.shape[1]), F32),
                        pltpu.SemaphoreType.DMA((2,))],
        compiler_params=_params(1),
        name="outproj_router",
    )(conv, ml, x2d, mod, mod, mod, g2, w_router, b_router, w_out)


def _sc_workers():
    info = plsc.get_sparse_core_info()
    return info.num_cores, info.num_cores * info.num_subcores


def _sc_dispatch(h_rows, dest_chunks, n_slots):
    n_tok = h_rows.shape[0]
    n_cores, n_workers = _sc_workers()
    per_worker = n_tok // (n_workers * SC_CHUNK)
    assert per_worker * n_workers * SC_CHUNK == n_tok
    mesh = plsc.VectorSubcoreMesh(core_axis_name="c", subcore_axis_name="s")

    @functools.partial(
        pl.kernel, mesh=mesh,
        out_type=jax.ShapeDtypeStruct((n_slots,) + h_rows.shape[1:], h_rows.dtype),
        scratch_types=[pltpu.VMEM((TOP_K, SC_CHUNK), I32),
                       pltpu.VMEM((SC_CHUNK,) + h_rows.shape[1:], h_rows.dtype)],
    )
    def dispatch(h_hbm, dest_hbm, out_hbm, idx_v, rows_v):
        wid = lax.axis_index("s") * n_cores + lax.axis_index("c")

        @pl.loop(0, per_worker)
        def _(i):
            chunk = wid * per_worker + i
            pltpu.sync_copy(dest_hbm.at[chunk], idx_v)
            pltpu.sync_copy(h_hbm.at[pl.ds(chunk * SC_CHUNK, SC_CHUNK)], rows_v)
            for k in range(TOP_K):
                pltpu.sync_copy(rows_v, out_hbm.at[idx_v.at[k]])

    return dispatch(h_rows, dest_chunks)


def _sc_combine(y_sorted, dest_chunks, n_tok):
    n_cores, n_workers = _sc_workers()
    per_worker = n_tok // (n_workers * SC_CHUNK)
    mesh = plsc.VectorSubcoreMesh(core_axis_name="c", subcore_axis_name="s")

    @functools.partial(
        pl.kernel, mesh=mesh,
        out_type=jax.ShapeDtypeStruct((TOP_K, n_tok) + y_sorted.shape[1:], y_sorted.dtype),
        scratch_types=[pltpu.VMEM((TOP_K, SC_CHUNK), I32),
                       pltpu.VMEM((SC_CHUNK,) + y_sorted.shape[1:], y_sorted.dtype)],
    )
    def combine(y_hbm, dest_hbm, out_hbm, idx_v, rows_v):
        wid = lax.axis_index("s") * n_cores + lax.axis_index("c")

        @pl.loop(0, per_worker)
        def _(i):
            chunk = wid * per_worker + i
            pltpu.sync_copy(dest_hbm.at[chunk], idx_v)
            for k in range(TOP_K):
                pltpu.sync_copy(y_hbm.at[idx_v.at[k]], rows_v)
                pltpu.sync_copy(rows_v, out_hbm.at[k, pl.ds(chunk * SC_CHUNK, SC_CHUNK)])

    return combine(y_sorted, dest_chunks)


def _moe_body(ord_ref, order_ref, glo_ref, ghi_ref, tot_ref, nb_ref,
              x_hbm, wg_hbm, wu_hbm, wd_hbm, y_hbm,
              wgu, wd, stage_a, stage_d, xw, yw, wsem, xsem, ysem, *, d_expert):
    b = pl.program_id(0)
    nb = nb_ref[0]
    total = tot_ref[0]
    d_model = wgu.shape[1]
    rows_a = d_model // WEIGHT_PARTS
    rows_d = d_expert // WEIGHT_PARTS

    def part_copies(g):
        e = order_ref[lax.shift_right_logical(g, PART_SHIFT)]
        i = g & (WEIGHT_PARTS - 1)
        s = lax.rem(g, WEIGHT_RING)
        return (pltpu.make_async_copy(wg_hbm.at[e, pl.ds(i * rows_a, rows_a)], stage_a.at[s, 0],
                                      wsem.at[s, 0]),
                pltpu.make_async_copy(wu_hbm.at[e, pl.ds(i * rows_a, rows_a)], stage_a.at[s, 1],
                                      wsem.at[s, 1]),
                pltpu.make_async_copy(wd_hbm.at[e, pl.ds(i * rows_d, rows_d)], stage_d.at[s],
                                      wsem.at[s, 2]))

    def start_part(g):
        for cp in part_copies(g):
            cp.start()

    def wait_part(g):
        for cp in part_copies(g):
            cp.wait()

    def cast_part(g):
        i = g & (WEIGHT_PARTS - 1)
        s = lax.rem(g, WEIGHT_RING)
        par = lax.shift_right_logical(g, PART_SHIFT) & 1
        ra = pl.multiple_of(i * rows_a, rows_a)
        rd = pl.multiple_of(i * rows_d, rows_d)
        wgu[par, pl.ds(ra, rows_a), 0:d_expert] = stage_a[s, 0].astype(BF16)
        wgu[par, pl.ds(ra, rows_a), d_expert:2 * d_expert] = stage_a[s, 1].astype(BF16)
        wd[par, pl.ds(rd, rows_d), :] = stage_d[s].astype(BF16)

    def refill(g):
        @pl.when(g + WEIGHT_RING < total)
        def _():
            start_part(g + WEIGHT_RING)

    def cast_parts(lo, hi):
        def body(g, carry):
            wait_part(g)
            cast_part(g)
            refill(g)
            return carry
        lax.fori_loop(lo, hi, body, 0)

    slot = b % 2

    def x_in(blk, s):
        return _row_tile_copies(x_hbm, blk * MOE_BLOCK, xw.at[s], xsem.at[s], to_hbm=False)

    def y_out(blk, s):
        return _row_tile_copies(y_hbm, blk * MOE_BLOCK, yw.at[s], ysem.at[s], to_hbm=True)

    @pl.when(b == 0)
    def _():
        _start_all(x_in(0, 0))
        for g in range(WEIGHT_RING):
            start_part(g)
        cast_parts(0, WEIGHT_PARTS)

    @pl.when(b + 1 < nb)
    def _():
        _start_all(x_in(b + 1, 1 - slot))

    @pl.when(b < nb)
    def _():
        par = ord_ref[b] & 1
        _wait_all(x_in(b, slot))

        @pl.when(b >= 2)
        def _():
            _wait_all(y_out(b - 2, slot))

        x = jnp.concatenate(_unpack_words(xw[slot]), axis=1).astype(BF16)
        gu = jnp.dot(x, wgu[par], preferred_element_type=F32)
        hb = (_silu(gu[:, 0:d_expert]) * gu[:, d_expert:2 * d_expert]).astype(BF16)
        yw[slot] = _pack_words(jnp.dot(hb, wd[par], preferred_element_type=F32))
        _start_all(y_out(b, slot))
        cast_parts(glo_ref[b], ghi_ref[b])

        @pl.when(b == nb - 1)
        def _():
            @pl.when(b >= 1)
            def _():
                _wait_all(y_out(b - 1, 1 - slot))
            _wait_all(y_out(b, slot))


def _moe(x_sorted, we_gate, we_up, we_down, tables):
    d, d_expert = we_gate.shape[1], we_gate.shape[2]
    nb_max = x_sorted.shape[0] // MOE_BLOCK
    any_spec = pl.BlockSpec(memory_space=pl.ANY)
    grid_spec = pltpu.PrefetchScalarGridSpec(
        num_scalar_prefetch=len(tables),
        grid=(nb_max,),
        in_specs=[any_spec, any_spec, any_spec, any_spec],
        out_specs=any_spec,
        scratch_shapes=[pltpu.VMEM((2, d, 2 * d_expert), BF16),
                        pltpu.VMEM((2, d_expert, d), BF16),
                        pltpu.VMEM((WEIGHT_RING, 2, d // WEIGHT_PARTS, d_expert), F32),
                        pltpu.VMEM((WEIGHT_RING, d_expert // WEIGHT_PARTS, d), F32),
                        pltpu.VMEM((2, MOE_BLOCK, d // 2), I32),
                        pltpu.VMEM((2, MOE_BLOCK, d // 2), I32),
                        pltpu.SemaphoreType.DMA((WEIGHT_RING, 3)),
                        pltpu.SemaphoreType.DMA((2,)),
                        pltpu.SemaphoreType.DMA((2,))],
    )
    return pl.pallas_call(
        functools.partial(_moe_body, d_expert=d_expert),
        grid_spec=grid_spec,
        out_shape=jax.ShapeDtypeStruct(x_sorted.shape, x_sorted.dtype),
        compiler_params=pltpu.CompilerParams(
            dimension_semantics=("arbitrary",), vmem_limit_bytes=MOE_VMEM_LIMIT),
        name="moe_routed",
    )(*tables, x_sorted, we_gate, we_up, we_down)


def _final_body(x1_ref, gate_ref, gt2_ref, fg_ref, h_hbm, y_hbm, wsg_hbm, wsu_hbm, wsd_hbm, out_ref,
                hw, yw, sem, wsgu_ref, wsd_ref, stage_a, stage_d, wsem, *, d_shared):
    tm = x1_ref.shape[0]
    step = pl.program_id(0)
    slot = step % 2

    def rows_in(i, s):
        copies = _row_tile_copies(h_hbm, i * tm, hw.at[s], sem.at[s], to_hbm=False)
        for k in range(TOP_K):
            copies += _row_tile_copies(y_hbm.at[k], i * tm, yw.at[s, k], sem.at[s], to_hbm=False)
        return copies

    @pl.when(step == 0)
    def _():
        _start_all(rows_in(0, 0))
        _load_cast(wsg_hbm, wsgu_ref, stage_a, wsem)
        _load_cast(wsu_hbm, wsgu_ref, stage_a, wsem, dst_col0=d_shared)
        _load_cast(wsd_hbm, wsd_ref, stage_d, wsem)

    @pl.when(step + 1 < pl.num_programs(0))
    def _():
        _start_all(rows_in(step + 1, 1 - slot))

    _wait_all(rows_in(step, slot))
    h = jnp.concatenate(_unpack_words(hw[slot]), axis=1).astype(BF16)
    gu = jnp.dot(h, wsgu_ref[...], preferred_element_type=F32)
    hb = (_silu(gu[:, 0:d_shared]) * gu[:, d_shared:2 * d_shared]).astype(BF16)
    acc = jnp.dot(hb, wsd_ref[...], preferred_element_type=F32)
    for k in range(TOP_K):
        acc = acc + gate_ref[:, k:k + 1] * jnp.concatenate(_unpack_words(yw[slot, k]), axis=1)
    x2 = x1_ref[...] + gt2_ref[0] * acc
    out_ref[...] = x2 * lax.rsqrt(jnp.mean(x2 * x2, axis=-1, keepdims=True) + EPS) * fg_ref[...]


def _final(h_rows, x1, y_tok, gates, mod, ws_gate, ws_up, ws_down, final_g, rows_per_batch):
    n, d = x1.shape
    d_shared = ws_down.shape[0]
    tm = ROW_TILE
    tiles_per_batch = rows_per_batch // tm
    row = lambda i: (i, 0)
    any_spec = pl.BlockSpec(memory_space=pl.ANY)
    return pl.pallas_call(
        functools.partial(_final_body, d_shared=d_shared),
        grid=(n // tm,),
        in_specs=[pl.BlockSpec((tm, d), row), pl.BlockSpec((tm, SUBLANES), row),
                  pl.BlockSpec((1, 1, d), _mod_spec(5, tiles_per_batch)),
                  _resident(final_g.shape), any_spec, any_spec, any_spec, any_spec, any_spec],
        out_specs=pl.BlockSpec((tm, d), row),
        out_shape=jax.ShapeDtypeStruct((n, d), F32),
        scratch_shapes=[pltpu.VMEM((2, tm, d // 2), I32), pltpu.VMEM((2, TOP_K, tm, d // 2), I32),
                        pltpu.SemaphoreType.DMA((2,)),
                        pltpu.VMEM((d, 2 * d_shared), BF16), pltpu.VMEM((d_shared, d), BF16),
                        pltpu.VMEM((2, CAST_ROWS, d_shared), F32),
                        pltpu.VMEM((2, CAST_ROWS, d), F32),
                        pltpu.SemaphoreType.DMA((2,))],
        compiler_params=_params(1),
        name="shared_combine_final",
    )(x1, gates, mod, final_g, h_rows, y_tok, ws_gate, ws_up, ws_down)


def _routing_tables(idx, rank, counts, n_tok):
    nb_max = -(-(n_tok * TOP_K) // MOE_BLOCK) + N_EXPERTS
    nblk = (counts + MOE_BLOCK - 1) // MOE_BLOCK
    blk_end = jnp.cumsum(nblk)
    blk_start = blk_end - nblk
    experts = jnp.arange(N_EXPERTS, dtype=I32)[:, None, None]
    first_slot = (blk_start * MOE_BLOCK)[:, None, None]
    dest = jnp.sum(jnp.where(idx[None] == experts, first_slot, 0), axis=0) + rank
    dest_chunks = dest.reshape(TOP_K, n_tok // SC_CHUNK, SC_CHUNK).transpose(1, 0, 2)

    blocks = jnp.arange(nb_max, dtype=I32)
    block_e = jnp.minimum(jnp.searchsorted(blk_end, blocks, side="right"), N_EXPERTS - 1).astype(I32)
    nonempty = nblk > 0
    n_visited = jnp.sum(nonempty.astype(I32))
    ordinal_of = jnp.cumsum(nonempty.astype(I32)) - 1
    order = jnp.argsort(jnp.where(nonempty, 0, 1), stable=True).astype(I32)
    ordinal = ordinal_of[block_e]
    k_in_e = blocks - blk_start[block_e]
    nb_e = jnp.maximum(nblk[block_e], 1)
    live = (ordinal + 1 < n_visited) & (blocks < blk_end[-1])
    first = WEIGHT_PARTS * (ordinal + 1)
    lo = jnp.where(live, first + WEIGHT_PARTS * k_in_e // nb_e, 0)
    hi = jnp.where(live, first + WEIGHT_PARTS * (k_in_e + 1) // nb_e, 0)
    tables = (ordinal.astype(I32), order, lo.astype(I32), hi.astype(I32),
              (WEIGHT_PARTS * n_visited).reshape(1).astype(I32), blk_end[-1:].astype(I32))
    return tables, dest_chunks, nb_max * MOE_BLOCK


def kernel(x, c, ctx, c_ctx, norm1_g, norm2_g, w_ada, b_ada, w_in, conv_w, gate_b, head_g, w_out,
           w_router, b_router, we_gate, we_up, we_down, ws_gate, ws_up, ws_down, final_g):
    assert w_ada.shape[0] == 1, "single-layer block"
    bsz, seq, d = x.shape
    ctx_len = ctx.shape[1]
    n_tok = bsz * seq
    conv_dim = conv_w.shape[2]
    v_all = head_g.shape[1]
    qk_all = (w_in.shape[2] - 3 * conv_dim - 2 * v_all - N_GATES) // 2
    assert seq % ROW_TILE == 0 and ctx_len % ROW_TILE == 0 and ROW_TILE % GRID_W == 0
    assert bsz + 1 <= SUBLANES

    cc = jnp.zeros((SUBLANES, d), F32).at[:bsz].set(c).at[bsz].set(c_ctx)
    mod = _adaln(cc, w_ada[0], b_ada).reshape(SUBLANES * 6, 1, d)

    n_main = 3 * conv_dim + 2 * qk_all + 2 * v_all
    k_lo = 3 * conv_dim + qk_all
    w_kt = w_in[0, :, k_lo:k_lo + qk_all].T.astype(BF16)
    w_v = w_in[0, :, k_lo + qk_all:k_lo + qk_all + v_all].astype(BF16)
    w_gate = jnp.zeros((d, LANES), BF16).at[:, :N_GATES].set(w_in[0, :, n_main:].astype(BF16))
    gate_bias = jnp.zeros((1, LANES), F32).at[0, :N_GATES].set(gate_b[0].reshape(-1))

    x2d = x.reshape(n_tok, d)
    conv, q, kt, v, og, g, gt = _inproj(x2d, mod, norm1_g, w_in[0], w_kt, w_gate, gate_bias, conv_w[0],
                                       seq, conv_dim, qk_all, v_all)
    ktc, vc, _, gtc = _inproj_ctx(ctx.reshape(bsz * ctx_len, d), mod, norm1_g, w_v, w_kt, w_gate,
                                  gate_bias, bsz)

    h_bwd = _mlstm(1, q, kt, v, g, gt, ktc, vc, gtc, None, bsz)
    ml = _mlstm(0, q, kt, v, g, gt, ktc, vc, gtc, (h_bwd, og), bsz, head_g)

    assert 2 * N_EXPERTS == LANES
    w_r_hi = w_router[0].astype(BF16)
    w_r = jnp.concatenate([w_r_hi, (w_router[0] - w_r_hi.astype(F32)).astype(BF16)], axis=1)
    b_r = jnp.zeros((1, LANES), F32).at[0, :N_EXPERTS].set(b_router[0])
    x1, idx, gates, rank, cnt, h_rows = _outproj(conv, ml, x2d, mod, norm2_g, w_out[0],
                                             w_r, b_r, seq)

    tables, dest_chunks, n_slots = _routing_tables(idx[:TOP_K], rank[:TOP_K], cnt[0, :N_EXPERTS], n_tok)
    x_sorted = _sc_dispatch(h_rows, dest_chunks, n_slots)
    y_sorted = _moe(x_sorted, we_gate[0], we_up[0], we_down[0], tables)
    y_tok = _sc_combine(y_sorted, dest_chunks, n_tok)

    out = _final(h_rows, x1, y_tok, gates, mod, ws_gate[0], ws_up[0], ws_down[0],
                 final_g.reshape(1, d), seq)
    return out.reshape(bsz, seq, d)
```

```python
import functools

import jax
import jax.numpy as jnp
from jax import lax
from jax.experimental import pallas as pl
from jax.experimental.pallas import tpu as pltpu
from jax.experimental.pallas import tpu_sc as plsc

F32 = jnp.float32
BF16 = jnp.bfloat16
I32 = jnp.int32

N_HEADS = 4
GRID_W = 64
CHUNK = 128
TOP_K = 6
N_EXPERTS = 64
ROUTED_SCALE = 2.446
EPS = 1e-6
N_GATES = 4 * N_HEADS

LANES = 128
SUBLANES = 8
MOE_BLOCK = 256
ROW_TILE = 256
ADALN_TILE = 1024
WEIGHT_PARTS = 8
PART_SHIFT = 3
WEIGHT_RING = 3
SC_CHUNK = 64
HIGH_HALF = -65536
VMEM_LIMIT = 56 * 1024 * 1024
MOE_VMEM_LIMIT = 62 * 1024 * 1024

_HIGHEST = lax.Precision.HIGHEST
_NEG_INF = float("-inf")


def _resident(shape):
    nd = len(shape)
    return pl.BlockSpec(shape, lambda *_: (0,) * nd, pipeline_mode=pl.Buffered(1))


def _params(n_axes):
    return pltpu.CompilerParams(
        dimension_semantics=("arbitrary",) * n_axes, vmem_limit_bytes=VMEM_LIMIT)


def _log_sigmoid(x):
    return jnp.minimum(x, 0.0) - jnp.log1p(jnp.exp(-jnp.abs(x)))


def _silu(x):
    return x * jax.nn.sigmoid(x)


def _pack_words(val):
    half = val.shape[1] // 2
    lo = lax.bitcast_convert_type(val[:, :half].astype(BF16).astype(F32), I32)
    hi = lax.bitcast_convert_type(val[:, half:].astype(BF16).astype(F32), I32)
    return (hi & HIGH_HALF) | lax.shift_right_logical(lo, 16)


def _unpack_words(word):
    lo = lax.bitcast_convert_type(lax.shift_left(word, 16), F32)
    hi = lax.bitcast_convert_type(word & HIGH_HALF, F32)
    return lo, hi


def _row_tile_copies(hbm_rows, row0, tile, sem, to_hbm):
    n = tile.shape[0]
    copies = []
    for c in range(SUBLANES):
        hbm = hbm_rows.at[pl.ds(row0, n), c, :]
        vmem = tile.at[:, pl.ds(c * LANES, LANES)]
        copies.append(pltpu.make_async_copy(vmem, hbm, sem) if to_hbm
                      else pltpu.make_async_copy(hbm, vmem, sem))
    return copies


def _start_all(copies):
    for cp in copies:
        cp.start()


def _wait_all(copies):
    for cp in copies:
        cp.wait()


def _adaln_body(c_ref, w_ref, b_ref, o_ref):
    s = _silu(c_ref[...])
    o_ref[...] = jnp.dot(s.astype(BF16), w_ref[...].astype(BF16),
                         preferred_element_type=F32) + b_ref[...]


def _adaln(cc, w, b):
    d, n6 = w.shape
    return pl.pallas_call(
        _adaln_body,
        grid=(n6 // ADALN_TILE,),
        in_specs=[pl.BlockSpec((SUBLANES, d), lambda j: (0, 0)),
                  pl.BlockSpec((d, ADALN_TILE), lambda j: (0, j)),
                  pl.BlockSpec((1, ADALN_TILE), lambda j: (0, j))],
        out_specs=pl.BlockSpec((SUBLANES, ADALN_TILE), lambda j: (0, j)),
        out_shape=jax.ShapeDtypeStruct((SUBLANES, n6), F32),
        compiler_params=_params(1),
        name="adaln",
    )(cc, w, b)


def _norm_mod(x, g, shift, scale):
    y = x * lax.rsqrt(jnp.mean(x * x, axis=-1, keepdims=True) + EPS) * g
    return y * (1.0 + scale) + shift


def _gate_prep(xb, wg_ref, gb_ref, g_ref, gt_ref):
    tm = xb.shape[0]
    gg = jnp.dot(xb, wg_ref[...], preferred_element_type=F32) + gb_ref[...]
    lane = lax.broadcasted_iota(I32, (tm, LANES), 1)
    is_f = (lane & N_HEADS) != 0
    is_bwd = (lane & (2 * N_HEADS)) != 0
    lf = jnp.where(is_f, _log_sigmoid(gg), 0.0)
    r = lax.broadcasted_iota(I32, (tm, tm), 0)
    c = lax.broadcasted_iota(I32, (tm, tm), 1)
    same = (r // CHUNK) == (c // CHUNK)
    tri_l = jnp.where(same & (c <= r), 1.0, 0.0).astype(F32)
    tri_u = jnp.where(same & (c >= r), 1.0, 0.0).astype(F32)
    pre = jnp.dot(tri_l, lf, precision=_HIGHEST, preferred_element_type=F32)
    suf = jnp.dot(tri_u, lf, precision=_HIGHEST, preferred_element_type=F32)
    out = jnp.where(is_f, jnp.where(is_bwd, suf, pre), gg)
    g_ref[...] = out[:, :N_GATES]
    gt_ref[...] = out.T[:N_GATES, :]


def _project_transposed(wt_ref, xb):
    return lax.dot_general(wt_ref[...], xb, (((1,), (1,)), ((), ())),
                           preferred_element_type=F32).astype(BF16)


def _inproj_body(x_ref, sh_ref, sc_ref, g1_ref, w_ref, wkt_ref, wg_ref, gb_ref, cw_ref,
                 conv_ref, q_ref, k_ref, v_ref, o_ref, g_ref, gt_ref, *, conv_dim, qk_all, v_all):
    tm = x_ref.shape[0]
    xb = _norm_mod(x_ref[...], g1_ref[...], sh_ref[0], sc_ref[0]).astype(BF16)

    def proj(lo, width):
        return jnp.dot(xb, w_ref[:, lo:lo + width], preferred_element_type=F32)

    u = proj(conv_dim, conv_dim) * proj(2 * conv_dim, conv_dim)
    pos = lax.broadcasted_iota(I32, (tm, 1), 0) % GRID_W
    um = jnp.where(pos == 0, 0.0, pltpu.roll(u, 1, axis=0))
    up = jnp.where(pos == GRID_W - 1, 0.0, pltpu.roll(u, tm - 1, axis=0))
    y = um * cw_ref[0:1, :] + u * cw_ref[1:2, :] + up * cw_ref[2:3, :]
    conv_ref[...] = (proj(0, conv_dim) * y).astype(BF16)

    off = 3 * conv_dim
    qscale = (qk_all // N_HEADS) ** -0.5
    q_ref[...] = (proj(off, qk_all) * qscale).astype(BF16)
    k_ref[...] = _project_transposed(wkt_ref, xb)
    v_ref[...] = proj(off + 2 * qk_all, v_all).astype(BF16)
    o_ref[...] = jax.nn.sigmoid(proj(off + 2 * qk_all + v_all, v_all)).astype(BF16)
    _gate_prep(xb, wg_ref, gb_ref, g_ref, gt_ref)


def _inproj_ctx_body(x_ref, sh_ref, sc_ref, g1_ref, w_ref, wkt_ref, wg_ref, gb_ref,
                     k_ref, v_ref, g_ref, gt_ref):
    xb = _norm_mod(x_ref[...], g1_ref[...], sh_ref[0], sc_ref[0]).astype(BF16)
    k_ref[...] = _project_transposed(wkt_ref, xb)
    v_ref[...] = jnp.dot(xb, w_ref[...], preferred_element_type=F32).astype(BF16)
    _gate_prep(xb, wg_ref, gb_ref, g_ref, gt_ref)


def _mod_spec(part, tiles_per_row, fixed_row=None):
    def index(i):
        row = fixed_row if fixed_row is not None else i // tiles_per_row
        return (row * 6 + part, 0, 0)

    return index


def _inproj(x2d, mod, g1, w_main, w_kt, w_gate, gate_b, conv_w, rows_per_batch, conv_dim, qk_all, v_all):
    n, d = x2d.shape
    tm = ROW_TILE
    tiles_per_batch = rows_per_batch // tm
    row = lambda i: (i, 0)
    mod_block = (1, 1, d)
    out_shapes = (
        jax.ShapeDtypeStruct((n, conv_dim), BF16),
        jax.ShapeDtypeStruct((n, qk_all), BF16),
        jax.ShapeDtypeStruct((qk_all, n), BF16),
        jax.ShapeDtypeStruct((n, v_all), BF16),
        jax.ShapeDtypeStruct((n, v_all), BF16),
        jax.ShapeDtypeStruct((n, N_GATES), F32),
        jax.ShapeDtypeStruct((N_GATES, n), F32),
    )
    out_specs = (
        pl.BlockSpec((tm, conv_dim), row),
        pl.BlockSpec((tm, qk_all), row),
        pl.BlockSpec((qk_all, tm), lambda i: (0, i)),
        pl.BlockSpec((tm, v_all), row),
        pl.BlockSpec((tm, v_all), row),
        pl.BlockSpec((tm, N_GATES), row),
        pl.BlockSpec((N_GATES, tm), lambda i: (0, i)),
    )
    return pl.pallas_call(
        functools.partial(_inproj_body, conv_dim=conv_dim, qk_all=qk_all, v_all=v_all),
        grid=(n // tm,),
        in_specs=[pl.BlockSpec((tm, d), row),
                  pl.BlockSpec(mod_block, _mod_spec(0, tiles_per_batch)),
                  pl.BlockSpec(mod_block, _mod_spec(1, tiles_per_batch)),
                  _resident(g1.shape), _resident(w_main.shape), _resident(w_kt.shape),
                  _resident(w_gate.shape), _resident(gate_b.shape), _resident(conv_w.shape)],
        out_specs=out_specs,
        out_shape=out_shapes,
        compiler_params=_params(1),
        name="inproj",
    )(x2d, mod, mod, g1, w_main, w_kt, w_gate, gate_b, conv_w)


def _inproj_ctx(c2d, mod, g1, w_v, w_kt, w_gate, gate_b, ctx_mod_row):
    n, d = c2d.shape
    tm = ROW_TILE
    row = lambda i: (i, 0)
    mod_block = (1, 1, d)
    qk_all, v_all = w_kt.shape[0], w_v.shape[1]
    return pl.pallas_call(
        _inproj_ctx_body,
        grid=(n // tm,),
        in_specs=[pl.BlockSpec((tm, d), row),
                  pl.BlockSpec(mod_block, _mod_spec(0, 1, ctx_mod_row)),
                  pl.BlockSpec(mod_block, _mod_spec(1, 1, ctx_mod_row)),
                  _resident(g1.shape), _resident(w_v.shape), _resident(w_kt.shape),
                  _resident(w_gate.shape), _resident(gate_b.shape)],
        out_specs=(pl.BlockSpec((qk_all, tm), lambda i: (0, i)), pl.BlockSpec((tm, v_all), row),
                   pl.BlockSpec((tm, N_GATES), row), pl.BlockSpec((N_GATES, tm), lambda i: (0, i))),
        out_shape=(jax.ShapeDtypeStruct((qk_all, n), BF16), jax.ShapeDtypeStruct((n, v_all), BF16),
                   jax.ShapeDtypeStruct((n, N_GATES), F32), jax.ShapeDtypeStruct((N_GATES, n), F32)),
        compiler_params=_params(1),
        name="inproj_ctx",
    )(c2d, mod, mod, g1, w_v, w_kt, w_gate, gate_b)


def _with_ones(v):
    return jnp.concatenate([v, jnp.ones((v.shape[0], LANES), v.dtype)], axis=1)


def _mlstm_state_update(h, direction, kt_ref, v_ref, gt_ref, s_ref, m_ref, qk, vh):
    ci = direction * 2 * N_HEADS + h
    cb = ci + N_HEADS
    last = 0 if direction else CHUNK - 1
    kt = kt_ref[h * qk:(h + 1) * qk, :].astype(F32)
    va = _with_ones(v_ref[:, h * vh:(h + 1) * vh])
    b_last = gt_ref[cb:cb + 1, last:last + 1]
    m_prev = m_ref[h][0:1, 0:1]
    g_r = b_last - gt_ref[cb:cb + 1, :] + gt_ref[ci:ci + 1, :]
    m_new = jnp.maximum(b_last + m_prev, jnp.max(g_r, axis=1, keepdims=True))
    a = jnp.exp(b_last + m_prev - m_new)
    kw = (kt * jnp.exp(g_r - m_new)).astype(BF16)
    s_ref[h] = a * s_ref[h] + jnp.dot(kw, va, preferred_element_type=F32)
    m_ref[h] = jnp.broadcast_to(m_new, m_ref.shape[1:])


def _mlstm_head_output(h, direction, q_ref, kt_ref, v_ref, g_ref, gt_ref, s_ref, m_ref, qk, vh):
    ci = direction * 2 * N_HEADS + h
    cb = ci + N_HEADS
    q = q_ref[:, h * qk:(h + 1) * qk]
    kt = kt_ref[h * qk:(h + 1) * qk, :]
    va = _with_ones(v_ref[:, h * vh:(h + 1) * vh])
    ig_r = gt_ref[ci:ci + 1, :]
    b_r = gt_ref[cb:cb + 1, :]
    b_c = g_ref[:, cb:cb + 1]
    m_prev = m_ref[h][0:1, 0:1]
    row = lax.broadcasted_iota(I32, (CHUNK, CHUNK), 0)
    col = lax.broadcasted_iota(I32, (CHUNK, CHUNK), 1)
    mask = (col >= row) if direction else (col <= row)
    dm = jnp.where(mask, b_c + (ig_r - b_r), _NEG_INF)
    inter = b_c + m_prev
    m_t = jnp.maximum(inter, jnp.max(dm, axis=1, keepdims=True))
    w_inter = jnp.exp(inter - m_t)
    s = jnp.dot(q, kt, preferred_element_type=F32) * jnp.exp(dm - m_t)
    intra = jnp.dot(s.astype(BF16), va, preferred_element_type=F32)
    carried = jnp.dot(q, s_ref[h].astype(BF16), preferred_element_type=F32)
    num = intra[:, 0:vh] + w_inter * carried[:, 0:vh]
    den = intra[:, vh:vh + 1] + w_inter * carried[:, vh:vh + 1]
    return num / jnp.maximum(jnp.abs(den), jnp.exp(-m_t))


def _mlstm_body(*refs, direction, bsz, n_ctx_chunks, qk, vh):
    q_ref, v_ref, g_ref, vc_ref = refs[0:4]
    kt_refs, gt_refs = refs[4:4 + bsz], refs[4 + bsz:4 + 2 * bsz]
    ktc_refs, gtc_refs = refs[4 + 2 * bsz:4 + 3 * bsz], refs[4 + 3 * bsz:4 + 4 * bsz]
    rest = refs[4 + 4 * bsz:]
    if direction:
        out_ref, s_ref, m_ref = rest
    else:
        hb_ref, og_ref, hg_ref, out_ref, s_ref, m_ref = rest
    step = pl.program_id(0)

    @pl.when(step == 0)
    def _():
        s_ref[...] = jnp.zeros_like(s_ref)
        m_ref[...] = jnp.full_like(m_ref, _NEG_INF)

    @pl.when(step < n_ctx_chunks)
    def _():
        for b in range(bsz):
            for h in range(N_HEADS):
                _mlstm_state_update(h, direction, ktc_refs[b], vc_ref.at[b], gtc_refs[b],
                                    s_ref.at[b], m_ref.at[b], qk, vh)

    @pl.when(step >= n_ctx_chunks)
    def _():
        for b in range(bsz):
            for h in range(N_HEADS):
                hh = _mlstm_head_output(h, direction, q_ref.at[b], kt_refs[b], v_ref.at[b], g_ref.at[b],
                                        gt_refs[b], s_ref.at[b], m_ref.at[b], qk, vh)
                cols = slice(h * vh, (h + 1) * vh)
                if direction:
                    out_ref[b, :, cols] = hh
                else:
                    hs = hh + hb_ref[b, :, cols]
                    hs = hs * lax.rsqrt(jnp.mean(hs * hs, axis=-1, keepdims=True) + EPS)
                    out_ref[b, :, cols] = (hs * hg_ref[:, cols]
                                           * og_ref[b, :, cols].astype(F32)).astype(BF16)
                _mlstm_state_update(h, direction, kt_refs[b], v_ref.at[b], gt_refs[b],
                                    s_ref.at[b], m_ref.at[b], qk, vh)


def _mlstm(direction, q, kt, v, g, gt, ktc, vc, gtc, extra, bsz, head_g=None):
    n, qk_all = q.shape
    v_all = v.shape[1]
    qk, vh = qk_all // N_HEADS, v_all // N_HEADS
    seq = n // bsz
    nc = seq // CHUNK
    ncc = vc.shape[0] // bsz // CHUNK

    def lat(s):
        j = jnp.clip(s - ncc, 0, nc - 1)
        return nc - 1 - j if direction else j

    def ctx(s):
        j = jnp.clip(s, 0, ncc - 1)
        return ncc - 1 - j if direction else j

    def per_batch(a):
        return a.reshape(bsz, a.shape[0] // bsz, a.shape[1])

    lat_blk = lambda c: pl.BlockSpec((bsz, CHUNK, c), lambda s: (0, lat(s), 0))
    in_specs = [lat_blk(qk_all), lat_blk(v_all), lat_blk(N_GATES),
                pl.BlockSpec((bsz, CHUNK, v_all), lambda s: (0, ctx(s), 0))]
    args = [per_batch(q), per_batch(v), per_batch(g), per_batch(vc)]
    for arr, rows, n_chunks, pos in ((kt, qk_all, nc, lat), (gt, N_GATES, nc, lat),
                                     (ktc, qk_all, ncc, ctx), (gtc, N_GATES, ncc, ctx)):
        for b in range(bsz):
            in_specs.append(pl.BlockSpec((rows, CHUNK), lambda s, b=b, n_chunks=n_chunks, pos=pos:
                                         (0, b * n_chunks + pos(s))))
            args.append(arr)
    if direction:
        out_dtype = F32
    else:
        hb, og = extra
        in_specs += [lat_blk(v_all), lat_blk(v_all), pl.BlockSpec((1, v_all), lambda s: (0, 0))]
        args += [per_batch(hb), per_batch(og), head_g]
        out_dtype = BF16
    out = pl.pallas_call(
        functools.partial(_mlstm_body, direction=direction, bsz=bsz, n_ctx_chunks=ncc, qk=qk, vh=vh),
        grid=(ncc + nc,),
        in_specs=in_specs,
        out_specs=lat_blk(v_all),
        out_shape=jax.ShapeDtypeStruct((bsz, seq, v_all), out_dtype),
        scratch_shapes=[pltpu.VMEM((bsz, N_HEADS, qk, vh + LANES), F32),
                        pltpu.VMEM((bsz, N_HEADS, SUBLANES, LANES), F32)],
        compiler_params=_params(1),
        name="mlstm_bwd" if direction else "mlstm_fwd",
    )(*args)
    return out.reshape(n, v_all)


def _outproj_body(conv_ref, ml_ref, x_ref, gt1_ref, sh2_ref, sc2_ref, g2_ref, wo_ref, wr_ref, br_ref,
                  x1_ref, idx_ref, gate_ref, rank_ref, cnt_ref, h_hbm, carry_ref, hw, hsem):
    tm = x_ref.shape[0]
    half = conv_ref.shape[1]
    step = pl.program_id(0)
    buf = step % 2

    def h_out(i, s):
        return _row_tile_copies(h_hbm, i * tm, hw.at[s], hsem.at[s], to_hbm=True)

    @pl.when(step == 0)
    def _():
        carry_ref[...] = jnp.zeros_like(carry_ref)

    @pl.when(step >= 2)
    def _():
        _wait_all(h_out(step - 2, buf))

    y = (jnp.dot(conv_ref[...], wo_ref[0:half, :], preferred_element_type=F32)
         + jnp.dot(ml_ref[...], wo_ref[half:2 * half, :], preferred_element_type=F32))
    x1 = x_ref[...] + gt1_ref[0] * y
    x1_ref[...] = x1
    hn = _norm_mod(x1, g2_ref[...], sh2_ref[0], sc2_ref[0])
    hw[buf] = _pack_words(hn)
    _start_all(h_out(step, buf))

    h_hi = hn.astype(BF16)
    h_lo = (hn - h_hi.astype(F32)).astype(BF16)
    parts = (jnp.dot(h_hi, wr_ref[...], preferred_element_type=F32)
             + jnp.dot(h_lo, wr_ref[...], preferred_element_type=F32))
    scores = jax.nn.sigmoid(parts + pltpu.roll(parts, N_EXPERTS, axis=1))
    lane = lax.broadcasted_iota(I32, (tm, LANES), 1).astype(F32)
    biased = jnp.where(lane < N_EXPERTS, scores + br_ref[...], _NEG_INF)
    onehot = jnp.zeros((tm, LANES), F32)
    picks, sels = [], []
    for _ in range(TOP_K):
        mx = jnp.max(biased, axis=1, keepdims=True)
        pick = jnp.min(jnp.where(biased == mx, lane, float(LANES)), axis=1, keepdims=True)
        hit = lane == pick
        sels.append(jnp.sum(jnp.where(hit, scores, 0.0), axis=1, keepdims=True))
        picks.append(pick)
        biased = jnp.where(hit, _NEG_INF, biased)
        onehot = onehot + hit.astype(F32)
    total = sels[0]
    for s in sels[1:]:
        total = total + s

    r = lax.broadcasted_iota(I32, (tm, tm), 0)
    c = lax.broadcasted_iota(I32, (tm, tm), 1)
    strict = jnp.where(c < r, 1.0, 0.0).astype(BF16)
    before = jnp.dot(strict, onehot.astype(BF16), preferred_element_type=F32) + carry_ref[...]
    slot = lax.broadcasted_iota(I32, (tm, SUBLANES), 1)
    idx_out = jnp.zeros((tm, LANES), F32)
    rank_out = jnp.zeros((tm, LANES), F32)
    gate_out = jnp.zeros((tm, SUBLANES), F32)
    for j in range(TOP_K):
        rank = jnp.sum(jnp.where(lane == picks[j], before, 0.0), axis=1, keepdims=True)
        idx_out = jnp.where(lane == float(j), picks[j], idx_out)
        rank_out = jnp.where(lane == float(j), rank, rank_out)
        gate_out = jnp.where(slot == j, sels[j] / total * ROUTED_SCALE, gate_out)
    idx_ref[...] = idx_out.T[:SUBLANES, :].astype(I32)
    rank_ref[...] = rank_out.T[:SUBLANES, :].astype(I32)
    gate_ref[...] = gate_out
    carry_ref[...] = carry_ref[...] + jnp.sum(onehot, axis=0, keepdims=True)
    cnt_ref[...] = jnp.broadcast_to(carry_ref[...], cnt_ref.shape).astype(I32)

    @pl.when(step == pl.num_programs(0) - 1)
    def _():
        @pl.when(step >= 1)
        def _():
            _wait_all(h_out(step - 1, 1 - buf))
        _wait_all(h_out(step, buf))


def _outproj(conv, ml, x2d, mod, g2, w_out, w_router, b_router, rows_per_batch):
    n, d = x2d.shape
    tm = ROW_TILE
    tiles_per_batch = rows_per_batch // tm
    row = lambda i: (i, 0)
    mod_block = (1, 1, d)
    half = conv.shape[1]
    return pl.pallas_call(
        _outproj_body,
        grid=(n // tm,),
        in_specs=[pl.BlockSpec((tm, half), row), pl.BlockSpec((tm, half), row), pl.BlockSpec((tm, d), row),
                  pl.BlockSpec(mod_block, _mod_spec(2, tiles_per_batch)),
                  pl.BlockSpec(mod_block, _mod_spec(3, tiles_per_batch)),
                  pl.BlockSpec(mod_block, _mod_spec(4, tiles_per_batch)),
                  _resident(g2.shape), _resident(w_out.shape), _resident(w_router.shape),
                  _resident(b_router.shape)],
        out_specs=(pl.BlockSpec((tm, d), row),
                   pl.BlockSpec((SUBLANES, tm), lambda i: (0, i)), pl.BlockSpec((tm, SUBLANES), row),
                   pl.BlockSpec((SUBLANES, tm), lambda i: (0, i)),
                   pl.BlockSpec((SUBLANES, LANES), lambda i: (0, 0)),
                   pl.BlockSpec(memory_space=pl.ANY)),
        out_shape=(jax.ShapeDtypeStruct((n, d), F32),
                   jax.ShapeDtypeStruct((SUBLANES, n), I32), jax.ShapeDtypeStruct((n, SUBLANES), F32),
                   jax.ShapeDtypeStruct((SUBLANES, n), I32),
                   jax.ShapeDtypeStruct((SUBLANES, LANES), I32),
                   jax.ShapeDtypeStruct((n, SUBLANES, LANES), I32)),
        scratch_shapes=[pltpu.VMEM((1, LANES), F32), pltpu.VMEM((2, tm, d // 2), I32),
                        pltpu.SemaphoreType.DMA((2,))],
        compiler_params=_params(1),
        name="outproj_router",
    )(conv, ml, x2d, mod, mod, mod, g2, w_out, w_router, b_router)


def _sc_workers():
    info = plsc.get_sparse_core_info()
    return info.num_cores, info.num_cores * info.num_subcores


def _sc_dispatch(h_rows, dest_chunks, n_slots):
    n_tok = h_rows.shape[0]
    n_cores, n_workers = _sc_workers()
    per_worker = n_tok // (n_workers * SC_CHUNK)
    assert per_worker * n_workers * SC_CHUNK == n_tok
    mesh = plsc.VectorSubcoreMesh(core_axis_name="c", subcore_axis_name="s")

    @functools.partial(
        pl.kernel, mesh=mesh,
        out_type=jax.ShapeDtypeStruct((n_slots,) + h_rows.shape[1:], h_rows.dtype),
        scratch_types=[pltpu.VMEM((TOP_K, SC_CHUNK), I32),
                       pltpu.VMEM((SC_CHUNK,) + h_rows.shape[1:], h_rows.dtype)],
    )
    def dispatch(h_hbm, dest_hbm, out_hbm, idx_v, rows_v):
        wid = lax.axis_index("s") * n_cores + lax.axis_index("c")

        @pl.loop(0, per_worker)
        def _(i):
            chunk = wid * per_worker + i
            pltpu.sync_copy(dest_hbm.at[chunk], idx_v)
            pltpu.sync_copy(h_hbm.at[pl.ds(chunk * SC_CHUNK, SC_CHUNK)], rows_v)
            for k in range(TOP_K):
                pltpu.sync_copy(rows_v, out_hbm.at[idx_v.at[k]])

    return dispatch(h_rows, dest_chunks)


def _sc_combine(y_sorted, dest_chunks, n_tok):
    n_cores, n_workers = _sc_workers()
    per_worker = n_tok // (n_workers * SC_CHUNK)
    mesh = plsc.VectorSubcoreMesh(core_axis_name="c", subcore_axis_name="s")

    @functools.partial(
        pl.kernel, mesh=mesh,
        out_type=jax.ShapeDtypeStruct((TOP_K, n_tok) + y_sorted.shape[1:], y_sorted.dtype),
        scratch_types=[pltpu.VMEM((TOP_K, SC_CHUNK), I32),
                       pltpu.VMEM((SC_CHUNK,) + y_sorted.shape[1:], y_sorted.dtype)],
    )
    def combine(y_hbm, dest_hbm, out_hbm, idx_v, rows_v):
        wid = lax.axis_index("s") * n_cores + lax.axis_index("c")

        @pl.loop(0, per_worker)
        def _(i):
            chunk = wid * per_worker + i
            pltpu.sync_copy(dest_hbm.at[chunk], idx_v)
            for k in range(TOP_K):
                pltpu.sync_copy(y_hbm.at[idx_v.at[k]], rows_v)
                pltpu.sync_copy(rows_v, out_hbm.at[k, pl.ds(chunk * SC_CHUNK, SC_CHUNK)])

    return combine(y_sorted, dest_chunks)


def _moe_body(ord_ref, order_ref, glo_ref, ghi_ref, tot_ref, nb_ref,
              x_hbm, wg_hbm, wu_hbm, wd_hbm, y_hbm,
              wgu, wd, stage_a, stage_d, xw, yw, wsem, xsem, ysem, *, d_expert):
    b = pl.program_id(0)
    nb = nb_ref[0]
    total = tot_ref[0]
    d_model = wgu.shape[1]
    rows_a = d_model // WEIGHT_PARTS
    rows_d = d_expert // WEIGHT_PARTS

    def part_copies(g):
        e = order_ref[lax.shift_right_logical(g, PART_SHIFT)]
        i = g & (WEIGHT_PARTS - 1)
        s = lax.rem(g, WEIGHT_RING)
        return (pltpu.make_async_copy(wg_hbm.at[e, pl.ds(i * rows_a, rows_a)], stage_a.at[s, 0],
                                      wsem.at[s, 0]),
                pltpu.make_async_copy(wu_hbm.at[e, pl.ds(i * rows_a, rows_a)], stage_a.at[s, 1],
                                      wsem.at[s, 1]),
                pltpu.make_async_copy(wd_hbm.at[e, pl.ds(i * rows_d, rows_d)], stage_d.at[s],
                                      wsem.at[s, 2]))

    def start_part(g):
        for cp in part_copies(g):
            cp.start()

    def wait_part(g):
        for cp in part_copies(g):
            cp.wait()

    def cast_part(g):
        i = g & (WEIGHT_PARTS - 1)
        s = lax.rem(g, WEIGHT_RING)
        par = lax.shift_right_logical(g, PART_SHIFT) & 1
        ra = pl.multiple_of(i * rows_a, rows_a)
        rd = pl.multiple_of(i * rows_d, rows_d)
        wgu[par, pl.ds(ra, rows_a), 0:d_expert] = stage_a[s, 0].astype(BF16)
        wgu[par, pl.ds(ra, rows_a), d_expert:2 * d_expert] = stage_a[s, 1].astype(BF16)
        wd[par, pl.ds(rd, rows_d), :] = stage_d[s].astype(BF16)

    def refill(g):
        @pl.when(g + WEIGHT_RING < total)
        def _():
            start_part(g + WEIGHT_RING)

    def cast_parts(lo, hi):
        def body(g, carry):
            wait_part(g)
            cast_part(g)
            refill(g)
            return carry
        lax.fori_loop(lo, hi, body, 0)

    slot = b % 2

    def x_in(blk, s):
        return _row_tile_copies(x_hbm, blk * MOE_BLOCK, xw.at[s], xsem.at[s], to_hbm=False)

    def y_out(blk, s):
        return _row_tile_copies(y_hbm, blk * MOE_BLOCK, yw.at[s], ysem.at[s], to_hbm=True)

    @pl.when(b == 0)
    def _():
        _start_all(x_in(0, 0))
        for g in range(WEIGHT_RING):
            start_part(g)
        cast_parts(0, WEIGHT_PARTS)

    @pl.when(b + 1 < nb)
    def _():
        _start_all(x_in(b + 1, 1 - slot))

    @pl.when(b < nb)
    def _():
        par = ord_ref[b] & 1
        _wait_all(x_in(b, slot))

        @pl.when(b >= 2)
        def _():
            _wait_all(y_out(b - 2, slot))

        x = jnp.concatenate(_unpack_words(xw[slot]), axis=1).astype(BF16)
        gu = jnp.dot(x, wgu[par], preferred_element_type=F32)
        hb = (_silu(gu[:, 0:d_expert]) * gu[:, d_expert:2 * d_expert]).astype(BF16)
        yw[slot] = _pack_words(jnp.dot(hb, wd[par], preferred_element_type=F32))
        _start_all(y_out(b, slot))
        cast_parts(glo_ref[b], ghi_ref[b])

        @pl.when(b == nb - 1)
        def _():
            @pl.when(b >= 1)
            def _():
                _wait_all(y_out(b - 1, 1 - slot))
            _wait_all(y_out(b, slot))


def _moe(x_sorted, we_gate, we_up, we_down, tables):
    d, d_expert = we_gate.shape[1], we_gate.shape[2]
    nb_max = x_sorted.shape[0] // MOE_BLOCK
    any_spec = pl.BlockSpec(memory_space=pl.ANY)
    grid_spec = pltpu.PrefetchScalarGridSpec(
        num_scalar_prefetch=len(tables),
        grid=(nb_max,),
        in_specs=[any_spec, any_spec, any_spec, any_spec],
        out_specs=any_spec,
        scratch_shapes=[pltpu.VMEM((2, d, 2 * d_expert), BF16),
                        pltpu.VMEM((2, d_expert, d), BF16),
                        pltpu.VMEM((WEIGHT_RING, 2, d // WEIGHT_PARTS, d_expert), F32),
                        pltpu.VMEM((WEIGHT_RING, d_expert // WEIGHT_PARTS, d), F32),
                        pltpu.VMEM((2, MOE_BLOCK, d // 2), I32),
                        pltpu.VMEM((2, MOE_BLOCK, d // 2), I32),
                        pltpu.SemaphoreType.DMA((WEIGHT_RING, 3)),
                        pltpu.SemaphoreType.DMA((2,)),
                        pltpu.SemaphoreType.DMA((2,))],
    )
    return pl.pallas_call(
        functools.partial(_moe_body, d_expert=d_expert),
        grid_spec=grid_spec,
        out_shape=jax.ShapeDtypeStruct(x_sorted.shape, x_sorted.dtype),
        compiler_params=pltpu.CompilerParams(
            dimension_semantics=("arbitrary",), vmem_limit_bytes=MOE_VMEM_LIMIT),
        name="moe_routed",
    )(*tables, x_sorted, we_gate, we_up, we_down)


def _final_body(x1_ref, gate_ref, gt2_ref, wsgu_ref, wsd_ref, fg_ref, h_hbm, y_hbm, out_ref,
                hw, yw, sem, *, d_shared):
    tm = x1_ref.shape[0]
    step = pl.program_id(0)
    slot = step % 2

    def rows_in(i, s):
        copies = _row_tile_copies(h_hbm, i * tm, hw.at[s], sem.at[s], to_hbm=False)
        for k in range(TOP_K):
            copies += _row_tile_copies(y_hbm.at[k], i * tm, yw.at[s, k], sem.at[s], to_hbm=False)
        return copies

    @pl.when(step == 0)
    def _():
        _start_all(rows_in(0, 0))

    @pl.when(step + 1 < pl.num_programs(0))
    def _():
        _start_all(rows_in(step + 1, 1 - slot))

    _wait_all(rows_in(step, slot))
    routed = gate_ref[:, 0:1] * jnp.concatenate(_unpack_words(yw[slot, 0]), axis=1)
    for k in range(1, TOP_K):
        routed = routed + gate_ref[:, k:k + 1] * jnp.concatenate(_unpack_words(yw[slot, k]), axis=1)
    h = jnp.concatenate(_unpack_words(hw[slot]), axis=1).astype(BF16)
    gu = jnp.dot(h, wsgu_ref[...], preferred_element_type=F32)
    hb = (_silu(gu[:, 0:d_shared]) * gu[:, d_shared:2 * d_shared]).astype(BF16)
    x2 = x1_ref[...] + gt2_ref[0] * (routed + jnp.dot(hb, wsd_ref[...], preferred_element_type=F32))
    out_ref[...] = x2 * lax.rsqrt(jnp.mean(x2 * x2, axis=-1, keepdims=True) + EPS) * fg_ref[...]


def _final(h_rows, x1, y_tok, gates, mod, ws_gu, ws_d, final_g, rows_per_batch):
    n, d = x1.shape
    tm = ROW_TILE
    tiles_per_batch = rows_per_batch // tm
    row = lambda i: (i, 0)
    any_spec = pl.BlockSpec(memory_space=pl.ANY)
    return pl.pallas_call(
        functools.partial(_final_body, d_shared=ws_d.shape[0]),
        grid=(n // tm,),
        in_specs=[pl.BlockSpec((tm, d), row), pl.BlockSpec((tm, SUBLANES), row),
                  pl.BlockSpec((1, 1, d), _mod_spec(5, tiles_per_batch)),
                  _resident(ws_gu.shape), _resident(ws_d.shape), _resident(final_g.shape),
                  any_spec, any_spec],
        out_specs=pl.BlockSpec((tm, d), row),
        out_shape=jax.ShapeDtypeStruct((n, d), F32),
        scratch_shapes=[pltpu.VMEM((2, tm, d // 2), I32), pltpu.VMEM((2, TOP_K, tm, d // 2), I32),
                        pltpu.SemaphoreType.DMA((2,))],
        compiler_params=_params(1),
        name="shared_combine_final",
    )(x1, gates, mod, ws_gu, ws_d, final_g, h_rows, y_tok)


def _routing_tables(idx, rank, counts, n_tok):
    nb_max = -(-(n_tok * TOP_K) // MOE_BLOCK) + N_EXPERTS
    nblk = (counts + MOE_BLOCK - 1) // MOE_BLOCK
    blk_end = jnp.cumsum(nblk)
    blk_start = blk_end - nblk
    experts = jnp.arange(N_EXPERTS, dtype=I32)[:, None, None]
    first_slot = (blk_start * MOE_BLOCK)[:, None, None]
    dest = jnp.sum(jnp.where(idx[None] == experts, first_slot, 0), axis=0) + rank
    dest_chunks = dest.reshape(TOP_K, n_tok // SC_CHUNK, SC_CHUNK).transpose(1, 0, 2)

    blocks = jnp.arange(nb_max, dtype=I32)[:, None]
    member = (blk_start[None, :] <= blocks) & (blocks < blk_end[None, :])
    lookup = lambda table: jnp.sum(jnp.where(member, table[None, :], 0), axis=1)
    blocks = blocks[:, 0]
    nonempty = nblk > 0
    n_visited = jnp.sum(nonempty.astype(I32))
    ordinal_of = jnp.cumsum(nonempty.astype(I32)) - 1
    slots = jnp.arange(N_EXPERTS, dtype=I32)
    order = jnp.sum(jnp.where(nonempty[None, :] & (ordinal_of[None, :] == slots[:, None]),
                              slots[None, :], 0), axis=1)
    ordinal = lookup(ordinal_of)
    k_in_e = blocks - lookup(blk_start)
    nb_e = jnp.maximum(lookup(nblk), 1)
    live = (ordinal + 1 < n_visited) & (blocks < blk_end[-1])
    first = WEIGHT_PARTS * (ordinal + 1)
    lo = jnp.where(live, first + WEIGHT_PARTS * k_in_e // nb_e, 0)
    hi = jnp.where(live, first + WEIGHT_PARTS * (k_in_e + 1) // nb_e, 0)
    tables = (ordinal.astype(I32), order, lo.astype(I32), hi.astype(I32),
              (WEIGHT_PARTS * n_visited).reshape(1).astype(I32), blk_end[-1:].astype(I32))
    return tables, dest_chunks, nb_max * MOE_BLOCK


def kernel(x, c, ctx, c_ctx, norm1_g, norm2_g, w_ada, b_ada, w_in, conv_w, gate_b, head_g, w_out,
           w_router, b_router, we_gate, we_up, we_down, ws_gate, ws_up, ws_down, final_g):
    assert w_ada.shape[0] == 1, "single-layer block"
    bsz, seq, d = x.shape
    ctx_len = ctx.shape[1]
    n_tok = bsz * seq
    conv_dim = conv_w.shape[2]
    v_all = head_g.shape[1]
    qk_all = (w_in.shape[2] - 3 * conv_dim - 2 * v_all - N_GATES) // 2
    assert seq % ROW_TILE == 0 and ctx_len % ROW_TILE == 0 and ROW_TILE % GRID_W == 0
    assert bsz + 1 <= SUBLANES

    cc = jnp.zeros((SUBLANES, d), F32).at[:bsz].set(c).at[bsz].set(c_ctx)
    mod = _adaln(cc, w_ada[0], b_ada).reshape(SUBLANES * 6, 1, d)

    n_main = 3 * conv_dim + 2 * qk_all + 2 * v_all
    w_main = w_in[0, :, :n_main].astype(BF16)
    k_lo = 3 * conv_dim + qk_all
    w_kt = w_in[0, :, k_lo:k_lo + qk_all].T.astype(BF16)
    w_v = w_in[0, :, k_lo + qk_all:k_lo + qk_all + v_all].astype(BF16)
    w_gate = jnp.zeros((d, LANES), BF16).at[:, :N_GATES].set(w_in[0, :, n_main:].astype(BF16))
    gate_bias = jnp.zeros((1, LANES), F32).at[0, :N_GATES].set(gate_b[0].reshape(-1))

    x2d = x.reshape(n_tok, d)
    conv, q, kt, v, og, g, gt = _inproj(x2d, mod, norm1_g, w_main, w_kt, w_gate, gate_bias, conv_w[0],
                                       seq, conv_dim, qk_all, v_all)
    ktc, vc, _, gtc = _inproj_ctx(ctx.reshape(bsz * ctx_len, d), mod, norm1_g, w_v, w_kt, w_gate,
                                  gate_bias, bsz)

    h_bwd = _mlstm(1, q, kt, v, g, gt, ktc, vc, gtc, None, bsz)
    ml = _mlstm(0, q, kt, v, g, gt, ktc, vc, gtc, (h_bwd, og), bsz, head_g)

    assert 2 * N_EXPERTS == LANES
    w_r_hi = w_router[0].astype(BF16)
    w_r = jnp.concatenate([w_r_hi, (w_router[0] - w_r_hi.astype(F32)).astype(BF16)], axis=1)
    b_r = jnp.zeros((1, LANES), F32).at[0, :N_EXPERTS].set(b_router[0])
    x1, idx, gates, rank, cnt, h_rows = _outproj(conv, ml, x2d, mod, norm2_g, w_out[0].astype(BF16),
                                             w_r, b_r, seq)

    tables, dest_chunks, n_slots = _routing_tables(idx[:TOP_K], rank[:TOP_K], cnt[0, :N_EXPERTS], n_tok)
    x_sorted = _sc_dispatch(h_rows, dest_chunks, n_slots)
    y_sorted = _moe(x_sorted, we_gate[0], we_up[0], we_down[0], tables)
    y_tok = _sc_combine(y_sorted, dest_chunks, n_tok)

    ws_gu = jnp.concatenate([ws_gate[0], ws_up[0]], axis=1).astype(BF16)
    out = _final(h_rows, x1, y_tok, gates, mod, ws_gu, ws_down[0].astype(BF16),
                 final_g.reshape(1, d), seq)
    return out.reshape(bsz, seq, d)
```

```python
import functools

import jax
import jax.numpy as jnp
from jax import lax
from jax.experimental import pallas as pl
from jax.experimental.pallas import tpu as pltpu
from jax.experimental.pallas import tpu_sc as plsc

F32 = jnp.float32
BF16 = jnp.bfloat16
I32 = jnp.int32

N_HEADS = 4
GRID_W = 64
CHUNK = 128
TOP_K = 6
N_EXPERTS = 64
ROUTED_SCALE = 2.446
EPS = 1e-6
N_GATES = 4 * N_HEADS

LANES = 128
SUBLANES = 8
MOE_BLOCK = 256
ROW_TILE = 256
ADALN_TILE = 1024
WEIGHT_PARTS = 8
PART_SHIFT = 3
WEIGHT_RING = 3
SC_CHUNK = 64
HIGH_HALF = -65536
VMEM_LIMIT = 56 * 1024 * 1024
MOE_VMEM_LIMIT = 62 * 1024 * 1024

_HIGHEST = lax.Precision.HIGHEST
_NEG_INF = float("-inf")


def _resident(shape):
    nd = len(shape)
    return pl.BlockSpec(shape, lambda *_: (0,) * nd, pipeline_mode=pl.Buffered(1))


def _params(n_axes):
    return pltpu.CompilerParams(
        dimension_semantics=("arbitrary",) * n_axes, vmem_limit_bytes=VMEM_LIMIT)


def _log_sigmoid(x):
    return jnp.minimum(x, 0.0) - jnp.log1p(jnp.exp(-jnp.abs(x)))


def _silu(x):
    return x * jax.nn.sigmoid(x)


def _pack_words(val):
    half = val.shape[1] // 2
    lo = lax.bitcast_convert_type(val[:, :half].astype(BF16).astype(F32), I32)
    hi = lax.bitcast_convert_type(val[:, half:].astype(BF16).astype(F32), I32)
    return (hi & HIGH_HALF) | lax.shift_right_logical(lo, 16)


def _unpack_words(word):
    lo = lax.bitcast_convert_type(lax.shift_left(word, 16), F32)
    hi = lax.bitcast_convert_type(word & HIGH_HALF, F32)
    return lo, hi


def _row_tile_copies(hbm_rows, row0, tile, sem, to_hbm):
    n = tile.shape[0]
    copies = []
    for c in range(SUBLANES):
        hbm = hbm_rows.at[pl.ds(row0, n), c, :]
        vmem = tile.at[:, pl.ds(c * LANES, LANES)]
        copies.append(pltpu.make_async_copy(vmem, hbm, sem) if to_hbm
                      else pltpu.make_async_copy(hbm, vmem, sem))
    return copies


def _start_all(copies):
    for cp in copies:
        cp.start()


def _wait_all(copies):
    for cp in copies:
        cp.wait()


def _adaln_body(c_ref, w_ref, b_ref, o_ref):
    s = _silu(c_ref[...])
    o_ref[...] = jnp.dot(s.astype(BF16), w_ref[...].astype(BF16),
                         preferred_element_type=F32) + b_ref[...]


def _adaln(cc, w, b):
    d, n6 = w.shape
    return pl.pallas_call(
        _adaln_body,
        grid=(n6 // ADALN_TILE,),
        in_specs=[pl.BlockSpec((SUBLANES, d), lambda j: (0, 0)),
                  pl.BlockSpec((d, ADALN_TILE), lambda j: (0, j)),
                  pl.BlockSpec((1, ADALN_TILE), lambda j: (0, j))],
        out_specs=pl.BlockSpec((SUBLANES, ADALN_TILE), lambda j: (0, j)),
        out_shape=jax.ShapeDtypeStruct((SUBLANES, n6), F32),
        compiler_params=_params(1),
        name="adaln",
    )(cc, w, b)


def _norm_mod(x, g, shift, scale):
    y = x * lax.rsqrt(jnp.mean(x * x, axis=-1, keepdims=True) + EPS) * g
    return y * (1.0 + scale) + shift


def _gate_prep(xb, wg, gb_ref, g_ref, gt_ref):
    tm = xb.shape[0]
    gg = jnp.dot(xb, wg, preferred_element_type=F32) + gb_ref[...]
    lane = lax.broadcasted_iota(I32, (tm, LANES), 1)
    is_f = (lane & N_HEADS) != 0
    is_bwd = (lane & (2 * N_HEADS)) != 0
    lf = jnp.where(is_f, _log_sigmoid(gg), 0.0)
    r = lax.broadcasted_iota(I32, (tm, tm), 0)
    c = lax.broadcasted_iota(I32, (tm, tm), 1)
    same = (r // CHUNK) == (c // CHUNK)
    tri_l = jnp.where(same & (c <= r), 1.0, 0.0).astype(F32)
    tri_u = jnp.where(same & (c >= r), 1.0, 0.0).astype(F32)
    pre = jnp.dot(tri_l, lf, precision=_HIGHEST, preferred_element_type=F32)
    suf = jnp.dot(tri_u, lf, precision=_HIGHEST, preferred_element_type=F32)
    out = jnp.where(is_f, jnp.where(is_bwd, suf, pre), gg)
    g_ref[...] = out[:, :N_GATES]
    gt_ref[...] = out.T[:N_GATES, :]


def _project_transposed(wt_ref, xb):
    return lax.dot_general(wt_ref[...], xb, (((1,), (1,)), ((), ())),
                           preferred_element_type=F32).astype(BF16)


def _inproj_body(x_ref, sh_ref, sc_ref, g1_ref, w_ref, wkt_ref, gb_ref, cw_ref,
                 conv_ref, q_ref, k_ref, v_ref, o_ref, g_ref, gt_ref, *, conv_dim, qk_all, v_all):
    tm = x_ref.shape[0]
    xb = _norm_mod(x_ref[...], g1_ref[...], sh_ref[0], sc_ref[0]).astype(BF16)

    def proj(lo, width):
        return jnp.dot(xb, w_ref[:, lo:lo + width], preferred_element_type=F32)

    u = proj(conv_dim, conv_dim) * proj(2 * conv_dim, conv_dim)
    pos = lax.broadcasted_iota(I32, (tm, 1), 0) % GRID_W
    um = jnp.where(pos == 0, 0.0, pltpu.roll(u, 1, axis=0))
    up = jnp.where(pos == GRID_W - 1, 0.0, pltpu.roll(u, tm - 1, axis=0))
    y = um * cw_ref[0:1, :] + u * cw_ref[1:2, :] + up * cw_ref[2:3, :]
    conv_ref[...] = (proj(0, conv_dim) * y).astype(BF16)

    off = 3 * conv_dim
    qscale = (qk_all // N_HEADS) ** -0.5
    q_ref[...] = (proj(off, qk_all) * qscale).astype(BF16)
    k_ref[...] = _project_transposed(wkt_ref, xb)
    v_ref[...] = proj(off + 2 * qk_all, v_all).astype(BF16)
    o_ref[...] = jax.nn.sigmoid(proj(off + 2 * qk_all + v_all, v_all)).astype(BF16)
    gate_lo = off + 2 * qk_all + 2 * v_all
    _gate_prep(xb, w_ref[:, gate_lo:gate_lo + LANES], gb_ref, g_ref, gt_ref)


def _inproj_ctx_body(x_ref, sh_ref, sc_ref, g1_ref, w_ref, wkt_ref, wg_ref, gb_ref,
                     k_ref, v_ref, g_ref, gt_ref):
    xb = _norm_mod(x_ref[...], g1_ref[...], sh_ref[0], sc_ref[0]).astype(BF16)
    k_ref[...] = _project_transposed(wkt_ref, xb)
    v_ref[...] = jnp.dot(xb, w_ref[...], preferred_element_type=F32).astype(BF16)
    _gate_prep(xb, wg_ref[...], gb_ref, g_ref, gt_ref)


def _mod_spec(part, tiles_per_row, fixed_row=None):
    def index(i):
        row = fixed_row if fixed_row is not None else i // tiles_per_row
        return (row * 6 + part, 0, 0)

    return index


def _inproj(x2d, mod, g1, w_all, w_kt, gate_b, conv_w, rows_per_batch, conv_dim, qk_all, v_all):
    n, d = x2d.shape
    tm = ROW_TILE
    tiles_per_batch = rows_per_batch // tm
    row = lambda i: (i, 0)
    mod_block = (1, 1, d)
    out_shapes = (
        jax.ShapeDtypeStruct((n, conv_dim), BF16),
        jax.ShapeDtypeStruct((n, qk_all), BF16),
        jax.ShapeDtypeStruct((qk_all, n), BF16),
        jax.ShapeDtypeStruct((n, v_all), BF16),
        jax.ShapeDtypeStruct((n, v_all), BF16),
        jax.ShapeDtypeStruct((n, N_GATES), F32),
        jax.ShapeDtypeStruct((N_GATES, n), F32),
    )
    out_specs = (
        pl.BlockSpec((tm, conv_dim), row),
        pl.BlockSpec((tm, qk_all), row),
        pl.BlockSpec((qk_all, tm), lambda i: (0, i)),
        pl.BlockSpec((tm, v_all), row),
        pl.BlockSpec((tm, v_all), row),
        pl.BlockSpec((tm, N_GATES), row),
        pl.BlockSpec((N_GATES, tm), lambda i: (0, i)),
    )
    return pl.pallas_call(
        functools.partial(_inproj_body, conv_dim=conv_dim, qk_all=qk_all, v_all=v_all),
        grid=(n // tm,),
        in_specs=[pl.BlockSpec((tm, d), row),
                  pl.BlockSpec(mod_block, _mod_spec(0, tiles_per_batch)),
                  pl.BlockSpec(mod_block, _mod_spec(1, tiles_per_batch)),
                  _resident(g1.shape), _resident(w_all.shape), _resident(w_kt.shape),
                  _resident(gate_b.shape), _resident(conv_w.shape)],
        out_specs=out_specs,
        out_shape=out_shapes,
        compiler_params=_params(1),
        name="inproj",
    )(x2d, mod, mod, g1, w_all, w_kt, gate_b, conv_w)


def _inproj_ctx(c2d, mod, g1, w_v, w_kt, w_gate, gate_b, ctx_mod_row):
    n, d = c2d.shape
    tm = ROW_TILE
    row = lambda i: (i, 0)
    mod_block = (1, 1, d)
    qk_all, v_all = w_kt.shape[0], w_v.shape[1]
    return pl.pallas_call(
        _inproj_ctx_body,
        grid=(n // tm,),
        in_specs=[pl.BlockSpec((tm, d), row),
                  pl.BlockSpec(mod_block, _mod_spec(0, 1, ctx_mod_row)),
                  pl.BlockSpec(mod_block, _mod_spec(1, 1, ctx_mod_row)),
                  _resident(g1.shape), _resident(w_v.shape), _resident(w_kt.shape),
                  _resident(w_gate.shape), _resident(gate_b.shape)],
        out_specs=(pl.BlockSpec((qk_all, tm), lambda i: (0, i)), pl.BlockSpec((tm, v_all), row),
                   pl.BlockSpec((tm, N_GATES), row), pl.BlockSpec((N_GATES, tm), lambda i: (0, i))),
        out_shape=(jax.ShapeDtypeStruct((qk_all, n), BF16), jax.ShapeDtypeStruct((n, v_all), BF16),
                   jax.ShapeDtypeStruct((n, N_GATES), F32), jax.ShapeDtypeStruct((N_GATES, n), F32)),
        compiler_params=_params(1),
        name="inproj_ctx",
    )(c2d, mod, mod, g1, w_v, w_kt, w_gate, gate_b)


def _with_ones(v):
    return jnp.concatenate([v, jnp.ones((v.shape[0], LANES), v.dtype)], axis=1)


def _mlstm_state_update(h, direction, kt_ref, v_ref, gt_ref, s_ref, m_ref, qk, vh):
    ci = direction * 2 * N_HEADS + h
    cb = ci + N_HEADS
    last = 0 if direction else CHUNK - 1
    kt = kt_ref[h * qk:(h + 1) * qk, :].astype(F32)
    va = _with_ones(v_ref[:, h * vh:(h + 1) * vh])
    b_last = gt_ref[cb:cb + 1, last:last + 1]
    m_prev = m_ref[h][0:1, 0:1]
    g_r = b_last - gt_ref[cb:cb + 1, :] + gt_ref[ci:ci + 1, :]
    m_new = jnp.maximum(b_last + m_prev, jnp.max(g_r, axis=1, keepdims=True))
    a = jnp.exp(b_last + m_prev - m_new)
    kw = (kt * jnp.exp(g_r - m_new)).astype(BF16)
    s_ref[h] = a * s_ref[h] + jnp.dot(kw, va, preferred_element_type=F32)
    m_ref[h] = jnp.broadcast_to(m_new, m_ref.shape[1:])


def _mlstm_head_output(h, direction, q_ref, kt_ref, v_ref, g_ref, gt_ref, s_ref, m_ref, qk, vh):
    ci = direction * 2 * N_HEADS + h
    cb = ci + N_HEADS
    q = q_ref[:, h * qk:(h + 1) * qk]
    kt = kt_ref[h * qk:(h + 1) * qk, :]
    va = _with_ones(v_ref[:, h * vh:(h + 1) * vh])
    ig_r = gt_ref[ci:ci + 1, :]
    b_r = gt_ref[cb:cb + 1, :]
    b_c = g_ref[:, cb:cb + 1]
    m_prev = m_ref[h][0:1, 0:1]
    row = lax.broadcasted_iota(I32, (CHUNK, CHUNK), 0)
    col = lax.broadcasted_iota(I32, (CHUNK, CHUNK), 1)
    mask = (col >= row) if direction else (col <= row)
    dm = jnp.where(mask, b_c + (ig_r - b_r), _NEG_INF)
    inter = b_c + m_prev
    m_t = jnp.maximum(inter, jnp.max(dm, axis=1, keepdims=True))
    w_inter = jnp.exp(inter - m_t)
    s = jnp.dot(q, kt, preferred_element_type=F32) * jnp.exp(dm - m_t)
    intra = jnp.dot(s.astype(BF16), va, preferred_element_type=F32)
    carried = jnp.dot(q, s_ref[h].astype(BF16), preferred_element_type=F32)
    num = intra[:, 0:vh] + w_inter * carried[:, 0:vh]
    den = intra[:, vh:vh + 1] + w_inter * carried[:, vh:vh + 1]
    return num / jnp.maximum(jnp.abs(den), jnp.exp(-m_t))


def _mlstm_body(*refs, direction, bsz, n_ctx_chunks, qk, vh):
    q_ref, v_ref, g_ref, vc_ref = refs[0:4]
    kt_refs, gt_refs = refs[4:4 + bsz], refs[4 + bsz:4 + 2 * bsz]
    ktc_refs, gtc_refs = refs[4 + 2 * bsz:4 + 3 * bsz], refs[4 + 3 * bsz:4 + 4 * bsz]
    rest = refs[4 + 4 * bsz:]
    if direction:
        out_ref, s_ref, m_ref = rest
    else:
        hb_ref, og_ref, hg_ref, out_ref, s_ref, m_ref = rest
    step = pl.program_id(0)

    @pl.when(step == 0)
    def _():
        s_ref[...] = jnp.zeros_like(s_ref)
        m_ref[...] = jnp.full_like(m_ref, _NEG_INF)

    @pl.when(step < n_ctx_chunks)
    def _():
        for b in range(bsz):
            for h in range(N_HEADS):
                _mlstm_state_update(h, direction, ktc_refs[b], vc_ref.at[b], gtc_refs[b],
                                    s_ref.at[b], m_ref.at[b], qk, vh)

    @pl.when(step >= n_ctx_chunks)
    def _():
        for b in range(bsz):
            for h in range(N_HEADS):
                hh = _mlstm_head_output(h, direction, q_ref.at[b], kt_refs[b], v_ref.at[b], g_ref.at[b],
                                        gt_refs[b], s_ref.at[b], m_ref.at[b], qk, vh)
                cols = slice(h * vh, (h + 1) * vh)
                if direction:
                    out_ref[b, :, cols] = hh
                else:
                    hs = hh + hb_ref[b, :, cols]
                    hs = hs * lax.rsqrt(jnp.mean(hs * hs, axis=-1, keepdims=True) + EPS)
                    out_ref[b, :, cols] = (hs * hg_ref[:, cols]
                                           * og_ref[b, :, cols].astype(F32)).astype(BF16)
                _mlstm_state_update(h, direction, kt_refs[b], v_ref.at[b], gt_refs[b],
                                    s_ref.at[b], m_ref.at[b], qk, vh)


def _mlstm(direction, q, kt, v, g, gt, ktc, vc, gtc, extra, bsz, head_g=None):
    n, qk_all = q.shape
    v_all = v.shape[1]
    qk, vh = qk_all // N_HEADS, v_all // N_HEADS
    seq = n // bsz
    nc = seq // CHUNK
    ncc = vc.shape[0] // bsz // CHUNK

    def lat(s):
        j = jnp.clip(s - ncc, 0, nc - 1)
        return nc - 1 - j if direction else j

    def ctx(s):
        j = jnp.clip(s, 0, ncc - 1)
        return ncc - 1 - j if direction else j

    def per_batch(a):
        return a.reshape(bsz, a.shape[0] // bsz, a.shape[1])

    lat_blk = lambda c: pl.BlockSpec((bsz, CHUNK, c), lambda s: (0, lat(s), 0))
    in_specs = [lat_blk(qk_all), lat_blk(v_all), lat_blk(N_GATES),
                pl.BlockSpec((bsz, CHUNK, v_all), lambda s: (0, ctx(s), 0))]
    args = [per_batch(q), per_batch(v), per_batch(g), per_batch(vc)]
    for arr, rows, n_chunks, pos in ((kt, qk_all, nc, lat), (gt, N_GATES, nc, lat),
                                     (ktc, qk_all, ncc, ctx), (gtc, N_GATES, ncc, ctx)):
        for b in range(bsz):
            in_specs.append(pl.BlockSpec((rows, CHUNK), lambda s, b=b, n_chunks=n_chunks, pos=pos:
                                         (0, b * n_chunks + pos(s))))
            args.append(arr)
    if direction:
        out_dtype = F32
    else:
        hb, og = extra
        in_specs += [lat_blk(v_all), lat_blk(v_all), pl.BlockSpec((1, v_all), lambda s: (0, 0))]
        args += [per_batch(hb), per_batch(og), head_g]
        out_dtype = BF16
    out = pl.pallas_call(
        functools.partial(_mlstm_body, direction=direction, bsz=bsz, n_ctx_chunks=ncc, qk=qk, vh=vh),
        grid=(ncc + nc,),
        in_specs=in_specs,
        out_specs=lat_blk(v_all),
        out_shape=jax.ShapeDtypeStruct((bsz, seq, v_all), out_dtype),
        scratch_shapes=[pltpu.VMEM((bsz, N_HEADS, qk, vh + LANES), F32),
                        pltpu.VMEM((bsz, N_HEADS, SUBLANES, LANES), F32)],
        compiler_params=_params(1),
        name="mlstm_bwd" if direction else "mlstm_fwd",
    )(*args)
    return out.reshape(n, v_all)


def _outproj_body(conv_ref, ml_ref, x_ref, gt1_ref, sh2_ref, sc2_ref, g2_ref, wo_ref, wr_ref, br_ref,
                  x1_ref, idx_ref, gate_ref, rank_ref, cnt_ref, h_hbm, carry_ref, hw, hsem):
    tm = x_ref.shape[0]
    half = conv_ref.shape[1]
    step = pl.program_id(0)
    buf = step % 2

    def h_out(i, s):
        return _row_tile_copies(h_hbm, i * tm, hw.at[s], hsem.at[s], to_hbm=True)

    @pl.when(step == 0)
    def _():
        carry_ref[...] = jnp.zeros_like(carry_ref)

    @pl.when(step >= 2)
    def _():
        _wait_all(h_out(step - 2, buf))

    y = (jnp.dot(conv_ref[...], wo_ref[0:half, :], preferred_element_type=F32)
         + jnp.dot(ml_ref[...], wo_ref[half:2 * half, :], preferred_element_type=F32))
    x1 = x_ref[...] + gt1_ref[0] * y
    x1_ref[...] = x1
    hn = _norm_mod(x1, g2_ref[...], sh2_ref[0], sc2_ref[0])
    hw[buf] = _pack_words(hn)
    _start_all(h_out(step, buf))

    h_hi = hn.astype(BF16)
    h_lo = (hn - h_hi.astype(F32)).astype(BF16)
    parts = (jnp.dot(h_hi, wr_ref[...], preferred_element_type=F32)
             + jnp.dot(h_lo, wr_ref[...], preferred_element_type=F32))
    scores = jax.nn.sigmoid(parts + pltpu.roll(parts, N_EXPERTS, axis=1))
    lane = lax.broadcasted_iota(I32, (tm, LANES), 1).astype(F32)
    biased = jnp.where(lane < N_EXPERTS, scores + br_ref[...], _NEG_INF)
    onehot = jnp.zeros((tm, LANES), F32)
    picks, sels = [], []
    for _ in range(TOP_K):
        mx = jnp.max(biased, axis=1, keepdims=True)
        pick = jnp.min(jnp.where(biased == mx, lane, float(LANES)), axis=1, keepdims=True)
        hit = lane == pick
        sels.append(jnp.sum(jnp.where(hit, scores, 0.0), axis=1, keepdims=True))
        picks.append(pick)
        biased = jnp.where(hit, _NEG_INF, biased)
        onehot = onehot + hit.astype(F32)
    total = sels[0]
    for s in sels[1:]:
        total = total + s

    r = lax.broadcasted_iota(I32, (tm, tm), 0)
    c = lax.broadcasted_iota(I32, (tm, tm), 1)
    strict = jnp.where(c < r, 1.0, 0.0).astype(BF16)
    before = jnp.dot(strict, onehot.astype(BF16), preferred_element_type=F32) + carry_ref[...]
    slot = lax.broadcasted_iota(I32, (tm, SUBLANES), 1)
    idx_out = jnp.zeros((tm, LANES), F32)
    rank_out = jnp.zeros((tm, LANES), F32)
    gate_out = jnp.zeros((tm, SUBLANES), F32)
    for j in range(TOP_K):
        rank = jnp.sum(jnp.where(lane == picks[j], before, 0.0), axis=1, keepdims=True)
        idx_out = jnp.where(lane == float(j), picks[j], idx_out)
        rank_out = jnp.where(lane == float(j), rank, rank_out)
        gate_out = jnp.where(slot == j, sels[j] / total * ROUTED_SCALE, gate_out)
    idx_ref[...] = idx_out.T[:SUBLANES, :].astype(I32)
    rank_ref[...] = rank_out.T[:SUBLANES, :].astype(I32)
    gate_ref[...] = gate_out
    carry_ref[...] = carry_ref[...] + jnp.sum(onehot, axis=0, keepdims=True)
    cnt_ref[...] = jnp.broadcast_to(carry_ref[...], cnt_ref.shape).astype(I32)

    @pl.when(step == pl.num_programs(0) - 1)
    def _():
        @pl.when(step >= 1)
        def _():
            _wait_all(h_out(step - 1, 1 - buf))
        _wait_all(h_out(step, buf))


def _outproj(conv, ml, x2d, mod, g2, w_out, w_router, b_router, rows_per_batch):
    n, d = x2d.shape
    tm = ROW_TILE
    tiles_per_batch = rows_per_batch // tm
    row = lambda i: (i, 0)
    mod_block = (1, 1, d)
    half = conv.shape[1]
    return pl.pallas_call(
        _outproj_body,
        grid=(n // tm,),
        in_specs=[pl.BlockSpec((tm, half), row), pl.BlockSpec((tm, half), row), pl.BlockSpec((tm, d), row),
                  pl.BlockSpec(mod_block, _mod_spec(2, tiles_per_batch)),
                  pl.BlockSpec(mod_block, _mod_spec(3, tiles_per_batch)),
                  pl.BlockSpec(mod_block, _mod_spec(4, tiles_per_batch)),
                  _resident(g2.shape), _resident(w_out.shape), _resident(w_router.shape),
                  _resident(b_router.shape)],
        out_specs=(pl.BlockSpec((tm, d), row),
                   pl.BlockSpec((SUBLANES, tm), lambda i: (0, i)), pl.BlockSpec((tm, SUBLANES), row),
                   pl.BlockSpec((SUBLANES, tm), lambda i: (0, i)),
                   pl.BlockSpec((SUBLANES, LANES), lambda i: (0, 0)),
                   pl.BlockSpec(memory_space=pl.ANY)),
        out_shape=(jax.ShapeDtypeStruct((n, d), F32),
                   jax.ShapeDtypeStruct((SUBLANES, n), I32), jax.ShapeDtypeStruct((n, SUBLANES), F32),
                   jax.ShapeDtypeStruct((SUBLANES, n), I32),
                   jax.ShapeDtypeStruct((SUBLANES, LANES), I32),
                   jax.ShapeDtypeStruct((n, SUBLANES, LANES), I32)),
        scratch_shapes=[pltpu.VMEM((1, LANES), F32), pltpu.VMEM((2, tm, d // 2), I32),
                        pltpu.SemaphoreType.DMA((2,))],
        compiler_params=_params(1),
        name="outproj_router",
    )(conv, ml, x2d, mod, mod, mod, g2, w_out, w_router, b_router)


def _sc_workers():
    info = plsc.get_sparse_core_info()
    return info.num_cores, info.num_cores * info.num_subcores


def _sc_dispatch(h_rows, dest_chunks, n_slots):
    n_tok = h_rows.shape[0]
    n_cores, n_workers = _sc_workers()
    per_worker = n_tok // (n_workers * SC_CHUNK)
    assert per_worker * n_workers * SC_CHUNK == n_tok
    mesh = plsc.VectorSubcoreMesh(core_axis_name="c", subcore_axis_name="s")

    @functools.partial(
        pl.kernel, mesh=mesh,
        out_type=jax.ShapeDtypeStruct((n_slots,) + h_rows.shape[1:], h_rows.dtype),
        scratch_types=[pltpu.VMEM((TOP_K, SC_CHUNK), I32),
                       pltpu.VMEM((SC_CHUNK,) + h_rows.shape[1:], h_rows.dtype)],
    )
    def dispatch(h_hbm, dest_hbm, out_hbm, idx_v, rows_v):
        wid = lax.axis_index("s") * n_cores + lax.axis_index("c")

        @pl.loop(0, per_worker)
        def _(i):
            chunk = wid * per_worker + i
            pltpu.sync_copy(dest_hbm.at[chunk], idx_v)
            pltpu.sync_copy(h_hbm.at[pl.ds(chunk * SC_CHUNK, SC_CHUNK)], rows_v)
            for k in range(TOP_K):
                pltpu.sync_copy(rows_v, out_hbm.at[idx_v.at[k]])

    return dispatch(h_rows, dest_chunks)


def _sc_combine(y_sorted, dest_chunks, n_tok):
    n_cores, n_workers = _sc_workers()
    per_worker = n_tok // (n_workers * SC_CHUNK)
    mesh = plsc.VectorSubcoreMesh(core_axis_name="c", subcore_axis_name="s")

    @functools.partial(
        pl.kernel, mesh=mesh,
        out_type=jax.ShapeDtypeStruct((TOP_K, n_tok) + y_sorted.shape[1:], y_sorted.dtype),
        scratch_types=[pltpu.VMEM((TOP_K, SC_CHUNK), I32),
                       pltpu.VMEM((SC_CHUNK,) + y_sorted.shape[1:], y_sorted.dtype)],
    )
    def combine(y_hbm, dest_hbm, out_hbm, idx_v, rows_v):
        wid = lax.axis_index("s") * n_cores + lax.axis_index("c")

        @pl.loop(0, per_worker)
        def _(i):
            chunk = wid * per_worker + i
            pltpu.sync_copy(dest_hbm.at[chunk], idx_v)
            for k in range(TOP_K):
                pltpu.sync_copy(y_hbm.at[idx_v.at[k]], rows_v)
                pltpu.sync_copy(rows_v, out_hbm.at[k, pl.ds(chunk * SC_CHUNK, SC_CHUNK)])

    return combine(y_sorted, dest_chunks)


def _moe_body(ord_ref, order_ref, glo_ref, ghi_ref, tot_ref, nb_ref,
              x_hbm, wg_hbm, wu_hbm, wd_hbm, y_hbm,
              wgu, wd, stage_a, stage_d, xw, yw, wsem, xsem, ysem, *, d_expert):
    b = pl.program_id(0)
    nb = nb_ref[0]
    total = tot_ref[0]
    d_model = wgu.shape[1]
    rows_a = d_model // WEIGHT_PARTS
    rows_d = d_expert // WEIGHT_PARTS

    def part_copies(g):
        e = order_ref[lax.shift_right_logical(g, PART_SHIFT)]
        i = g & (WEIGHT_PARTS - 1)
        s = lax.rem(g, WEIGHT_RING)
        return (pltpu.make_async_copy(wg_hbm.at[e, pl.ds(i * rows_a, rows_a)], stage_a.at[s, 0],
                                      wsem.at[s, 0]),
                pltpu.make_async_copy(wu_hbm.at[e, pl.ds(i * rows_a, rows_a)], stage_a.at[s, 1],
                                      wsem.at[s, 1]),
                pltpu.make_async_copy(wd_hbm.at[e, pl.ds(i * rows_d, rows_d)], stage_d.at[s],
                                      wsem.at[s, 2]))

    def start_part(g):
        for cp in part_copies(g):
            cp.start()

    def wait_part(g):
        for cp in part_copies(g):
            cp.wait()

    def cast_part(g):
        i = g & (WEIGHT_PARTS - 1)
        s = lax.rem(g, WEIGHT_RING)
        par = lax.shift_right_logical(g, PART_SHIFT) & 1
        ra = pl.multiple_of(i * rows_a, rows_a)
        rd = pl.multiple_of(i * rows_d, rows_d)
        wgu[par, pl.ds(ra, rows_a), 0:d_expert] = stage_a[s, 0].astype(BF16)
        wgu[par, pl.ds(ra, rows_a), d_expert:2 * d_expert] = stage_a[s, 1].astype(BF16)
        wd[par, pl.ds(rd, rows_d), :] = stage_d[s].astype(BF16)

    def refill(g):
        @pl.when(g + WEIGHT_RING < total)
        def _():
            start_part(g + WEIGHT_RING)

    def cast_parts(lo, hi):
        def body(g, carry):
            wait_part(g)
            cast_part(g)
            refill(g)
            return carry
        lax.fori_loop(lo, hi, body, 0)

    slot = b % 2

    def x_in(blk, s):
        return _row_tile_copies(x_hbm, blk * MOE_BLOCK, xw.at[s], xsem.at[s], to_hbm=False)

    def y_out(blk, s):
        return _row_tile_copies(y_hbm, blk * MOE_BLOCK, yw.at[s], ysem.at[s], to_hbm=True)

    @pl.when(b == 0)
    def _():
        _start_all(x_in(0, 0))
        for g in range(WEIGHT_RING):
            start_part(g)
        cast_parts(0, WEIGHT_PARTS)

    @pl.when(b + 1 < nb)
    def _():
        _start_all(x_in(b + 1, 1 - slot))

    @pl.when(b < nb)
    def _():
        par = ord_ref[b] & 1
        _wait_all(x_in(b, slot))

        @pl.when(b >= 2)
        def _():
            _wait_all(y_out(b - 2, slot))

        x = jnp.concatenate(_unpack_words(xw[slot]), axis=1).astype(BF16)
        gu = jnp.dot(x, wgu[par], preferred_element_type=F32)
        hb = (_silu(gu[:, 0:d_expert]) * gu[:, d_expert:2 * d_expert]).astype(BF16)
        yw[slot] = _pack_words(jnp.dot(hb, wd[par], preferred_element_type=F32))
        _start_all(y_out(b, slot))
        cast_parts(glo_ref[b], ghi_ref[b])

        @pl.when(b == nb - 1)
        def _():
            @pl.when(b >= 1)
            def _():
                _wait_all(y_out(b - 1, 1 - slot))
            _wait_all(y_out(b, slot))


def _moe(x_sorted, we_gate, we_up, we_down, tables):
    d, d_expert = we_gate.shape[1], we_gate.shape[2]
    nb_max = x_sorted.shape[0] // MOE_BLOCK
    any_spec = pl.BlockSpec(memory_space=pl.ANY)
    grid_spec = pltpu.PrefetchScalarGridSpec(
        num_scalar_prefetch=len(tables),
        grid=(nb_max,),
        in_specs=[any_spec, any_spec, any_spec, any_spec],
        out_specs=any_spec,
        scratch_shapes=[pltpu.VMEM((2, d, 2 * d_expert), BF16),
                        pltpu.VMEM((2, d_expert, d), BF16),
                        pltpu.VMEM((WEIGHT_RING, 2, d // WEIGHT_PARTS, d_expert), F32),
                        pltpu.VMEM((WEIGHT_RING, d_expert // WEIGHT_PARTS, d), F32),
                        pltpu.VMEM((2, MOE_BLOCK, d // 2), I32),
                        pltpu.VMEM((2, MOE_BLOCK, d // 2), I32),
                        pltpu.SemaphoreType.DMA((WEIGHT_RING, 3)),
                        pltpu.SemaphoreType.DMA((2,)),
                        pltpu.SemaphoreType.DMA((2,))],
    )
    return pl.pallas_call(
        functools.partial(_moe_body, d_expert=d_expert),
        grid_spec=grid_spec,
        out_shape=jax.ShapeDtypeStruct(x_sorted.shape, x_sorted.dtype),
        compiler_params=pltpu.CompilerParams(
            dimension_semantics=("arbitrary",), vmem_limit_bytes=MOE_VMEM_LIMIT),
        name="moe_routed",
    )(*tables, x_sorted, we_gate, we_up, we_down)


def _final_body(x1_ref, gate_ref, gt2_ref, wsgu_ref, wsd_ref, fg_ref, h_hbm, y_hbm, out_ref,
                hw, yw, sem, *, d_shared):
    tm = x1_ref.shape[0]
    step = pl.program_id(0)
    slot = step % 2

    def rows_in(i, s):
        copies = _row_tile_copies(h_hbm, i * tm, hw.at[s], sem.at[s], to_hbm=False)
        for k in range(TOP_K):
            copies += _row_tile_copies(y_hbm.at[k], i * tm, yw.at[s, k], sem.at[s], to_hbm=False)
        return copies

    @pl.when(step == 0)
    def _():
        _start_all(rows_in(0, 0))

    @pl.when(step + 1 < pl.num_programs(0))
    def _():
        _start_all(rows_in(step + 1, 1 - slot))

    _wait_all(rows_in(step, slot))
    routed = gate_ref[:, 0:1] * jnp.concatenate(_unpack_words(yw[slot, 0]), axis=1)
    for k in range(1, TOP_K):
        routed = routed + gate_ref[:, k:k + 1] * jnp.concatenate(_unpack_words(yw[slot, k]), axis=1)
    h = jnp.concatenate(_unpack_words(hw[slot]), axis=1).astype(BF16)
    gu = jnp.dot(h, wsgu_ref[...], preferred_element_type=F32)
    hb = (_silu(gu[:, 0:d_shared]) * gu[:, d_shared:2 * d_shared]).astype(BF16)
    x2 = x1_ref[...] + gt2_ref[0] * (routed + jnp.dot(hb, wsd_ref[...], preferred_element_type=F32))
    out_ref[...] = x2 * lax.rsqrt(jnp.mean(x2 * x2, axis=-1, keepdims=True) + EPS) * fg_ref[...]


def _final(h_rows, x1, y_tok, gates, mod, ws_gu, ws_d, final_g, rows_per_batch):
    n, d = x1.shape
    tm = ROW_TILE
    tiles_per_batch = rows_per_batch // tm
    row = lambda i: (i, 0)
    any_spec = pl.BlockSpec(memory_space=pl.ANY)
    return pl.pallas_call(
        functools.partial(_final_body, d_shared=ws_d.shape[0]),
        grid=(n // tm,),
        in_specs=[pl.BlockSpec((tm, d), row), pl.BlockSpec((tm, SUBLANES), row),
                  pl.BlockSpec((1, 1, d), _mod_spec(5, tiles_per_batch)),
                  _resident(ws_gu.shape), _resident(ws_d.shape), _resident(final_g.shape),
                  any_spec, any_spec],
        out_specs=pl.BlockSpec((tm, d), row),
        out_shape=jax.ShapeDtypeStruct((n, d), F32),
        scratch_shapes=[pltpu.VMEM((2, tm, d // 2), I32), pltpu.VMEM((2, TOP_K, tm, d // 2), I32),
                        pltpu.SemaphoreType.DMA((2,))],
        compiler_params=_params(1),
        name="shared_combine_final",
    )(x1, gates, mod, ws_gu, ws_d, final_g, h_rows, y_tok)


def _routing_tables(idx, rank, counts, n_tok):
    nb_max = -(-(n_tok * TOP_K) // MOE_BLOCK) + N_EXPERTS
    nblk = (counts + MOE_BLOCK - 1) // MOE_BLOCK
    blk_end = jnp.cumsum(nblk)
    blk_start = blk_end - nblk
    experts = jnp.arange(N_EXPERTS, dtype=I32)[:, None, None]
    first_slot = (blk_start * MOE_BLOCK)[:, None, None]
    dest = jnp.sum(jnp.where(idx[None] == experts, first_slot, 0), axis=0) + rank
    dest_chunks = dest.reshape(TOP_K, n_tok // SC_CHUNK, SC_CHUNK).transpose(1, 0, 2)

    blocks = jnp.arange(nb_max, dtype=I32)[:, None]
    member = (blk_start[None, :] <= blocks) & (blocks < blk_end[None, :])
    lookup = lambda table: jnp.sum(jnp.where(member, table[None, :], 0), axis=1)
    blocks = blocks[:, 0]
    nonempty = nblk > 0
    n_visited = jnp.sum(nonempty.astype(I32))
    ordinal_of = jnp.cumsum(nonempty.astype(I32)) - 1
    slots = jnp.arange(N_EXPERTS, dtype=I32)
    order = jnp.sum(jnp.where(nonempty[None, :] & (ordinal_of[None, :] == slots[:, None]),
                              slots[None, :], 0), axis=1)
    ordinal = lookup(ordinal_of)
    k_in_e = blocks - lookup(blk_start)
    nb_e = jnp.maximum(lookup(nblk), 1)
    live = (ordinal + 1 < n_visited) & (blocks < blk_end[-1])
    first = WEIGHT_PARTS * (ordinal + 1)
    lo = jnp.where(live, first + WEIGHT_PARTS * k_in_e // nb_e, 0)
    hi = jnp.where(live, first + WEIGHT_PARTS * (k_in_e + 1) // nb_e, 0)
    tables = (ordinal.astype(I32), order, lo.astype(I32), hi.astype(I32),
              (WEIGHT_PARTS * n_visited).reshape(1).astype(I32), blk_end[-1:].astype(I32))
    return tables, dest_chunks, nb_max * MOE_BLOCK


def kernel(x, c, ctx, c_ctx, norm1_g, norm2_g, w_ada, b_ada, w_in, conv_w, gate_b, head_g, w_out,
           w_router, b_router, we_gate, we_up, we_down, ws_gate, ws_up, ws_down, final_g):
    assert w_ada.shape[0] == 1, "single-layer block"
    bsz, seq, d = x.shape
    ctx_len = ctx.shape[1]
    n_tok = bsz * seq
    conv_dim = conv_w.shape[2]
    v_all = head_g.shape[1]
    qk_all = (w_in.shape[2] - 3 * conv_dim - 2 * v_all - N_GATES) // 2
    assert seq % ROW_TILE == 0 and ctx_len % ROW_TILE == 0 and ROW_TILE % GRID_W == 0
    assert bsz + 1 <= SUBLANES

    cc = jnp.zeros((SUBLANES, d), F32).at[:bsz].set(c).at[bsz].set(c_ctx)
    mod = _adaln(cc, w_ada[0], b_ada).reshape(SUBLANES * 6, 1, d)

    n_main = 3 * conv_dim + 2 * qk_all + 2 * v_all
    w_all = jnp.pad(w_in[0].astype(BF16), ((0, 0), (0, LANES - N_GATES)))
    k_lo = 3 * conv_dim + qk_all
    w_kt = w_all[:, k_lo:k_lo + qk_all].T
    w_v = w_all[:, k_lo + qk_all:k_lo + qk_all + v_all]
    w_gate = w_all[:, n_main:n_main + LANES]
    gate_bias = jnp.zeros((1, LANES), F32).at[0, :N_GATES].set(gate_b[0].reshape(-1))

    x2d = x.reshape(n_tok, d)
    conv, q, kt, v, og, g, gt = _inproj(x2d, mod, norm1_g, w_all, w_kt, gate_bias, conv_w[0],
                                       seq, conv_dim, qk_all, v_all)
    ktc, vc, _, gtc = _inproj_ctx(ctx.reshape(bsz * ctx_len, d), mod, norm1_g, w_v, w_kt, w_gate,
                                  gate_bias, bsz)

    h_bwd = _mlstm(1, q, kt, v, g, gt, ktc, vc, gtc, None, bsz)
    ml = _mlstm(0, q, kt, v, g, gt, ktc, vc, gtc, (h_bwd, og), bsz, head_g)

    assert 2 * N_EXPERTS == LANES
    w_r_hi = w_router[0].astype(BF16)
    w_r = jnp.concatenate([w_r_hi, (w_router[0] - w_r_hi.astype(F32)).astype(BF16)], axis=1)
    b_r = jnp.zeros((1, LANES), F32).at[0, :N_EXPERTS].set(b_router[0])
    x1, idx, gates, rank, cnt, h_rows = _outproj(conv, ml, x2d, mod, norm2_g, w_out[0].astype(BF16),
                                             w_r, b_r, seq)

    tables, dest_chunks, n_slots = _routing_tables(idx[:TOP_K], rank[:TOP_K], cnt[0, :N_EXPERTS], n_tok)
    x_sorted = _sc_dispatch(h_rows, dest_chunks, n_slots)
    y_sorted = _moe(x_sorted, we_gate[0], we_up[0], we_down[0], tables)
    y_tok = _sc_combine(y_sorted, dest_chunks, n_tok)

    ws_gu = jnp.concatenate([ws_gate[0], ws_up[0]], axis=1).astype(BF16)
    out = _final(h_rows, x1, y_tok, gates, mod, ws_gu, ws_down[0].astype(BF16),
                 final_g.reshape(1, d), seq)
    return out.reshape(bsz, seq, d)
```

```python
import functools

import jax
import jax.numpy as jnp
from jax import lax
from jax.experimental import pallas as pl
from jax.experimental.pallas import tpu as pltpu
from jax.experimental.pallas import tpu_sc as plsc

F32 = jnp.float32
BF16 = jnp.bfloat16
I32 = jnp.int32

N_HEADS = 4
GRID_W = 64
CHUNK = 128
TOP_K = 6
N_EXPERTS = 64
ROUTED_SCALE = 2.446
EPS = 1e-6
N_GATES = 4 * N_HEADS

LANES = 128
SUBLANES = 8
MOE_BLOCK = 256
ROW_TILE = 256
ADALN_TILE = 1024
WEIGHT_PARTS = 8
PART_SHIFT = 3
WEIGHT_RING = 3
SC_CHUNK = 64
HIGH_HALF = -65536
VMEM_LIMIT = 56 * 1024 * 1024
MOE_VMEM_LIMIT = 62 * 1024 * 1024

_HIGHEST = lax.Precision.HIGHEST
_NEG_INF = float("-inf")


def _resident(shape):
    nd = len(shape)
    return pl.BlockSpec(shape, lambda *_: (0,) * nd, pipeline_mode=pl.Buffered(1))


def _params(n_axes):
    return pltpu.CompilerParams(
        dimension_semantics=("arbitrary",) * n_axes, vmem_limit_bytes=VMEM_LIMIT)


def _log_sigmoid(x):
    return jnp.minimum(x, 0.0) - jnp.log1p(jnp.exp(-jnp.abs(x)))


def _silu(x):
    return x * jax.nn.sigmoid(x)


def _pack_words(val):
    half = val.shape[1] // 2
    lo = lax.bitcast_convert_type(val[:, :half].astype(BF16).astype(F32), I32)
    hi = lax.bitcast_convert_type(val[:, half:].astype(BF16).astype(F32), I32)
    return (hi & HIGH_HALF) | lax.shift_right_logical(lo, 16)


def _unpack_words(word):
    lo = lax.bitcast_convert_type(lax.shift_left(word, 16), F32)
    hi = lax.bitcast_convert_type(word & HIGH_HALF, F32)
    return lo, hi


def _row_tile_copies(hbm_rows, row0, tile, sem, to_hbm):
    n = tile.shape[0]
    copies = []
    for c in range(SUBLANES):
        hbm = hbm_rows.at[pl.ds(row0, n), c, :]
        vmem = tile.at[:, pl.ds(c * LANES, LANES)]
        copies.append(pltpu.make_async_copy(vmem, hbm, sem) if to_hbm
                      else pltpu.make_async_copy(hbm, vmem, sem))
    return copies


def _start_all(copies):
    for cp in copies:
        cp.start()


def _wait_all(copies):
    for cp in copies:
        cp.wait()


def _adaln_body(c_ref, w_ref, b_ref, o_ref):
    s = _silu(c_ref[...])
    o_ref[...] = jnp.dot(s.astype(BF16), w_ref[...].astype(BF16),
                         preferred_element_type=F32) + b_ref[...]


def _adaln(cc, w, b):
    d, n6 = w.shape
    return pl.pallas_call(
        _adaln_body,
        grid=(n6 // ADALN_TILE,),
        in_specs=[pl.BlockSpec((SUBLANES, d), lambda j: (0, 0)),
                  pl.BlockSpec((d, ADALN_TILE), lambda j: (0, j)),
                  pl.BlockSpec((1, ADALN_TILE), lambda j: (0, j))],
        out_specs=pl.BlockSpec((SUBLANES, ADALN_TILE), lambda j: (0, j)),
        out_shape=jax.ShapeDtypeStruct((SUBLANES, n6), F32),
        compiler_params=_params(1),
        name="adaln",
    )(cc, w, b)


def _norm_mod(x, g, shift, scale):
    y = x * lax.rsqrt(jnp.mean(x * x, axis=-1, keepdims=True) + EPS) * g
    return y * (1.0 + scale) + shift


def _gate_prep(xb, wg, gb_ref, g_ref, gt_ref):
    tm = xb.shape[0]
    gg = jnp.dot(xb, wg, preferred_element_type=F32) + gb_ref[...]
    lane = lax.broadcasted_iota(I32, (tm, LANES), 1)
    is_f = (lane & N_HEADS) != 0
    is_bwd = (lane & (2 * N_HEADS)) != 0
    lf = jnp.where(is_f, _log_sigmoid(gg), 0.0)
    r = lax.broadcasted_iota(I32, (tm, tm), 0)
    c = lax.broadcasted_iota(I32, (tm, tm), 1)
    same = (r // CHUNK) == (c // CHUNK)
    tri_l = jnp.where(same & (c <= r), 1.0, 0.0).astype(F32)
    tri_u = jnp.where(same & (c >= r), 1.0, 0.0).astype(F32)
    pre = jnp.dot(tri_l, lf, precision=_HIGHEST, preferred_element_type=F32)
    suf = jnp.dot(tri_u, lf, precision=_HIGHEST, preferred_element_type=F32)
    out = jnp.where(is_f, jnp.where(is_bwd, suf, pre), gg)
    g_ref[...] = out[:, :N_GATES]
    gt_ref[...] = out.T[:N_GATES, :]


def _project_transposed(wt_ref, xb):
    return lax.dot_general(wt_ref[...], xb, (((1,), (1,)), ((), ())),
                           preferred_element_type=F32).astype(BF16)


def _inproj_body(x_ref, sh_ref, sc_ref, g1_ref, w_ref, wkt_ref, gb_ref, cw_ref,
                 conv_ref, q_ref, k_ref, v_ref, o_ref, g_ref, gt_ref, *, conv_dim, qk_all, v_all):
    tm = x_ref.shape[0]
    xb = _norm_mod(x_ref[...], g1_ref[...], sh_ref[0], sc_ref[0]).astype(BF16)

    def proj(lo, width):
        return jnp.dot(xb, w_ref[:, lo:lo + width], preferred_element_type=F32)

    u = proj(conv_dim, conv_dim) * proj(2 * conv_dim, conv_dim)
    pos = lax.broadcasted_iota(I32, (tm, 1), 0) % GRID_W
    um = jnp.where(pos == 0, 0.0, pltpu.roll(u, 1, axis=0))
    up = jnp.where(pos == GRID_W - 1, 0.0, pltpu.roll(u, tm - 1, axis=0))
    y = um * cw_ref[0:1, :] + u * cw_ref[1:2, :] + up * cw_ref[2:3, :]
    conv_ref[...] = (proj(0, conv_dim) * y).astype(BF16)

    off = 3 * conv_dim
    qscale = (qk_all // N_HEADS) ** -0.5
    q_ref[...] = (proj(off, qk_all) * qscale).astype(BF16)
    k_ref[...] = _project_transposed(wkt_ref, xb)
    v_ref[...] = proj(off + 2 * qk_all, v_all).astype(BF16)
    o_ref[...] = jax.nn.sigmoid(proj(off + 2 * qk_all + v_all, v_all)).astype(BF16)
    gate_lo = off + 2 * qk_all + 2 * v_all
    _gate_prep(xb, w_ref[:, gate_lo:gate_lo + LANES], gb_ref, g_ref, gt_ref)


def _inproj_ctx_body(x_ref, sh_ref, sc_ref, g1_ref, w_ref, wkt_ref, wg_ref, gb_ref,
                     k_ref, v_ref, g_ref, gt_ref):
    xb = _norm_mod(x_ref[...], g1_ref[...], sh_ref[0], sc_ref[0]).astype(BF16)
    k_ref[...] = _project_transposed(wkt_ref, xb)
    v_ref[...] = jnp.dot(xb, w_ref[...], preferred_element_type=F32).astype(BF16)
    _gate_prep(xb, wg_ref[...], gb_ref, g_ref, gt_ref)


def _mod_spec(part, tiles_per_row, fixed_row=None):
    def index(i):
        row = fixed_row if fixed_row is not None else i // tiles_per_row
        return (row * 6 + part, 0, 0)

    return index


def _inproj(x2d, mod, g1, w_all, w_kt, gate_b, conv_w, rows_per_batch, conv_dim, qk_all, v_all):
    n, d = x2d.shape
    tm = ROW_TILE
    tiles_per_batch = rows_per_batch // tm
    row = lambda i: (i, 0)
    mod_block = (1, 1, d)
    out_shapes = (
        jax.ShapeDtypeStruct((n, conv_dim), BF16),
        jax.ShapeDtypeStruct((n, qk_all), BF16),
        jax.ShapeDtypeStruct((qk_all, n), BF16),
        jax.ShapeDtypeStruct((n, v_all), BF16),
        jax.ShapeDtypeStruct((n, v_all), BF16),
        jax.ShapeDtypeStruct((n, N_GATES), F32),
        jax.ShapeDtypeStruct((N_GATES, n), F32),
    )
    out_specs = (
        pl.BlockSpec((tm, conv_dim), row),
        pl.BlockSpec((tm, qk_all), row),
        pl.BlockSpec((qk_all, tm), lambda i: (0, i)),
        pl.BlockSpec((tm, v_all), row),
        pl.BlockSpec((tm, v_all), row),
        pl.BlockSpec((tm, N_GATES), row),
        pl.BlockSpec((N_GATES, tm), lambda i: (0, i)),
    )
    return pl.pallas_call(
        functools.partial(_inproj_body, conv_dim=conv_dim, qk_all=qk_all, v_all=v_all),
        grid=(n // tm,),
        in_specs=[pl.BlockSpec((tm, d), row),
                  pl.BlockSpec(mod_block, _mod_spec(0, tiles_per_batch)),
                  pl.BlockSpec(mod_block, _mod_spec(1, tiles_per_batch)),
                  _resident(g1.shape), _resident(w_all.shape), _resident(w_kt.shape),
                  _resident(gate_b.shape), _resident(conv_w.shape)],
        out_specs=out_specs,
        out_shape=out_shapes,
        compiler_params=_params(1),
        name="inproj",
    )(x2d, mod, mod, g1, w_all, w_kt, gate_b, conv_w)


def _inproj_ctx(c2d, mod, g1, w_v, w_kt, w_gate, gate_b, ctx_mod_row):
    n, d = c2d.shape
    tm = ROW_TILE
    row = lambda i: (i, 0)
    mod_block = (1, 1, d)
    qk_all, v_all = w_kt.shape[0], w_v.shape[1]
    return pl.pallas_call(
        _inproj_ctx_body,
        grid=(n // tm,),
        in_specs=[pl.BlockSpec((tm, d), row),
                  pl.BlockSpec(mod_block, _mod_spec(0, 1, ctx_mod_row)),
                  pl.BlockSpec(mod_block, _mod_spec(1, 1, ctx_mod_row)),
                  _resident(g1.shape), _resident(w_v.shape), _resident(w_kt.shape),
                  _resident(w_gate.shape), _resident(gate_b.shape)],
        out_specs=(pl.BlockSpec((qk_all, tm), lambda i: (0, i)), pl.BlockSpec((tm, v_all), row),
                   pl.BlockSpec((tm, N_GATES), row), pl.BlockSpec((N_GATES, tm), lambda i: (0, i))),
        out_shape=(jax.ShapeDtypeStruct((qk_all, n), BF16), jax.ShapeDtypeStruct((n, v_all), BF16),
                   jax.ShapeDtypeStruct((n, N_GATES), F32), jax.ShapeDtypeStruct((N_GATES, n), F32)),
        compiler_params=_params(1),
        name="inproj_ctx",
    )(c2d, mod, mod, g1, w_v, w_kt, w_gate, gate_b)


def _with_ones(v):
    return jnp.concatenate([v, jnp.ones((v.shape[0], LANES), v.dtype)], axis=1)


def _mlstm_state_update(h, direction, kt_ref, v_ref, gt_ref, s_ref, m_ref, qk, vh):
    ci = direction * 2 * N_HEADS + h
    cb = ci + N_HEADS
    last = 0 if direction else CHUNK - 1
    kt = kt_ref[h * qk:(h + 1) * qk, :].astype(F32)
    va = _with_ones(v_ref[:, h * vh:(h + 1) * vh])
    b_last = gt_ref[cb:cb + 1, last:last + 1]
    m_prev = m_ref[h][0:1, 0:1]
    g_r = b_last - gt_ref[cb:cb + 1, :] + gt_ref[ci:ci + 1, :]
    m_new = jnp.maximum(b_last + m_prev, jnp.max(g_r, axis=1, keepdims=True))
    a = jnp.exp(b_last + m_prev - m_new)
    kw = (kt * jnp.exp(g_r - m_new)).astype(BF16)
    s_ref[h] = a * s_ref[h] + jnp.dot(kw, va, preferred_element_type=F32)
    m_ref[h] = jnp.broadcast_to(m_new, m_ref.shape[1:])


def _mlstm_head_output(h, direction, q_ref, kt_ref, v_ref, g_ref, gt_ref, s_ref, m_ref, qk, vh):
    ci = direction * 2 * N_HEADS + h
    cb = ci + N_HEADS
    q = q_ref[:, h * qk:(h + 1) * qk]
    kt = kt_ref[h * qk:(h + 1) * qk, :]
    va = _with_ones(v_ref[:, h * vh:(h + 1) * vh])
    ig_r = gt_ref[ci:ci + 1, :]
    b_r = gt_ref[cb:cb + 1, :]
    b_c = g_ref[:, cb:cb + 1]
    m_prev = m_ref[h][0:1, 0:1]
    row = lax.broadcasted_iota(I32, (CHUNK, CHUNK), 0)
    col = lax.broadcasted_iota(I32, (CHUNK, CHUNK), 1)
    mask = (col >= row) if direction else (col <= row)
    dm = jnp.where(mask, b_c + (ig_r - b_r), _NEG_INF)
    inter = b_c + m_prev
    m_t = jnp.maximum(inter, jnp.max(dm, axis=1, keepdims=True))
    w_inter = jnp.exp(inter - m_t)
    s = jnp.dot(q, kt, preferred_element_type=F32) * jnp.exp(dm - m_t)
    intra = jnp.dot(s.astype(BF16), va, preferred_element_type=F32)
    carried = jnp.dot(q, s_ref[h].astype(BF16), preferred_element_type=F32)
    num = intra[:, 0:vh] + w_inter * carried[:, 0:vh]
    den = intra[:, vh:vh + 1] + w_inter * carried[:, vh:vh + 1]
    return num / jnp.maximum(jnp.abs(den), jnp.exp(-m_t))


def _mlstm_body(*refs, direction, bsz, n_ctx_chunks, qk, vh):
    q_ref, v_ref, g_ref, vc_ref = refs[0:4]
    kt_refs, gt_refs = refs[4:4 + bsz], refs[4 + bsz:4 + 2 * bsz]
    ktc_refs, gtc_refs = refs[4 + 2 * bsz:4 + 3 * bsz], refs[4 + 3 * bsz:4 + 4 * bsz]
    rest = refs[4 + 4 * bsz:]
    if direction:
        out_ref, s_ref, m_ref = rest
    else:
        hb_ref, og_ref, hg_ref, out_ref, s_ref, m_ref = rest
    step = pl.program_id(0)

    @pl.when(step == 0)
    def _():
        s_ref[...] = jnp.zeros_like(s_ref)
        m_ref[...] = jnp.full_like(m_ref, _NEG_INF)

    @pl.when(step < n_ctx_chunks)
    def _():
        for b in range(bsz):
            for h in range(N_HEADS):
                _mlstm_state_update(h, direction, ktc_refs[b], vc_ref.at[b], gtc_refs[b],
                                    s_ref.at[b], m_ref.at[b], qk, vh)

    @pl.when(step >= n_ctx_chunks)
    def _():
        for b in range(bsz):
            for h in range(N_HEADS):
                hh = _mlstm_head_output(h, direction, q_ref.at[b], kt_refs[b], v_ref.at[b], g_ref.at[b],
                                        gt_refs[b], s_ref.at[b], m_ref.at[b], qk, vh)
                cols = slice(h * vh, (h + 1) * vh)
                if direction:
                    out_ref[b, :, cols] = hh
                else:
                    hs = hh + hb_ref[b, :, cols]
                    hs = hs * lax.rsqrt(jnp.mean(hs * hs, axis=-1, keepdims=True) + EPS)
                    out_ref[b, :, cols] = (hs * hg_ref[:, cols]
                                           * og_ref[b, :, cols].astype(F32)).astype(BF16)
                _mlstm_state_update(h, direction, kt_refs[b], v_ref.at[b], gt_refs[b],
                                    s_ref.at[b], m_ref.at[b], qk, vh)


def _mlstm(direction, q, kt, v, g, gt, ktc, vc, gtc, extra, bsz, head_g=None):
    n, qk_all = q.shape
    v_all = v.shape[1]
    qk, vh = qk_all // N_HEADS, v_all // N_HEADS
    seq = n // bsz
    nc = seq // CHUNK
    ncc = vc.shape[0] // bsz // CHUNK

    def lat(s):
        j = jnp.clip(s - ncc, 0, nc - 1)
        return nc - 1 - j if direction else j

    def ctx(s):
        j = jnp.clip(s, 0, ncc - 1)
        return ncc - 1 - j if direction else j

    def per_batch(a):
        return a.reshape(bsz, a.shape[0] // bsz, a.shape[1])

    lat_blk = lambda c: pl.BlockSpec((bsz, CHUNK, c), lambda s: (0, lat(s), 0))
    in_specs = [lat_blk(qk_all), lat_blk(v_all), lat_blk(N_GATES),
                pl.BlockSpec((bsz, CHUNK, v_all), lambda s: (0, ctx(s), 0))]
    args = [per_batch(q), per_batch(v), per_batch(g), per_batch(vc)]
    for arr, rows, n_chunks, pos in ((kt, qk_all, nc, lat), (gt, N_GATES, nc, lat),
                                     (ktc, qk_all, ncc, ctx), (gtc, N_GATES, ncc, ctx)):
        for b in range(bsz):
            in_specs.append(pl.BlockSpec((rows, CHUNK), lambda s, b=b, n_chunks=n_chunks, pos=pos:
                                         (0, b * n_chunks + pos(s))))
            args.append(arr)
    if direction:
        out_dtype = F32
    else:
        hb, og = extra
        in_specs += [lat_blk(v_all), lat_blk(v_all), pl.BlockSpec((1, v_all), lambda s: (0, 0))]
        args += [per_batch(hb), per_batch(og), head_g]
        out_dtype = BF16
    out = pl.pallas_call(
        functools.partial(_mlstm_body, direction=direction, bsz=bsz, n_ctx_chunks=ncc, qk=qk, vh=vh),
        grid=(ncc + nc,),
        in_specs=in_specs,
        out_specs=lat_blk(v_all),
        out_shape=jax.ShapeDtypeStruct((bsz, seq, v_all), out_dtype),
        scratch_shapes=[pltpu.VMEM((bsz, N_HEADS, qk, vh + LANES), F32),
                        pltpu.VMEM((bsz, N_HEADS, SUBLANES, LANES), F32)],
        compiler_params=_params(1),
        name="mlstm_bwd" if direction else "mlstm_fwd",
    )(*args)
    return out.reshape(n, v_all)


def _outproj_body(conv_ref, ml_ref, x_ref, gt1_ref, sh2_ref, sc2_ref, g2_ref, wo_ref, wr_ref, br_ref,
                  x1_ref, idx_ref, gate_ref, rank_ref, cnt_ref, h_hbm, carry_ref, hw, hsem):
    tm = x_ref.shape[0]
    half = conv_ref.shape[1]
    step = pl.program_id(0)
    buf = step % 2

    def h_out(i, s):
        return _row_tile_copies(h_hbm, i * tm, hw.at[s], hsem.at[s], to_hbm=True)

    @pl.when(step == 0)
    def _():
        carry_ref[...] = jnp.zeros_like(carry_ref)

    @pl.when(step >= 2)
    def _():
        _wait_all(h_out(step - 2, buf))

    y = (jnp.dot(conv_ref[...], wo_ref[0:half, :], preferred_element_type=F32)
         + jnp.dot(ml_ref[...], wo_ref[half:2 * half, :], preferred_element_type=F32))
    x1 = x_ref[...] + gt1_ref[0] * y
    x1_ref[...] = x1
    hn = _norm_mod(x1, g2_ref[...], sh2_ref[0], sc2_ref[0])
    hw[buf] = _pack_words(hn)
    _start_all(h_out(step, buf))

    h_hi = hn.astype(BF16)
    h_lo = (hn - h_hi.astype(F32)).astype(BF16)
    parts = (jnp.dot(h_hi, wr_ref[...], preferred_element_type=F32)
             + jnp.dot(h_lo, wr_ref[...], preferred_element_type=F32))
    scores = jax.nn.sigmoid(parts + pltpu.roll(parts, N_EXPERTS, axis=1))
    lane = lax.broadcasted_iota(I32, (tm, LANES), 1).astype(F32)
    biased = jnp.where(lane < N_EXPERTS, scores + br_ref[...], _NEG_INF)
    onehot = jnp.zeros((tm, LANES), F32)
    picks, sels = [], []
    for _ in range(TOP_K):
        mx = jnp.max(biased, axis=1, keepdims=True)
        pick = jnp.min(jnp.where(biased == mx, lane, float(LANES)), axis=1, keepdims=True)
        hit = lane == pick
        sels.append(jnp.sum(jnp.where(hit, scores, 0.0), axis=1, keepdims=True))
        picks.append(pick)
        biased = jnp.where(hit, _NEG_INF, biased)
        onehot = onehot + hit.astype(F32)
    total = sels[0]
    for s in sels[1:]:
        total = total + s

    r = lax.broadcasted_iota(I32, (tm, tm), 0)
    c = lax.broadcasted_iota(I32, (tm, tm), 1)
    strict = jnp.where(c < r, 1.0, 0.0).astype(BF16)
    before = jnp.dot(strict, onehot.astype(BF16), preferred_element_type=F32) + carry_ref[...]
    slot = lax.broadcasted_iota(I32, (tm, SUBLANES), 1)
    idx_out = jnp.zeros((tm, LANES), F32)
    rank_out = jnp.zeros((tm, LANES), F32)
    gate_out = jnp.zeros((tm, SUBLANES), F32)
    for j in range(TOP_K):
        rank = jnp.sum(jnp.where(lane == picks[j], before, 0.0), axis=1, keepdims=True)
        idx_out = jnp.where(lane == float(j), picks[j], idx_out)
        rank_out = jnp.where(lane == float(j), rank, rank_out)
        gate_out = jnp.where(slot == j, sels[j] / total * ROUTED_SCALE, gate_out)
    idx_ref[...] = idx_out.T[:SUBLANES, :].astype(I32)
    rank_ref[...] = rank_out.T[:SUBLANES, :].astype(I32)
    gate_ref[...] = gate_out
    carry_ref[...] = carry_ref[...] + jnp.sum(onehot, axis=0, keepdims=True)
    cnt_ref[...] = jnp.broadcast_to(carry_ref[...], cnt_ref.shape).astype(I32)

    @pl.when(step == pl.num_programs(0) - 1)
    def _():
        @pl.when(step >= 1)
        def _():
            _wait_all(h_out(step - 1, 1 - buf))
        _wait_all(h_out(step, buf))


def _outproj(conv, ml, x2d, mod, g2, w_out, w_router, b_router, rows_per_batch):
    n, d = x2d.shape
    tm = ROW_TILE
    tiles_per_batch = rows_per_batch // tm
    row = lambda i: (i, 0)
    mod_block = (1, 1, d)
    half = conv.shape[1]
    return pl.pallas_call(
        _outproj_body,
        grid=(n // tm,),
        in_specs=[pl.BlockSpec((tm, half), row), pl.BlockSpec((tm, half), row), pl.BlockSpec((tm, d), row),
                  pl.BlockSpec(mod_block, _mod_spec(2, tiles_per_batch)),
                  pl.BlockSpec(mod_block, _mod_spec(3, tiles_per_batch)),
                  pl.BlockSpec(mod_block, _mod_spec(4, tiles_per_batch)),
                  _resident(g2.shape), _resident(w_out.shape), _resident(w_router.shape),
                  _resident(b_router.shape)],
        out_specs=(pl.BlockSpec((tm, d), row),
                   pl.BlockSpec((SUBLANES, tm), lambda i: (0, i)), pl.BlockSpec((tm, SUBLANES), row),
                   pl.BlockSpec((SUBLANES, tm), lambda i: (0, i)),
                   pl.BlockSpec((SUBLANES, LANES), lambda i: (0, 0)),
                   pl.BlockSpec(memory_space=pl.ANY)),
        out_shape=(jax.ShapeDtypeStruct((n, d), F32),
                   jax.ShapeDtypeStruct((SUBLANES, n), I32), jax.ShapeDtypeStruct((n, SUBLANES), F32),
                   jax.ShapeDtypeStruct((SUBLANES, n), I32),
                   jax.ShapeDtypeStruct((SUBLANES, LANES), I32),
                   jax.ShapeDtypeStruct((n, SUBLANES, LANES), I32)),
        scratch_shapes=[pltpu.VMEM((1, LANES), F32), pltpu.VMEM((2, tm, d // 2), I32),
                        pltpu.SemaphoreType.DMA((2,))],
        compiler_params=_params(1),
        name="outproj_router",
    )(conv, ml, x2d, mod, mod, mod, g2, w_out, w_router, b_router)


def _sc_workers():
    info = plsc.get_sparse_core_info()
    return info.num_cores, info.num_cores * info.num_subcores


def _sc_dispatch(h_rows, dest_chunks, n_slots):
    n_tok = h_rows.shape[0]
    n_cores, n_workers = _sc_workers()
    per_worker = n_tok // (n_workers * SC_CHUNK)
    assert per_worker * n_workers * SC_CHUNK == n_tok
    mesh = plsc.VectorSubcoreMesh(core_axis_name="c", subcore_axis_name="s")

    @functools.partial(
        pl.kernel, mesh=mesh,
        out_type=jax.ShapeDtypeStruct((n_slots,) + h_rows.shape[1:], h_rows.dtype),
        scratch_types=[pltpu.VMEM((TOP_K, SC_CHUNK), I32),
                       pltpu.VMEM((SC_CHUNK,) + h_rows.shape[1:], h_rows.dtype)],
    )
    def dispatch(h_hbm, dest_hbm, out_hbm, idx_v, rows_v):
        wid = lax.axis_index("s") * n_cores + lax.axis_index("c")

        @pl.loop(0, per_worker)
        def _(i):
            chunk = wid * per_worker + i
            pltpu.sync_copy(dest_hbm.at[chunk], idx_v)
            pltpu.sync_copy(h_hbm.at[pl.ds(chunk * SC_CHUNK, SC_CHUNK)], rows_v)
            for k in range(TOP_K):
                pltpu.sync_copy(rows_v, out_hbm.at[idx_v.at[k]])

    return dispatch(h_rows, dest_chunks)


def _sc_combine(y_sorted, dest_chunks, n_tok):
    n_cores, n_workers = _sc_workers()
    per_worker = n_tok // (n_workers * SC_CHUNK)
    mesh = plsc.VectorSubcoreMesh(core_axis_name="c", subcore_axis_name="s")

    @functools.partial(
        pl.kernel, mesh=mesh,
        out_type=jax.ShapeDtypeStruct((TOP_K, n_tok) + y_sorted.shape[1:], y_sorted.dtype),
        scratch_types=[pltpu.VMEM((TOP_K, SC_CHUNK), I32),
                       pltpu.VMEM((SC_CHUNK,) + y_sorted.shape[1:], y_sorted.dtype)],
    )
    def combine(y_hbm, dest_hbm, out_hbm, idx_v, rows_v):
        wid = lax.axis_index("s") * n_cores + lax.axis_index("c")

        @pl.loop(0, per_worker)
        def _(i):
            chunk = wid * per_worker + i
            pltpu.sync_copy(dest_hbm.at[chunk], idx_v)
            for k in range(TOP_K):
                pltpu.sync_copy(y_hbm.at[idx_v.at[k]], rows_v)
                pltpu.sync_copy(rows_v, out_hbm.at[k, pl.ds(chunk * SC_CHUNK, SC_CHUNK)])

    return combine(y_sorted, dest_chunks)


def _moe_body(ord_ref, order_ref, glo_ref, ghi_ref, tot_ref, nb_ref,
              x_hbm, wg_hbm, wu_hbm, wd_hbm, y_hbm,
              wgu, wd, stage_a, stage_d, xw, yw, wsem, xsem, ysem, *, d_expert):
    b = pl.program_id(0)
    nb = nb_ref[0]
    total = tot_ref[0]
    d_model = wgu.shape[1]
    rows_a = d_model // WEIGHT_PARTS
    rows_d = d_expert // WEIGHT_PARTS

    def part_copies(g):
        e = order_ref[lax.shift_right_logical(g, PART_SHIFT)]
        i = g & (WEIGHT_PARTS - 1)
        s = lax.rem(g, WEIGHT_RING)
        return (pltpu.make_async_copy(wg_hbm.at[e, pl.ds(i * rows_a, rows_a)], stage_a.at[s, 0],
                                      wsem.at[s, 0]),
                pltpu.make_async_copy(wu_hbm.at[e, pl.ds(i * rows_a, rows_a)], stage_a.at[s, 1],
                                      wsem.at[s, 1]),
                pltpu.make_async_copy(wd_hbm.at[e, pl.ds(i * rows_d, rows_d)], stage_d.at[s],
                                      wsem.at[s, 2]))

    def start_part(g):
        for cp in part_copies(g):
            cp.start()

    def wait_part(g):
        for cp in part_copies(g):
            cp.wait()

    def cast_part(g):
        i = g & (WEIGHT_PARTS - 1)
        s = lax.rem(g, WEIGHT_RING)
        par = lax.shift_right_logical(g, PART_SHIFT) & 1
        ra = pl.multiple_of(i * rows_a, rows_a)
        rd = pl.multiple_of(i * rows_d, rows_d)
        wgu[par, pl.ds(ra, rows_a), 0:d_expert] = stage_a[s, 0].astype(BF16)
        wgu[par, pl.ds(ra, rows_a), d_expert:2 * d_expert] = stage_a[s, 1].astype(BF16)
        wd[par, pl.ds(rd, rows_d), :] = stage_d[s].astype(BF16)

    def refill(g):
        @pl.when(g + WEIGHT_RING < total)
        def _():
            start_part(g + WEIGHT_RING)

    def cast_parts(lo, hi):
        def body(g, carry):
            wait_part(g)
            cast_part(g)
            refill(g)
            return carry
        lax.fori_loop(lo, hi, body, 0)

    slot = b % 2

    def x_in(blk, s):
        return _row_tile_copies(x_hbm, blk * MOE_BLOCK, xw.at[s], xsem.at[s], to_hbm=False)

    def y_out(blk, s):
        return _row_tile_copies(y_hbm, blk * MOE_BLOCK, yw.at[s], ysem.at[s], to_hbm=True)

    @pl.when(b == 0)
    def _():
        _start_all(x_in(0, 0))
        for g in range(WEIGHT_RING):
            start_part(g)
        cast_parts(0, WEIGHT_PARTS)

    @pl.when(b + 1 < nb)
    def _():
        _start_all(x_in(b + 1, 1 - slot))

    @pl.when(b < nb)
    def _():
        par = ord_ref[b] & 1
        _wait_all(x_in(b, slot))

        @pl.when(b >= 2)
        def _():
            _wait_all(y_out(b - 2, slot))

        x = jnp.concatenate(_unpack_words(xw[slot]), axis=1).astype(BF16)
        gu = jnp.dot(x, wgu[par], preferred_element_type=F32)
        hb = (_silu(gu[:, 0:d_expert]) * gu[:, d_expert:2 * d_expert]).astype(BF16)
        yw[slot] = _pack_words(jnp.dot(hb, wd[par], preferred_element_type=F32))
        _start_all(y_out(b, slot))
        cast_parts(glo_ref[b], ghi_ref[b])

        @pl.when(b == nb - 1)
        def _():
            @pl.when(b >= 1)
            def _():
                _wait_all(y_out(b - 1, 1 - slot))
            _wait_all(y_out(b, slot))


def _moe(x_sorted, we_gate, we_up, we_down, tables):
    d, d_expert = we_gate.shape[1], we_gate.shape[2]
    nb_max = x_sorted.shape[0] // MOE_BLOCK
    any_spec = pl.BlockSpec(memory_space=pl.ANY)
    grid_spec = pltpu.PrefetchScalarGridSpec(
        num_scalar_prefetch=len(tables),
        grid=(nb_max,),
        in_specs=[any_spec, any_spec, any_spec, any_spec],
        out_specs=any_spec,
        scratch_shapes=[pltpu.VMEM((2, d, 2 * d_expert), BF16),
                        pltpu.VMEM((2, d_expert, d), BF16),
                        pltpu.VMEM((WEIGHT_RING, 2, d // WEIGHT_PARTS, d_expert), F32),
                        pltpu.VMEM((WEIGHT_RING, d_expert // WEIGHT_PARTS, d), F32),
                        pltpu.VMEM((2, MOE_BLOCK, d // 2), I32),
                        pltpu.VMEM((2, MOE_BLOCK, d // 2), I32),
                        pltpu.SemaphoreType.DMA((WEIGHT_RING, 3)),
                        pltpu.SemaphoreType.DMA((2,)),
                        pltpu.SemaphoreType.DMA((2,))],
    )
    return pl.pallas_call(
        functools.partial(_moe_body, d_expert=d_expert),
        grid_spec=grid_spec,
        out_shape=jax.ShapeDtypeStruct(x_sorted.shape, x_sorted.dtype),
        compiler_params=pltpu.CompilerParams(
            dimension_semantics=("arbitrary",), vmem_limit_bytes=MOE_VMEM_LIMIT),
        name="moe_routed",
    )(*tables, x_sorted, we_gate, we_up, we_down)


def _final_body(x1_ref, gate_ref, gt2_ref, wsgu_ref, wsd_ref, fg_ref, h_hbm, y_hbm, *rest,
                d_shared, first_tile):
    out_ref, hw, yw, sem = rest[-4:]
    tm = x1_ref.shape[0]
    step = pl.program_id(0)
    slot = step % 2

    def rows_in(i, s):
        copies = _row_tile_copies(h_hbm, (first_tile + i) * tm, hw.at[s], sem.at[s], to_hbm=False)
        for k in range(TOP_K):
            copies += _row_tile_copies(y_hbm.at[k], i * tm, yw.at[s, k], sem.at[s], to_hbm=False)
        return copies

    @pl.when(step == 0)
    def _():
        _start_all(rows_in(0, 0))

    @pl.when(step + 1 < pl.num_programs(0))
    def _():
        _start_all(rows_in(step + 1, 1 - slot))

    _wait_all(rows_in(step, slot))
    routed = gate_ref[:, 0:1] * jnp.concatenate(_unpack_words(yw[slot, 0]), axis=1)
    for k in range(1, TOP_K):
        routed = routed + gate_ref[:, k:k + 1] * jnp.concatenate(_unpack_words(yw[slot, k]), axis=1)
    h = jnp.concatenate(_unpack_words(hw[slot]), axis=1).astype(BF16)
    gu = jnp.dot(h, wsgu_ref[...], preferred_element_type=F32)
    hb = (_silu(gu[:, 0:d_shared]) * gu[:, d_shared:2 * d_shared]).astype(BF16)
    x2 = x1_ref[...] + gt2_ref[0] * (routed + jnp.dot(hb, wsd_ref[...], preferred_element_type=F32))
    out_ref[...] = x2 * lax.rsqrt(jnp.mean(x2 * x2, axis=-1, keepdims=True) + EPS) * fg_ref[...]


def _final(h_rows, x1, y_tok, gates, mod, ws_gu, ws_d, final_g, rows_per_batch, first_tile, prev):
    n, d = x1.shape
    tm = ROW_TILE
    tiles_per_batch = rows_per_batch // tm
    row = lambda i: (first_tile + i, 0)
    mod_index = _mod_spec(5, tiles_per_batch)
    any_spec = pl.BlockSpec(memory_space=pl.ANY)
    carried = () if prev is None else (prev,)
    return pl.pallas_call(
        functools.partial(_final_body, d_shared=ws_d.shape[0], first_tile=first_tile),
        grid=(y_tok.shape[1] // tm,),
        in_specs=[pl.BlockSpec((tm, d), row), pl.BlockSpec((tm, SUBLANES), row),
                  pl.BlockSpec((1, 1, d), lambda i: mod_index(first_tile + i)),
                  _resident(ws_gu.shape), _resident(ws_d.shape), _resident(final_g.shape),
                  any_spec, any_spec] + [any_spec] * len(carried),
        out_specs=pl.BlockSpec((tm, d), row),
        out_shape=jax.ShapeDtypeStruct((n, d), F32),
        scratch_shapes=[pltpu.VMEM((2, tm, d // 2), I32), pltpu.VMEM((2, TOP_K, tm, d // 2), I32),
                        pltpu.SemaphoreType.DMA((2,))],
        input_output_aliases={8: 0} if carried else {},
        compiler_params=_params(1),
        name="shared_combine_final",
    )(x1, gates, mod, ws_gu, ws_d, final_g, h_rows, y_tok, *carried)


def _routing_tables(idx, rank, counts, n_tok):
    nb_max = -(-(n_tok * TOP_K) // MOE_BLOCK) + N_EXPERTS
    nblk = (counts + MOE_BLOCK - 1) // MOE_BLOCK
    blk_end = jnp.cumsum(nblk)
    blk_start = blk_end - nblk
    experts = jnp.arange(N_EXPERTS, dtype=I32)[:, None, None]
    first_slot = (blk_start * MOE_BLOCK)[:, None, None]
    dest = jnp.sum(jnp.where(idx[None] == experts, first_slot, 0), axis=0) + rank
    dest_chunks = dest.reshape(TOP_K, n_tok // SC_CHUNK, SC_CHUNK).transpose(1, 0, 2)

    blocks = jnp.arange(nb_max, dtype=I32)[:, None]
    member = (blk_start[None, :] <= blocks) & (blocks < blk_end[None, :])
    lookup = lambda table: jnp.sum(jnp.where(member, table[None, :], 0), axis=1)
    blocks = blocks[:, 0]
    nonempty = nblk > 0
    n_visited = jnp.sum(nonempty.astype(I32))
    ordinal_of = jnp.cumsum(nonempty.astype(I32)) - 1
    slots = jnp.arange(N_EXPERTS, dtype=I32)
    order = jnp.sum(jnp.where(nonempty[None, :] & (ordinal_of[None, :] == slots[:, None]),
                              slots[None, :], 0), axis=1)
    ordinal = lookup(ordinal_of)
    k_in_e = blocks - lookup(blk_start)
    nb_e = jnp.maximum(lookup(nblk), 1)
    live = (ordinal + 1 < n_visited) & (blocks < blk_end[-1])
    first = WEIGHT_PARTS * (ordinal + 1)
    lo = jnp.where(live, first + WEIGHT_PARTS * k_in_e // nb_e, 0)
    hi = jnp.where(live, first + WEIGHT_PARTS * (k_in_e + 1) // nb_e, 0)
    tables = (ordinal.astype(I32), order, lo.astype(I32), hi.astype(I32),
              (WEIGHT_PARTS * n_visited).reshape(1).astype(I32), blk_end[-1:].astype(I32))
    return tables, dest_chunks, nb_max * MOE_BLOCK


def kernel(x, c, ctx, c_ctx, norm1_g, norm2_g, w_ada, b_ada, w_in, conv_w, gate_b, head_g, w_out,
           w_router, b_router, we_gate, we_up, we_down, ws_gate, ws_up, ws_down, final_g):
    assert w_ada.shape[0] == 1, "single-layer block"
    bsz, seq, d = x.shape
    ctx_len = ctx.shape[1]
    n_tok = bsz * seq
    conv_dim = conv_w.shape[2]
    v_all = head_g.shape[1]
    qk_all = (w_in.shape[2] - 3 * conv_dim - 2 * v_all - N_GATES) // 2
    assert seq % ROW_TILE == 0 and ctx_len % ROW_TILE == 0 and ROW_TILE % GRID_W == 0
    assert bsz + 1 <= SUBLANES

    cc = jnp.zeros((SUBLANES, d), F32).at[:bsz].set(c).at[bsz].set(c_ctx)
    mod = _adaln(cc, w_ada[0], b_ada).reshape(SUBLANES * 6, 1, d)

    n_main = 3 * conv_dim + 2 * qk_all + 2 * v_all
    w_all = jnp.pad(w_in[0].astype(BF16), ((0, 0), (0, LANES - N_GATES)))
    k_lo = 3 * conv_dim + qk_all
    w_kt = w_all[:, k_lo:k_lo + qk_all].T
    w_v = w_all[:, k_lo + qk_all:k_lo + qk_all + v_all]
    w_gate = w_all[:, n_main:n_main + LANES]
    gate_bias = jnp.zeros((1, LANES), F32).at[0, :N_GATES].set(gate_b[0].reshape(-1))

    x2d = x.reshape(n_tok, d)
    conv, q, kt, v, og, g, gt = _inproj(x2d, mod, norm1_g, w_all, w_kt, gate_bias, conv_w[0],
                                       seq, conv_dim, qk_all, v_all)
    ktc, vc, _, gtc = _inproj_ctx(ctx.reshape(bsz * ctx_len, d), mod, norm1_g, w_v, w_kt, w_gate,
                                  gate_bias, bsz)

    h_bwd = _mlstm(1, q, kt, v, g, gt, ktc, vc, gtc, None, bsz)
    ml = _mlstm(0, q, kt, v, g, gt, ktc, vc, gtc, (h_bwd, og), bsz, head_g)

    assert 2 * N_EXPERTS == LANES
    w_r_hi = w_router[0].astype(BF16)
    w_r = jnp.concatenate([w_r_hi, (w_router[0] - w_r_hi.astype(F32)).astype(BF16)], axis=1)
    b_r = jnp.zeros((1, LANES), F32).at[0, :N_EXPERTS].set(b_router[0])
    x1, idx, gates, rank, cnt, h_rows = _outproj(conv, ml, x2d, mod, norm2_g, w_out[0].astype(BF16),
                                             w_r, b_r, seq)

    tables, dest_chunks, n_slots = _routing_tables(idx[:TOP_K], rank[:TOP_K], cnt[0, :N_EXPERTS], n_tok)
    x_sorted = _sc_dispatch(h_rows, dest_chunks, n_slots)
    y_sorted = _moe(x_sorted, we_gate[0], we_up[0], we_down[0], tables)
    ws_gu = jnp.concatenate([ws_gate[0], ws_up[0]], axis=1).astype(BF16)
    ws_d = ws_down[0].astype(BF16)
    half_chunks = dest_chunks.shape[0] // 2
    half_tok = half_chunks * SC_CHUNK
    out = None
    for part in range(2):
        y_tok = _sc_combine(y_sorted, dest_chunks[part * half_chunks:(part + 1) * half_chunks], half_tok)
        out = _final(h_rows, x1, y_tok, gates, mod, ws_gu, ws_d, final_g.reshape(1, d), seq,
                     part * half_tok // ROW_TILE, out)
    return out.reshape(bsz, seq, d)
```

```python
import functools

import jax
import jax.numpy as jnp
from jax import lax
from jax.experimental import pallas as pl
from jax.experimental.pallas import tpu as pltpu
from jax.experimental.pallas import tpu_sc as plsc

F32 = jnp.float32
BF16 = jnp.bfloat16
I32 = jnp.int32

N_HEADS = 4
GRID_W = 64
CHUNK = 128
TOP_K = 6
N_EXPERTS = 64
ROUTED_SCALE = 2.446
EPS = 1e-6
N_GATES = 4 * N_HEADS
GATE_COLS = 6 * N_HEADS

LANES = 128
SUBLANES = 8
MOE_BLOCK = 256
ROW_TILE = 256
ADALN_TILE = 1024
WEIGHT_PARTS = 8
PART_SHIFT = 3
WEIGHT_RING = 3
SC_CHUNK = 64
HIGH_HALF = -65536
VMEM_LIMIT = 56 * 1024 * 1024
MOE_VMEM_LIMIT = 62 * 1024 * 1024

_HIGHEST = lax.Precision.HIGHEST
_NEG_INF = float("-inf")


def _resident(shape):
    nd = len(shape)
    return pl.BlockSpec(shape, lambda *_: (0,) * nd, pipeline_mode=pl.Buffered(1))


def _params(n_axes):
    return pltpu.CompilerParams(
        dimension_semantics=("arbitrary",) * n_axes, vmem_limit_bytes=VMEM_LIMIT)


def _log_sigmoid(x):
    return jnp.minimum(x, 0.0) - jnp.log1p(jnp.exp(-jnp.abs(x)))


def _silu(x):
    return x * jax.nn.sigmoid(x)


def _pack_words(val):
    half = val.shape[1] // 2
    lo = lax.bitcast_convert_type(val[:, :half].astype(BF16).astype(F32), I32)
    hi = lax.bitcast_convert_type(val[:, half:].astype(BF16).astype(F32), I32)
    return (hi & HIGH_HALF) | lax.shift_right_logical(lo, 16)


def _unpack_words(word):
    lo = lax.bitcast_convert_type(lax.shift_left(word, 16), F32)
    hi = lax.bitcast_convert_type(word & HIGH_HALF, F32)
    return lo, hi


def _row_tile_copies(hbm_rows, row0, tile, sem, to_hbm):
    n = tile.shape[0]
    copies = []
    for c in range(SUBLANES):
        hbm = hbm_rows.at[pl.ds(row0, n), c, :]
        vmem = tile.at[:, pl.ds(c * LANES, LANES)]
        copies.append(pltpu.make_async_copy(vmem, hbm, sem) if to_hbm
                      else pltpu.make_async_copy(hbm, vmem, sem))
    return copies


def _start_all(copies):
    for cp in copies:
        cp.start()


def _wait_all(copies):
    for cp in copies:
        cp.wait()


def _adaln_body(c_ref, w_ref, b_ref, o_ref):
    s = _silu(c_ref[...])
    o_ref[...] = jnp.dot(s.astype(BF16), w_ref[...].astype(BF16),
                         preferred_element_type=F32) + b_ref[...]


def _adaln(cc, w, b):
    d, n6 = w.shape
    return pl.pallas_call(
        _adaln_body,
        grid=(n6 // ADALN_TILE,),
        in_specs=[pl.BlockSpec((SUBLANES, d), lambda j: (0, 0)),
                  pl.BlockSpec((d, ADALN_TILE), lambda j: (0, j)),
                  pl.BlockSpec((1, ADALN_TILE), lambda j: (0, j))],
        out_specs=pl.BlockSpec((SUBLANES, ADALN_TILE), lambda j: (0, j)),
        out_shape=jax.ShapeDtypeStruct((SUBLANES, n6), F32),
        compiler_params=_params(1),
        name="adaln",
    )(cc, w, b)


def _norm_mod(x, g, shift, scale):
    y = x * lax.rsqrt(jnp.mean(x * x, axis=-1, keepdims=True) + EPS) * g
    return y * (1.0 + scale) + shift


def _gate_prep(xb, wg, gb_ref, g_ref, gt_ref):
    tm = xb.shape[0]
    gg = jnp.dot(xb, wg, preferred_element_type=F32) + gb_ref[...]
    lane = lax.broadcasted_iota(I32, (tm, LANES), 1)
    is_f = (lane & N_HEADS) != 0
    is_bwd = (lane & (2 * N_HEADS)) != 0
    lf = jnp.where(is_f, _log_sigmoid(gg), 0.0)
    r = lax.broadcasted_iota(I32, (tm, tm), 0)
    c = lax.broadcasted_iota(I32, (tm, tm), 1)
    same = (r // CHUNK) == (c // CHUNK)
    tri_l = jnp.where(same & (c <= r), 1.0, 0.0).astype(F32)
    tri_u = jnp.where(same & (c >= r), 1.0, 0.0).astype(F32)
    pre = jnp.dot(tri_l, lf, precision=_HIGHEST, preferred_element_type=F32)
    suf = jnp.dot(tri_u, lf, precision=_HIGHEST, preferred_element_type=F32)
    out = jnp.where(is_f, jnp.where(is_bwd, suf, pre), gg)
    diff = out - pltpu.roll(out, LANES - N_HEADS, axis=1)
    pos = lax.broadcasted_iota(I32, (tm, 1), 0) % CHUNK
    run_f, run_b = diff, diff
    k = 1
    while k < CHUNK:
        run_f = jnp.maximum(run_f, jnp.where(pos >= k, pltpu.roll(run_f, k, axis=0), _NEG_INF))
        run_b = jnp.maximum(run_b, jnp.where(pos < CHUNK - k, pltpu.roll(run_b, tm - k, axis=0), _NEG_INF))
        k *= 2
    run = jnp.where(is_bwd, run_b, run_f)
    fwd_lanes = (lane >= N_GATES) & (lane < N_GATES + N_HEADS)
    bwd_lanes = (lane >= N_GATES + N_HEADS) & (lane < GATE_COLS)
    out = jnp.where(fwd_lanes, pltpu.roll(run, N_GATES, axis=1),
                    jnp.where(bwd_lanes, pltpu.roll(run, N_GATES - N_HEADS, axis=1), out))
    g_ref[...] = out[:, :GATE_COLS]
    gt_ref[...] = out.T[:GATE_COLS, :]


def _project_transposed(wt_ref, xb):
    return lax.dot_general(wt_ref[...], xb, (((1,), (1,)), ((), ())),
                           preferred_element_type=F32).astype(BF16)


def _inproj_body(x_ref, sh_ref, sc_ref, g1_ref, w_ref, wkt_ref, gb_ref, cw_ref,
                 conv_ref, q_ref, k_ref, v_ref, o_ref, g_ref, gt_ref, *, conv_dim, qk_all, v_all):
    tm = x_ref.shape[0]
    xb = _norm_mod(x_ref[...], g1_ref[...], sh_ref[0], sc_ref[0]).astype(BF16)

    def proj(lo, width):
        return jnp.dot(xb, w_ref[:, lo:lo + width], preferred_element_type=F32)

    u = proj(conv_dim, conv_dim) * proj(2 * conv_dim, conv_dim)
    pos = lax.broadcasted_iota(I32, (tm, 1), 0) % GRID_W
    um = jnp.where(pos == 0, 0.0, pltpu.roll(u, 1, axis=0))
    up = jnp.where(pos == GRID_W - 1, 0.0, pltpu.roll(u, tm - 1, axis=0))
    y = um * cw_ref[0:1, :] + u * cw_ref[1:2, :] + up * cw_ref[2:3, :]
    conv_ref[...] = (proj(0, conv_dim) * y).astype(BF16)

    off = 3 * conv_dim
    qscale = (qk_all // N_HEADS) ** -0.5
    q_ref[...] = (proj(off, qk_all) * qscale).astype(BF16)
    k_ref[...] = _project_transposed(wkt_ref, xb)
    v_ref[...] = proj(off + 2 * qk_all, v_all).astype(BF16)
    o_ref[...] = jax.nn.sigmoid(proj(off + 2 * qk_all + v_all, v_all)).astype(BF16)
    gate_lo = off + 2 * qk_all + 2 * v_all
    _gate_prep(xb, w_ref[:, gate_lo:gate_lo + LANES], gb_ref, g_ref, gt_ref)


def _inproj_ctx_body(x_ref, sh_ref, sc_ref, g1_ref, w_ref, wkt_ref, wg_ref, gb_ref,
                     k_ref, v_ref, g_ref, gt_ref):
    xb = _norm_mod(x_ref[...], g1_ref[...], sh_ref[0], sc_ref[0]).astype(BF16)
    k_ref[...] = _project_transposed(wkt_ref, xb)
    v_ref[...] = jnp.dot(xb, w_ref[...], preferred_element_type=F32).astype(BF16)
    _gate_prep(xb, wg_ref[...], gb_ref, g_ref, gt_ref)


def _mod_spec(part, tiles_per_row, fixed_row=None):
    def index(i):
        row = fixed_row if fixed_row is not None else i // tiles_per_row
        return (row * 6 + part, 0, 0)

    return index


def _inproj(x2d, mod, g1, w_all, w_kt, gate_b, conv_w, rows_per_batch, conv_dim, qk_all, v_all):
    n, d = x2d.shape
    tm = ROW_TILE
    tiles_per_batch = rows_per_batch // tm
    row = lambda i: (i, 0)
    mod_block = (1, 1, d)
    out_shapes = (
        jax.ShapeDtypeStruct((n, conv_dim), BF16),
        jax.ShapeDtypeStruct((n, qk_all), BF16),
        jax.ShapeDtypeStruct((qk_all, n), BF16),
        jax.ShapeDtypeStruct((n, v_all), BF16),
        jax.ShapeDtypeStruct((n, v_all), BF16),
        jax.ShapeDtypeStruct((n, GATE_COLS), F32),
        jax.ShapeDtypeStruct((GATE_COLS, n), F32),
    )
    out_specs = (
        pl.BlockSpec((tm, conv_dim), row),
        pl.BlockSpec((tm, qk_all), row),
        pl.BlockSpec((qk_all, tm), lambda i: (0, i)),
        pl.BlockSpec((tm, v_all), row),
        pl.BlockSpec((tm, v_all), row),
        pl.BlockSpec((tm, GATE_COLS), row),
        pl.BlockSpec((GATE_COLS, tm), lambda i: (0, i)),
    )
    return pl.pallas_call(
        functools.partial(_inproj_body, conv_dim=conv_dim, qk_all=qk_all, v_all=v_all),
        grid=(n // tm,),
        in_specs=[pl.BlockSpec((tm, d), row),
                  pl.BlockSpec(mod_block, _mod_spec(0, tiles_per_batch)),
                  pl.BlockSpec(mod_block, _mod_spec(1, tiles_per_batch)),
                  _resident(g1.shape), _resident(w_all.shape), _resident(w_kt.shape),
                  _resident(gate_b.shape), _resident(conv_w.shape)],
        out_specs=out_specs,
        out_shape=out_shapes,
        compiler_params=_params(1),
        name="inproj",
    )(x2d, mod, mod, g1, w_all, w_kt, gate_b, conv_w)


def _inproj_ctx(c2d, mod, g1, w_v, w_kt, w_gate, gate_b, ctx_mod_row):
    n, d = c2d.shape
    tm = ROW_TILE
    row = lambda i: (i, 0)
    mod_block = (1, 1, d)
    qk_all, v_all = w_kt.shape[0], w_v.shape[1]
    return pl.pallas_call(
        _inproj_ctx_body,
        grid=(n // tm,),
        in_specs=[pl.BlockSpec((tm, d), row),
                  pl.BlockSpec(mod_block, _mod_spec(0, 1, ctx_mod_row)),
                  pl.BlockSpec(mod_block, _mod_spec(1, 1, ctx_mod_row)),
                  _resident(g1.shape), _resident(w_v.shape), _resident(w_kt.shape),
                  _resident(w_gate.shape), _resident(gate_b.shape)],
        out_specs=(pl.BlockSpec((qk_all, tm), lambda i: (0, i)), pl.BlockSpec((tm, v_all), row),
                   pl.BlockSpec((tm, GATE_COLS), row), pl.BlockSpec((GATE_COLS, tm), lambda i: (0, i))),
        out_shape=(jax.ShapeDtypeStruct((qk_all, n), BF16), jax.ShapeDtypeStruct((n, v_all), BF16),
                   jax.ShapeDtypeStruct((n, GATE_COLS), F32), jax.ShapeDtypeStruct((GATE_COLS, n), F32)),
        compiler_params=_params(1),
        name="inproj_ctx",
    )(c2d, mod, mod, g1, w_v, w_kt, w_gate, gate_b)


def _with_ones(v):
    return jnp.concatenate([v, jnp.ones((v.shape[0], LANES), v.dtype)], axis=1)


def _mlstm_state_update(h, direction, kt_ref, v_ref, gt_ref, s_ref, m_ref, qk, vh):
    ci = direction * 2 * N_HEADS + h
    cb = ci + N_HEADS
    last = 0 if direction else CHUNK - 1
    kt = kt_ref[h * qk:(h + 1) * qk, :].astype(F32)
    va = _with_ones(v_ref[:, h * vh:(h + 1) * vh])
    b_last = gt_ref[cb:cb + 1, last:last + 1]
    m_prev = m_ref[h][0:1, 0:1]
    g_r = b_last - gt_ref[cb:cb + 1, :] + gt_ref[ci:ci + 1, :]
    cm = N_GATES + direction * N_HEADS + h
    m_new = b_last + jnp.maximum(m_prev, gt_ref[cm:cm + 1, last:last + 1])
    a = jnp.exp(b_last + m_prev - m_new)
    kw = (kt * jnp.exp(g_r - m_new)).astype(BF16)
    s_ref[h] = a * s_ref[h] + jnp.dot(kw, va, preferred_element_type=F32)
    m_ref[h] = jnp.broadcast_to(m_new, m_ref.shape[1:])


def _mlstm_head_output(h, direction, q_ref, kt_ref, v_ref, g_ref, gt_ref, s_ref, m_ref, qk, vh):
    ci = direction * 2 * N_HEADS + h
    cb = ci + N_HEADS
    q = q_ref[:, h * qk:(h + 1) * qk]
    kt = kt_ref[h * qk:(h + 1) * qk, :]
    va = _with_ones(v_ref[:, h * vh:(h + 1) * vh])
    ig_r = gt_ref[ci:ci + 1, :]
    b_r = gt_ref[cb:cb + 1, :]
    b_c = g_ref[:, cb:cb + 1]
    m_prev = m_ref[h][0:1, 0:1]
    row = lax.broadcasted_iota(I32, (CHUNK, CHUNK), 0)
    col = lax.broadcasted_iota(I32, (CHUNK, CHUNK), 1)
    mask = (col >= row) if direction else (col <= row)
    dm = jnp.where(mask, b_c + (ig_r - b_r), _NEG_INF)
    inter = b_c + m_prev
    cm = N_GATES + direction * N_HEADS + h
    m_t = jnp.maximum(inter, b_c + g_ref[:, cm:cm + 1])
    w_inter = jnp.exp(inter - m_t)
    s = jnp.dot(q, kt, preferred_element_type=F32) * jnp.exp(dm - m_t)
    intra = jnp.dot(s.astype(BF16), va, preferred_element_type=F32)
    carried = jnp.dot(q, s_ref[h].astype(BF16), preferred_element_type=F32)
    num = intra[:, 0:vh] + w_inter * carried[:, 0:vh]
    den = intra[:, vh:vh + 1] + w_inter * carried[:, vh:vh + 1]
    return num / jnp.maximum(jnp.abs(den), jnp.exp(-m_t))


def _mlstm_body(*refs, direction, bsz, n_ctx_chunks, qk, vh):
    q_ref, v_ref, g_ref, vc_ref = refs[0:4]
    kt_refs, gt_refs = refs[4:4 + bsz], refs[4 + bsz:4 + 2 * bsz]
    ktc_refs, gtc_refs = refs[4 + 2 * bsz:4 + 3 * bsz], refs[4 + 3 * bsz:4 + 4 * bsz]
    rest = refs[4 + 4 * bsz:]
    if direction:
        out_ref, s_ref, m_ref = rest
    else:
        hb_ref, og_ref, hg_ref, out_ref, s_ref, m_ref = rest
    step = pl.program_id(0)

    @pl.when(step == 0)
    def _():
        s_ref[...] = jnp.zeros_like(s_ref)
        m_ref[...] = jnp.full_like(m_ref, _NEG_INF)

    @pl.when(step < n_ctx_chunks)
    def _():
        for b in range(bsz):
            for h in range(N_HEADS):
                _mlstm_state_update(h, direction, ktc_refs[b], vc_ref.at[b], gtc_refs[b],
                                    s_ref.at[b], m_ref.at[b], qk, vh)

    @pl.when(step >= n_ctx_chunks)
    def _():
        for b in range(bsz):
            for h in range(N_HEADS):
                hh = _mlstm_head_output(h, direction, q_ref.at[b], kt_refs[b], v_ref.at[b], g_ref.at[b],
                                        gt_refs[b], s_ref.at[b], m_ref.at[b], qk, vh)
                cols = slice(h * vh, (h + 1) * vh)
                if direction:
                    out_ref[b, :, cols] = hh
                else:
                    hs = hh + hb_ref[b, :, cols]
                    hs = hs * lax.rsqrt(jnp.mean(hs * hs, axis=-1, keepdims=True) + EPS)
                    out_ref[b, :, cols] = (hs * hg_ref[:, cols]
                                           * og_ref[b, :, cols].astype(F32)).astype(BF16)
                _mlstm_state_update(h, direction, kt_refs[b], v_ref.at[b], gt_refs[b],
                                    s_ref.at[b], m_ref.at[b], qk, vh)


def _mlstm(direction, q, kt, v, g, gt, ktc, vc, gtc, extra, bsz, head_g=None):
    n, qk_all = q.shape
    v_all = v.shape[1]
    qk, vh = qk_all // N_HEADS, v_all // N_HEADS
    seq = n // bsz
    nc = seq // CHUNK
    ncc = vc.shape[0] // bsz // CHUNK

    def lat(s):
        j = jnp.clip(s - ncc, 0, nc - 1)
        return nc - 1 - j if direction else j

    def ctx(s):
        j = jnp.clip(s, 0, ncc - 1)
        return ncc - 1 - j if direction else j

    def per_batch(a):
        return a.reshape(bsz, a.shape[0] // bsz, a.shape[1])

    lat_blk = lambda c: pl.BlockSpec((bsz, CHUNK, c), lambda s: (0, lat(s), 0))
    in_specs = [lat_blk(qk_all), lat_blk(v_all), lat_blk(GATE_COLS),
                pl.BlockSpec((bsz, CHUNK, v_all), lambda s: (0, ctx(s), 0))]
    args = [per_batch(q), per_batch(v), per_batch(g), per_batch(vc)]
    for arr, rows, n_chunks, pos in ((kt, qk_all, nc, lat), (gt, GATE_COLS, nc, lat),
                                     (ktc, qk_all, ncc, ctx), (gtc, GATE_COLS, ncc, ctx)):
        for b in range(bsz):
            in_specs.append(pl.BlockSpec((rows, CHUNK), lambda s, b=b, n_chunks=n_chunks, pos=pos:
                                         (0, b * n_chunks + pos(s))))
            args.append(arr)
    if direction:
        out_dtype = F32
    else:
        hb, og = extra
        in_specs += [lat_blk(v_all), lat_blk(v_all), pl.BlockSpec((1, v_all), lambda s: (0, 0))]
        args += [per_batch(hb), per_batch(og), head_g]
        out_dtype = BF16
    out = pl.pallas_call(
        functools.partial(_mlstm_body, direction=direction, bsz=bsz, n_ctx_chunks=ncc, qk=qk, vh=vh),
        grid=(ncc + nc,),
        in_specs=in_specs,
        out_specs=lat_blk(v_all),
        out_shape=jax.ShapeDtypeStruct((bsz, seq, v_all), out_dtype),
        scratch_shapes=[pltpu.VMEM((bsz, N_HEADS, qk, vh + LANES), F32),
                        pltpu.VMEM((bsz, N_HEADS, SUBLANES, LANES), F32)],
        compiler_params=_params(1),
        name="mlstm_bwd" if direction else "mlstm_fwd",
    )(*args)
    return out.reshape(n, v_all)


def _outproj_body(conv_ref, ml_ref, x_ref, gt1_ref, sh2_ref, sc2_ref, g2_ref, wo_ref, wr_ref, br_ref,
                  x1_ref, idx_ref, gate_ref, rank_ref, cnt_ref, h_hbm, carry_ref, hw, hsem):
    tm = x_ref.shape[0]
    half = conv_ref.shape[1]
    step = pl.program_id(0)
    buf = step % 2

    def h_out(i, s):
        return _row_tile_copies(h_hbm, i * tm, hw.at[s], hsem.at[s], to_hbm=True)

    @pl.when(step == 0)
    def _():
        carry_ref[...] = jnp.zeros_like(carry_ref)

    @pl.when(step >= 2)
    def _():
        _wait_all(h_out(step - 2, buf))

    y = (jnp.dot(conv_ref[...], wo_ref[0:half, :], preferred_element_type=F32)
         + jnp.dot(ml_ref[...], wo_ref[half:2 * half, :], preferred_element_type=F32))
    x1 = x_ref[...] + gt1_ref[0] * y
    x1_ref[...] = x1
    hn = _norm_mod(x1, g2_ref[...], sh2_ref[0], sc2_ref[0])
    hw[buf] = _pack_words(hn)
    _start_all(h_out(step, buf))

    h_hi = hn.astype(BF16)
    h_lo = (hn - h_hi.astype(F32)).astype(BF16)
    parts = (jnp.dot(h_hi, wr_ref[...], preferred_element_type=F32)
             + jnp.dot(h_lo, wr_ref[...], preferred_element_type=F32))
    scores = jax.nn.sigmoid(parts + pltpu.roll(parts, N_EXPERTS, axis=1))
    lane = lax.broadcasted_iota(I32, (tm, LANES), 1).astype(F32)
    biased = jnp.where(lane < N_EXPERTS, scores + br_ref[...], _NEG_INF)
    onehot = jnp.zeros((tm, LANES), F32)
    picks, sels = [], []
    for _ in range(TOP_K):
        mx = jnp.max(biased, axis=1, keepdims=True)
        pick = jnp.min(jnp.where(biased == mx, lane, float(LANES)), axis=1, keepdims=True)
        hit = lane == pick
        sels.append(jnp.sum(jnp.where(hit, scores, 0.0), axis=1, keepdims=True))
        picks.append(pick)
        biased = jnp.where(hit, _NEG_INF, biased)
        onehot = onehot + hit.astype(F32)
    total = sels[0]
    for s in sels[1:]:
        total = total + s

    r = lax.broadcasted_iota(I32, (tm, tm), 0)
    c = lax.broadcasted_iota(I32, (tm, tm), 1)
    strict = jnp.where(c < r, 1.0, 0.0).astype(BF16)
    before = jnp.dot(strict, onehot.astype(BF16), preferred_element_type=F32) + carry_ref[...]
    slot = lax.broadcasted_iota(I32, (tm, SUBLANES), 1)
    idx_out = jnp.zeros((tm, LANES), F32)
    rank_out = jnp.zeros((tm, LANES), F32)
    gate_out = jnp.zeros((tm, SUBLANES), F32)
    for j in range(TOP_K):
        rank = jnp.sum(jnp.where(lane == picks[j], before, 0.0), axis=1, keepdims=True)
        idx_out = jnp.where(lane == float(j), picks[j], idx_out)
        rank_out = jnp.where(lane == float(j), rank, rank_out)
        gate_out = jnp.where(slot == j, sels[j] / total * ROUTED_SCALE, gate_out)
    idx_ref[...] = idx_out.T[:SUBLANES, :].astype(I32)
    rank_ref[...] = rank_out.T[:SUBLANES, :].astype(I32)
    gate_ref[...] = gate_out
    carry_ref[...] = carry_ref[...] + jnp.sum(onehot, axis=0, keepdims=True)
    cnt_ref[...] = jnp.broadcast_to(carry_ref[...], cnt_ref.shape).astype(I32)

    @pl.when(step == pl.num_programs(0) - 1)
    def _():
        @pl.when(step >= 1)
        def _():
            _wait_all(h_out(step - 1, 1 - buf))
        _wait_all(h_out(step, buf))


def _outproj(conv, ml, x2d, mod, g2, w_out, w_router, b_router, rows_per_batch):
    n, d = x2d.shape
    tm = ROW_TILE
    tiles_per_batch = rows_per_batch // tm
    row = lambda i: (i, 0)
    mod_block = (1, 1, d)
    half = conv.shape[1]
    return pl.pallas_call(
        _outproj_body,
        grid=(n // tm,),
        in_specs=[pl.BlockSpec((tm, half), row), pl.BlockSpec((tm, half), row), pl.BlockSpec((tm, d), row),
                  pl.BlockSpec(mod_block, _mod_spec(2, tiles_per_batch)),
                  pl.BlockSpec(mod_block, _mod_spec(3, tiles_per_batch)),
                  pl.BlockSpec(mod_block, _mod_spec(4, tiles_per_batch)),
                  _resident(g2.shape), _resident(w_out.shape), _resident(w_router.shape),
                  _resident(b_router.shape)],
        out_specs=(pl.BlockSpec((tm, d), row),
                   pl.BlockSpec((SUBLANES, tm), lambda i: (0, i)), pl.BlockSpec((tm, SUBLANES), row),
                   pl.BlockSpec((SUBLANES, tm), lambda i: (0, i)),
                   pl.BlockSpec((SUBLANES, LANES), lambda i: (0, 0)),
                   pl.BlockSpec(memory_space=pl.ANY)),
        out_shape=(jax.ShapeDtypeStruct((n, d), F32),
                   jax.ShapeDtypeStruct((SUBLANES, n), I32), jax.ShapeDtypeStruct((n, SUBLANES), F32),
                   jax.ShapeDtypeStruct((SUBLANES, n), I32),
                   jax.ShapeDtypeStruct((SUBLANES, LANES), I32),
                   jax.ShapeDtypeStruct((n, SUBLANES, LANES), I32)),
        scratch_shapes=[pltpu.VMEM((1, LANES), F32), pltpu.VMEM((2, tm, d // 2), I32),
                        pltpu.SemaphoreType.DMA((2,))],
        compiler_params=_params(1),
        name="outproj_router",
    )(conv, ml, x2d, mod, mod, mod, g2, w_out, w_router, b_router)


def _sc_workers():
    info = plsc.get_sparse_core_info()
    return info.num_cores, info.num_cores * info.num_subcores


def _sc_dispatch(h_rows, dest_chunks, n_slots):
    n_tok = h_rows.shape[0]
    n_cores, n_workers = _sc_workers()
    per_worker = n_tok // (n_workers * SC_CHUNK)
    assert per_worker * n_workers * SC_CHUNK == n_tok
    mesh = plsc.VectorSubcoreMesh(core_axis_name="c", subcore_axis_name="s")

    @functools.partial(
        pl.kernel, mesh=mesh,
        out_type=jax.ShapeDtypeStruct((n_slots,) + h_rows.shape[1:], h_rows.dtype),
        scratch_types=[pltpu.VMEM((TOP_K, SC_CHUNK), I32),
                       pltpu.VMEM((SC_CHUNK,) + h_rows.shape[1:], h_rows.dtype)],
    )
    def dispatch(h_hbm, dest_hbm, out_hbm, idx_v, rows_v):
        wid = lax.axis_index("s") * n_cores + lax.axis_index("c")

        @pl.loop(0, per_worker)
        def _(i):
            chunk = wid * per_worker + i
            pltpu.sync_copy(dest_hbm.at[chunk], idx_v)
            pltpu.sync_copy(h_hbm.at[pl.ds(chunk * SC_CHUNK, SC_CHUNK)], rows_v)
            for k in range(TOP_K):
                pltpu.sync_copy(rows_v, out_hbm.at[idx_v.at[k]])

    return dispatch(h_rows, dest_chunks)


def _sc_combine(y_sorted, dest_chunks, n_tok):
    n_cores, n_workers = _sc_workers()
    per_worker = n_tok // (n_workers * SC_CHUNK)
    mesh = plsc.VectorSubcoreMesh(core_axis_name="c", subcore_axis_name="s")

    @functools.partial(
        pl.kernel, mesh=mesh,
        out_type=jax.ShapeDtypeStruct((TOP_K, n_tok) + y_sorted.shape[1:], y_sorted.dtype),
        scratch_types=[pltpu.VMEM((TOP_K, SC_CHUNK), I32),
                       pltpu.VMEM((SC_CHUNK,) + y_sorted.shape[1:], y_sorted.dtype)],
    )
    def combine(y_hbm, dest_hbm, out_hbm, idx_v, rows_v):
        wid = lax.axis_index("s") * n_cores + lax.axis_index("c")

        @pl.loop(0, per_worker)
        def _(i):
            chunk = wid * per_worker + i
            pltpu.sync_copy(dest_hbm.at[chunk], idx_v)
            for k in range(TOP_K):
                pltpu.sync_copy(y_hbm.at[idx_v.at[k]], rows_v)
                pltpu.sync_copy(rows_v, out_hbm.at[k, pl.ds(chunk * SC_CHUNK, SC_CHUNK)])

    return combine(y_sorted, dest_chunks)


def _moe_body(ord_ref, order_ref, glo_ref, ghi_ref, tot_ref, nb_ref,
              x_hbm, wg_hbm, wu_hbm, wd_hbm, y_hbm,
              wgu, wd, stage_a, stage_d, xw, yw, wsem, xsem, ysem, *, d_expert):
    b = pl.program_id(0)
    nb = nb_ref[0]
    total = tot_ref[0]
    d_model = wgu.shape[1]
    rows_a = d_model // WEIGHT_PARTS
    rows_d = d_expert // WEIGHT_PARTS

    def part_copies(g):
        e = order_ref[lax.shift_right_logical(g, PART_SHIFT)]
        i = g & (WEIGHT_PARTS - 1)
        s = lax.rem(g, WEIGHT_RING)
        return (pltpu.make_async_copy(wg_hbm.at[e, pl.ds(i * rows_a, rows_a)], stage_a.at[s, 0],
                                      wsem.at[s, 0]),
                pltpu.make_async_copy(wu_hbm.at[e, pl.ds(i * rows_a, rows_a)], stage_a.at[s, 1],
                                      wsem.at[s, 1]),
                pltpu.make_async_copy(wd_hbm.at[e, pl.ds(i * rows_d, rows_d)], stage_d.at[s],
                                      wsem.at[s, 2]))

    def start_part(g):
        for cp in part_copies(g):
            cp.start()

    def wait_part(g):
        for cp in part_copies(g):
            cp.wait()

    def cast_part(g):
        i = g & (WEIGHT_PARTS - 1)
        s = lax.rem(g, WEIGHT_RING)
        par = lax.shift_right_logical(g, PART_SHIFT) & 1
        ra = pl.multiple_of(i * rows_a, rows_a)
        rd = pl.multiple_of(i * rows_d, rows_d)
        wgu[par, pl.ds(ra, rows_a), 0:d_expert] = stage_a[s, 0].astype(BF16)
        wgu[par, pl.ds(ra, rows_a), d_expert:2 * d_expert] = stage_a[s, 1].astype(BF16)
        wd[par, pl.ds(rd, rows_d), :] = stage_d[s].astype(BF16)

    def refill(g):
        @pl.when(g + WEIGHT_RING < total)
        def _():
            start_part(g + WEIGHT_RING)

    def cast_parts(lo, hi):
        def body(g, carry):
            wait_part(g)
            cast_part(g)
            refill(g)
            return carry
        lax.fori_loop(lo, hi, body, 0)

    slot = b % 2

    def x_in(blk, s):
        return _row_tile_copies(x_hbm, blk * MOE_BLOCK, xw.at[s], xsem.at[s], to_hbm=False)

    def y_out(blk, s):
        return _row_tile_copies(y_hbm, blk * MOE_BLOCK, yw.at[s], ysem.at[s], to_hbm=True)

    @pl.when(b == 0)
    def _():
        _start_all(x_in(0, 0))
        for g in range(WEIGHT_RING):
            start_part(g)
        cast_parts(0, WEIGHT_PARTS)

    @pl.when(b + 1 < nb)
    def _():
        _start_all(x_in(b + 1, 1 - slot))

    @pl.when(b < nb)
    def _():
        par = ord_ref[b] & 1
        _wait_all(x_in(b, slot))

        @pl.when(b >= 2)
        def _():
            _wait_all(y_out(b - 2, slot))

        x = jnp.concatenate(_unpack_words(xw[slot]), axis=1).astype(BF16)
        gu = jnp.dot(x, wgu[par], preferred_element_type=F32)
        hb = (_silu(gu[:, 0:d_expert]) * gu[:, d_expert:2 * d_expert]).astype(BF16)
        yw[slot] = _pack_words(jnp.dot(hb, wd[par], preferred_element_type=F32))
        _start_all(y_out(b, slot))
        cast_parts(glo_ref[b], ghi_ref[b])

        @pl.when(b == nb - 1)
        def _():
            @pl.when(b >= 1)
            def _():
                _wait_all(y_out(b - 1, 1 - slot))
            _wait_all(y_out(b, slot))


def _moe(x_sorted, we_gate, we_up, we_down, tables):
    d, d_expert = we_gate.shape[1], we_gate.shape[2]
    nb_max = x_sorted.shape[0] // MOE_BLOCK
    any_spec = pl.BlockSpec(memory_space=pl.ANY)
    grid_spec = pltpu.PrefetchScalarGridSpec(
        num_scalar_prefetch=len(tables),
        grid=(nb_max,),
        in_specs=[any_spec, any_spec, any_spec, any_spec],
        out_specs=any_spec,
        scratch_shapes=[pltpu.VMEM((2, d, 2 * d_expert), BF16),
                        pltpu.VMEM((2, d_expert, d), BF16),
                        pltpu.VMEM((WEIGHT_RING, 2, d // WEIGHT_PARTS, d_expert), F32),
                        pltpu.VMEM((WEIGHT_RING, d_expert // WEIGHT_PARTS, d), F32),
                        pltpu.VMEM((2, MOE_BLOCK, d // 2), I32),
                        pltpu.VMEM((2, MOE_BLOCK, d // 2), I32),
                        pltpu.SemaphoreType.DMA((WEIGHT_RING, 3)),
                        pltpu.SemaphoreType.DMA((2,)),
                        pltpu.SemaphoreType.DMA((2,))],
    )
    return pl.pallas_call(
        functools.partial(_moe_body, d_expert=d_expert),
        grid_spec=grid_spec,
        out_shape=jax.ShapeDtypeStruct(x_sorted.shape, x_sorted.dtype),
        compiler_params=pltpu.CompilerParams(
            dimension_semantics=("arbitrary",), vmem_limit_bytes=MOE_VMEM_LIMIT),
        name="moe_routed",
    )(*tables, x_sorted, we_gate, we_up, we_down)


def _final_body(x1_ref, gate_ref, gt2_ref, wsgu_ref, wsd_ref, fg_ref, h_hbm, y_hbm, out_ref,
                hw, yw, sem, *, d_shared):
    tm = x1_ref.shape[0]
    step = pl.program_id(0)
    slot = step % 2

    def rows_in(i, s):
        copies = _row_tile_copies(h_hbm, i * tm, hw.at[s], sem.at[s], to_hbm=False)
        for k in range(TOP_K):
            copies += _row_tile_copies(y_hbm.at[k], i * tm, yw.at[s, k], sem.at[s], to_hbm=False)
        return copies

    @pl.when(step == 0)
    def _():
        _start_all(rows_in(0, 0))

    @pl.when(step + 1 < pl.num_programs(0))
    def _():
        _start_all(rows_in(step + 1, 1 - slot))

    _wait_all(rows_in(step, slot))
    routed = gate_ref[:, 0:1] * jnp.concatenate(_unpack_words(yw[slot, 0]), axis=1)
    for k in range(1, TOP_K):
        routed = routed + gate_ref[:, k:k + 1] * jnp.concatenate(_unpack_words(yw[slot, k]), axis=1)
    h = jnp.concatenate(_unpack_words(hw[slot]), axis=1).astype(BF16)
    gu = jnp.dot(h, wsgu_ref[...], preferred_element_type=F32)
    hb = (_silu(gu[:, 0:d_shared]) * gu[:, d_shared:2 * d_shared]).astype(BF16)
    x2 = x1_ref[...] + gt2_ref[0] * (routed + jnp.dot(hb, wsd_ref[...], preferred_element_type=F32))
    out_ref[...] = x2 * lax.rsqrt(jnp.mean(x2 * x2, axis=-1, keepdims=True) + EPS) * fg_ref[...]


def _final(h_rows, x1, y_tok, gates, mod, ws_gu, ws_d, final_g, rows_per_batch):
    n, d = x1.shape
    tm = ROW_TILE
    tiles_per_batch = rows_per_batch // tm
    row = lambda i: (i, 0)
    any_spec = pl.BlockSpec(memory_space=pl.ANY)
    return pl.pallas_call(
        functools.partial(_final_body, d_shared=ws_d.shape[0]),
        grid=(n // tm,),
        in_specs=[pl.BlockSpec((tm, d), row), pl.BlockSpec((tm, SUBLANES), row),
                  pl.BlockSpec((1, 1, d), _mod_spec(5, tiles_per_batch)),
                  _resident(ws_gu.shape), _resident(ws_d.shape), _resident(final_g.shape),
                  any_spec, any_spec],
        out_specs=pl.BlockSpec((tm, d), row),
        out_shape=jax.ShapeDtypeStruct((n, d), F32),
        scratch_shapes=[pltpu.VMEM((2, tm, d // 2), I32), pltpu.VMEM((2, TOP_K, tm, d // 2), I32),
                        pltpu.SemaphoreType.DMA((2,))],
        compiler_params=_params(1),
        name="shared_combine_final",
    )(x1, gates, mod, ws_gu, ws_d, final_g, h_rows, y_tok)


def _routing_tables(idx, rank, counts, n_tok):
    nb_max = -(-(n_tok * TOP_K) // MOE_BLOCK) + N_EXPERTS
    nblk = (counts + MOE_BLOCK - 1) // MOE_BLOCK
    blk_end = jnp.cumsum(nblk)
    blk_start = blk_end - nblk
    experts = jnp.arange(N_EXPERTS, dtype=I32)[:, None, None]
    first_slot = (blk_start * MOE_BLOCK)[:, None, None]
    dest = jnp.sum(jnp.where(idx[None] == experts, first_slot, 0), axis=0) + rank
    dest_chunks = dest.reshape(TOP_K, n_tok // SC_CHUNK, SC_CHUNK).transpose(1, 0, 2)

    blocks = jnp.arange(nb_max, dtype=I32)[:, None]
    member = (blk_start[None, :] <= blocks) & (blocks < blk_end[None, :])
    lookup = lambda table: jnp.sum(jnp.where(member, table[None, :], 0), axis=1)
    blocks = blocks[:, 0]
    nonempty = nblk > 0
    n_visited = jnp.sum(nonempty.astype(I32))
    ordinal_of = jnp.cumsum(nonempty.astype(I32)) - 1
    slots = jnp.arange(N_EXPERTS, dtype=I32)
    order = jnp.sum(jnp.where(nonempty[None, :] & (ordinal_of[None, :] == slots[:, None]),
                              slots[None, :], 0), axis=1)
    ordinal = lookup(ordinal_of)
    k_in_e = blocks - lookup(blk_start)
    nb_e = jnp.maximum(lookup(nblk), 1)
    live = (ordinal + 1 < n_visited) & (blocks < blk_end[-1])
    first = WEIGHT_PARTS * (ordinal + 1)
    lo = jnp.where(live, first + WEIGHT_PARTS * k_in_e // nb_e, 0)
    hi = jnp.where(live, first + WEIGHT_PARTS * (k_in_e + 1) // nb_e, 0)
    tables = (ordinal.astype(I32), order, lo.astype(I32), hi.astype(I32),
              (WEIGHT_PARTS * n_visited).reshape(1).astype(I32), blk_end[-1:].astype(I32))
    return tables, dest_chunks, nb_max * MOE_BLOCK


def kernel(x, c, ctx, c_ctx, norm1_g, norm2_g, w_ada, b_ada, w_in, conv_w, gate_b, head_g, w_out,
           w_router, b_router, we_gate, we_up, we_down, ws_gate, ws_up, ws_down, final_g):
    assert w_ada.shape[0] == 1, "single-layer block"
    bsz, seq, d = x.shape
    ctx_len = ctx.shape[1]
    n_tok = bsz * seq
    conv_dim = conv_w.shape[2]
    v_all = head_g.shape[1]
    qk_all = (w_in.shape[2] - 3 * conv_dim - 2 * v_all - N_GATES) // 2
    assert seq % ROW_TILE == 0 and ctx_len % ROW_TILE == 0 and ROW_TILE % GRID_W == 0
    assert bsz + 1 <= SUBLANES

    cc = jnp.zeros((SUBLANES, d), F32).at[:bsz].set(c).at[bsz].set(c_ctx)
    mod = _adaln(cc, w_ada[0], b_ada).reshape(SUBLANES * 6, 1, d)

    n_main = 3 * conv_dim + 2 * qk_all + 2 * v_all
    w_all = jnp.pad(w_in[0].astype(BF16), ((0, 0), (0, LANES - N_GATES)))
    k_lo = 3 * conv_dim + qk_all
    w_kt = w_all[:, k_lo:k_lo + qk_all].T
    w_v = w_all[:, k_lo + qk_all:k_lo + qk_all + v_all]
    w_gate = w_all[:, n_main:n_main + LANES]
    gate_bias = jnp.zeros((1, LANES), F32).at[0, :N_GATES].set(gate_b[0].reshape(-1))

    x2d = x.reshape(n_tok, d)
    conv, q, kt, v, og, g, gt = _inproj(x2d, mod, norm1_g, w_all, w_kt, gate_bias, conv_w[0],
                                       seq, conv_dim, qk_all, v_all)
    ktc, vc, _, gtc = _inproj_ctx(ctx.reshape(bsz * ctx_len, d), mod, norm1_g, w_v, w_kt, w_gate,
                                  gate_bias, bsz)

    h_bwd = _mlstm(1, q, kt, v, g, gt, ktc, vc, gtc, None, bsz)
    ml = _mlstm(0, q, kt, v, g, gt, ktc, vc, gtc, (h_bwd, og), bsz, head_g)

    assert 2 * N_EXPERTS == LANES
    w_r_hi = w_router[0].astype(BF16)
    w_r = jnp.concatenate([w_r_hi, (w_router[0] - w_r_hi.astype(F32)).astype(BF16)], axis=1)
    b_r = jnp.zeros((1, LANES), F32).at[0, :N_EXPERTS].set(b_router[0])
    x1, idx, gates, rank, cnt, h_rows = _outproj(conv, ml, x2d, mod, norm2_g, w_out[0].astype(BF16),
                                             w_r, b_r, seq)

    tables, dest_chunks, n_slots = _routing_tables(idx[:TOP_K], rank[:TOP_K], cnt[0, :N_EXPERTS], n_tok)
    x_sorted = _sc_dispatch(h_rows, dest_chunks, n_slots)
    y_sorted = _moe(x_sorted, we_gate[0], we_up[0], we_down[0], tables)
    y_tok = _sc_combine(y_sorted, dest_chunks, n_tok)

    ws_gu = jnp.concatenate([ws_gate[0], ws_up[0]], axis=1).astype(BF16)
    out = _final(h_rows, x1, y_tok, gates, mod, ws_gu, ws_down[0].astype(BF16),
                 final_g.reshape(1, d), seq)
    return out.reshape(bsz, seq, d)
```

```python
import functools

import jax
import jax.numpy as jnp
from jax import lax
from jax.experimental import pallas as pl
from jax.experimental.pallas import tpu as pltpu
from jax.experimental.pallas import tpu_sc as plsc

F32 = jnp.float32
BF16 = jnp.bfloat16
I32 = jnp.int32

N_HEADS = 4
GRID_W = 64
CHUNK = 128
TOP_K = 6
N_EXPERTS = 64
ROUTED_SCALE = 2.446
EPS = 1e-6
N_GATES = 4 * N_HEADS
GATE_COLS = 6 * N_HEADS

LANES = 128
SUBLANES = 8
MOE_BLOCK = 256
ROW_TILE = 256
ADALN_TILE = 1024
WEIGHT_PARTS = 8
PART_SHIFT = 3
WEIGHT_RING = 3
SC_CHUNK = 64
HIGH_HALF = -65536
VMEM_LIMIT = 56 * 1024 * 1024
MOE_VMEM_LIMIT = 62 * 1024 * 1024

_HIGHEST = lax.Precision.HIGHEST
_NEG_INF = float("-inf")
assert CHUNK == LANES


def _resident(shape):
    nd = len(shape)
    return pl.BlockSpec(shape, lambda *_: (0,) * nd, pipeline_mode=pl.Buffered(1))


def _params(n_axes):
    return pltpu.CompilerParams(
        dimension_semantics=("arbitrary",) * n_axes, vmem_limit_bytes=VMEM_LIMIT)


def _log_sigmoid(x):
    return jnp.minimum(x, 0.0) - jnp.log1p(jnp.exp(-jnp.abs(x)))


def _silu(x):
    return x * jax.nn.sigmoid(x)


def _pack_words(val):
    half = val.shape[1] // 2
    lo = lax.bitcast_convert_type(val[:, :half].astype(BF16).astype(F32), I32)
    hi = lax.bitcast_convert_type(val[:, half:].astype(BF16).astype(F32), I32)
    return (hi & HIGH_HALF) | lax.shift_right_logical(lo, 16)


def _unpack_words(word):
    lo = lax.bitcast_convert_type(lax.shift_left(word, 16), F32)
    hi = lax.bitcast_convert_type(word & HIGH_HALF, F32)
    return lo, hi


def _row_tile_copies(hbm_rows, row0, tile, sem, to_hbm):
    n = tile.shape[0]
    copies = []
    for c in range(SUBLANES):
        hbm = hbm_rows.at[pl.ds(row0, n), c, :]
        vmem = tile.at[:, pl.ds(c * LANES, LANES)]
        copies.append(pltpu.make_async_copy(vmem, hbm, sem) if to_hbm
                      else pltpu.make_async_copy(hbm, vmem, sem))
    return copies


def _start_all(copies):
    for cp in copies:
        cp.start()


def _wait_all(copies):
    for cp in copies:
        cp.wait()


def _adaln_body(c_ref, w_ref, b_ref, o_ref):
    s = _silu(c_ref[...])
    o_ref[...] = jnp.dot(s.astype(BF16), w_ref[...].astype(BF16),
                         preferred_element_type=F32) + b_ref[...]


def _adaln(cc, w, b):
    d, n6 = w.shape
    return pl.pallas_call(
        _adaln_body,
        grid=(n6 // ADALN_TILE,),
        in_specs=[pl.BlockSpec((SUBLANES, d), lambda j: (0, 0)),
                  pl.BlockSpec((d, ADALN_TILE), lambda j: (0, j)),
                  pl.BlockSpec((1, ADALN_TILE), lambda j: (0, j))],
        out_specs=pl.BlockSpec((SUBLANES, ADALN_TILE), lambda j: (0, j)),
        out_shape=jax.ShapeDtypeStruct((SUBLANES, n6), F32),
        compiler_params=_params(1),
        name="adaln",
    )(cc, w, b)


def _norm_mod(x, g, shift, scale):
    y = x * lax.rsqrt(jnp.mean(x * x, axis=-1, keepdims=True) + EPS) * g
    return y * (1.0 + scale) + shift


def _gate_prep(xb, wg, gb_ref, g_ref, gt_ref):
    tm = xb.shape[0]
    gg = jnp.dot(xb, wg, preferred_element_type=F32) + gb_ref[...]
    lane = lax.broadcasted_iota(I32, (tm, LANES), 1)
    is_f = (lane & N_HEADS) != 0
    is_bwd = (lane & (2 * N_HEADS)) != 0
    lf = jnp.where(is_f, _log_sigmoid(gg), 0.0)
    r = lax.broadcasted_iota(I32, (tm, tm), 0)
    c = lax.broadcasted_iota(I32, (tm, tm), 1)
    same = (r // CHUNK) == (c // CHUNK)
    tri_l = jnp.where(same & (c <= r), 1.0, 0.0).astype(F32)
    tri_u = jnp.where(same & (c >= r), 1.0, 0.0).astype(F32)
    pre = jnp.dot(tri_l, lf, precision=_HIGHEST, preferred_element_type=F32)
    suf = jnp.dot(tri_u, lf, precision=_HIGHEST, preferred_element_type=F32)
    out = jnp.where(is_f, jnp.where(is_bwd, suf, pre), gg)
    diff = out - pltpu.roll(out, LANES - N_HEADS, axis=1)
    pos = lax.broadcasted_iota(I32, (tm, 1), 0) % CHUNK
    run_f, run_b = diff, diff
    k = 1
    while k < CHUNK:
        run_f = jnp.maximum(run_f, jnp.where(pos >= k, pltpu.roll(run_f, k, axis=0), _NEG_INF))
        run_b = jnp.maximum(run_b, jnp.where(pos < CHUNK - k, pltpu.roll(run_b, tm - k, axis=0), _NEG_INF))
        k *= 2
    run = jnp.where(is_bwd, run_b, run_f)
    fwd_lanes = (lane >= N_GATES) & (lane < N_GATES + N_HEADS)
    bwd_lanes = (lane >= N_GATES + N_HEADS) & (lane < GATE_COLS)
    out = jnp.where(fwd_lanes, pltpu.roll(run, N_GATES, axis=1),
                    jnp.where(bwd_lanes, pltpu.roll(run, N_GATES - N_HEADS, axis=1), out))
    g_ref[...] = out[:, :GATE_COLS]
    gt_ref[...] = out.T[:GATE_COLS, :]


def _project_transposed(wt_ref, xb):
    return lax.dot_general(wt_ref[...], xb, (((1,), (1,)), ((), ())),
                           preferred_element_type=F32).astype(BF16)


def _inproj_body(x_ref, sh_ref, sc_ref, g1_ref, w_ref, wkt_ref, gb_ref, cw_ref,
                 conv_ref, q_ref, k_ref, v_ref, o_ref, g_ref, gt_ref, *, conv_dim, qk_all, v_all):
    tm = x_ref.shape[0]
    xb = _norm_mod(x_ref[...], g1_ref[...], sh_ref[0], sc_ref[0]).astype(BF16)

    def proj(lo, width):
        return jnp.dot(xb, w_ref[:, lo:lo + width], preferred_element_type=F32)

    u = proj(conv_dim, conv_dim) * proj(2 * conv_dim, conv_dim)
    pos = lax.broadcasted_iota(I32, (tm, 1), 0) % GRID_W
    um = jnp.where(pos == 0, 0.0, pltpu.roll(u, 1, axis=0))
    up = jnp.where(pos == GRID_W - 1, 0.0, pltpu.roll(u, tm - 1, axis=0))
    y = um * cw_ref[0:1, :] + u * cw_ref[1:2, :] + up * cw_ref[2:3, :]
    conv_ref[...] = (proj(0, conv_dim) * y).astype(BF16)

    off = 3 * conv_dim
    qscale = (qk_all // N_HEADS) ** -0.5
    q_ref[...] = (proj(off, qk_all) * qscale).astype(BF16)
    k_ref[...] = _project_transposed(wkt_ref, xb)
    v_ref[...] = proj(off + 2 * qk_all, v_all).astype(BF16)
    o_ref[...] = jax.nn.sigmoid(proj(off + 2 * qk_all + v_all, v_all)).astype(BF16)
    gate_lo = off + 2 * qk_all + 2 * v_all
    _gate_prep(xb, w_ref[:, gate_lo:gate_lo + LANES], gb_ref, g_ref, gt_ref)


def _inproj_ctx_body(x_ref, sh_ref, sc_ref, g1_ref, w_ref, wkt_ref, wg_ref, gb_ref,
                     k_ref, v_ref, g_ref, gt_ref):
    xb = _norm_mod(x_ref[...], g1_ref[...], sh_ref[0], sc_ref[0]).astype(BF16)
    k_ref[...] = _project_transposed(wkt_ref, xb)
    v_ref[...] = jnp.dot(xb, w_ref[...], preferred_element_type=F32).astype(BF16)
    _gate_prep(xb, wg_ref[...], gb_ref, g_ref, gt_ref)


def _mod_spec(part, tiles_per_row, fixed_row=None):
    def index(i):
        row = fixed_row if fixed_row is not None else i // tiles_per_row
        return (row * 6 + part, 0, 0)

    return index


def _inproj(x2d, mod, g1, w_all, w_kt, gate_b, conv_w, rows_per_batch, conv_dim, qk_all, v_all):
    n, d = x2d.shape
    tm = ROW_TILE
    tiles_per_batch = rows_per_batch // tm
    row = lambda i: (i, 0)
    mod_block = (1, 1, d)
    out_shapes = (
        jax.ShapeDtypeStruct((n, conv_dim), BF16),
        jax.ShapeDtypeStruct((n, qk_all), BF16),
        jax.ShapeDtypeStruct((qk_all, n), BF16),
        jax.ShapeDtypeStruct((n, v_all), BF16),
        jax.ShapeDtypeStruct((n, v_all), BF16),
        jax.ShapeDtypeStruct((n, GATE_COLS), F32),
        jax.ShapeDtypeStruct((GATE_COLS, n), F32),
    )
    out_specs = (
        pl.BlockSpec((tm, conv_dim), row),
        pl.BlockSpec((tm, qk_all), row),
        pl.BlockSpec((qk_all, tm), lambda i: (0, i)),
        pl.BlockSpec((tm, v_all), row),
        pl.BlockSpec((tm, v_all), row),
        pl.BlockSpec((tm, GATE_COLS), row),
        pl.BlockSpec((GATE_COLS, tm), lambda i: (0, i)),
    )
    return pl.pallas_call(
        functools.partial(_inproj_body, conv_dim=conv_dim, qk_all=qk_all, v_all=v_all),
        grid=(n // tm,),
        in_specs=[pl.BlockSpec((tm, d), row),
                  pl.BlockSpec(mod_block, _mod_spec(0, tiles_per_batch)),
                  pl.BlockSpec(mod_block, _mod_spec(1, tiles_per_batch)),
                  _resident(g1.shape), _resident(w_all.shape), _resident(w_kt.shape),
                  _resident(gate_b.shape), _resident(conv_w.shape)],
        out_specs=out_specs,
        out_shape=out_shapes,
        compiler_params=_params(1),
        name="inproj",
    )(x2d, mod, mod, g1, w_all, w_kt, gate_b, conv_w)


def _inproj_ctx(c2d, mod, g1, w_v, w_kt, w_gate, gate_b, ctx_mod_row):
    n, d = c2d.shape
    tm = ROW_TILE
    row = lambda i: (i, 0)
    mod_block = (1, 1, d)
    qk_all, v_all = w_kt.shape[0], w_v.shape[1]
    return pl.pallas_call(
        _inproj_ctx_body,
        grid=(n // tm,),
        in_specs=[pl.BlockSpec((tm, d), row),
                  pl.BlockSpec(mod_block, _mod_spec(0, 1, ctx_mod_row)),
                  pl.BlockSpec(mod_block, _mod_spec(1, 1, ctx_mod_row)),
                  _resident(g1.shape), _resident(w_v.shape), _resident(w_kt.shape),
                  _resident(w_gate.shape), _resident(gate_b.shape)],
        out_specs=(pl.BlockSpec((qk_all, tm), lambda i: (0, i)), pl.BlockSpec((tm, v_all), row),
                   pl.BlockSpec((tm, GATE_COLS), row), pl.BlockSpec((GATE_COLS, tm), lambda i: (0, i))),
        out_shape=(jax.ShapeDtypeStruct((qk_all, n), BF16), jax.ShapeDtypeStruct((n, v_all), BF16),
                   jax.ShapeDtypeStruct((n, GATE_COLS), F32), jax.ShapeDtypeStruct((GATE_COLS, n), F32)),
        compiler_params=_params(1),
        name="inproj_ctx",
    )(c2d, mod, mod, g1, w_v, w_kt, w_gate, gate_b)


def _with_ones(v):
    return jnp.concatenate([v, jnp.ones((v.shape[0], LANES), v.dtype)], axis=1)


def _mlstm_state_update(h, direction, kt_ref, v_ref, gt_ref, s_ref, m_ref, qk, vh):
    ci = direction * 2 * N_HEADS + h
    cb = ci + N_HEADS
    last = 0 if direction else CHUNK - 1
    kt = kt_ref[h * qk:(h + 1) * qk, :].astype(F32)
    va = _with_ones(v_ref[:, h * vh:(h + 1) * vh])
    b_last = gt_ref[cb:cb + 1, last:last + 1]
    m_prev = m_ref[h][0:1, 0:1]
    g_r = b_last - gt_ref[cb:cb + 1, :] + gt_ref[ci:ci + 1, :]
    cm = N_GATES + direction * N_HEADS + h
    m_new = b_last + jnp.maximum(m_prev, gt_ref[cm:cm + 1, last:last + 1])
    a = jnp.exp(b_last + m_prev - m_new)
    kw = (kt * jnp.exp(g_r - m_new)).astype(BF16)
    s_ref[h] = a * s_ref[h] + jnp.dot(kw, va, preferred_element_type=F32)
    m_ref[h] = jnp.broadcast_to(m_new, m_ref.shape[1:])


def _mlstm_head_output(h, direction, q_ref, kt_ref, v_ref, g_ref, gt_ref, s_ref, m_ref, qk, vh):
    ci = direction * 2 * N_HEADS + h
    cb = ci + N_HEADS
    q = q_ref[:, h * qk:(h + 1) * qk]
    kt = kt_ref[h * qk:(h + 1) * qk, :]
    va = _with_ones(v_ref[:, h * vh:(h + 1) * vh])
    ig_r = gt_ref[ci:ci + 1, :]
    b_r = gt_ref[cb:cb + 1, :]
    cm = N_GATES + direction * N_HEADS + h
    m_prev = m_ref[h][0:1, 0:1]
    b_rep = jnp.broadcast_to(g_ref[:, cb:cb + 1], (CHUNK, LANES))
    run_rep = jnp.broadcast_to(g_ref[:, cm:cm + 1], (CHUNK, LANES))
    row = lax.broadcasted_iota(I32, (CHUNK, CHUNK), 0)
    col = lax.broadcasted_iota(I32, (CHUNK, CHUNK), 1)
    mask = (col >= row) if direction else (col <= row)
    dm = jnp.where(mask, b_rep + (ig_r - b_r), _NEG_INF)
    inter = b_rep + m_prev
    m_t = jnp.maximum(inter, b_rep + run_rep)
    w_inter = jnp.exp(inter - m_t)
    s = jnp.dot(q, kt, preferred_element_type=F32) * jnp.exp(dm - m_t)
    intra = jnp.dot(s.astype(BF16), va, preferred_element_type=F32)
    carried = jnp.dot(q, s_ref[h].astype(BF16), preferred_element_type=F32)
    den = intra[:, vh:vh + LANES] + w_inter * carried[:, vh:vh + LANES]
    scale = 1.0 / jnp.maximum(jnp.abs(den), jnp.exp(-m_t))
    return jnp.concatenate(
        [(intra[:, j:j + LANES] + w_inter * carried[:, j:j + LANES]) * scale for j in range(0, vh, LANES)],
        axis=1)


def _mlstm_body(*refs, direction, bsz, n_ctx_chunks, qk, vh):
    q_ref, v_ref, g_ref, vc_ref = refs[0:4]
    kt_refs, gt_refs = refs[4:4 + bsz], refs[4 + bsz:4 + 2 * bsz]
    ktc_refs, gtc_refs = refs[4 + 2 * bsz:4 + 3 * bsz], refs[4 + 3 * bsz:4 + 4 * bsz]
    rest = refs[4 + 4 * bsz:]
    if direction:
        out_ref, s_ref, m_ref = rest
    else:
        hb_ref, og_ref, hg_ref, out_ref, s_ref, m_ref = rest
    step = pl.program_id(0)

    @pl.when(step == 0)
    def _():
        s_ref[...] = jnp.zeros_like(s_ref)
        m_ref[...] = jnp.full_like(m_ref, _NEG_INF)

    @pl.when(step < n_ctx_chunks)
    def _():
        for b in range(bsz):
            for h in range(N_HEADS):
                _mlstm_state_update(h, direction, ktc_refs[b], vc_ref.at[b], gtc_refs[b],
                                    s_ref.at[b], m_ref.at[b], qk, vh)

    @pl.when(step >= n_ctx_chunks)
    def _():
        for b in range(bsz):
            for h in range(N_HEADS):
                hh = _mlstm_head_output(h, direction, q_ref.at[b], kt_refs[b], v_ref.at[b], g_ref.at[b],
                                        gt_refs[b], s_ref.at[b], m_ref.at[b], qk, vh)
                cols = slice(h * vh, (h + 1) * vh)
                if direction:
                    out_ref[b, :, cols] = hh
                else:
                    hs = hh + hb_ref[b, :, cols]
                    hs = hs * lax.rsqrt(jnp.mean(hs * hs, axis=-1, keepdims=True) + EPS)
                    out_ref[b, :, cols] = (hs * hg_ref[:, cols]
                                           * og_ref[b, :, cols].astype(F32)).astype(BF16)
                _mlstm_state_update(h, direction, kt_refs[b], v_ref.at[b], gt_refs[b],
                                    s_ref.at[b], m_ref.at[b], qk, vh)


def _mlstm(direction, q, kt, v, g, gt, ktc, vc, gtc, extra, bsz, head_g=None):
    n, qk_all = q.shape
    v_all = v.shape[1]
    qk, vh = qk_all // N_HEADS, v_all // N_HEADS
    seq = n // bsz
    nc = seq // CHUNK
    ncc = vc.shape[0] // bsz // CHUNK

    def lat(s):
        j = jnp.clip(s - ncc, 0, nc - 1)
        return nc - 1 - j if direction else j

    def ctx(s):
        j = jnp.clip(s, 0, ncc - 1)
        return ncc - 1 - j if direction else j

    def per_batch(a):
        return a.reshape(bsz, a.shape[0] // bsz, a.shape[1])

    lat_blk = lambda c: pl.BlockSpec((bsz, CHUNK, c), lambda s: (0, lat(s), 0))
    in_specs = [lat_blk(qk_all), lat_blk(v_all), lat_blk(GATE_COLS),
                pl.BlockSpec((bsz, CHUNK, v_all), lambda s: (0, ctx(s), 0))]
    args = [per_batch(q), per_batch(v), per_batch(g), per_batch(vc)]
    for arr, rows, n_chunks, pos in ((kt, qk_all, nc, lat), (gt, GATE_COLS, nc, lat),
                                     (ktc, qk_all, ncc, ctx), (gtc, GATE_COLS, ncc, ctx)):
        for b in range(bsz):
            in_specs.append(pl.BlockSpec((rows, CHUNK), lambda s, b=b, n_chunks=n_chunks, pos=pos:
                                         (0, b * n_chunks + pos(s))))
            args.append(arr)
    if direction:
        out_dtype = F32
    else:
        hb, og = extra
        in_specs += [lat_blk(v_all), lat_blk(v_all), pl.BlockSpec((1, v_all), lambda s: (0, 0))]
        args += [per_batch(hb), per_batch(og), head_g]
        out_dtype = BF16
    out = pl.pallas_call(
        functools.partial(_mlstm_body, direction=direction, bsz=bsz, n_ctx_chunks=ncc, qk=qk, vh=vh),
        grid=(ncc + nc,),
        in_specs=in_specs,
        out_specs=lat_blk(v_all),
        out_shape=jax.ShapeDtypeStruct((bsz, seq, v_all), out_dtype),
        scratch_shapes=[pltpu.VMEM((bsz, N_HEADS, qk, vh + LANES), F32),
                        pltpu.VMEM((bsz, N_HEADS, SUBLANES, LANES), F32)],
        compiler_params=_params(1),
        name="mlstm_bwd" if direction else "mlstm_fwd",
    )(*args)
    return out.reshape(n, v_all)


def _outproj_body(conv_ref, ml_ref, x_ref, gt1_ref, sh2_ref, sc2_ref, g2_ref, wo_ref, wr_ref, br_ref,
                  x1_ref, idx_ref, gate_ref, rank_ref, cnt_ref, h_hbm, carry_ref, hw, hsem):
    tm = x_ref.shape[0]
    half = conv_ref.shape[1]
    step = pl.program_id(0)
    buf = step % 2

    def h_out(i, s):
        return _row_tile_copies(h_hbm, i * tm, hw.at[s], hsem.at[s], to_hbm=True)

    @pl.when(step == 0)
    def _():
        carry_ref[...] = jnp.zeros_like(carry_ref)

    @pl.when(step >= 2)
    def _():
        _wait_all(h_out(step - 2, buf))

    y = (jnp.dot(conv_ref[...], wo_ref[0:half, :], preferred_element_type=F32)
         + jnp.dot(ml_ref[...], wo_ref[half:2 * half, :], preferred_element_type=F32))
    x1 = x_ref[...] + gt1_ref[0] * y
    x1_ref[...] = x1
    hn = _norm_mod(x1, g2_ref[...], sh2_ref[0], sc2_ref[0])
    hw[buf] = _pack_words(hn)
    _start_all(h_out(step, buf))

    h_hi = hn.astype(BF16)
    h_lo = (hn - h_hi.astype(F32)).astype(BF16)
    parts = (jnp.dot(h_hi, wr_ref[...], preferred_element_type=F32)
             + jnp.dot(h_lo, wr_ref[...], preferred_element_type=F32))
    scores = jax.nn.sigmoid(parts + pltpu.roll(parts, N_EXPERTS, axis=1))
    lane = lax.broadcasted_iota(I32, (tm, LANES), 1).astype(F32)
    biased = jnp.where(lane < N_EXPERTS, scores + br_ref[...], _NEG_INF)
    onehot = jnp.zeros((tm, LANES), F32)
    picks, sels = [], []
    for _ in range(TOP_K):
        mx = jnp.max(biased, axis=1, keepdims=True)
        pick = jnp.min(jnp.where(biased == mx, lane, float(LANES)), axis=1, keepdims=True)
        hit = lane == pick
        sels.append(jnp.sum(jnp.where(hit, scores, 0.0), axis=1, keepdims=True))
        picks.append(pick)
        biased = jnp.where(hit, _NEG_INF, biased)
        onehot = onehot + hit.astype(F32)
    total = sels[0]
    for s in sels[1:]:
        total = total + s

    r = lax.broadcasted_iota(I32, (tm, tm), 0)
    c = lax.broadcasted_iota(I32, (tm, tm), 1)
    strict = jnp.where(c < r, 1.0, 0.0).astype(BF16)
    before = jnp.dot(strict, onehot.astype(BF16), preferred_element_type=F32) + carry_ref[...]
    slot = lax.broadcasted_iota(I32, (tm, SUBLANES), 1)
    idx_out = jnp.zeros((tm, LANES), F32)
    rank_out = jnp.zeros((tm, LANES), F32)
    gate_out = jnp.zeros((tm, SUBLANES), F32)
    for j in range(TOP_K):
        rank = jnp.sum(jnp.where(lane == picks[j], before, 0.0), axis=1, keepdims=True)
        idx_out = jnp.where(lane == float(j), picks[j], idx_out)
        rank_out = jnp.where(lane == float(j), rank, rank_out)
        gate_out = jnp.where(slot == j, sels[j] / total * ROUTED_SCALE, gate_out)
    idx_ref[...] = idx_out.T[:SUBLANES, :].astype(I32)
    rank_ref[...] = rank_out.T[:SUBLANES, :].astype(I32)
    gate_ref[...] = gate_out
    carry_ref[...] = carry_ref[...] + jnp.sum(onehot, axis=0, keepdims=True)
    cnt_ref[...] = jnp.broadcast_to(carry_ref[...], cnt_ref.shape).astype(I32)

    @pl.when(step == pl.num_programs(0) - 1)
    def _():
        @pl.when(step >= 1)
        def _():
            _wait_all(h_out(step - 1, 1 - buf))
        _wait_all(h_out(step, buf))


def _outproj(conv, ml, x2d, mod, g2, w_out, w_router, b_router, rows_per_batch):
    n, d = x2d.shape
    tm = ROW_TILE
    tiles_per_batch = rows_per_batch // tm
    row = lambda i: (i, 0)
    mod_block = (1, 1, d)
    half = conv.shape[1]
    return pl.pallas_call(
        _outproj_body,
        grid=(n // tm,),
        in_specs=[pl.BlockSpec((tm, half), row), pl.BlockSpec((tm, half), row), pl.BlockSpec((tm, d), row),
                  pl.BlockSpec(mod_block, _mod_spec(2, tiles_per_batch)),
                  pl.BlockSpec(mod_block, _mod_spec(3, tiles_per_batch)),
                  pl.BlockSpec(mod_block, _mod_spec(4, tiles_per_batch)),
                  _resident(g2.shape), _resident(w_out.shape), _resident(w_router.shape),
                  _resident(b_router.shape)],
        out_specs=(pl.BlockSpec((tm, d), row),
                   pl.BlockSpec((SUBLANES, tm), lambda i: (0, i)), pl.BlockSpec((tm, SUBLANES), row),
                   pl.BlockSpec((SUBLANES, tm), lambda i: (0, i)),
                   pl.BlockSpec((SUBLANES, LANES), lambda i: (0, 0)),
                   pl.BlockSpec(memory_space=pl.ANY)),
        out_shape=(jax.ShapeDtypeStruct((n, d), F32),
                   jax.ShapeDtypeStruct((SUBLANES, n), I32), jax.ShapeDtypeStruct((n, SUBLANES), F32),
                   jax.ShapeDtypeStruct((SUBLANES, n), I32),
                   jax.ShapeDtypeStruct((SUBLANES, LANES), I32),
                   jax.ShapeDtypeStruct((n, SUBLANES, LANES), I32)),
        scratch_shapes=[pltpu.VMEM((1, LANES), F32), pltpu.VMEM((2, tm, d // 2), I32),
                        pltpu.SemaphoreType.DMA((2,))],
        compiler_params=_params(1),
        name="outproj_router",
    )(conv, ml, x2d, mod, mod, mod, g2, w_out, w_router, b_router)


def _sc_workers():
    info = plsc.get_sparse_core_info()
    return info.num_cores, info.num_cores * info.num_subcores


def _sc_dispatch(h_rows, dest_chunks, n_slots):
    n_tok = h_rows.shape[0]
    n_cores, n_workers = _sc_workers()
    per_worker = n_tok // (n_workers * SC_CHUNK)
    assert per_worker * n_workers * SC_CHUNK == n_tok
    mesh = plsc.VectorSubcoreMesh(core_axis_name="c", subcore_axis_name="s")

    @functools.partial(
        pl.kernel, mesh=mesh,
        out_type=jax.ShapeDtypeStruct((n_slots,) + h_rows.shape[1:], h_rows.dtype),
        scratch_types=[pltpu.VMEM((TOP_K, SC_CHUNK), I32),
                       pltpu.VMEM((SC_CHUNK,) + h_rows.shape[1:], h_rows.dtype)],
    )
    def dispatch(h_hbm, dest_hbm, out_hbm, idx_v, rows_v):
        wid = lax.axis_index("s") * n_cores + lax.axis_index("c")

        @pl.loop(0, per_worker)
        def _(i):
            chunk = wid * per_worker + i
            pltpu.sync_copy(dest_hbm.at[chunk], idx_v)
            pltpu.sync_copy(h_hbm.at[pl.ds(chunk * SC_CHUNK, SC_CHUNK)], rows_v)
            for k in range(TOP_K):
                pltpu.sync_copy(rows_v, out_hbm.at[idx_v.at[k]])

    return dispatch(h_rows, dest_chunks)


def _sc_combine(y_sorted, dest_chunks, n_tok):
    n_cores, n_workers = _sc_workers()
    per_worker = n_tok // (n_workers * SC_CHUNK)
    mesh = plsc.VectorSubcoreMesh(core_axis_name="c", subcore_axis_name="s")

    @functools.partial(
        pl.kernel, mesh=mesh,
        out_type=jax.ShapeDtypeStruct((TOP_K, n_tok) + y_sorted.shape[1:], y_sorted.dtype),
        scratch_types=[pltpu.VMEM((TOP_K, SC_CHUNK), I32),
                       pltpu.VMEM((SC_CHUNK,) + y_sorted.shape[1:], y_sorted.dtype)],
    )
    def combine(y_hbm, dest_hbm, out_hbm, idx_v, rows_v):
        wid = lax.axis_index("s") * n_cores + lax.axis_index("c")

        @pl.loop(0, per_worker)
        def _(i):
            chunk = wid * per_worker + i
            pltpu.sync_copy(dest_hbm.at[chunk], idx_v)
            for k in range(TOP_K):
                pltpu.sync_copy(y_hbm.at[idx_v.at[k]], rows_v)
                pltpu.sync_copy(rows_v, out_hbm.at[k, pl.ds(chunk * SC_CHUNK, SC_CHUNK)])

    return combine(y_sorted, dest_chunks)


def _moe_body(ord_ref, order_ref, glo_ref, ghi_ref, tot_ref, nb_ref,
              x_hbm, wg_hbm, wu_hbm, wd_hbm, y_hbm,
              wgu, wd, stage_a, stage_d, xw, yw, wsem, xsem, ysem, *, d_expert):
    b = pl.program_id(0)
    nb = nb_ref[0]
    total = tot_ref[0]
    d_model = wgu.shape[1]
    rows_a = d_model // WEIGHT_PARTS
    rows_d = d_expert // WEIGHT_PARTS

    def part_copies(g):
        e = order_ref[lax.shift_right_logical(g, PART_SHIFT)]
        i = g & (WEIGHT_PARTS - 1)
        s = lax.rem(g, WEIGHT_RING)
        return (pltpu.make_async_copy(wg_hbm.at[e, pl.ds(i * rows_a, rows_a)], stage_a.at[s, 0],
                                      wsem.at[s, 0]),
                pltpu.make_async_copy(wu_hbm.at[e, pl.ds(i * rows_a, rows_a)], stage_a.at[s, 1],
                                      wsem.at[s, 1]),
                pltpu.make_async_copy(wd_hbm.at[e, pl.ds(i * rows_d, rows_d)], stage_d.at[s],
                                      wsem.at[s, 2]))

    def start_part(g):
        for cp in part_copies(g):
            cp.start()

    def wait_part(g):
        for cp in part_copies(g):
            cp.wait()

    def cast_part(g):
        i = g & (WEIGHT_PARTS - 1)
        s = lax.rem(g, WEIGHT_RING)
        par = lax.shift_right_logical(g, PART_SHIFT) & 1
        ra = pl.multiple_of(i * rows_a, rows_a)
        rd = pl.multiple_of(i * rows_d, rows_d)
        wgu[par, pl.ds(ra, rows_a), 0:d_expert] = stage_a[s, 0].astype(BF16)
        wgu[par, pl.ds(ra, rows_a), d_expert:2 * d_expert] = stage_a[s, 1].astype(BF16)
        wd[par, pl.ds(rd, rows_d), :] = stage_d[s].astype(BF16)

    def refill(g):
        @pl.when(g + WEIGHT_RING < total)
        def _():
            start_part(g + WEIGHT_RING)

    def cast_parts(lo, hi):
        def body(g, carry):
            wait_part(g)
            cast_part(g)
            refill(g)
            return carry
        lax.fori_loop(lo, hi, body, 0)

    slot = b % 2

    def x_in(blk, s):
        return _row_tile_copies(x_hbm, blk * MOE_BLOCK, xw.at[s], xsem.at[s], to_hbm=False)

    def y_out(blk, s):
        return _row_tile_copies(y_hbm, blk * MOE_BLOCK, yw.at[s], ysem.at[s], to_hbm=True)

    @pl.when(b == 0)
    def _():
        _start_all(x_in(0, 0))
        for g in range(WEIGHT_RING):
            start_part(g)
        cast_parts(0, WEIGHT_PARTS)

    @pl.when(b + 1 < nb)
    def _():
        _start_all(x_in(b + 1, 1 - slot))

    @pl.when(b < nb)
    def _():
        par = ord_ref[b] & 1
        _wait_all(x_in(b, slot))

        @pl.when(b >= 2)
        def _():
            _wait_all(y_out(b - 2, slot))

        x = jnp.concatenate(_unpack_words(xw[slot]), axis=1).astype(BF16)
        gu = jnp.dot(x, wgu[par], preferred_element_type=F32)
        hb = (_silu(gu[:, 0:d_expert]) * gu[:, d_expert:2 * d_expert]).astype(BF16)
        yw[slot] = _pack_words(jnp.dot(hb, wd[par], preferred_element_type=F32))
        _start_all(y_out(b, slot))
        cast_parts(glo_ref[b], ghi_ref[b])

        @pl.when(b == nb - 1)
        def _():
            @pl.when(b >= 1)
            def _():
                _wait_all(y_out(b - 1, 1 - slot))
            _wait_all(y_out(b, slot))


def _moe(x_sorted, we_gate, we_up, we_down, tables):
    d, d_expert = we_gate.shape[1], we_gate.shape[2]
    nb_max = x_sorted.shape[0] // MOE_BLOCK
    any_spec = pl.BlockSpec(memory_space=pl.ANY)
    grid_spec = pltpu.PrefetchScalarGridSpec(
        num_scalar_prefetch=len(tables),
        grid=(nb_max,),
        in_specs=[any_spec, any_spec, any_spec, any_spec],
        out_specs=any_spec,
        scratch_shapes=[pltpu.VMEM((2, d, 2 * d_expert), BF16),
                        pltpu.VMEM((2, d_expert, d), BF16),
                        pltpu.VMEM((WEIGHT_RING, 2, d // WEIGHT_PARTS, d_expert), F32),
                        pltpu.VMEM((WEIGHT_RING, d_expert // WEIGHT_PARTS, d), F32),
                        pltpu.VMEM((2, MOE_BLOCK, d // 2), I32),
                        pltpu.VMEM((2, MOE_BLOCK, d // 2), I32),
                        pltpu.SemaphoreType.DMA((WEIGHT_RING, 3)),
                        pltpu.SemaphoreType.DMA((2,)),
                        pltpu.SemaphoreType.DMA((2,))],
    )
    return pl.pallas_call(
        functools.partial(_moe_body, d_expert=d_expert),
        grid_spec=grid_spec,
        out_shape=jax.ShapeDtypeStruct(x_sorted.shape, x_sorted.dtype),
        compiler_params=pltpu.CompilerParams(
            dimension_semantics=("arbitrary",), vmem_limit_bytes=MOE_VMEM_LIMIT),
        name="moe_routed",
    )(*tables, x_sorted, we_gate, we_up, we_down)


def _final_body(x1_ref, gate_ref, gt2_ref, wsgu_ref, wsd_ref, fg_ref, h_hbm, y_hbm, out_ref,
                hw, yw, sem, *, d_shared):
    tm = x1_ref.shape[0]
    step = pl.program_id(0)
    slot = step % 2

    def rows_in(i, s):
        copies = _row_tile_copies(h_hbm, i * tm, hw.at[s], sem.at[s], to_hbm=False)
        for k in range(TOP_K):
            copies += _row_tile_copies(y_hbm.at[k], i * tm, yw.at[s, k], sem.at[s], to_hbm=False)
        return copies

    @pl.when(step == 0)
    def _():
        _start_all(rows_in(0, 0))

    @pl.when(step + 1 < pl.num_programs(0))
    def _():
        _start_all(rows_in(step + 1, 1 - slot))

    _wait_all(rows_in(step, slot))
    routed = gate_ref[:, 0:1] * jnp.concatenate(_unpack_words(yw[slot, 0]), axis=1)
    for k in range(1, TOP_K):
        routed = routed + gate_ref[:, k:k + 1] * jnp.concatenate(_unpack_words(yw[slot, k]), axis=1)
    h = jnp.concatenate(_unpack_words(hw[slot]), axis=1).astype(BF16)
    gu = jnp.dot(h, wsgu_ref[...], preferred_element_type=F32)
    hb = (_silu(gu[:, 0:d_shared]) * gu[:, d_shared:2 * d_shared]).astype(BF16)
    x2 = x1_ref[...] + gt2_ref[0] * (routed + jnp.dot(hb, wsd_ref[...], preferred_element_type=F32))
    out_ref[...] = x2 * lax.rsqrt(jnp.mean(x2 * x2, axis=-1, keepdims=True) + EPS) * fg_ref[...]


def _final(h_rows, x1, y_tok, gates, mod, ws_gu, ws_d, final_g, rows_per_batch):
    n, d = x1.shape
    tm = ROW_TILE
    tiles_per_batch = rows_per_batch // tm
    row = lambda i: (i, 0)
    any_spec = pl.BlockSpec(memory_space=pl.ANY)
    return pl.pallas_call(
        functools.partial(_final_body, d_shared=ws_d.shape[0]),
        grid=(n // tm,),
        in_specs=[pl.BlockSpec((tm, d), row), pl.BlockSpec((tm, SUBLANES), row),
                  pl.BlockSpec((1, 1, d), _mod_spec(5, tiles_per_batch)),
                  _resident(ws_gu.shape), _resident(ws_d.shape), _resident(final_g.shape),
                  any_spec, any_spec],
        out_specs=pl.BlockSpec((tm, d), row),
        out_shape=jax.ShapeDtypeStruct((n, d), F32),
        scratch_shapes=[pltpu.VMEM((2, tm, d // 2), I32), pltpu.VMEM((2, TOP_K, tm, d // 2), I32),
                        pltpu.SemaphoreType.DMA((2,))],
        compiler_params=_params(1),
        name="shared_combine_final",
    )(x1, gates, mod, ws_gu, ws_d, final_g, h_rows, y_tok)


def _routing_tables(idx, rank, counts, n_tok):
    nb_max = -(-(n_tok * TOP_K) // MOE_BLOCK) + N_EXPERTS
    nblk = (counts + MOE_BLOCK - 1) // MOE_BLOCK
    blk_end = jnp.cumsum(nblk)
    blk_start = blk_end - nblk
    experts = jnp.arange(N_EXPERTS, dtype=I32)[:, None, None]
    first_slot = (blk_start * MOE_BLOCK)[:, None, None]
    dest = jnp.sum(jnp.where(idx[None] == experts, first_slot, 0), axis=0) + rank
    dest_chunks = dest.reshape(TOP_K, n_tok // SC_CHUNK, SC_CHUNK).transpose(1, 0, 2)

    blocks = jnp.arange(nb_max, dtype=I32)[:, None]
    member = (blk_start[None, :] <= blocks) & (blocks < blk_end[None, :])
    lookup = lambda table: jnp.sum(jnp.where(member, table[None, :], 0), axis=1)
    blocks = blocks[:, 0]
    nonempty = nblk > 0
    n_visited = jnp.sum(nonempty.astype(I32))
    ordinal_of = jnp.cumsum(nonempty.astype(I32)) - 1
    slots = jnp.arange(N_EXPERTS, dtype=I32)
    order = jnp.sum(jnp.where(nonempty[None, :] & (ordinal_of[None, :] == slots[:, None]),
                              slots[None, :], 0), axis=1)
    ordinal = lookup(ordinal_of)
    k_in_e = blocks - lookup(blk_start)
    nb_e = jnp.maximum(lookup(nblk), 1)
    live = (ordinal + 1 < n_visited) & (blocks < blk_end[-1])
    first = WEIGHT_PARTS * (ordinal + 1)
    lo = jnp.where(live, first + WEIGHT_PARTS * k_in_e // nb_e, 0)
    hi = jnp.where(live, first + WEIGHT_PARTS * (k_in_e + 1) // nb_e, 0)
    tables = (ordinal.astype(I32), order, lo.astype(I32), hi.astype(I32),
              (WEIGHT_PARTS * n_visited).reshape(1).astype(I32), blk_end[-1:].astype(I32))
    return tables, dest_chunks, nb_max * MOE_BLOCK


def kernel(x, c, ctx, c_ctx, norm1_g, norm2_g, w_ada, b_ada, w_in, conv_w, gate_b, head_g, w_out,
           w_router, b_router, we_gate, we_up, we_down, ws_gate, ws_up, ws_down, final_g):
    assert w_ada.shape[0] == 1, "single-layer block"
    bsz, seq, d = x.shape
    ctx_len = ctx.shape[1]
    n_tok = bsz * seq
    conv_dim = conv_w.shape[2]
    v_all = head_g.shape[1]
    qk_all = (w_in.shape[2] - 3 * conv_dim - 2 * v_all - N_GATES) // 2
    assert seq % ROW_TILE == 0 and ctx_len % ROW_TILE == 0 and ROW_TILE % GRID_W == 0
    assert bsz + 1 <= SUBLANES

    cc = jnp.zeros((SUBLANES, d), F32).at[:bsz].set(c).at[bsz].set(c_ctx)
    mod = _adaln(cc, w_ada[0], b_ada).reshape(SUBLANES * 6, 1, d)

    n_main = 3 * conv_dim + 2 * qk_all + 2 * v_all
    w_all = jnp.pad(w_in[0].astype(BF16), ((0, 0), (0, LANES - N_GATES)))
    k_lo = 3 * conv_dim + qk_all
    w_kt = w_all[:, k_lo:k_lo + qk_all].T
    w_v = w_all[:, k_lo + qk_all:k_lo + qk_all + v_all]
    w_gate = w_all[:, n_main:n_main + LANES]
    gate_bias = jnp.zeros((1, LANES), F32).at[0, :N_GATES].set(gate_b[0].reshape(-1))

    x2d = x.reshape(n_tok, d)
    conv, q, kt, v, og, g, gt = _inproj(x2d, mod, norm1_g, w_all, w_kt, gate_bias, conv_w[0],
                                       seq, conv_dim, qk_all, v_all)
    ktc, vc, _, gtc = _inproj_ctx(ctx.reshape(bsz * ctx_len, d), mod, norm1_g, w_v, w_kt, w_gate,
                                  gate_bias, bsz)

    h_bwd = _mlstm(1, q, kt, v, g, gt, ktc, vc, gtc, None, bsz)
    ml = _mlstm(0, q, kt, v, g, gt, ktc, vc, gtc, (h_bwd, og), bsz, head_g)

    assert 2 * N_EXPERTS == LANES
    w_r_hi = w_router[0].astype(BF16)
    w_r = jnp.concatenate([w_r_hi, (w_router[0] - w_r_hi.astype(F32)).astype(BF16)], axis=1)
    b_r = jnp.zeros((1, LANES), F32).at[0, :N_EXPERTS].set(b_router[0])
    x1, idx, gates, rank, cnt, h_rows = _outproj(conv, ml, x2d, mod, norm2_g, w_out[0].astype(BF16),
                                             w_r, b_r, seq)

    tables, dest_chunks, n_slots = _routing_tables(idx[:TOP_K], rank[:TOP_K], cnt[0, :N_EXPERTS], n_tok)
    x_sorted = _sc_dispatch(h_rows, dest_chunks, n_slots)
    y_sorted = _moe(x_sorted, we_gate[0], we_up[0], we_down[0], tables)
    y_tok = _sc_combine(y_sorted, dest_chunks, n_tok)

    ws_gu = jnp.concatenate([ws_gate[0], ws_up[0]], axis=1).astype(BF16)
    out = _final(h_rows, x1, y_tok, gates, mod, ws_gu, ws_down[0].astype(BF16),
                 final_g.reshape(1, d), seq)
    return out.reshape(bsz, seq, d)
```

```python
import functools

import jax
import jax.numpy as jnp
from jax import lax
from jax.experimental import pallas as pl
from jax.experimental.pallas import tpu as pltpu
from jax.experimental.pallas import tpu_sc as plsc

F32 = jnp.float32
BF16 = jnp.bfloat16
I32 = jnp.int32

N_HEADS = 4
GRID_W = 64
CHUNK = 128
TOP_K = 6
N_EXPERTS = 64
ROUTED_SCALE = 2.446
EPS = 1e-6
N_GATES = 4 * N_HEADS
GATE_COLS = 6 * N_HEADS

LANES = 128
SUBLANES = 8
MOE_BLOCK = 256
ROW_TILE = 256
ADALN_TILE = 1024
WEIGHT_PARTS = 8
PART_SHIFT = 3
WEIGHT_RING = 3
SC_CHUNK = 64
HIGH_HALF = -65536
VMEM_LIMIT = 56 * 1024 * 1024
MOE_VMEM_LIMIT = 62 * 1024 * 1024

_HIGHEST = lax.Precision.HIGHEST
_NEG_INF = float("-inf")
assert CHUNK == LANES


def _resident(shape):
    nd = len(shape)
    return pl.BlockSpec(shape, lambda *_: (0,) * nd, pipeline_mode=pl.Buffered(1))


def _params(n_axes):
    return pltpu.CompilerParams(
        dimension_semantics=("arbitrary",) * n_axes, vmem_limit_bytes=VMEM_LIMIT)


def _log_sigmoid(x):
    return jnp.minimum(x, 0.0) - jnp.log1p(jnp.exp(-jnp.abs(x)))


def _silu(x):
    return x * jax.nn.sigmoid(x)


def _pack_words(val):
    half = val.shape[1] // 2
    lo = lax.bitcast_convert_type(val[:, :half].astype(BF16).astype(F32), I32)
    hi = lax.bitcast_convert_type(val[:, half:].astype(BF16).astype(F32), I32)
    return (hi & HIGH_HALF) | lax.shift_right_logical(lo, 16)


def _unpack_words(word):
    lo = lax.bitcast_convert_type(lax.shift_left(word, 16), F32)
    hi = lax.bitcast_convert_type(word & HIGH_HALF, F32)
    return lo, hi


def _row_tile_copies(hbm_rows, row0, tile, sem, to_hbm):
    n = tile.shape[0]
    copies = []
    for c in range(SUBLANES):
        hbm = hbm_rows.at[pl.ds(row0, n), c, :]
        vmem = tile.at[:, pl.ds(c * LANES, LANES)]
        copies.append(pltpu.make_async_copy(vmem, hbm, sem) if to_hbm
                      else pltpu.make_async_copy(hbm, vmem, sem))
    return copies


def _start_all(copies):
    for cp in copies:
        cp.start()


def _wait_all(copies):
    for cp in copies:
        cp.wait()


def _adaln_body(c_ref, w_ref, b_ref, o_ref):
    s = _silu(c_ref[...])
    o_ref[...] = jnp.dot(s.astype(BF16), w_ref[...].astype(BF16),
                         preferred_element_type=F32) + b_ref[...]


def _adaln(cc, w, b):
    d, n6 = w.shape
    return pl.pallas_call(
        _adaln_body,
        grid=(n6 // ADALN_TILE,),
        in_specs=[pl.BlockSpec((SUBLANES, d), lambda j: (0, 0)),
                  pl.BlockSpec((d, ADALN_TILE), lambda j: (0, j)),
                  pl.BlockSpec((1, ADALN_TILE), lambda j: (0, j))],
        out_specs=pl.BlockSpec((SUBLANES, ADALN_TILE), lambda j: (0, j)),
        out_shape=jax.ShapeDtypeStruct((SUBLANES, n6), F32),
        compiler_params=_params(1),
        name="adaln",
    )(cc, w, b)


def _norm_mod(x, g, shift, scale):
    y = x * lax.rsqrt(jnp.mean(x * x, axis=-1, keepdims=True) + EPS) * g
    return y * (1.0 + scale) + shift


def _gate_prep(xb, wg, gb_ref, g_ref, gt_ref):
    tm = xb.shape[0]
    gg = jnp.dot(xb, wg, preferred_element_type=F32) + gb_ref[...]
    lane = lax.broadcasted_iota(I32, (tm, LANES), 1)
    is_f = (lane & N_HEADS) != 0
    is_bwd = (lane & (2 * N_HEADS)) != 0
    lf = jnp.where(is_f, _log_sigmoid(gg), 0.0)
    r = lax.broadcasted_iota(I32, (tm, tm), 0)
    c = lax.broadcasted_iota(I32, (tm, tm), 1)
    same = (r // CHUNK) == (c // CHUNK)
    tri_l = jnp.where(same & (c <= r), 1.0, 0.0).astype(F32)
    tri_u = jnp.where(same & (c >= r), 1.0, 0.0).astype(F32)
    pre = jnp.dot(tri_l, lf, precision=_HIGHEST, preferred_element_type=F32)
    suf = jnp.dot(tri_u, lf, precision=_HIGHEST, preferred_element_type=F32)
    out = jnp.where(is_f, jnp.where(is_bwd, suf, pre), gg)
    diff = out - pltpu.roll(out, LANES - N_HEADS, axis=1)
    pos = lax.broadcasted_iota(I32, (tm, 1), 0) % CHUNK
    run_f, run_b = diff, diff
    k = 1
    while k < CHUNK:
        run_f = jnp.maximum(run_f, jnp.where(pos >= k, pltpu.roll(run_f, k, axis=0), _NEG_INF))
        run_b = jnp.maximum(run_b, jnp.where(pos < CHUNK - k, pltpu.roll(run_b, tm - k, axis=0), _NEG_INF))
        k *= 2
    run = jnp.where(is_bwd, run_b, run_f)
    fwd_lanes = (lane >= N_GATES) & (lane < N_GATES + N_HEADS)
    bwd_lanes = (lane >= N_GATES + N_HEADS) & (lane < GATE_COLS)
    out = jnp.where(fwd_lanes, pltpu.roll(run, N_GATES, axis=1),
                    jnp.where(bwd_lanes, pltpu.roll(run, N_GATES - N_HEADS, axis=1), out))
    g_ref[...] = out[:, :GATE_COLS]
    gt_ref[...] = out.T[:GATE_COLS, :]


def _project_transposed(wt_ref, xb):
    return lax.dot_general(wt_ref[...], xb, (((1,), (1,)), ((), ())),
                           preferred_element_type=F32).astype(BF16)


def _inproj_body(x_ref, sh_ref, sc_ref, g1_ref, w_ref, wkt_ref, gb_ref, cw_ref,
                 conv_ref, q_ref, k_ref, v_ref, o_ref, g_ref, gt_ref, *, conv_dim, qk_all, v_all):
    tm = x_ref.shape[0]
    xb = _norm_mod(x_ref[...], g1_ref[...], sh_ref[0], sc_ref[0]).astype(BF16)

    def proj(lo, width):
        return jnp.dot(xb, w_ref[:, lo:lo + width], preferred_element_type=F32)

    u = proj(conv_dim, conv_dim) * proj(2 * conv_dim, conv_dim)
    pos = lax.broadcasted_iota(I32, (tm, 1), 0) % GRID_W
    um = jnp.where(pos == 0, 0.0, pltpu.roll(u, 1, axis=0))
    up = jnp.where(pos == GRID_W - 1, 0.0, pltpu.roll(u, tm - 1, axis=0))
    y = um * cw_ref[0:1, :] + u * cw_ref[1:2, :] + up * cw_ref[2:3, :]
    conv_ref[...] = (proj(0, conv_dim) * y).astype(BF16)

    off = 3 * conv_dim
    qscale = (qk_all // N_HEADS) ** -0.5
    q_ref[...] = (proj(off, qk_all) * qscale).astype(BF16)
    k_ref[...] = _project_transposed(wkt_ref, xb)
    v_ref[...] = proj(off + 2 * qk_all, v_all).astype(BF16)
    o_ref[...] = jax.nn.sigmoid(proj(off + 2 * qk_all + v_all, v_all)).astype(BF16)
    gate_lo = off + 2 * qk_all + 2 * v_all
    _gate_prep(xb, w_ref[:, gate_lo:gate_lo + LANES], gb_ref, g_ref, gt_ref)


def _inproj_ctx_body(x_ref, sh_ref, sc_ref, g1_ref, w_ref, wkt_ref, wg_ref, gb_ref,
                     k_ref, v_ref, g_ref, gt_ref):
    xb = _norm_mod(x_ref[...], g1_ref[...], sh_ref[0], sc_ref[0]).astype(BF16)
    k_ref[...] = _project_transposed(wkt_ref, xb)
    v_ref[...] = jnp.dot(xb, w_ref[...], preferred_element_type=F32).astype(BF16)
    _gate_prep(xb, wg_ref[...], gb_ref, g_ref, gt_ref)


def _mod_spec(part, tiles_per_row, fixed_row=None):
    def index(i):
        row = fixed_row if fixed_row is not None else i // tiles_per_row
        return (row * 6 + part, 0, 0)

    return index


def _inproj(x2d, mod, g1, w_all, w_kt, gate_b, conv_w, rows_per_batch, conv_dim, qk_all, v_all):
    n, d = x2d.shape
    tm = ROW_TILE
    tiles_per_batch = rows_per_batch // tm
    row = lambda i: (i, 0)
    mod_block = (1, 1, d)
    out_shapes = (
        jax.ShapeDtypeStruct((n, conv_dim), BF16),
        jax.ShapeDtypeStruct((n, qk_all), BF16),
        jax.ShapeDtypeStruct((qk_all, n), BF16),
        jax.ShapeDtypeStruct((n, v_all), BF16),
        jax.ShapeDtypeStruct((n, v_all), BF16),
        jax.ShapeDtypeStruct((n, GATE_COLS), F32),
        jax.ShapeDtypeStruct((GATE_COLS, n), F32),
    )
    out_specs = (
        pl.BlockSpec((tm, conv_dim), row),
        pl.BlockSpec((tm, qk_all), row),
        pl.BlockSpec((qk_all, tm), lambda i: (0, i)),
        pl.BlockSpec((tm, v_all), row),
        pl.BlockSpec((tm, v_all), row),
        pl.BlockSpec((tm, GATE_COLS), row),
        pl.BlockSpec((GATE_COLS, tm), lambda i: (0, i)),
    )
    return pl.pallas_call(
        functools.partial(_inproj_body, conv_dim=conv_dim, qk_all=qk_all, v_all=v_all),
        grid=(n // tm,),
        in_specs=[pl.BlockSpec((tm, d), row),
                  pl.BlockSpec(mod_block, _mod_spec(0, tiles_per_batch)),
                  pl.BlockSpec(mod_block, _mod_spec(1, tiles_per_batch)),
                  _resident(g1.shape), _resident(w_all.shape), _resident(w_kt.shape),
                  _resident(gate_b.shape), _resident(conv_w.shape)],
        out_specs=out_specs,
        out_shape=out_shapes,
        compiler_params=_params(1),
        name="inproj",
    )(x2d, mod, mod, g1, w_all, w_kt, gate_b, conv_w)


def _inproj_ctx(c2d, mod, g1, w_v, w_kt, w_gate, gate_b, ctx_mod_row):
    n, d = c2d.shape
    tm = ROW_TILE
    row = lambda i: (i, 0)
    mod_block = (1, 1, d)
    qk_all, v_all = w_kt.shape[0], w_v.shape[1]
    return pl.pallas_call(
        _inproj_ctx_body,
        grid=(n // tm,),
        in_specs=[pl.BlockSpec((tm, d), row),
                  pl.BlockSpec(mod_block, _mod_spec(0, 1, ctx_mod_row)),
                  pl.BlockSpec(mod_block, _mod_spec(1, 1, ctx_mod_row)),
                  _resident(g1.shape), _resident(w_v.shape), _resident(w_kt.shape),
                  _resident(w_gate.shape), _resident(gate_b.shape)],
        out_specs=(pl.BlockSpec((qk_all, tm), lambda i: (0, i)), pl.BlockSpec((tm, v_all), row),
                   pl.BlockSpec((tm, GATE_COLS), row), pl.BlockSpec((GATE_COLS, tm), lambda i: (0, i))),
        out_shape=(jax.ShapeDtypeStruct((qk_all, n), BF16), jax.ShapeDtypeStruct((n, v_all), BF16),
                   jax.ShapeDtypeStruct((n, GATE_COLS), F32), jax.ShapeDtypeStruct((GATE_COLS, n), F32)),
        compiler_params=_params(1),
        name="inproj_ctx",
    )(c2d, mod, mod, g1, w_v, w_kt, w_gate, gate_b)


def _with_ones(v):
    return jnp.concatenate([v, jnp.ones((v.shape[0], LANES), v.dtype)], axis=1)


def _mlstm_state_update(h, direction, kt_ref, v_ref, gt_ref, s_ref, m_ref, qk, vh):
    ci = direction * 2 * N_HEADS + h
    cb = ci + N_HEADS
    last = 0 if direction else CHUNK - 1
    kt = kt_ref[h * qk:(h + 1) * qk, :].astype(F32)
    va = _with_ones(v_ref[:, h * vh:(h + 1) * vh])
    b_last = gt_ref[cb:cb + 1, last:last + 1]
    m_prev = m_ref[h][0:1, 0:1]
    g_r = b_last - gt_ref[cb:cb + 1, :] + gt_ref[ci:ci + 1, :]
    cm = N_GATES + direction * N_HEADS + h
    m_new = b_last + jnp.maximum(m_prev, gt_ref[cm:cm + 1, last:last + 1])
    a = jnp.exp(b_last + m_prev - m_new)
    kw = (kt * jnp.exp(g_r - m_new)).astype(BF16)
    s_ref[h] = a * s_ref[h] + jnp.dot(kw, va, preferred_element_type=F32)
    m_ref[h] = jnp.broadcast_to(m_new, m_ref.shape[1:])


def _mlstm_head_output(h, direction, q_ref, kt_ref, v_ref, g_ref, gt_ref, s_ref, m_ref, qk, vh):
    ci = direction * 2 * N_HEADS + h
    cb = ci + N_HEADS
    q = q_ref[:, h * qk:(h + 1) * qk]
    kt = kt_ref[h * qk:(h + 1) * qk, :]
    va = _with_ones(v_ref[:, h * vh:(h + 1) * vh])
    ig_r = gt_ref[ci:ci + 1, :]
    b_r = gt_ref[cb:cb + 1, :]
    cm = N_GATES + direction * N_HEADS + h
    m_prev = m_ref[h][0:1, 0:1]
    b_rep = jnp.broadcast_to(g_ref[:, cb:cb + 1], (CHUNK, LANES))
    run_rep = jnp.broadcast_to(g_ref[:, cm:cm + 1], (CHUNK, LANES))
    row = lax.broadcasted_iota(I32, (CHUNK, CHUNK), 0)
    col = lax.broadcasted_iota(I32, (CHUNK, CHUNK), 1)
    mask = (col >= row) if direction else (col <= row)
    dm = jnp.where(mask, b_rep + (ig_r - b_r), _NEG_INF)
    inter = b_rep + m_prev
    m_t = jnp.maximum(inter, b_rep + run_rep)
    w_inter = jnp.exp(inter - m_t)
    s = jnp.dot(q, kt, preferred_element_type=F32) * jnp.exp(dm - m_t)
    intra = jnp.dot(s.astype(BF16), va, preferred_element_type=F32)
    carried = jnp.dot(q, s_ref[h].astype(BF16), preferred_element_type=F32)
    den = intra[:, vh:vh + LANES] + w_inter * carried[:, vh:vh + LANES]
    scale = 1.0 / jnp.maximum(jnp.abs(den), jnp.exp(-m_t))
    return jnp.concatenate(
        [(intra[:, j:j + LANES] + w_inter * carried[:, j:j + LANES]) * scale for j in range(0, vh, LANES)],
        axis=1)


def _mlstm_body(*refs, direction, bsz, n_ctx_chunks, qk, vh):
    q_ref, v_ref, g_ref, vc_ref = refs[0:4]
    kt_refs, gt_refs = refs[4:4 + bsz], refs[4 + bsz:4 + 2 * bsz]
    ktc_refs, gtc_refs = refs[4 + 2 * bsz:4 + 3 * bsz], refs[4 + 3 * bsz:4 + 4 * bsz]
    rest = refs[4 + 4 * bsz:]
    if direction:
        out_ref, s_ref, m_ref = rest
    else:
        hb_ref, og_ref, hg_ref, out_ref, s_ref, m_ref = rest
    step = pl.program_id(0)

    @pl.when(step == 0)
    def _():
        s_ref[...] = jnp.zeros_like(s_ref)
        m_ref[...] = jnp.full_like(m_ref, _NEG_INF)

    @pl.when(step < n_ctx_chunks)
    def _():
        for b in range(bsz):
            for h in range(N_HEADS):
                _mlstm_state_update(h, direction, ktc_refs[b], vc_ref.at[b], gtc_refs[b],
                                    s_ref.at[b], m_ref.at[b], qk, vh)

    @pl.when(step >= n_ctx_chunks)
    def _():
        for b in range(bsz):
            for h in range(N_HEADS):
                hh = _mlstm_head_output(h, direction, q_ref.at[b], kt_refs[b], v_ref.at[b], g_ref.at[b],
                                        gt_refs[b], s_ref.at[b], m_ref.at[b], qk, vh)
                cols = slice(h * vh, (h + 1) * vh)
                if direction:
                    out_ref[b, :, cols] = hh
                else:
                    hs = hh + hb_ref[b, :, cols]
                    hs = hs * lax.rsqrt(jnp.mean(hs * hs, axis=-1, keepdims=True) + EPS)
                    out_ref[b, :, cols] = (hs * hg_ref[:, cols]
                                           * og_ref[b, :, cols].astype(F32)).astype(BF16)
                _mlstm_state_update(h, direction, kt_refs[b], v_ref.at[b], gt_refs[b],
                                    s_ref.at[b], m_ref.at[b], qk, vh)


def _mlstm(direction, q, kt, v, g, gt, ktc, vc, gtc, extra, bsz, head_g=None):
    n, qk_all = q.shape
    v_all = v.shape[1]
    qk, vh = qk_all // N_HEADS, v_all // N_HEADS
    seq = n // bsz
    nc = seq // CHUNK
    ncc = vc.shape[0] // bsz // CHUNK

    def lat(s):
        j = jnp.clip(s - ncc, 0, nc - 1)
        return nc - 1 - j if direction else j

    def ctx(s):
        j = jnp.clip(s, 0, ncc - 1)
        return ncc - 1 - j if direction else j

    def per_batch(a):
        return a.reshape(bsz, a.shape[0] // bsz, a.shape[1])

    lat_blk = lambda c: pl.BlockSpec((bsz, CHUNK, c), lambda s: (0, lat(s), 0))
    in_specs = [lat_blk(qk_all), lat_blk(v_all), lat_blk(GATE_COLS),
                pl.BlockSpec((bsz, CHUNK, v_all), lambda s: (0, ctx(s), 0))]
    args = [per_batch(q), per_batch(v), per_batch(g), per_batch(vc)]
    for arr, rows, n_chunks, pos in ((kt, qk_all, nc, lat), (gt, GATE_COLS, nc, lat),
                                     (ktc, qk_all, ncc, ctx), (gtc, GATE_COLS, ncc, ctx)):
        for b in range(bsz):
            in_specs.append(pl.BlockSpec((rows, CHUNK), lambda s, b=b, n_chunks=n_chunks, pos=pos:
                                         (0, b * n_chunks + pos(s))))
            args.append(arr)
    if direction:
        out_dtype = F32
    else:
        hb, og = extra
        in_specs += [lat_blk(v_all), lat_blk(v_all), pl.BlockSpec((1, v_all), lambda s: (0, 0))]
        args += [per_batch(hb), per_batch(og), head_g]
        out_dtype = BF16
    out = pl.pallas_call(
        functools.partial(_mlstm_body, direction=direction, bsz=bsz, n_ctx_chunks=ncc, qk=qk, vh=vh),
        grid=(ncc + nc,),
        in_specs=in_specs,
        out_specs=lat_blk(v_all),
        out_shape=jax.ShapeDtypeStruct((bsz, seq, v_all), out_dtype),
        scratch_shapes=[pltpu.VMEM((bsz, N_HEADS, qk, vh + LANES), F32),
                        pltpu.VMEM((bsz, N_HEADS, SUBLANES, LANES), F32)],
        compiler_params=_params(1),
        name="mlstm_bwd" if direction else "mlstm_fwd",
    )(*args)
    return out.reshape(n, v_all)


def _outproj_body(conv_ref, ml_ref, x_ref, gt1_ref, sh2_ref, sc2_ref, g2_ref, wo_ref, wr_ref, br_ref,
                  x1_ref, idx_ref, gate_ref, rank_ref, cnt_ref, h_hbm, carry_ref, hw, hsem):
    tm = x_ref.shape[0]
    half = conv_ref.shape[1]
    step = pl.program_id(0)
    buf = step % 2

    def h_out(i, s):
        return _row_tile_copies(h_hbm, i * tm, hw.at[s], hsem.at[s], to_hbm=True)

    @pl.when(step == 0)
    def _():
        carry_ref[...] = jnp.zeros_like(carry_ref)

    @pl.when(step >= 2)
    def _():
        _wait_all(h_out(step - 2, buf))

    y = (jnp.dot(conv_ref[...], wo_ref[0:half, :], preferred_element_type=F32)
         + jnp.dot(ml_ref[...], wo_ref[half:2 * half, :], preferred_element_type=F32))
    x1 = x_ref[...] + gt1_ref[0] * y
    x1_ref[...] = x1
    hn = _norm_mod(x1, g2_ref[...], sh2_ref[0], sc2_ref[0])
    hw[buf] = _pack_words(hn)
    _start_all(h_out(step, buf))

    h_hi = hn.astype(BF16)
    h_lo = (hn - h_hi.astype(F32)).astype(BF16)
    parts = (jnp.dot(h_hi, wr_ref[...], preferred_element_type=F32)
             + jnp.dot(h_lo, wr_ref[...], preferred_element_type=F32))
    scores = jax.nn.sigmoid(parts + pltpu.roll(parts, N_EXPERTS, axis=1))
    lane = lax.broadcasted_iota(I32, (tm, LANES), 1).astype(F32)
    biased = jnp.where(lane < N_EXPERTS, scores + br_ref[...], _NEG_INF)
    onehot = jnp.zeros((tm, LANES), F32)
    picks, sels = [], []
    for _ in range(TOP_K):
        mx = jnp.max(biased, axis=1, keepdims=True)
        pick = jnp.min(jnp.where(biased == mx, lane, float(LANES)), axis=1, keepdims=True)
        hit = lane == pick
        sels.append(jnp.sum(jnp.where(hit, scores, 0.0), axis=1, keepdims=True))
        picks.append(pick)
        biased = jnp.where(hit, _NEG_INF, biased)
        onehot = onehot + hit.astype(F32)
    total = sels[0]
    for s in sels[1:]:
        total = total + s

    r = lax.broadcasted_iota(I32, (tm, tm), 0)
    c = lax.broadcasted_iota(I32, (tm, tm), 1)
    strict = jnp.where(c < r, 1.0, 0.0).astype(BF16)
    before = jnp.dot(strict, onehot.astype(BF16), preferred_element_type=F32) + carry_ref[...]
    slot = lax.broadcasted_iota(I32, (tm, SUBLANES), 1)
    idx_out = jnp.zeros((tm, LANES), F32)
    rank_out = jnp.zeros((tm, LANES), F32)
    gate_out = jnp.zeros((tm, SUBLANES), F32)
    for j in range(TOP_K):
        rank = jnp.sum(jnp.where(lane == picks[j], before, 0.0), axis=1, keepdims=True)
        idx_out = jnp.where(lane == float(j), picks[j], idx_out)
        rank_out = jnp.where(lane == float(j), rank, rank_out)
        gate_out = jnp.where(slot == j, sels[j] / total * ROUTED_SCALE, gate_out)
    idx_ref[...] = idx_out.T[:SUBLANES, :].astype(I32)
    rank_ref[...] = rank_out.T[:SUBLANES, :].astype(I32)
    gate_ref[...] = gate_out
    carry_ref[...] = carry_ref[...] + jnp.sum(onehot, axis=0, keepdims=True)
    cnt_ref[...] = jnp.broadcast_to(carry_ref[...], cnt_ref.shape).astype(I32)

    @pl.when(step == pl.num_programs(0) - 1)
    def _():
        @pl.when(step >= 1)
        def _():
            _wait_all(h_out(step - 1, 1 - buf))
        _wait_all(h_out(step, buf))


def _outproj(conv, ml, x2d, mod, g2, w_out, w_router, b_router, rows_per_batch):
    n, d = x2d.shape
    tm = ROW_TILE
    tiles_per_batch = rows_per_batch // tm
    row = lambda i: (i, 0)
    mod_block = (1, 1, d)
    half = conv.shape[1]
    return pl.pallas_call(
        _outproj_body,
        grid=(n // tm,),
        in_specs=[pl.BlockSpec((tm, half), row), pl.BlockSpec((tm, half), row), pl.BlockSpec((tm, d), row),
                  pl.BlockSpec(mod_block, _mod_spec(2, tiles_per_batch)),
                  pl.BlockSpec(mod_block, _mod_spec(3, tiles_per_batch)),
                  pl.BlockSpec(mod_block, _mod_spec(4, tiles_per_batch)),
                  _resident(g2.shape), _resident(w_out.shape), _resident(w_router.shape),
                  _resident(b_router.shape)],
        out_specs=(pl.BlockSpec((tm, d), row),
                   pl.BlockSpec((SUBLANES, tm), lambda i: (0, i)), pl.BlockSpec((tm, SUBLANES), row),
                   pl.BlockSpec((SUBLANES, tm), lambda i: (0, i)),
                   pl.BlockSpec((SUBLANES, LANES), lambda i: (0, 0)),
                   pl.BlockSpec(memory_space=pl.ANY)),
        out_shape=(jax.ShapeDtypeStruct((n, d), F32),
                   jax.ShapeDtypeStruct((SUBLANES, n), I32), jax.ShapeDtypeStruct((n, SUBLANES), F32),
                   jax.ShapeDtypeStruct((SUBLANES, n), I32),
                   jax.ShapeDtypeStruct((SUBLANES, LANES), I32),
                   jax.ShapeDtypeStruct((n, SUBLANES, LANES), I32)),
        scratch_shapes=[pltpu.VMEM((1, LANES), F32), pltpu.VMEM((2, tm, d // 2), I32),
                        pltpu.SemaphoreType.DMA((2,))],
        compiler_params=_params(1),
        name="outproj_router",
    )(conv, ml, x2d, mod, mod, mod, g2, w_out, w_router, b_router)


def _sc_workers():
    info = plsc.get_sparse_core_info()
    return info.num_cores, info.num_cores * info.num_subcores


def _sc_dispatch(h_rows, dest_chunks, n_slots):
    n_tok = h_rows.shape[0]
    n_cores, n_workers = _sc_workers()
    per_worker = n_tok // (n_workers * SC_CHUNK)
    assert per_worker * n_workers * SC_CHUNK == n_tok
    mesh = plsc.VectorSubcoreMesh(core_axis_name="c", subcore_axis_name="s")

    @functools.partial(
        pl.kernel, mesh=mesh,
        out_type=jax.ShapeDtypeStruct((n_slots,) + h_rows.shape[1:], h_rows.dtype),
        scratch_types=[pltpu.VMEM((TOP_K, SC_CHUNK), I32),
                       pltpu.VMEM((SC_CHUNK,) + h_rows.shape[1:], h_rows.dtype)],
    )
    def dispatch(h_hbm, dest_hbm, out_hbm, idx_v, rows_v):
        wid = lax.axis_index("s") * n_cores + lax.axis_index("c")

        @pl.loop(0, per_worker)
        def _(i):
            chunk = wid * per_worker + i
            pltpu.sync_copy(dest_hbm.at[chunk], idx_v)
            pltpu.sync_copy(h_hbm.at[pl.ds(chunk * SC_CHUNK, SC_CHUNK)], rows_v)
            for k in range(TOP_K):
                pltpu.sync_copy(rows_v, out_hbm.at[idx_v.at[k]])

    return dispatch(h_rows, dest_chunks)


def _sc_combine(y_sorted, dest_chunks, n_tok):
    n_cores, n_workers = _sc_workers()
    per_worker = n_tok // (n_workers * SC_CHUNK)
    assert per_worker * n_workers * SC_CHUNK == n_tok
    mesh = plsc.VectorSubcoreMesh(core_axis_name="c", subcore_axis_name="s")

    @functools.partial(
        pl.kernel, mesh=mesh,
        out_type=jax.ShapeDtypeStruct((TOP_K, n_tok) + y_sorted.shape[1:], y_sorted.dtype),
        scratch_types=[pltpu.VMEM((TOP_K, SC_CHUNK), I32),
                       pltpu.VMEM((SC_CHUNK,) + y_sorted.shape[1:], y_sorted.dtype)],
    )
    def combine(y_hbm, dest_hbm, out_hbm, idx_v, rows_v):
        wid = lax.axis_index("s") * n_cores + lax.axis_index("c")

        @pl.loop(0, per_worker)
        def _(i):
            chunk = wid * per_worker + i
            pltpu.sync_copy(dest_hbm.at[chunk], idx_v)
            for k in range(TOP_K):
                pltpu.sync_copy(y_hbm.at[idx_v.at[k]], rows_v)
                pltpu.sync_copy(rows_v, out_hbm.at[k, pl.ds(chunk * SC_CHUNK, SC_CHUNK)])

    return combine(y_sorted, dest_chunks)


def _moe_body(ord_ref, order_ref, glo_ref, ghi_ref, tot_ref, nb_ref,
              x_hbm, wg_hbm, wu_hbm, wd_hbm, y_hbm,
              wgu, wd, stage_a, stage_d, xw, yw, wsem, xsem, ysem, *, d_expert):
    b = pl.program_id(0)
    nb = nb_ref[0]
    total = tot_ref[0]
    d_model = wgu.shape[1]
    rows_a = d_model // WEIGHT_PARTS
    rows_d = d_expert // WEIGHT_PARTS

    def part_copies(g):
        e = order_ref[lax.shift_right_logical(g, PART_SHIFT)]
        i = g & (WEIGHT_PARTS - 1)
        s = lax.rem(g, WEIGHT_RING)
        return (pltpu.make_async_copy(wg_hbm.at[e, pl.ds(i * rows_a, rows_a)], stage_a.at[s, 0],
                                      wsem.at[s, 0]),
                pltpu.make_async_copy(wu_hbm.at[e, pl.ds(i * rows_a, rows_a)], stage_a.at[s, 1],
                                      wsem.at[s, 1]),
                pltpu.make_async_copy(wd_hbm.at[e, pl.ds(i * rows_d, rows_d)], stage_d.at[s],
                                      wsem.at[s, 2]))

    def start_part(g):
        for cp in part_copies(g):
            cp.start()

    def wait_part(g):
        for cp in part_copies(g):
            cp.wait()

    def cast_part(g):
        i = g & (WEIGHT_PARTS - 1)
        s = lax.rem(g, WEIGHT_RING)
        par = lax.shift_right_logical(g, PART_SHIFT) & 1
        ra = pl.multiple_of(i * rows_a, rows_a)
        rd = pl.multiple_of(i * rows_d, rows_d)
        wgu[par, pl.ds(ra, rows_a), 0:d_expert] = stage_a[s, 0].astype(BF16)
        wgu[par, pl.ds(ra, rows_a), d_expert:2 * d_expert] = stage_a[s, 1].astype(BF16)
        wd[par, pl.ds(rd, rows_d), :] = stage_d[s].astype(BF16)

    def refill(g):
        @pl.when(g + WEIGHT_RING < total)
        def _():
            start_part(g + WEIGHT_RING)

    def cast_parts(lo, hi):
        def body(g, carry):
            wait_part(g)
            cast_part(g)
            refill(g)
            return carry
        lax.fori_loop(lo, hi, body, 0)

    slot = b % 2

    def x_in(blk, s):
        return _row_tile_copies(x_hbm, blk * MOE_BLOCK, xw.at[s], xsem.at[s], to_hbm=False)

    def y_out(blk, s):
        return _row_tile_copies(y_hbm, blk * MOE_BLOCK, yw.at[s], ysem.at[s], to_hbm=True)

    @pl.when(b == 0)
    def _():
        _start_all(x_in(0, 0))
        for g in range(WEIGHT_RING):
            start_part(g)
        cast_parts(0, WEIGHT_PARTS)

    @pl.when(b + 1 < nb)
    def _():
        _start_all(x_in(b + 1, 1 - slot))

    @pl.when(b < nb)
    def _():
        par = ord_ref[b] & 1
        _wait_all(x_in(b, slot))

        @pl.when(b >= 2)
        def _():
            _wait_all(y_out(b - 2, slot))

        x = jnp.concatenate(_unpack_words(xw[slot]), axis=1).astype(BF16)
        gu = jnp.dot(x, wgu[par], preferred_element_type=F32)
        hb = (_silu(gu[:, 0:d_expert]) * gu[:, d_expert:2 * d_expert]).astype(BF16)
        yw[slot] = _pack_words(jnp.dot(hb, wd[par], preferred_element_type=F32))
        _start_all(y_out(b, slot))
        cast_parts(glo_ref[b], ghi_ref[b])

        @pl.when(b == nb - 1)
        def _():
            @pl.when(b >= 1)
            def _():
                _wait_all(y_out(b - 1, 1 - slot))
            _wait_all(y_out(b, slot))


def _moe(x_sorted, we_gate, we_up, we_down, tables):
    d, d_expert = we_gate.shape[1], we_gate.shape[2]
    nb_max = x_sorted.shape[0] // MOE_BLOCK
    any_spec = pl.BlockSpec(memory_space=pl.ANY)
    grid_spec = pltpu.PrefetchScalarGridSpec(
        num_scalar_prefetch=len(tables),
        grid=(nb_max,),
        in_specs=[any_spec, any_spec, any_spec, any_spec],
        out_specs=any_spec,
        scratch_shapes=[pltpu.VMEM((2, d, 2 * d_expert), BF16),
                        pltpu.VMEM((2, d_expert, d), BF16),
                        pltpu.VMEM((WEIGHT_RING, 2, d // WEIGHT_PARTS, d_expert), F32),
                        pltpu.VMEM((WEIGHT_RING, d_expert // WEIGHT_PARTS, d), F32),
                        pltpu.VMEM((2, MOE_BLOCK, d // 2), I32),
                        pltpu.VMEM((2, MOE_BLOCK, d // 2), I32),
                        pltpu.SemaphoreType.DMA((WEIGHT_RING, 3)),
                        pltpu.SemaphoreType.DMA((2,)),
                        pltpu.SemaphoreType.DMA((2,))],
    )
    return pl.pallas_call(
        functools.partial(_moe_body, d_expert=d_expert),
        grid_spec=grid_spec,
        out_shape=jax.ShapeDtypeStruct(x_sorted.shape, x_sorted.dtype),
        compiler_params=pltpu.CompilerParams(
            dimension_semantics=("arbitrary",), vmem_limit_bytes=MOE_VMEM_LIMIT),
        name="moe_routed",
    )(*tables, x_sorted, we_gate, we_up, we_down)


def _final_body(x1_ref, gate_ref, gt2_ref, wsgu_ref, wsd_ref, fg_ref, h_hbm, y_hbm, out_ref,
                hw, yw, sem, *, d_shared):
    tm = x1_ref.shape[0]
    step = pl.program_id(0)
    slot = step % 2

    def rows_in(i, s):
        copies = _row_tile_copies(h_hbm, i * tm, hw.at[s], sem.at[s], to_hbm=False)
        for k in range(TOP_K):
            copies += _row_tile_copies(y_hbm.at[k], i * tm, yw.at[s, k], sem.at[s], to_hbm=False)
        return copies

    @pl.when(step == 0)
    def _():
        _start_all(rows_in(0, 0))

    @pl.when(step + 1 < pl.num_programs(0))
    def _():
        _start_all(rows_in(step + 1, 1 - slot))

    _wait_all(rows_in(step, slot))
    routed = gate_ref[:, 0:1] * jnp.concatenate(_unpack_words(yw[slot, 0]), axis=1)
    for k in range(1, TOP_K):
        routed = routed + gate_ref[:, k:k + 1] * jnp.concatenate(_unpack_words(yw[slot, k]), axis=1)
    h = jnp.concatenate(_unpack_words(hw[slot]), axis=1).astype(BF16)
    gu = jnp.dot(h, wsgu_ref[...], preferred_element_type=F32)
    hb = (_silu(gu[:, 0:d_shared]) * gu[:, d_shared:2 * d_shared]).astype(BF16)
    x2 = x1_ref[...] + gt2_ref[0] * (routed + jnp.dot(hb, wsd_ref[...], preferred_element_type=F32))
    out_ref[...] = x2 * lax.rsqrt(jnp.mean(x2 * x2, axis=-1, keepdims=True) + EPS) * fg_ref[...]


def _final(h_rows, x1, y_tok, gates, mod, ws_gu, ws_d, final_g, rows_per_batch):
    n, d = x1.shape
    tm = ROW_TILE
    tiles_per_batch = rows_per_batch // tm
    row = lambda i: (i, 0)
    any_spec = pl.BlockSpec(memory_space=pl.ANY)
    return pl.pallas_call(
        functools.partial(_final_body, d_shared=ws_d.shape[0]),
        grid=(n // tm,),
        in_specs=[pl.BlockSpec((tm, d), row), pl.BlockSpec((tm, SUBLANES), row),
                  pl.BlockSpec((1, 1, d), _mod_spec(5, tiles_per_batch)),
                  _resident(ws_gu.shape), _resident(ws_d.shape), _resident(final_g.shape),
                  any_spec, any_spec],
        out_specs=pl.BlockSpec((tm, d), row),
        out_shape=jax.ShapeDtypeStruct((n, d), F32),
        scratch_shapes=[pltpu.VMEM((2, tm, d // 2), I32), pltpu.VMEM((2, TOP_K, tm, d // 2), I32),
                        pltpu.SemaphoreType.DMA((2,))],
        compiler_params=_params(1),
        name="shared_combine_final",
    )(x1, gates, mod, ws_gu, ws_d, final_g, h_rows, y_tok)


def _routing_tables(idx, rank, counts, n_tok):
    nb_max = -(-(n_tok * TOP_K) // MOE_BLOCK) + N_EXPERTS
    nblk = (counts + MOE_BLOCK - 1) // MOE_BLOCK
    blk_end = jnp.cumsum(nblk)
    blk_start = blk_end - nblk
    experts = jnp.arange(N_EXPERTS, dtype=I32)[:, None, None]
    first_slot = (blk_start * MOE_BLOCK)[:, None, None]
    dest = jnp.sum(jnp.where(idx[None] == experts, first_slot, 0), axis=0) + rank
    dest_chunks = dest.reshape(TOP_K, n_tok // SC_CHUNK, SC_CHUNK).transpose(1, 0, 2)

    blocks = jnp.arange(nb_max, dtype=I32)[:, None]
    member = (blk_start[None, :] <= blocks) & (blocks < blk_end[None, :])
    lookup = lambda table: jnp.sum(jnp.where(member, table[None, :], 0), axis=1)
    blocks = blocks[:, 0]
    nonempty = nblk > 0
    n_visited = jnp.sum(nonempty.astype(I32))
    ordinal_of = jnp.cumsum(nonempty.astype(I32)) - 1
    slots = jnp.arange(N_EXPERTS, dtype=I32)
    order = jnp.sum(jnp.where(nonempty[None, :] & (ordinal_of[None, :] == slots[:, None]),
                              slots[None, :], 0), axis=1)
    ordinal = lookup(ordinal_of)
    k_in_e = blocks - lookup(blk_start)
    nb_e = jnp.maximum(lookup(nblk), 1)
    live = (ordinal + 1 < n_visited) & (blocks < blk_end[-1])
    first = WEIGHT_PARTS * (ordinal + 1)
    lo = jnp.where(live, first + WEIGHT_PARTS * k_in_e // nb_e, 0)
    hi = jnp.where(live, first + WEIGHT_PARTS * (k_in_e + 1) // nb_e, 0)
    tables = (ordinal.astype(I32), order, lo.astype(I32), hi.astype(I32),
              (WEIGHT_PARTS * n_visited).reshape(1).astype(I32), blk_end[-1:].astype(I32))
    return tables, dest_chunks, nb_max * MOE_BLOCK


def kernel(x, c, ctx, c_ctx, norm1_g, norm2_g, w_ada, b_ada, w_in, conv_w, gate_b, head_g, w_out,
           w_router, b_router, we_gate, we_up, we_down, ws_gate, ws_up, ws_down, final_g):
    assert w_ada.shape[0] == 1, "single-layer block"
    bsz, seq, d = x.shape
    ctx_len = ctx.shape[1]
    n_tok = bsz * seq
    conv_dim = conv_w.shape[2]
    v_all = head_g.shape[1]
    qk_all = (w_in.shape[2] - 3 * conv_dim - 2 * v_all - N_GATES) // 2
    assert seq % ROW_TILE == 0 and ctx_len % ROW_TILE == 0 and ROW_TILE % GRID_W == 0
    assert bsz + 1 <= SUBLANES

    cc = jnp.zeros((SUBLANES, d), F32).at[:bsz].set(c).at[bsz].set(c_ctx)
    mod = _adaln(cc, w_ada[0], b_ada).reshape(SUBLANES * 6, 1, d)

    n_main = 3 * conv_dim + 2 * qk_all + 2 * v_all
    w_all = jnp.pad(w_in[0].astype(BF16), ((0, 0), (0, LANES - N_GATES)))
    k_lo = 3 * conv_dim + qk_all
    w_kt = w_all[:, k_lo:k_lo + qk_all].T
    w_v = w_all[:, k_lo + qk_all:k_lo + qk_all + v_all]
    w_gate = w_all[:, n_main:n_main + LANES]
    gate_bias = jnp.zeros((1, LANES), F32).at[0, :N_GATES].set(gate_b[0].reshape(-1))

    x2d = x.reshape(n_tok, d)
    conv, q, kt, v, og, g, gt = _inproj(x2d, mod, norm1_g, w_all, w_kt, gate_bias, conv_w[0],
                                       seq, conv_dim, qk_all, v_all)
    ktc, vc, _, gtc = _inproj_ctx(ctx.reshape(bsz * ctx_len, d), mod, norm1_g, w_v, w_kt, w_gate,
                                  gate_bias, bsz)

    h_bwd = _mlstm(1, q, kt, v, g, gt, ktc, vc, gtc, None, bsz)
    ml = _mlstm(0, q, kt, v, g, gt, ktc, vc, gtc, (h_bwd, og), bsz, head_g)

    assert 2 * N_EXPERTS == LANES
    w_r_hi = w_router[0].astype(BF16)
    w_r = jnp.concatenate([w_r_hi, (w_router[0] - w_r_hi.astype(F32)).astype(BF16)], axis=1)
    b_r = jnp.zeros((1, LANES), F32).at[0, :N_EXPERTS].set(b_router[0])
    x1, idx, gates, rank, cnt, h_rows = _outproj(conv, ml, x2d, mod, norm2_g, w_out[0].astype(BF16),
                                             w_r, b_r, seq)

    tables, dest_chunks, n_slots = _routing_tables(idx[:TOP_K], rank[:TOP_K], cnt[0, :N_EXPERTS], n_tok)
    x_sorted = _sc_dispatch(h_rows, dest_chunks, n_slots)
    y_sorted = _moe(x_sorted, we_gate[0], we_up[0], we_down[0], tables)
    y_tok = _sc_combine(y_sorted, dest_chunks, n_tok)

    ws_gu = jnp.concatenate([ws_gate[0], ws_up[0]], axis=1).astype(BF16)
    out = _final(h_rows, x1, y_tok, gates, mod, ws_gu, ws_down[0].astype(BF16),
                 final_g.reshape(1, d), seq)
    return out.reshape(bsz, seq, d)
```

```python
import functools

import jax
import jax.numpy as jnp
from jax import lax
from jax.experimental import pallas as pl
from jax.experimental.pallas import tpu as pltpu
from jax.experimental.pallas import tpu_sc as plsc

F32 = jnp.float32
BF16 = jnp.bfloat16
I32 = jnp.int32

N_HEADS = 4
GRID_W = 64
CHUNK = 128
TOP_K = 6
N_EXPERTS = 64
ROUTED_SCALE = 2.446
EPS = 1e-6
N_GATES = 4 * N_HEADS
GATE_COLS = 6 * N_HEADS

LANES = 128
SUBLANES = 8
MOE_BLOCK = 256
ROW_TILE = 256
ADALN_TILE = 1024
WEIGHT_PARTS = 8
PART_SHIFT = 3
WEIGHT_RING = 3
SC_CHUNK = 64
HIGH_HALF = -65536
VMEM_LIMIT = 56 * 1024 * 1024
MOE_VMEM_LIMIT = 62 * 1024 * 1024

_HIGHEST = lax.Precision.HIGHEST
_NEG_INF = float("-inf")
assert CHUNK == LANES


def _resident(shape):
    nd = len(shape)
    return pl.BlockSpec(shape, lambda *_: (0,) * nd, pipeline_mode=pl.Buffered(1))


def _params(n_axes):
    return pltpu.CompilerParams(
        dimension_semantics=("arbitrary",) * n_axes, vmem_limit_bytes=VMEM_LIMIT)


def _log_sigmoid(x):
    return jnp.minimum(x, 0.0) - jnp.log1p(jnp.exp(-jnp.abs(x)))


def _silu(x):
    return x * jax.nn.sigmoid(x)


def _pack_words(val):
    half = val.shape[1] // 2
    lo = lax.bitcast_convert_type(val[:, :half].astype(BF16).astype(F32), I32)
    hi = lax.bitcast_convert_type(val[:, half:].astype(BF16).astype(F32), I32)
    return (hi & HIGH_HALF) | lax.shift_right_logical(lo, 16)


def _unpack_words(word):
    lo = lax.bitcast_convert_type(lax.shift_left(word, 16), F32)
    hi = lax.bitcast_convert_type(word & HIGH_HALF, F32)
    return lo, hi


def _row_tile_copies(hbm_rows, row0, tile, sem, to_hbm):
    n = tile.shape[0]
    copies = []
    for c in range(SUBLANES):
        hbm = hbm_rows.at[pl.ds(row0, n), c, :]
        vmem = tile.at[:, pl.ds(c * LANES, LANES)]
        copies.append(pltpu.make_async_copy(vmem, hbm, sem) if to_hbm
                      else pltpu.make_async_copy(hbm, vmem, sem))
    return copies


def _start_all(copies):
    for cp in copies:
        cp.start()


def _wait_all(copies):
    for cp in copies:
        cp.wait()


def _adaln_body(c_ref, w_ref, b_ref, o_ref):
    s = _silu(c_ref[...])
    o_ref[...] = jnp.dot(s.astype(BF16), w_ref[...].astype(BF16),
                         preferred_element_type=F32) + b_ref[...]


def _adaln(cc, w, b):
    d, n6 = w.shape
    return pl.pallas_call(
        _adaln_body,
        grid=(n6 // ADALN_TILE,),
        in_specs=[pl.BlockSpec((SUBLANES, d), lambda j: (0, 0)),
                  pl.BlockSpec((d, ADALN_TILE), lambda j: (0, j)),
                  pl.BlockSpec((1, ADALN_TILE), lambda j: (0, j))],
        out_specs=pl.BlockSpec((SUBLANES, ADALN_TILE), lambda j: (0, j)),
        out_shape=jax.ShapeDtypeStruct((SUBLANES, n6), F32),
        compiler_params=_params(1),
        name="adaln",
    )(cc, w, b)


def _norm_mod(x, g, shift, scale):
    y = x * lax.rsqrt(jnp.mean(x * x, axis=-1, keepdims=True) + EPS) * g
    return y * (1.0 + scale) + shift


def _gate_prep(xb, wg, gb_ref, g_ref, gt_ref):
    tm = xb.shape[0]
    gg = jnp.dot(xb, wg, preferred_element_type=F32) + gb_ref[...]
    lane = lax.broadcasted_iota(I32, (tm, LANES), 1)
    is_f = (lane & N_HEADS) != 0
    is_bwd = (lane & (2 * N_HEADS)) != 0
    lf = jnp.where(is_f, _log_sigmoid(gg), 0.0)
    r = lax.broadcasted_iota(I32, (tm, tm), 0)
    c = lax.broadcasted_iota(I32, (tm, tm), 1)
    same = (r // CHUNK) == (c // CHUNK)
    tri_l = jnp.where(same & (c <= r), 1.0, 0.0).astype(F32)
    tri_u = jnp.where(same & (c >= r), 1.0, 0.0).astype(F32)
    pre = jnp.dot(tri_l, lf, precision=_HIGHEST, preferred_element_type=F32)
    suf = jnp.dot(tri_u, lf, precision=_HIGHEST, preferred_element_type=F32)
    out = jnp.where(is_f, jnp.where(is_bwd, suf, pre), gg)
    diff = out - pltpu.roll(out, LANES - N_HEADS, axis=1)
    pos = lax.broadcasted_iota(I32, (tm, 1), 0) % CHUNK
    run_f, run_b = diff, diff
    k = 1
    while k < CHUNK:
        run_f = jnp.maximum(run_f, jnp.where(pos >= k, pltpu.roll(run_f, k, axis=0), _NEG_INF))
        run_b = jnp.maximum(run_b, jnp.where(pos < CHUNK - k, pltpu.roll(run_b, tm - k, axis=0), _NEG_INF))
        k *= 2
    run = jnp.where(is_bwd, run_b, run_f)
    fwd_lanes = (lane >= N_GATES) & (lane < N_GATES + N_HEADS)
    bwd_lanes = (lane >= N_GATES + N_HEADS) & (lane < GATE_COLS)
    out = jnp.where(fwd_lanes, pltpu.roll(run, N_GATES, axis=1),
                    jnp.where(bwd_lanes, pltpu.roll(run, N_GATES - N_HEADS, axis=1), out))
    g_ref[...] = out[:, :GATE_COLS]
    gt_ref[...] = out.T[:GATE_COLS, :]


def _project_transposed(wt_ref, xb):
    return lax.dot_general(wt_ref[...], xb, (((1,), (1,)), ((), ())),
                           preferred_element_type=F32).astype(BF16)


def _inproj_body(x_ref, sh_ref, sc_ref, g1_ref, w_ref, wkt_ref, wg_ref, gb_ref, cw_ref,
                 conv_ref, q_ref, k_ref, v_ref, o_ref, g_ref, gt_ref, *, conv_dim, qk_all, v_all):
    tm = x_ref.shape[0]
    xb = _norm_mod(x_ref[...], g1_ref[...], sh_ref[0], sc_ref[0]).astype(BF16)

    def proj(lo, width):
        return jnp.dot(xb, w_ref[:, lo:lo + width], preferred_element_type=F32)

    u = proj(conv_dim, conv_dim) * proj(2 * conv_dim, conv_dim)
    pos = lax.broadcasted_iota(I32, (tm, 1), 0) % GRID_W
    um = jnp.where(pos == 0, 0.0, pltpu.roll(u, 1, axis=0))
    up = jnp.where(pos == GRID_W - 1, 0.0, pltpu.roll(u, tm - 1, axis=0))
    y = um * cw_ref[0:1, :] + u * cw_ref[1:2, :] + up * cw_ref[2:3, :]
    conv_ref[...] = (proj(0, conv_dim) * y).astype(BF16)

    off = 3 * conv_dim
    qscale = (qk_all // N_HEADS) ** -0.5
    q_ref[...] = (proj(off, qk_all) * qscale).astype(BF16)
    k_ref[...] = _project_transposed(wkt_ref, xb)
    v_ref[...] = proj(off + 2 * qk_all, v_all).astype(BF16)
    o_ref[...] = jax.nn.sigmoid(proj(off + 2 * qk_all + v_all, v_all)).astype(BF16)
    _gate_prep(xb, wg_ref[...], gb_ref, g_ref, gt_ref)


def _inproj_ctx_body(x_ref, sh_ref, sc_ref, g1_ref, w_ref, wkt_ref, wg_ref, gb_ref,
                     k_ref, v_ref, g_ref, gt_ref):
    xb = _norm_mod(x_ref[...], g1_ref[...], sh_ref[0], sc_ref[0]).astype(BF16)
    k_ref[...] = _project_transposed(wkt_ref, xb)
    v_ref[...] = jnp.dot(xb, w_ref[...], preferred_element_type=F32).astype(BF16)
    _gate_prep(xb, wg_ref[...], gb_ref, g_ref, gt_ref)


def _mod_spec(part, tiles_per_row, fixed_row=None):
    def index(i):
        row = fixed_row if fixed_row is not None else i // tiles_per_row
        return (row * 6 + part, 0, 0)

    return index


def _inproj(x2d, mod, g1, w_all, w_kt, w_gate, gate_b, conv_w, rows_per_batch, conv_dim, qk_all, v_all):
    n, d = x2d.shape
    tm = ROW_TILE
    tiles_per_batch = rows_per_batch // tm
    row = lambda i: (i, 0)
    mod_block = (1, 1, d)
    out_shapes = (
        jax.ShapeDtypeStruct((n, conv_dim), BF16),
        jax.ShapeDtypeStruct((n, qk_all), BF16),
        jax.ShapeDtypeStruct((qk_all, n), BF16),
        jax.ShapeDtypeStruct((n, v_all), BF16),
        jax.ShapeDtypeStruct((n, v_all), BF16),
        jax.ShapeDtypeStruct((n, GATE_COLS), F32),
        jax.ShapeDtypeStruct((GATE_COLS, n), F32),
    )
    out_specs = (
        pl.BlockSpec((tm, conv_dim), row),
        pl.BlockSpec((tm, qk_all), row),
        pl.BlockSpec((qk_all, tm), lambda i: (0, i)),
        pl.BlockSpec((tm, v_all), row),
        pl.BlockSpec((tm, v_all), row),
        pl.BlockSpec((tm, GATE_COLS), row),
        pl.BlockSpec((GATE_COLS, tm), lambda i: (0, i)),
    )
    return pl.pallas_call(
        functools.partial(_inproj_body, conv_dim=conv_dim, qk_all=qk_all, v_all=v_all),
        grid=(n // tm,),
        in_specs=[pl.BlockSpec((tm, d), row),
                  pl.BlockSpec(mod_block, _mod_spec(0, tiles_per_batch)),
                  pl.BlockSpec(mod_block, _mod_spec(1, tiles_per_batch)),
                  _resident(g1.shape), _resident(w_all.shape), _resident(w_kt.shape),
                  _resident(w_gate.shape), _resident(gate_b.shape), _resident(conv_w.shape)],
        out_specs=out_specs,
        out_shape=out_shapes,
        compiler_params=_params(1),
        name="inproj",
    )(x2d, mod, mod, g1, w_all, w_kt, w_gate, gate_b, conv_w)


def _inproj_ctx(c2d, mod, g1, w_v, w_kt, w_gate, gate_b, ctx_mod_row):
    n, d = c2d.shape
    tm = ROW_TILE
    row = lambda i: (i, 0)
    mod_block = (1, 1, d)
    qk_all, v_all = w_kt.shape[0], w_v.shape[1]
    return pl.pallas_call(
        _inproj_ctx_body,
        grid=(n // tm,),
        in_specs=[pl.BlockSpec((tm, d), row),
                  pl.BlockSpec(mod_block, _mod_spec(0, 1, ctx_mod_row)),
                  pl.BlockSpec(mod_block, _mod_spec(1, 1, ctx_mod_row)),
                  _resident(g1.shape), _resident(w_v.shape), _resident(w_kt.shape),
                  _resident(w_gate.shape), _resident(gate_b.shape)],
        out_specs=(pl.BlockSpec((qk_all, tm), lambda i: (0, i)), pl.BlockSpec((tm, v_all), row),
                   pl.BlockSpec((tm, GATE_COLS), row), pl.BlockSpec((GATE_COLS, tm), lambda i: (0, i))),
        out_shape=(jax.ShapeDtypeStruct((qk_all, n), BF16), jax.ShapeDtypeStruct((n, v_all), BF16),
                   jax.ShapeDtypeStruct((n, GATE_COLS), F32), jax.ShapeDtypeStruct((GATE_COLS, n), F32)),
        compiler_params=_params(1),
        name="inproj_ctx",
    )(c2d, mod, mod, g1, w_v, w_kt, w_gate, gate_b)


def _with_ones(v):
    return jnp.concatenate([v, jnp.ones((v.shape[0], LANES), v.dtype)], axis=1)


def _mlstm_state_update(h, direction, kt_ref, v_ref, gt_ref, s_ref, m_ref, qk, vh):
    ci = direction * 2 * N_HEADS + h
    cb = ci + N_HEADS
    last = 0 if direction else CHUNK - 1
    kt = kt_ref[h * qk:(h + 1) * qk, :].astype(F32)
    va = _with_ones(v_ref[:, h * vh:(h + 1) * vh])
    b_last = gt_ref[cb:cb + 1, last:last + 1]
    m_prev = m_ref[h][0:1, 0:1]
    g_r = b_last - gt_ref[cb:cb + 1, :] + gt_ref[ci:ci + 1, :]
    cm = N_GATES + direction * N_HEADS + h
    m_new = b_last + jnp.maximum(m_prev, gt_ref[cm:cm + 1, last:last + 1])
    a = jnp.exp(b_last + m_prev - m_new)
    kw = (kt * jnp.exp(g_r - m_new)).astype(BF16)
    s_ref[h] = a * s_ref[h] + jnp.dot(kw, va, preferred_element_type=F32)
    m_ref[h] = jnp.broadcast_to(m_new, m_ref.shape[1:])


def _mlstm_head_output(h, direction, q_ref, kt_ref, v_ref, g_ref, gt_ref, s_ref, m_ref, qk, vh):
    ci = direction * 2 * N_HEADS + h
    cb = ci + N_HEADS
    q = q_ref[:, h * qk:(h + 1) * qk]
    kt = kt_ref[h * qk:(h + 1) * qk, :]
    va = _with_ones(v_ref[:, h * vh:(h + 1) * vh])
    ig_r = gt_ref[ci:ci + 1, :]
    b_r = gt_ref[cb:cb + 1, :]
    cm = N_GATES + direction * N_HEADS + h
    m_prev = m_ref[h][0:1, :]
    b_rep = jnp.broadcast_to(g_ref[:, cb:cb + 1], (CHUNK, LANES))
    run_rep = jnp.broadcast_to(g_ref[:, cm:cm + 1], (CHUNK, LANES))
    row = lax.broadcasted_iota(I32, (CHUNK, CHUNK), 0)
    col = lax.broadcasted_iota(I32, (CHUNK, CHUNK), 1)
    mask = (col >= row) if direction else (col <= row)
    dm = jnp.where(mask, b_rep + (ig_r - b_r), _NEG_INF)
    inter = b_rep + m_prev
    m_t = jnp.maximum(inter, b_rep + run_rep)
    w_inter = jnp.exp(inter - m_t)
    s = jnp.dot(q, kt, preferred_element_type=F32) * jnp.exp(dm - m_t)
    intra = jnp.dot(s.astype(BF16), va, preferred_element_type=F32)
    carried = jnp.dot(q, s_ref[h].astype(BF16), preferred_element_type=F32)
    den = intra[:, vh:vh + LANES] + w_inter * carried[:, vh:vh + LANES]
    scale = 1.0 / jnp.maximum(jnp.abs(den), jnp.exp(-m_t))
    return jnp.concatenate(
        [(intra[:, j:j + LANES] + w_inter * carried[:, j:j + LANES]) * scale for j in range(0, vh, LANES)],
        axis=1)


def _mlstm_body(*refs, direction, bsz, n_ctx_chunks, qk, vh):
    q_ref, v_ref, g_ref, vc_ref = refs[0:4]
    kt_refs, gt_refs = refs[4:4 + bsz], refs[4 + bsz:4 + 2 * bsz]
    ktc_refs, gtc_refs = refs[4 + 2 * bsz:4 + 3 * bsz], refs[4 + 3 * bsz:4 + 4 * bsz]
    rest = refs[4 + 4 * bsz:]
    if direction:
        out_ref, s_ref, m_ref = rest
    else:
        hb_ref, og_ref, hg_ref, out_ref, s_ref, m_ref = rest
    step = pl.program_id(0)

    @pl.when(step == 0)
    def _():
        s_ref[...] = jnp.zeros_like(s_ref)
        m_ref[...] = jnp.full_like(m_ref, _NEG_INF)

    @pl.when(step < n_ctx_chunks)
    def _():
        for b in range(bsz):
            for h in range(N_HEADS):
                _mlstm_state_update(h, direction, ktc_refs[b], vc_ref.at[b], gtc_refs[b],
                                    s_ref.at[b], m_ref.at[b], qk, vh)

    @pl.when(step >= n_ctx_chunks)
    def _():
        for b in range(bsz):
            for h in range(N_HEADS):
                hh = _mlstm_head_output(h, direction, q_ref.at[b], kt_refs[b], v_ref.at[b], g_ref.at[b],
                                        gt_refs[b], s_ref.at[b], m_ref.at[b], qk, vh)
                cols = slice(h * vh, (h + 1) * vh)
                if direction:
                    out_ref[b, :, cols] = hh
                else:
                    hs = hh + hb_ref[b, :, cols]
                    hs = hs * lax.rsqrt(jnp.mean(hs * hs, axis=-1, keepdims=True) + EPS)
                    out_ref[b, :, cols] = (hs * hg_ref[:, cols]
                                           * og_ref[b, :, cols].astype(F32)).astype(BF16)
                _mlstm_state_update(h, direction, kt_refs[b], v_ref.at[b], gt_refs[b],
                                    s_ref.at[b], m_ref.at[b], qk, vh)


def _mlstm(direction, q, kt, v, g, gt, ktc, vc, gtc, extra, bsz, head_g=None):
    n, qk_all = q.shape
    v_all = v.shape[1]
    qk, vh = qk_all // N_HEADS, v_all // N_HEADS
    seq = n // bsz
    nc = seq // CHUNK
    ncc = vc.shape[0] // bsz // CHUNK

    def lat(s):
        j = jnp.clip(s - ncc, 0, nc - 1)
        return nc - 1 - j if direction else j

    def ctx(s):
        j = jnp.clip(s, 0, ncc - 1)
        return ncc - 1 - j if direction else j

    def per_batch(a):
        return a.reshape(bsz, a.shape[0] // bsz, a.shape[1])

    lat_blk = lambda c: pl.BlockSpec((bsz, CHUNK, c), lambda s: (0, lat(s), 0))
    in_specs = [lat_blk(qk_all), lat_blk(v_all), lat_blk(GATE_COLS),
                pl.BlockSpec((bsz, CHUNK, v_all), lambda s: (0, ctx(s), 0))]
    args = [per_batch(q), per_batch(v), per_batch(g), per_batch(vc)]
    for arr, rows, n_chunks, pos in ((kt, qk_all, nc, lat), (gt, GATE_COLS, nc, lat),
                                     (ktc, qk_all, ncc, ctx), (gtc, GATE_COLS, ncc, ctx)):
        for b in range(bsz):
            in_specs.append(pl.BlockSpec((rows, CHUNK), lambda s, b=b, n_chunks=n_chunks, pos=pos:
                                         (0, b * n_chunks + pos(s))))
            args.append(arr)
    if direction:
        out_dtype = F32
    else:
        hb, og = extra
        in_specs += [lat_blk(v_all), lat_blk(v_all), pl.BlockSpec((1, v_all), lambda s: (0, 0))]
        args += [per_batch(hb), per_batch(og), head_g]
        out_dtype = BF16
    out = pl.pallas_call(
        functools.partial(_mlstm_body, direction=direction, bsz=bsz, n_ctx_chunks=ncc, qk=qk, vh=vh),
        grid=(ncc + nc,),
        in_specs=in_specs,
        out_specs=lat_blk(v_all),
        out_shape=jax.ShapeDtypeStruct((bsz, seq, v_all), out_dtype),
        scratch_shapes=[pltpu.VMEM((bsz, N_HEADS, qk, vh + LANES), F32),
                        pltpu.VMEM((bsz, N_HEADS, SUBLANES, LANES), F32)],
        compiler_params=_params(1),
        name="mlstm_bwd" if direction else "mlstm_fwd",
    )(*args)
    return out.reshape(n, v_all)


def _outproj_body(conv_ref, ml_ref, x_ref, gt1_ref, sh2_ref, sc2_ref, g2_ref, wo_ref, wr_ref, br_ref,
                  x1_ref, idx_ref, gate_ref, rank_ref, cnt_ref, h_hbm, carry_ref, hw, hsem):
    tm = x_ref.shape[0]
    half = conv_ref.shape[1]
    step = pl.program_id(0)
    buf = step % 2

    def h_out(i, s):
        return _row_tile_copies(h_hbm, i * tm, hw.at[s], hsem.at[s], to_hbm=True)

    @pl.when(step == 0)
    def _():
        carry_ref[...] = jnp.zeros_like(carry_ref)

    @pl.when(step >= 2)
    def _():
        _wait_all(h_out(step - 2, buf))

    y = (jnp.dot(conv_ref[...], wo_ref[0:half, :], preferred_element_type=F32)
         + jnp.dot(ml_ref[...], wo_ref[half:2 * half, :], preferred_element_type=F32))
    x1 = x_ref[...] + gt1_ref[0] * y
    x1_ref[...] = x1
    hn = _norm_mod(x1, g2_ref[...], sh2_ref[0], sc2_ref[0])
    hw[buf] = _pack_words(hn)
    _start_all(h_out(step, buf))

    h_hi = hn.astype(BF16)
    h_lo = (hn - h_hi.astype(F32)).astype(BF16)
    parts = (jnp.dot(h_hi, wr_ref[...], preferred_element_type=F32)
             + jnp.dot(h_lo, wr_ref[...], preferred_element_type=F32))
    scores = jax.nn.sigmoid(parts + pltpu.roll(parts, N_EXPERTS, axis=1))
    lane = lax.broadcasted_iota(I32, (tm, LANES), 1).astype(F32)
    biased = jnp.where(lane < N_EXPERTS, scores + br_ref[...], _NEG_INF)
    onehot = jnp.zeros((tm, LANES), F32)
    picks, sels = [], []
    for _ in range(TOP_K):
        mx = jnp.max(biased, axis=1, keepdims=True)
        pick = jnp.min(jnp.where(biased == mx, lane, float(LANES)), axis=1, keepdims=True)
        hit = lane == pick
        sels.append(jnp.sum(jnp.where(hit, scores, 0.0), axis=1, keepdims=True))
        picks.append(pick)
        biased = jnp.where(hit, _NEG_INF, biased)
        onehot = onehot + hit.astype(F32)
    total = sels[0]
    for s in sels[1:]:
        total = total + s

    r = lax.broadcasted_iota(I32, (tm, tm), 0)
    c = lax.broadcasted_iota(I32, (tm, tm), 1)
    strict = jnp.where(c < r, 1.0, 0.0).astype(BF16)
    before = jnp.dot(strict, onehot.astype(BF16), preferred_element_type=F32) + carry_ref[...]
    slot = lax.broadcasted_iota(I32, (tm, SUBLANES), 1)
    idx_out = jnp.zeros((tm, LANES), F32)
    rank_out = jnp.zeros((tm, LANES), F32)
    gate_out = jnp.zeros((tm, SUBLANES), F32)
    for j in range(TOP_K):
        rank = jnp.sum(jnp.where(lane == picks[j], before, 0.0), axis=1, keepdims=True)
        idx_out = jnp.where(lane == float(j), picks[j], idx_out)
        rank_out = jnp.where(lane == float(j), rank, rank_out)
        gate_out = jnp.where(slot == j, sels[j] / total * ROUTED_SCALE, gate_out)
    idx_ref[...] = idx_out.T[:SUBLANES, :].astype(I32)
    rank_ref[...] = rank_out.T[:SUBLANES, :].astype(I32)
    gate_ref[...] = gate_out
    carry_ref[...] = carry_ref[...] + jnp.sum(onehot, axis=0, keepdims=True)
    cnt_ref[...] = jnp.broadcast_to(carry_ref[...], cnt_ref.shape).astype(I32)

    @pl.when(step == pl.num_programs(0) - 1)
    def _():
        @pl.when(step >= 1)
        def _():
            _wait_all(h_out(step - 1, 1 - buf))
        _wait_all(h_out(step, buf))


def _outproj(conv, ml, x2d, mod, g2, w_out, w_router, b_router, rows_per_batch):
    n, d = x2d.shape
    tm = ROW_TILE
    tiles_per_batch = rows_per_batch // tm
    row = lambda i: (i, 0)
    mod_block = (1, 1, d)
    half = conv.shape[1]
    return pl.pallas_call(
        _outproj_body,
        grid=(n // tm,),
        in_specs=[pl.BlockSpec((tm, half), row), pl.BlockSpec((tm, half), row), pl.BlockSpec((tm, d), row),
                  pl.BlockSpec(mod_block, _mod_spec(2, tiles_per_batch)),
                  pl.BlockSpec(mod_block, _mod_spec(3, tiles_per_batch)),
                  pl.BlockSpec(mod_block, _mod_spec(4, tiles_per_batch)),
                  _resident(g2.shape), _resident(w_out.shape), _resident(w_router.shape),
                  _resident(b_router.shape)],
        out_specs=(pl.BlockSpec((tm, d), row),
                   pl.BlockSpec((SUBLANES, tm), lambda i: (0, i)), pl.BlockSpec((tm, SUBLANES), row),
                   pl.BlockSpec((SUBLANES, tm), lambda i: (0, i)),
                   pl.BlockSpec((SUBLANES, LANES), lambda i: (0, 0)),
                   pl.BlockSpec(memory_space=pl.ANY)),
        out_shape=(jax.ShapeDtypeStruct((n, d), F32),
                   jax.ShapeDtypeStruct((SUBLANES, n), I32), jax.ShapeDtypeStruct((n, SUBLANES), F32),
                   jax.ShapeDtypeStruct((SUBLANES, n), I32),
                   jax.ShapeDtypeStruct((SUBLANES, LANES), I32),
                   jax.ShapeDtypeStruct((n, SUBLANES, LANES), I32)),
        scratch_shapes=[pltpu.VMEM((1, LANES), F32), pltpu.VMEM((2, tm, d // 2), I32),
                        pltpu.SemaphoreType.DMA((2,))],
        compiler_params=_params(1),
        name="outproj_router",
    )(conv, ml, x2d, mod, mod, mod, g2, w_out, w_router, b_router)


def _sc_workers():
    info = plsc.get_sparse_core_info()
    return info.num_cores, info.num_cores * info.num_subcores


def _sc_dispatch(h_rows, dest_chunks, n_slots):
    n_tok = h_rows.shape[0]
    n_cores, n_workers = _sc_workers()
    per_worker = n_tok // (n_workers * SC_CHUNK)
    assert per_worker * n_workers * SC_CHUNK == n_tok
    mesh = plsc.VectorSubcoreMesh(core_axis_name="c", subcore_axis_name="s")

    @functools.partial(
        pl.kernel, mesh=mesh,
        out_type=jax.ShapeDtypeStruct((n_slots,) + h_rows.shape[1:], h_rows.dtype),
        scratch_types=[pltpu.VMEM((TOP_K, SC_CHUNK), I32),
                       pltpu.VMEM((SC_CHUNK,) + h_rows.shape[1:], h_rows.dtype)],
    )
    def dispatch(h_hbm, dest_hbm, out_hbm, idx_v, rows_v):
        wid = lax.axis_index("s") * n_cores + lax.axis_index("c")

        @pl.loop(0, per_worker)
        def _(i):
            chunk = wid * per_worker + i
            pltpu.sync_copy(dest_hbm.at[chunk], idx_v)
            pltpu.sync_copy(h_hbm.at[pl.ds(chunk * SC_CHUNK, SC_CHUNK)], rows_v)
            for k in range(TOP_K):
                pltpu.sync_copy(rows_v, out_hbm.at[idx_v.at[k]])

    return dispatch(h_rows, dest_chunks)


def _sc_combine(y_sorted, dest_chunks, n_tok):
    n_cores, n_workers = _sc_workers()
    per_worker = n_tok // (n_workers * SC_CHUNK)
    assert per_worker * n_workers * SC_CHUNK == n_tok
    mesh = plsc.VectorSubcoreMesh(core_axis_name="c", subcore_axis_name="s")

    @functools.partial(
        pl.kernel, mesh=mesh,
        out_type=jax.ShapeDtypeStruct((TOP_K, n_tok) + y_sorted.shape[1:], y_sorted.dtype),
        scratch_types=[pltpu.VMEM((TOP_K, SC_CHUNK), I32),
                       pltpu.VMEM((SC_CHUNK,) + y_sorted.shape[1:], y_sorted.dtype)],
    )
    def combine(y_hbm, dest_hbm, out_hbm, idx_v, rows_v):
        wid = lax.axis_index("s") * n_cores + lax.axis_index("c")

        @pl.loop(0, per_worker)
        def _(i):
            chunk = wid * per_worker + i
            pltpu.sync_copy(dest_hbm.at[chunk], idx_v)
            for k in range(TOP_K):
                pltpu.sync_copy(y_hbm.at[idx_v.at[k]], rows_v)
                pltpu.sync_copy(rows_v, out_hbm.at[k, pl.ds(chunk * SC_CHUNK, SC_CHUNK)])

    return combine(y_sorted, dest_chunks)


def _moe_body(ord_ref, order_ref, glo_ref, ghi_ref, tot_ref, nb_ref,
              x_hbm, wg_hbm, wu_hbm, wd_hbm, y_hbm,
              wgu, wd, stage_a, stage_d, xw, yw, wsem, xsem, ysem, *, d_expert):
    b = pl.program_id(0)
    nb = nb_ref[0]
    total = tot_ref[0]
    d_model = wgu.shape[1]
    rows_a = d_model // WEIGHT_PARTS
    rows_d = d_expert // WEIGHT_PARTS

    def part_copies(g):
        e = order_ref[lax.shift_right_logical(g, PART_SHIFT)]
        i = g & (WEIGHT_PARTS - 1)
        s = lax.rem(g, WEIGHT_RING)
        return (pltpu.make_async_copy(wg_hbm.at[e, pl.ds(i * rows_a, rows_a)], stage_a.at[s, 0],
                                      wsem.at[s, 0]),
                pltpu.make_async_copy(wu_hbm.at[e, pl.ds(i * rows_a, rows_a)], stage_a.at[s, 1],
                                      wsem.at[s, 1]),
                pltpu.make_async_copy(wd_hbm.at[e, pl.ds(i * rows_d, rows_d)], stage_d.at[s],
                                      wsem.at[s, 2]))

    def start_part(g):
        for cp in part_copies(g):
            cp.start()

    def wait_part(g):
        for cp in part_copies(g):
            cp.wait()

    def cast_part(g):
        i = g & (WEIGHT_PARTS - 1)
        s = lax.rem(g, WEIGHT_RING)
        par = lax.shift_right_logical(g, PART_SHIFT) & 1
        ra = pl.multiple_of(i * rows_a, rows_a)
        rd = pl.multiple_of(i * rows_d, rows_d)
        wgu[par, pl.ds(ra, rows_a), 0:d_expert] = stage_a[s, 0].astype(BF16)
        wgu[par, pl.ds(ra, rows_a), d_expert:2 * d_expert] = stage_a[s, 1].astype(BF16)
        wd[par, pl.ds(rd, rows_d), :] = stage_d[s].astype(BF16)

    def refill(g):
        @pl.when(g + WEIGHT_RING < total)
        def _():
            start_part(g + WEIGHT_RING)

    def cast_parts(lo, hi):
        def body(g, carry):
            wait_part(g)
            cast_part(g)
            refill(g)
            return carry
        lax.fori_loop(lo, hi, body, 0)

    slot = b % 2

    def x_in(blk, s):
        return _row_tile_copies(x_hbm, blk * MOE_BLOCK, xw.at[s], xsem.at[s], to_hbm=False)

    def y_out(blk, s):
        return _row_tile_copies(y_hbm, blk * MOE_BLOCK, yw.at[s], ysem.at[s], to_hbm=True)

    @pl.when(b == 0)
    def _():
        _start_all(x_in(0, 0))
        for g in range(WEIGHT_RING):
            start_part(g)
        cast_parts(0, WEIGHT_PARTS)

    @pl.when(b + 1 < nb)
    def _():
        _start_all(x_in(b + 1, 1 - slot))

    @pl.when(b < nb)
    def _():
        par = ord_ref[b] & 1
        _wait_all(x_in(b, slot))

        @pl.when(b >= 2)
        def _():
            _wait_all(y_out(b - 2, slot))

        x = jnp.concatenate(_unpack_words(xw[slot]), axis=1).astype(BF16)
        gu = jnp.dot(x, wgu[par], preferred_element_type=F32)
        hb = (_silu(gu[:, 0:d_expert]) * gu[:, d_expert:2 * d_expert]).astype(BF16)
        yw[slot] = _pack_words(jnp.dot(hb, wd[par], preferred_element_type=F32))
        _start_all(y_out(b, slot))
        cast_parts(glo_ref[b], ghi_ref[b])

        @pl.when(b == nb - 1)
        def _():
            @pl.when(b >= 1)
            def _():
                _wait_all(y_out(b - 1, 1 - slot))
            _wait_all(y_out(b, slot))


def _moe(x_sorted, we_gate, we_up, we_down, tables):
    d, d_expert = we_gate.shape[1], we_gate.shape[2]
    nb_max = x_sorted.shape[0] // MOE_BLOCK
    any_spec = pl.BlockSpec(memory_space=pl.ANY)
    grid_spec = pltpu.PrefetchScalarGridSpec(
        num_scalar_prefetch=len(tables),
        grid=(nb_max,),
        in_specs=[any_spec, any_spec, any_spec, any_spec],
        out_specs=any_spec,
        scratch_shapes=[pltpu.VMEM((2, d, 2 * d_expert), BF16),
                        pltpu.VMEM((2, d_expert, d), BF16),
                        pltpu.VMEM((WEIGHT_RING, 2, d // WEIGHT_PARTS, d_expert), F32),
                        pltpu.VMEM((WEIGHT_RING, d_expert // WEIGHT_PARTS, d), F32),
                        pltpu.VMEM((2, MOE_BLOCK, d // 2), I32),
                        pltpu.VMEM((2, MOE_BLOCK, d // 2), I32),
                        pltpu.SemaphoreType.DMA((WEIGHT_RING, 3)),
                        pltpu.SemaphoreType.DMA((2,)),
                        pltpu.SemaphoreType.DMA((2,))],
    )
    return pl.pallas_call(
        functools.partial(_moe_body, d_expert=d_expert),
        grid_spec=grid_spec,
        out_shape=jax.ShapeDtypeStruct(x_sorted.shape, x_sorted.dtype),
        compiler_params=pltpu.CompilerParams(
            dimension_semantics=("arbitrary",), vmem_limit_bytes=MOE_VMEM_LIMIT),
        name="moe_routed",
    )(*tables, x_sorted, we_gate, we_up, we_down)


def _final_body(x1_ref, gate_ref, gt2_ref, wsgu_ref, wsd_ref, fg_ref, h_hbm, y_hbm, out_ref,
                hw, yw, sem, *, d_shared):
    tm = x1_ref.shape[0]
    step = pl.program_id(0)
    slot = step % 2

    def rows_in(i, s):
        copies = _row_tile_copies(h_hbm, i * tm, hw.at[s], sem.at[s], to_hbm=False)
        for k in range(TOP_K):
            copies += _row_tile_copies(y_hbm.at[k], i * tm, yw.at[s, k], sem.at[s], to_hbm=False)
        return copies

    @pl.when(step == 0)
    def _():
        _start_all(rows_in(0, 0))

    @pl.when(step + 1 < pl.num_programs(0))
    def _():
        _start_all(rows_in(step + 1, 1 - slot))

    _wait_all(rows_in(step, slot))
    routed = gate_ref[:, 0:1] * jnp.concatenate(_unpack_words(yw[slot, 0]), axis=1)
    for k in range(1, TOP_K):
        routed = routed + gate_ref[:, k:k + 1] * jnp.concatenate(_unpack_words(yw[slot, k]), axis=1)
    h = jnp.concatenate(_unpack_words(hw[slot]), axis=1).astype(BF16)
    gu = jnp.dot(h, wsgu_ref[...], preferred_element_type=F32)
    hb = (_silu(gu[:, 0:d_shared]) * gu[:, d_shared:2 * d_shared]).astype(BF16)
    x2 = x1_ref[...] + gt2_ref[0] * (routed + jnp.dot(hb, wsd_ref[...], preferred_element_type=F32))
    out_ref[...] = x2 * lax.rsqrt(jnp.mean(x2 * x2, axis=-1, keepdims=True) + EPS) * fg_ref[...]


def _final(h_rows, x1, y_tok, gates, mod, ws_gu, ws_d, final_g, rows_per_batch):
    n, d = x1.shape
    tm = ROW_TILE
    tiles_per_batch = rows_per_batch // tm
    row = lambda i: (i, 0)
    any_spec = pl.BlockSpec(memory_space=pl.ANY)
    return pl.pallas_call(
        functools.partial(_final_body, d_shared=ws_d.shape[0]),
        grid=(n // tm,),
        in_specs=[pl.BlockSpec((tm, d), row), pl.BlockSpec((tm, SUBLANES), row),
                  pl.BlockSpec((1, 1, d), _mod_spec(5, tiles_per_batch)),
                  _resident(ws_gu.shape), _resident(ws_d.shape), _resident(final_g.shape),
                  any_spec, any_spec],
        out_specs=pl.BlockSpec((tm, d), row),
        out_shape=jax.ShapeDtypeStruct((n, d), F32),
        scratch_shapes=[pltpu.VMEM((2, tm, d // 2), I32), pltpu.VMEM((2, TOP_K, tm, d // 2), I32),
                        pltpu.SemaphoreType.DMA((2,))],
        compiler_params=_params(1),
        name="shared_combine_final",
    )(x1, gates, mod, ws_gu, ws_d, final_g, h_rows, y_tok)


def _routing_tables(idx, rank, counts, n_tok):
    nb_max = -(-(n_tok * TOP_K) // MOE_BLOCK) + N_EXPERTS
    nblk = (counts + MOE_BLOCK - 1) // MOE_BLOCK
    blk_end = jnp.cumsum(nblk)
    blk_start = blk_end - nblk
    experts = jnp.arange(N_EXPERTS, dtype=I32)[:, None, None]
    first_slot = (blk_start * MOE_BLOCK)[:, None, None]
    dest = jnp.sum(jnp.where(idx[None] == experts, first_slot, 0), axis=0) + rank
    dest_chunks = dest.reshape(TOP_K, n_tok // SC_CHUNK, SC_CHUNK).transpose(1, 0, 2)

    blocks = jnp.arange(nb_max, dtype=I32)[:, None]
    member = (blk_start[None, :] <= blocks) & (blocks < blk_end[None, :])
    lookup = lambda table: jnp.sum(jnp.where(member, table[None, :], 0), axis=1)
    blocks = blocks[:, 0]
    nonempty = nblk > 0
    n_visited = jnp.sum(nonempty.astype(I32))
    ordinal_of = jnp.cumsum(nonempty.astype(I32)) - 1
    slots = jnp.arange(N_EXPERTS, dtype=I32)
    order = jnp.sum(jnp.where(nonempty[None, :] & (ordinal_of[None, :] == slots[:, None]),
                              slots[None, :], 0), axis=1)
    ordinal = lookup(ordinal_of)
    k_in_e = blocks - lookup(blk_start)
    nb_e = jnp.maximum(lookup(nblk), 1)
    live = (ordinal + 1 < n_visited) & (blocks < blk_end[-1])
    first = WEIGHT_PARTS * (ordinal + 1)
    lo = jnp.where(live, first + WEIGHT_PARTS * k_in_e // nb_e, 0)
    hi = jnp.where(live, first + WEIGHT_PARTS * (k_in_e + 1) // nb_e, 0)
    tables = (ordinal.astype(I32), order, lo.astype(I32), hi.astype(I32),
              (WEIGHT_PARTS * n_visited).reshape(1).astype(I32), blk_end[-1:].astype(I32))
    return tables, dest_chunks, nb_max * MOE_BLOCK


def kernel(x, c, ctx, c_ctx, norm1_g, norm2_g, w_ada, b_ada, w_in, conv_w, gate_b, head_g, w_out,
           w_router, b_router, we_gate, we_up, we_down, ws_gate, ws_up, ws_down, final_g):
    assert w_ada.shape[0] == 1, "single-layer block"
    bsz, seq, d = x.shape
    ctx_len = ctx.shape[1]
    n_tok = bsz * seq
    conv_dim = conv_w.shape[2]
    v_all = head_g.shape[1]
    qk_all = (w_in.shape[2] - 3 * conv_dim - 2 * v_all - N_GATES) // 2
    assert seq % ROW_TILE == 0 and ctx_len % ROW_TILE == 0 and ROW_TILE % GRID_W == 0
    assert bsz + 1 <= SUBLANES

    cc = jnp.zeros((SUBLANES, d), F32).at[:bsz].set(c).at[bsz].set(c_ctx)
    mod = _adaln(cc, w_ada[0], b_ada).reshape(SUBLANES * 6, 1, d)

    n_main = 3 * conv_dim + 2 * qk_all + 2 * v_all
    w_all = w_in[0].astype(BF16)
    k_lo = 3 * conv_dim + qk_all
    w_kt = w_all[:, k_lo:k_lo + qk_all].T
    w_v = w_all[:, k_lo + qk_all:k_lo + qk_all + v_all]
    w_gate = jnp.pad(w_all[:, n_main:], ((0, 0), (0, LANES - N_GATES)))
    gate_bias = jnp.zeros((1, LANES), F32).at[0, :N_GATES].set(gate_b[0].reshape(-1))

    x2d = x.reshape(n_tok, d)
    conv, q, kt, v, og, g, gt = _inproj(x2d, mod, norm1_g, w_all, w_kt, w_gate, gate_bias, conv_w[0],
                                       seq, conv_dim, qk_all, v_all)
    ktc, vc, _, gtc = _inproj_ctx(ctx.reshape(bsz * ctx_len, d), mod, norm1_g, w_v, w_kt, w_gate,
                                  gate_bias, bsz)

    h_bwd = _mlstm(1, q, kt, v, g, gt, ktc, vc, gtc, None, bsz)
    ml = _mlstm(0, q, kt, v, g, gt, ktc, vc, gtc, (h_bwd, og), bsz, head_g)

    assert 2 * N_EXPERTS == LANES
    w_r_hi = w_router[0].astype(BF16)
    w_r = jnp.concatenate([w_r_hi, (w_router[0] - w_r_hi.astype(F32)).astype(BF16)], axis=1)
    b_r = jnp.zeros((1, LANES), F32).at[0, :N_EXPERTS].set(b_router[0])
    x1, idx, gates, rank, cnt, h_rows = _outproj(conv, ml, x2d, mod, norm2_g, w_out[0].astype(BF16),
                                             w_r, b_r, seq)

    tables, dest_chunks, n_slots = _routing_tables(idx[:TOP_K], rank[:TOP_K], cnt[0, :N_EXPERTS], n_tok)
    x_sorted = _sc_dispatch(h_rows, dest_chunks, n_slots)
    y_sorted = _moe(x_sorted, we_gate[0], we_up[0], we_down[0], tables)
    y_tok = _sc_combine(y_sorted, dest_chunks, n_tok)

    ws_gu = jnp.concatenate([ws_gate[0], ws_up[0]], axis=1).astype(BF16)
    out = _final(h_rows, x1, y_tok, gates, mod, ws_gu, ws_down[0].astype(BF16),
                 final_g.reshape(1, d), seq)
    return out.reshape(bsz, seq, d)
```

```python
import functools

import jax
import jax.numpy as jnp
from jax import lax
from jax.experimental import pallas as pl
from jax.experimental.pallas import tpu as pltpu
from jax.experimental.pallas import tpu_sc as plsc

F32 = jnp.float32
BF16 = jnp.bfloat16
I32 = jnp.int32

N_HEADS = 4
GRID_W = 64
CHUNK = 128
TOP_K = 6
N_EXPERTS = 64
ROUTED_SCALE = 2.446
EPS = 1e-6
N_GATES = 4 * N_HEADS
GATE_COLS = 6 * N_HEADS

LANES = 128
SUBLANES = 8
MOE_BLOCK = 256
ROW_TILE = 256
ADALN_TILE = 1024
WEIGHT_PARTS = 8
PART_SHIFT = 3
WEIGHT_RING = 3
SC_CHUNK = 64
HIGH_HALF = -65536
VMEM_LIMIT = 56 * 1024 * 1024
MOE_VMEM_LIMIT = 62 * 1024 * 1024

_HIGHEST = lax.Precision.HIGHEST
_NEG_INF = float("-inf")
assert CHUNK == LANES


def _resident(shape):
    nd = len(shape)
    return pl.BlockSpec(shape, lambda *_: (0,) * nd, pipeline_mode=pl.Buffered(1))


def _params(n_axes):
    return pltpu.CompilerParams(
        dimension_semantics=("arbitrary",) * n_axes, vmem_limit_bytes=VMEM_LIMIT)


def _log_sigmoid(x):
    return jnp.minimum(x, 0.0) - jnp.log1p(jnp.exp(-jnp.abs(x)))


def _silu(x):
    return x * jax.nn.sigmoid(x)


def _pack_words(val):
    half = val.shape[1] // 2
    lo = lax.bitcast_convert_type(val[:, :half].astype(BF16).astype(F32), I32)
    hi = lax.bitcast_convert_type(val[:, half:].astype(BF16).astype(F32), I32)
    return (hi & HIGH_HALF) | lax.shift_right_logical(lo, 16)


def _unpack_words(word):
    lo = lax.bitcast_convert_type(lax.shift_left(word, 16), F32)
    hi = lax.bitcast_convert_type(word & HIGH_HALF, F32)
    return lo, hi


def _row_tile_copies(hbm_rows, row0, tile, sem, to_hbm):
    n = tile.shape[0]
    copies = []
    for c in range(SUBLANES):
        hbm = hbm_rows.at[pl.ds(row0, n), c, :]
        vmem = tile.at[:, pl.ds(c * LANES, LANES)]
        copies.append(pltpu.make_async_copy(vmem, hbm, sem) if to_hbm
                      else pltpu.make_async_copy(hbm, vmem, sem))
    return copies


def _start_all(copies):
    for cp in copies:
        cp.start()


def _wait_all(copies):
    for cp in copies:
        cp.wait()


def _adaln_body(c_ref, w_ref, b_ref, o_ref):
    s = _silu(c_ref[...])
    o_ref[...] = jnp.dot(s.astype(BF16), w_ref[...].astype(BF16),
                         preferred_element_type=F32) + b_ref[...]


def _adaln(cc, w, b):
    d, n6 = w.shape
    return pl.pallas_call(
        _adaln_body,
        grid=(n6 // ADALN_TILE,),
        in_specs=[pl.BlockSpec((SUBLANES, d), lambda j: (0, 0)),
                  pl.BlockSpec((d, ADALN_TILE), lambda j: (0, j)),
                  pl.BlockSpec((1, ADALN_TILE), lambda j: (0, j))],
        out_specs=pl.BlockSpec((SUBLANES, ADALN_TILE), lambda j: (0, j)),
        out_shape=jax.ShapeDtypeStruct((SUBLANES, n6), F32),
        compiler_params=_params(1),
        name="adaln",
    )(cc, w, b)


def _norm_mod(x, g, shift, scale):
    y = x * lax.rsqrt(jnp.mean(x * x, axis=-1, keepdims=True) + EPS) * g
    return y * (1.0 + scale) + shift


def _gate_prep(xb, wg, gb_ref, g_ref, gt_ref):
    tm = xb.shape[0]
    gg = jnp.dot(xb, wg, preferred_element_type=F32) + gb_ref[...]
    lane = lax.broadcasted_iota(I32, (tm, LANES), 1)
    is_f = (lane & N_HEADS) != 0
    is_bwd = (lane & (2 * N_HEADS)) != 0
    lf = jnp.where(is_f, _log_sigmoid(gg), 0.0)
    r = lax.broadcasted_iota(I32, (tm, tm), 0)
    c = lax.broadcasted_iota(I32, (tm, tm), 1)
    same = (r // CHUNK) == (c // CHUNK)
    tri_l = jnp.where(same & (c <= r), 1.0, 0.0).astype(F32)
    tri_u = jnp.where(same & (c >= r), 1.0, 0.0).astype(F32)
    pre = jnp.dot(tri_l, lf, precision=_HIGHEST, preferred_element_type=F32)
    suf = jnp.dot(tri_u, lf, precision=_HIGHEST, preferred_element_type=F32)
    out = jnp.where(is_f, jnp.where(is_bwd, suf, pre), gg)
    diff = out - pltpu.roll(out, LANES - N_HEADS, axis=1)
    pos = lax.broadcasted_iota(I32, (tm, 1), 0) % CHUNK
    run_f, run_b = diff, diff
    k = 1
    while k < CHUNK:
        run_f = jnp.maximum(run_f, jnp.where(pos >= k, pltpu.roll(run_f, k, axis=0), _NEG_INF))
        run_b = jnp.maximum(run_b, jnp.where(pos < CHUNK - k, pltpu.roll(run_b, tm - k, axis=0), _NEG_INF))
        k *= 2
    run = jnp.where(is_bwd, run_b, run_f)
    fwd_lanes = (lane >= N_GATES) & (lane < N_GATES + N_HEADS)
    bwd_lanes = (lane >= N_GATES + N_HEADS) & (lane < GATE_COLS)
    out = jnp.where(fwd_lanes, pltpu.roll(run, N_GATES, axis=1),
                    jnp.where(bwd_lanes, pltpu.roll(run, N_GATES - N_HEADS, axis=1), out))
    g_ref[...] = out[:, :GATE_COLS]
    gt_ref[...] = out.T[:GATE_COLS, :]


def _project_transposed(wt_ref, xb):
    return lax.dot_general(wt_ref[...], xb, (((1,), (1,)), ((), ())),
                           preferred_element_type=F32).astype(BF16)


def _inproj_body(x_ref, sh_ref, sc_ref, g1_ref, w_ref, wkt_ref, wg_ref, gb_ref, cw_ref,
                 conv_ref, q_ref, k_ref, v_ref, o_ref, g_ref, gt_ref, *, conv_dim, qk_all, v_all):
    tm = x_ref.shape[0]
    xb = _norm_mod(x_ref[...], g1_ref[...], sh_ref[0], sc_ref[0]).astype(BF16)

    def proj(lo, width):
        return jnp.dot(xb, w_ref[:, lo:lo + width], preferred_element_type=F32)

    u = proj(conv_dim, conv_dim) * proj(2 * conv_dim, conv_dim)
    pos = lax.broadcasted_iota(I32, (tm, 1), 0) % GRID_W
    um = jnp.where(pos == 0, 0.0, pltpu.roll(u, 1, axis=0))
    up = jnp.where(pos == GRID_W - 1, 0.0, pltpu.roll(u, tm - 1, axis=0))
    y = um * cw_ref[0:1, :] + u * cw_ref[1:2, :] + up * cw_ref[2:3, :]
    conv_ref[...] = (proj(0, conv_dim) * y).astype(BF16)

    off = 3 * conv_dim
    qscale = (qk_all // N_HEADS) ** -0.5
    q_ref[...] = (proj(off, qk_all) * qscale).astype(BF16)
    k_ref[...] = _project_transposed(wkt_ref, xb)
    v_ref[...] = proj(off + 2 * qk_all, v_all).astype(BF16)
    o_ref[...] = jax.nn.sigmoid(proj(off + 2 * qk_all + v_all, v_all)).astype(BF16)
    _gate_prep(xb, wg_ref[...], gb_ref, g_ref, gt_ref)


def _inproj_ctx_body(x_ref, sh_ref, sc_ref, g1_ref, w_ref, wkt_ref, wg_ref, gb_ref,
                     k_ref, v_ref, g_ref, gt_ref):
    xb = _norm_mod(x_ref[...], g1_ref[...], sh_ref[0], sc_ref[0]).astype(BF16)
    k_ref[...] = _project_transposed(wkt_ref, xb)
    v_ref[...] = jnp.dot(xb, w_ref[...], preferred_element_type=F32).astype(BF16)
    _gate_prep(xb, wg_ref[...], gb_ref, g_ref, gt_ref)


def _mod_spec(part, tiles_per_row, fixed_row=None):
    def index(i):
        row = fixed_row if fixed_row is not None else i // tiles_per_row
        return (row * 6 + part, 0, 0)

    return index


def _inproj(x2d, mod, g1, w_all, w_kt, w_gate, gate_b, conv_w, rows_per_batch, conv_dim, qk_all, v_all):
    n, d = x2d.shape
    tm = ROW_TILE
    tiles_per_batch = rows_per_batch // tm
    row = lambda i: (i, 0)
    mod_block = (1, 1, d)
    out_shapes = (
        jax.ShapeDtypeStruct((n, conv_dim), BF16),
        jax.ShapeDtypeStruct((n, qk_all), BF16),
        jax.ShapeDtypeStruct((qk_all, n), BF16),
        jax.ShapeDtypeStruct((n, v_all), BF16),
        jax.ShapeDtypeStruct((n, v_all), BF16),
        jax.ShapeDtypeStruct((n, GATE_COLS), F32),
        jax.ShapeDtypeStruct((GATE_COLS, n), F32),
    )
    out_specs = (
        pl.BlockSpec((tm, conv_dim), row),
        pl.BlockSpec((tm, qk_all), row),
        pl.BlockSpec((qk_all, tm), lambda i: (0, i)),
        pl.BlockSpec((tm, v_all), row),
        pl.BlockSpec((tm, v_all), row),
        pl.BlockSpec((tm, GATE_COLS), row),
        pl.BlockSpec((GATE_COLS, tm), lambda i: (0, i)),
    )
    return pl.pallas_call(
        functools.partial(_inproj_body, conv_dim=conv_dim, qk_all=qk_all, v_all=v_all),
        grid=(n // tm,),
        in_specs=[pl.BlockSpec((tm, d), row),
                  pl.BlockSpec(mod_block, _mod_spec(0, tiles_per_batch)),
                  pl.BlockSpec(mod_block, _mod_spec(1, tiles_per_batch)),
                  _resident(g1.shape), _resident(w_all.shape), _resident(w_kt.shape),
                  _resident(w_gate.shape), _resident(gate_b.shape), _resident(conv_w.shape)],
        out_specs=out_specs,
        out_shape=out_shapes,
        compiler_params=_params(1),
        name="inproj",
    )(x2d, mod, mod, g1, w_all, w_kt, w_gate, gate_b, conv_w)


def _inproj_ctx(c2d, mod, g1, w_v, w_kt, w_gate, gate_b, ctx_mod_row):
    n, d = c2d.shape
    tm = ROW_TILE
    row = lambda i: (i, 0)
    mod_block = (1, 1, d)
    qk_all, v_all = w_kt.shape[0], w_v.shape[1]
    return pl.pallas_call(
        _inproj_ctx_body,
        grid=(n // tm,),
        in_specs=[pl.BlockSpec((tm, d), row),
                  pl.BlockSpec(mod_block, _mod_spec(0, 1, ctx_mod_row)),
                  pl.BlockSpec(mod_block, _mod_spec(1, 1, ctx_mod_row)),
                  _resident(g1.shape), _resident(w_v.shape), _resident(w_kt.shape),
                  _resident(w_gate.shape), _resident(gate_b.shape)],
        out_specs=(pl.BlockSpec((qk_all, tm), lambda i: (0, i)), pl.BlockSpec((tm, v_all), row),
                   pl.BlockSpec((tm, GATE_COLS), row), pl.BlockSpec((GATE_COLS, tm), lambda i: (0, i))),
        out_shape=(jax.ShapeDtypeStruct((qk_all, n), BF16), jax.ShapeDtypeStruct((n, v_all), BF16),
                   jax.ShapeDtypeStruct((n, GATE_COLS), F32), jax.ShapeDtypeStruct((GATE_COLS, n), F32)),
        compiler_params=_params(1),
        name="inproj_ctx",
    )(c2d, mod, mod, g1, w_v, w_kt, w_gate, gate_b)


def _with_ones(v):
    return jnp.concatenate([v, jnp.ones((v.shape[0], LANES), v.dtype)], axis=1)


def _mlstm_state_update(h, direction, kt_ref, v_ref, gt_ref, s_ref, m_ref, qk, vh):
    ci = direction * 2 * N_HEADS + h
    cb = ci + N_HEADS
    last = 0 if direction else CHUNK - 1
    kt = kt_ref[h * qk:(h + 1) * qk, :].astype(F32)
    va = _with_ones(v_ref[:, h * vh:(h + 1) * vh])
    b_last = gt_ref[cb:cb + 1, last:last + 1]
    m_prev = m_ref[h][0:1, 0:1]
    g_r = b_last - gt_ref[cb:cb + 1, :] + gt_ref[ci:ci + 1, :]
    cm = N_GATES + direction * N_HEADS + h
    m_new = b_last + jnp.maximum(m_prev, gt_ref[cm:cm + 1, last:last + 1])
    a = jnp.exp(b_last + m_prev - m_new)
    kw = (kt * jnp.exp(g_r - m_new)).astype(BF16)
    s_ref[h] = a * s_ref[h] + jnp.dot(kw, va, preferred_element_type=F32)
    m_ref[h] = jnp.broadcast_to(m_new, m_ref.shape[1:])


def _mlstm_head_output(h, direction, q_ref, kt_ref, v_ref, g_ref, gt_ref, s_ref, m_ref, qk, vh):
    ci = direction * 2 * N_HEADS + h
    cb = ci + N_HEADS
    q = q_ref[:, h * qk:(h + 1) * qk]
    kt = kt_ref[h * qk:(h + 1) * qk, :]
    va = _with_ones(v_ref[:, h * vh:(h + 1) * vh])
    ig_r = gt_ref[ci:ci + 1, :]
    b_r = gt_ref[cb:cb + 1, :]
    cm = N_GATES + direction * N_HEADS + h
    m_prev = m_ref[h][0:1, :]
    b_rep = jnp.broadcast_to(g_ref[:, cb:cb + 1], (CHUNK, LANES))
    run_rep = jnp.broadcast_to(g_ref[:, cm:cm + 1], (CHUNK, LANES))
    row = lax.broadcasted_iota(I32, (CHUNK, CHUNK), 0)
    col = lax.broadcasted_iota(I32, (CHUNK, CHUNK), 1)
    mask = (col >= row) if direction else (col <= row)
    dm = jnp.where(mask, b_rep + (ig_r - b_r), _NEG_INF)
    inter = b_rep + m_prev
    m_t = jnp.maximum(inter, b_rep + run_rep)
    w_inter = jnp.exp(inter - m_t)
    s = jnp.dot(q, kt, preferred_element_type=F32) * jnp.exp(dm - m_t)
    intra = jnp.dot(s.astype(BF16), va, preferred_element_type=F32)
    carried = jnp.dot(q, s_ref[h].astype(BF16), preferred_element_type=F32)
    den = intra[:, vh:vh + LANES] + w_inter * carried[:, vh:vh + LANES]
    scale = 1.0 / jnp.maximum(jnp.abs(den), jnp.exp(-m_t))
    return jnp.concatenate(
        [(intra[:, j:j + LANES] + w_inter * carried[:, j:j + LANES]) * scale for j in range(0, vh, LANES)],
        axis=1)


def _mlstm_body(*refs, direction, bsz, n_ctx_chunks, qk, vh):
    q_ref, v_ref, g_ref, vc_ref = refs[0:4]
    kt_refs, gt_refs = refs[4:4 + bsz], refs[4 + bsz:4 + 2 * bsz]
    ktc_refs, gtc_refs = refs[4 + 2 * bsz:4 + 3 * bsz], refs[4 + 3 * bsz:4 + 4 * bsz]
    rest = refs[4 + 4 * bsz:]
    if direction:
        out_ref, s_ref, m_ref = rest
    else:
        hb_ref, og_ref, hg_ref, out_ref, s_ref, m_ref = rest
    step = pl.program_id(0)

    @pl.when(step == 0)
    def _():
        s_ref[...] = jnp.zeros_like(s_ref)
        m_ref[...] = jnp.full_like(m_ref, _NEG_INF)

    @pl.when(step < n_ctx_chunks)
    def _():
        for b in range(bsz):
            for h in range(N_HEADS):
                _mlstm_state_update(h, direction, ktc_refs[b], vc_ref.at[b], gtc_refs[b],
                                    s_ref.at[b], m_ref.at[b], qk, vh)

    @pl.when(step >= n_ctx_chunks)
    def _():
        for b in range(bsz):
            for h in range(N_HEADS):
                hh = _mlstm_head_output(h, direction, q_ref.at[b], kt_refs[b], v_ref.at[b], g_ref.at[b],
                                        gt_refs[b], s_ref.at[b], m_ref.at[b], qk, vh)
                cols = slice(h * vh, (h + 1) * vh)
                if direction:
                    out_ref[b, :, cols] = hh
                else:
                    hs = hh + hb_ref[b, :, cols]
                    hs = hs * lax.rsqrt(jnp.mean(hs * hs, axis=-1, keepdims=True) + EPS)
                    out_ref[b, :, cols] = (hs * hg_ref[:, cols]
                                           * og_ref[b, :, cols].astype(F32)).astype(BF16)
                _mlstm_state_update(h, direction, kt_refs[b], v_ref.at[b], gt_refs[b],
                                    s_ref.at[b], m_ref.at[b], qk, vh)


def _mlstm(direction, q, kt, v, g, gt, ktc, vc, gtc, extra, bsz, head_g=None):
    n, qk_all = q.shape
    v_all = v.shape[1]
    qk, vh = qk_all // N_HEADS, v_all // N_HEADS
    seq = n // bsz
    nc = seq // CHUNK
    ncc = vc.shape[0] // bsz // CHUNK

    def lat(s):
        j = jnp.clip(s - ncc, 0, nc - 1)
        return nc - 1 - j if direction else j

    def ctx(s):
        j = jnp.clip(s, 0, ncc - 1)
        return ncc - 1 - j if direction else j

    def per_batch(a):
        return a.reshape(bsz, a.shape[0] // bsz, a.shape[1])

    lat_blk = lambda c: pl.BlockSpec((bsz, CHUNK, c), lambda s: (0, lat(s), 0))
    in_specs = [lat_blk(qk_all), lat_blk(v_all), lat_blk(GATE_COLS),
                pl.BlockSpec((bsz, CHUNK, v_all), lambda s: (0, ctx(s), 0))]
    args = [per_batch(q), per_batch(v), per_batch(g), per_batch(vc)]
    for arr, rows, n_chunks, pos in ((kt, qk_all, nc, lat), (gt, GATE_COLS, nc, lat),
                                     (ktc, qk_all, ncc, ctx), (gtc, GATE_COLS, ncc, ctx)):
        for b in range(bsz):
            in_specs.append(pl.BlockSpec((rows, CHUNK), lambda s, b=b, n_chunks=n_chunks, pos=pos:
                                         (0, b * n_chunks + pos(s))))
            args.append(arr)
    if direction:
        out_dtype = F32
    else:
        hb, og = extra
        in_specs += [lat_blk(v_all), lat_blk(v_all), pl.BlockSpec((1, v_all), lambda s: (0, 0))]
        args += [per_batch(hb), per_batch(og), head_g]
        out_dtype = BF16
    out = pl.pallas_call(
        functools.partial(_mlstm_body, direction=direction, bsz=bsz, n_ctx_chunks=ncc, qk=qk, vh=vh),
        grid=(ncc + nc,),
        in_specs=in_specs,
        out_specs=lat_blk(v_all),
        out_shape=jax.ShapeDtypeStruct((bsz, seq, v_all), out_dtype),
        scratch_shapes=[pltpu.VMEM((bsz, N_HEADS, qk, vh + LANES), F32),
                        pltpu.VMEM((bsz, N_HEADS, SUBLANES, LANES), F32)],
        compiler_params=_params(1),
        name="mlstm_bwd" if direction else "mlstm_fwd",
    )(*args)
    return out.reshape(n, v_all)


def _outproj_body(conv_ref, ml_ref, x_ref, gt1_ref, sh2_ref, sc2_ref, g2_ref, wo_ref, wr_ref, br_ref,
                  x1_ref, idx_ref, gate_ref, rank_ref, cnt_ref, h_hbm, carry_ref, hw, hsem):
    tm = x_ref.shape[0]
    half = conv_ref.shape[1]
    step = pl.program_id(0)
    buf = step % 2

    def h_out(i, s):
        return _row_tile_copies(h_hbm, i * tm, hw.at[s], hsem.at[s], to_hbm=True)

    @pl.when(step == 0)
    def _():
        carry_ref[...] = jnp.zeros_like(carry_ref)

    @pl.when(step >= 2)
    def _():
        _wait_all(h_out(step - 2, buf))

    y = (jnp.dot(conv_ref[...], wo_ref[0:half, :], preferred_element_type=F32)
         + jnp.dot(ml_ref[...], wo_ref[half:2 * half, :], preferred_element_type=F32))
    x1 = x_ref[...] + gt1_ref[0] * y
    x1_ref[...] = x1
    hn = _norm_mod(x1, g2_ref[...], sh2_ref[0], sc2_ref[0])
    hw[buf] = _pack_words(hn)
    _start_all(h_out(step, buf))

    h_hi = hn.astype(BF16)
    h_lo = (hn - h_hi.astype(F32)).astype(BF16)
    parts = (jnp.dot(h_hi, wr_ref[...], preferred_element_type=F32)
             + jnp.dot(h_lo, wr_ref[...], preferred_element_type=F32))
    scores = jax.nn.sigmoid(parts + pltpu.roll(parts, N_EXPERTS, axis=1))
    lane = lax.broadcasted_iota(I32, (tm, LANES), 1).astype(F32)
    biased = jnp.where(lane < N_EXPERTS, scores + br_ref[...], _NEG_INF)
    onehot = jnp.zeros((tm, LANES), F32)
    picks, sels = [], []
    for _ in range(TOP_K):
        mx = jnp.max(biased, axis=1, keepdims=True)
        pick = jnp.min(jnp.where(biased == mx, lane, float(LANES)), axis=1, keepdims=True)
        hit = lane == pick
        sels.append(jnp.sum(jnp.where(hit, scores, 0.0), axis=1, keepdims=True))
        picks.append(pick)
        biased = jnp.where(hit, _NEG_INF, biased)
        onehot = onehot + hit.astype(F32)
    total = sels[0]
    for s in sels[1:]:
        total = total + s

    r = lax.broadcasted_iota(I32, (tm, tm), 0)
    c = lax.broadcasted_iota(I32, (tm, tm), 1)
    strict = jnp.where(c < r, 1.0, 0.0).astype(BF16)
    before = jnp.dot(strict, onehot.astype(BF16), preferred_element_type=F32) + carry_ref[...]
    slot = lax.broadcasted_iota(I32, (tm, SUBLANES), 1)
    idx_out = jnp.zeros((tm, LANES), F32)
    rank_out = jnp.zeros((tm, LANES), F32)
    gate_out = jnp.zeros((tm, SUBLANES), F32)
    for j in range(TOP_K):
        rank = jnp.sum(jnp.where(lane == picks[j], before, 0.0), axis=1, keepdims=True)
        idx_out = jnp.where(lane == float(j), picks[j], idx_out)
        rank_out = jnp.where(lane == float(j), rank, rank_out)
        gate_out = jnp.where(slot == j, sels[j] / total * ROUTED_SCALE, gate_out)
    idx_ref[...] = idx_out.T[:SUBLANES, :].astype(I32)
    rank_ref[...] = rank_out.T[:SUBLANES, :].astype(I32)
    gate_ref[...] = gate_out
    carry_ref[...] = carry_ref[...] + jnp.sum(onehot, axis=0, keepdims=True)
    cnt_ref[...] = jnp.broadcast_to(carry_ref[...], cnt_ref.shape).astype(I32)

    @pl.when(step == pl.num_programs(0) - 1)
    def _():
        @pl.when(step >= 1)
        def _():
            _wait_all(h_out(step - 1, 1 - buf))
        _wait_all(h_out(step, buf))


def _outproj(conv, ml, x2d, mod, g2, w_out, w_router, b_router, rows_per_batch):
    n, d = x2d.shape
    tm = ROW_TILE
    tiles_per_batch = rows_per_batch // tm
    row = lambda i: (i, 0)
    mod_block = (1, 1, d)
    half = conv.shape[1]
    return pl.pallas_call(
        _outproj_body,
        grid=(n // tm,),
        in_specs=[pl.BlockSpec((tm, half), row), pl.BlockSpec((tm, half), row), pl.BlockSpec((tm, d), row),
                  pl.BlockSpec(mod_block, _mod_spec(2, tiles_per_batch)),
                  pl.BlockSpec(mod_block, _mod_spec(3, tiles_per_batch)),
                  pl.BlockSpec(mod_block, _mod_spec(4, tiles_per_batch)),
                  _resident(g2.shape), _resident(w_out.shape), _resident(w_router.shape),
                  _resident(b_router.shape)],
        out_specs=(pl.BlockSpec((tm, d), row),
                   pl.BlockSpec((SUBLANES, tm), lambda i: (0, i)), pl.BlockSpec((tm, SUBLANES), row),
                   pl.BlockSpec((SUBLANES, tm), lambda i: (0, i)),
                   pl.BlockSpec((SUBLANES, LANES), lambda i: (0, 0)),
                   pl.BlockSpec(memory_space=pl.ANY)),
        out_shape=(jax.ShapeDtypeStruct((n, d), F32),
                   jax.ShapeDtypeStruct((SUBLANES, n), I32), jax.ShapeDtypeStruct((n, SUBLANES), F32),
                   jax.ShapeDtypeStruct((SUBLANES, n), I32),
                   jax.ShapeDtypeStruct((SUBLANES, LANES), I32),
                   jax.ShapeDtypeStruct((n, SUBLANES, LANES), I32)),
        scratch_shapes=[pltpu.VMEM((1, LANES), F32), pltpu.VMEM((2, tm, d // 2), I32),
                        pltpu.SemaphoreType.DMA((2,))],
        compiler_params=_params(1),
        name="outproj_router",
    )(conv, ml, x2d, mod, mod, mod, g2, w_out, w_router, b_router)


def _sc_workers():
    info = plsc.get_sparse_core_info()
    return info.num_cores, info.num_cores * info.num_subcores


def _sc_dispatch(h_rows, dest_chunks, n_slots):
    n_tok = h_rows.shape[0]
    n_cores, n_workers = _sc_workers()
    per_worker = n_tok // (n_workers * SC_CHUNK)
    assert per_worker * n_workers * SC_CHUNK == n_tok
    mesh = plsc.VectorSubcoreMesh(core_axis_name="c", subcore_axis_name="s")

    @functools.partial(
        pl.kernel, mesh=mesh,
        out_type=jax.ShapeDtypeStruct((n_slots,) + h_rows.shape[1:], h_rows.dtype),
        scratch_types=[pltpu.VMEM((TOP_K, SC_CHUNK), I32),
                       pltpu.VMEM((SC_CHUNK,) + h_rows.shape[1:], h_rows.dtype)],
    )
    def dispatch(h_hbm, dest_hbm, out_hbm, idx_v, rows_v):
        wid = lax.axis_index("s") * n_cores + lax.axis_index("c")

        @pl.loop(0, per_worker)
        def _(i):
            chunk = wid * per_worker + i
            pltpu.sync_copy(dest_hbm.at[chunk], idx_v)
            pltpu.sync_copy(h_hbm.at[pl.ds(chunk * SC_CHUNK, SC_CHUNK)], rows_v)
            for k in range(TOP_K):
                pltpu.sync_copy(rows_v, out_hbm.at[idx_v.at[k]])

    return dispatch(h_rows, dest_chunks)


def _sc_combine(y_sorted, dest_chunks, n_tok):
    n_cores, n_workers = _sc_workers()
    per_worker = n_tok // (n_workers * SC_CHUNK)
    assert per_worker * n_workers * SC_CHUNK == n_tok
    mesh = plsc.VectorSubcoreMesh(core_axis_name="c", subcore_axis_name="s")

    @functools.partial(
        pl.kernel, mesh=mesh,
        out_type=jax.ShapeDtypeStruct((TOP_K, n_tok) + y_sorted.shape[1:], y_sorted.dtype),
        scratch_types=[pltpu.VMEM((TOP_K, SC_CHUNK), I32),
                       pltpu.VMEM((SC_CHUNK,) + y_sorted.shape[1:], y_sorted.dtype)],
    )
    def combine(y_hbm, dest_hbm, out_hbm, idx_v, rows_v):
        wid = lax.axis_index("s") * n_cores + lax.axis_index("c")

        @pl.loop(0, per_worker)
        def _(i):
            chunk = wid * per_worker + i
            pltpu.sync_copy(dest_hbm.at[chunk], idx_v)
            for k in range(TOP_K):
                pltpu.sync_copy(y_hbm.at[idx_v.at[k]], rows_v)
                pltpu.sync_copy(rows_v, out_hbm.at[k, pl.ds(chunk * SC_CHUNK, SC_CHUNK)])

    return combine(y_sorted, dest_chunks)


def _moe_body(ord_ref, order_ref, glo_ref, ghi_ref, tot_ref, nb_ref,
              x_hbm, wg_hbm, wu_hbm, wd_hbm, y_hbm,
              wgu, wd, stage_a, stage_d, xw, yw, wsem, xsem, ysem, *, d_expert):
    b = pl.program_id(0)
    nb = nb_ref[0]
    total = tot_ref[0]
    d_model = wgu.shape[1]
    rows_a = d_model // WEIGHT_PARTS
    rows_d = d_expert // WEIGHT_PARTS

    def part_copies(g):
        e = order_ref[lax.shift_right_logical(g, PART_SHIFT)]
        i = g & (WEIGHT_PARTS - 1)
        s = lax.rem(g, WEIGHT_RING)
        return (pltpu.make_async_copy(wg_hbm.at[e, pl.ds(i * rows_a, rows_a)], stage_a.at[s, 0],
                                      wsem.at[s, 0]),
                pltpu.make_async_copy(wu_hbm.at[e, pl.ds(i * rows_a, rows_a)], stage_a.at[s, 1],
                                      wsem.at[s, 1]),
                pltpu.make_async_copy(wd_hbm.at[e, pl.ds(i * rows_d, rows_d)], stage_d.at[s],
                                      wsem.at[s, 2]))

    def start_part(g):
        for cp in part_copies(g):
            cp.start(priority=1)

    def wait_part(g):
        for cp in part_copies(g):
            cp.wait()

    def cast_part(g):
        i = g & (WEIGHT_PARTS - 1)
        s = lax.rem(g, WEIGHT_RING)
        par = lax.shift_right_logical(g, PART_SHIFT) & 1
        ra = pl.multiple_of(i * rows_a, rows_a)
        rd = pl.multiple_of(i * rows_d, rows_d)
        wgu[par, pl.ds(ra, rows_a), 0:d_expert] = stage_a[s, 0].astype(BF16)
        wgu[par, pl.ds(ra, rows_a), d_expert:2 * d_expert] = stage_a[s, 1].astype(BF16)
        wd[par, pl.ds(rd, rows_d), :] = stage_d[s].astype(BF16)

    def refill(g):
        @pl.when(g + WEIGHT_RING < total)
        def _():
            start_part(g + WEIGHT_RING)

    def cast_parts(lo, hi):
        def body(g, carry):
            wait_part(g)
            cast_part(g)
            refill(g)
            return carry
        lax.fori_loop(lo, hi, body, 0)

    slot = b % 2

    def x_in(blk, s):
        return _row_tile_copies(x_hbm, blk * MOE_BLOCK, xw.at[s], xsem.at[s], to_hbm=False)

    def y_out(blk, s):
        return _row_tile_copies(y_hbm, blk * MOE_BLOCK, yw.at[s], ysem.at[s], to_hbm=True)

    @pl.when(b == 0)
    def _():
        _start_all(x_in(0, 0))
        for g in range(WEIGHT_RING):
            start_part(g)
        cast_parts(0, WEIGHT_PARTS)

    @pl.when(b + 1 < nb)
    def _():
        _start_all(x_in(b + 1, 1 - slot))

    @pl.when(b < nb)
    def _():
        par = ord_ref[b] & 1
        _wait_all(x_in(b, slot))

        @pl.when(b >= 2)
        def _():
            _wait_all(y_out(b - 2, slot))

        x = jnp.concatenate(_unpack_words(xw[slot]), axis=1).astype(BF16)
        gu = jnp.dot(x, wgu[par], preferred_element_type=F32)
        hb = (_silu(gu[:, 0:d_expert]) * gu[:, d_expert:2 * d_expert]).astype(BF16)
        yw[slot] = _pack_words(jnp.dot(hb, wd[par], preferred_element_type=F32))
        _start_all(y_out(b, slot))
        cast_parts(glo_ref[b], ghi_ref[b])

        @pl.when(b == nb - 1)
        def _():
            @pl.when(b >= 1)
            def _():
                _wait_all(y_out(b - 1, 1 - slot))
            _wait_all(y_out(b, slot))


def _moe(x_sorted, we_gate, we_up, we_down, tables):
    d, d_expert = we_gate.shape[1], we_gate.shape[2]
    nb_max = x_sorted.shape[0] // MOE_BLOCK
    any_spec = pl.BlockSpec(memory_space=pl.ANY)
    grid_spec = pltpu.PrefetchScalarGridSpec(
        num_scalar_prefetch=len(tables),
        grid=(nb_max,),
        in_specs=[any_spec, any_spec, any_spec, any_spec],
        out_specs=any_spec,
        scratch_shapes=[pltpu.VMEM((2, d, 2 * d_expert), BF16),
                        pltpu.VMEM((2, d_expert, d), BF16),
                        pltpu.VMEM((WEIGHT_RING, 2, d // WEIGHT_PARTS, d_expert), F32),
                        pltpu.VMEM((WEIGHT_RING, d_expert // WEIGHT_PARTS, d), F32),
                        pltpu.VMEM((2, MOE_BLOCK, d // 2), I32),
                        pltpu.VMEM((2, MOE_BLOCK, d // 2), I32),
                        pltpu.SemaphoreType.DMA((WEIGHT_RING, 3)),
                        pltpu.SemaphoreType.DMA((2,)),
                        pltpu.SemaphoreType.DMA((2,))],
    )
    return pl.pallas_call(
        functools.partial(_moe_body, d_expert=d_expert),
        grid_spec=grid_spec,
        out_shape=jax.ShapeDtypeStruct(x_sorted.shape, x_sorted.dtype),
        compiler_params=pltpu.CompilerParams(
            dimension_semantics=("arbitrary",), vmem_limit_bytes=MOE_VMEM_LIMIT),
        name="moe_routed",
    )(*tables, x_sorted, we_gate, we_up, we_down)


def _final_body(x1_ref, gate_ref, gt2_ref, wsgu_ref, wsd_ref, fg_ref, h_hbm, y_hbm, out_ref,
                hw, yw, sem, *, d_shared):
    tm = x1_ref.shape[0]
    step = pl.program_id(0)
    slot = step % 2

    def rows_in(i, s):
        copies = _row_tile_copies(h_hbm, i * tm, hw.at[s], sem.at[s], to_hbm=False)
        for k in range(TOP_K):
            copies += _row_tile_copies(y_hbm.at[k], i * tm, yw.at[s, k], sem.at[s], to_hbm=False)
        return copies

    @pl.when(step == 0)
    def _():
        _start_all(rows_in(0, 0))

    @pl.when(step + 1 < pl.num_programs(0))
    def _():
        _start_all(rows_in(step + 1, 1 - slot))

    _wait_all(rows_in(step, slot))
    routed = gate_ref[:, 0:1] * jnp.concatenate(_unpack_words(yw[slot, 0]), axis=1)
    for k in range(1, TOP_K):
        routed = routed + gate_ref[:, k:k + 1] * jnp.concatenate(_unpack_words(yw[slot, k]), axis=1)
    h = jnp.concatenate(_unpack_words(hw[slot]), axis=1).astype(BF16)
    gu = jnp.dot(h, wsgu_ref[...], preferred_element_type=F32)
    hb = (_silu(gu[:, 0:d_shared]) * gu[:, d_shared:2 * d_shared]).astype(BF16)
    x2 = x1_ref[...] + gt2_ref[0] * (routed + jnp.dot(hb, wsd_ref[...], preferred_element_type=F32))
    out_ref[...] = x2 * lax.rsqrt(jnp.mean(x2 * x2, axis=-1, keepdims=True) + EPS) * fg_ref[...]


def _final(h_rows, x1, y_tok, gates, mod, ws_gu, ws_d, final_g, rows_per_batch):
    n, d = x1.shape
    tm = ROW_TILE
    tiles_per_batch = rows_per_batch // tm
    row = lambda i: (i, 0)
    any_spec = pl.BlockSpec(memory_space=pl.ANY)
    return pl.pallas_call(
        functools.partial(_final_body, d_shared=ws_d.shape[0]),
        grid=(n // tm,),
        in_specs=[pl.BlockSpec((tm, d), row), pl.BlockSpec((tm, SUBLANES), row),
                  pl.BlockSpec((1, 1, d), _mod_spec(5, tiles_per_batch)),
                  _resident(ws_gu.shape), _resident(ws_d.shape), _resident(final_g.shape),
                  any_spec, any_spec],
        out_specs=pl.BlockSpec((tm, d), row),
        out_shape=jax.ShapeDtypeStruct((n, d), F32),
        scratch_shapes=[pltpu.VMEM((2, tm, d // 2), I32), pltpu.VMEM((2, TOP_K, tm, d // 2), I32),
                        pltpu.SemaphoreType.DMA((2,))],
        compiler_params=_params(1),
        name="shared_combine_final",
    )(x1, gates, mod, ws_gu, ws_d, final_g, h_rows, y_tok)


def _routing_tables(idx, rank, counts, n_tok):
    nb_max = -(-(n_tok * TOP_K) // MOE_BLOCK) + N_EXPERTS
    nblk = (counts + MOE_BLOCK - 1) // MOE_BLOCK
    blk_end = jnp.cumsum(nblk)
    blk_start = blk_end - nblk
    experts = jnp.arange(N_EXPERTS, dtype=I32)[:, None, None]
    first_slot = (blk_start * MOE_BLOCK)[:, None, None]
    dest = jnp.sum(jnp.where(idx[None] == experts, first_slot, 0), axis=0) + rank
    dest_chunks = dest.reshape(TOP_K, n_tok // SC_CHUNK, SC_CHUNK).transpose(1, 0, 2)

    blocks = jnp.arange(nb_max, dtype=I32)[:, None]
    member = (blk_start[None, :] <= blocks) & (blocks < blk_end[None, :])
    lookup = lambda table: jnp.sum(jnp.where(member, table[None, :], 0), axis=1)
    blocks = blocks[:, 0]
    nonempty = nblk > 0
    n_visited = jnp.sum(nonempty.astype(I32))
    ordinal_of = jnp.cumsum(nonempty.astype(I32)) - 1
    slots = jnp.arange(N_EXPERTS, dtype=I32)
    order = jnp.sum(jnp.where(nonempty[None, :] & (ordinal_of[None, :] == slots[:, None]),
                              slots[None, :], 0), axis=1)
    ordinal = lookup(ordinal_of)
    k_in_e = blocks - lookup(blk_start)
    nb_e = jnp.maximum(lookup(nblk), 1)
    live = (ordinal + 1 < n_visited) & (blocks < blk_end[-1])
    first = WEIGHT_PARTS * (ordinal + 1)
    lo = jnp.where(live, first + WEIGHT_PARTS * k_in_e // nb_e, 0)
    hi = jnp.where(live, first + WEIGHT_PARTS * (k_in_e + 1) // nb_e, 0)
    tables = (ordinal.astype(I32), order, lo.astype(I32), hi.astype(I32),
              (WEIGHT_PARTS * n_visited).reshape(1).astype(I32), blk_end[-1:].astype(I32))
    return tables, dest_chunks, nb_max * MOE_BLOCK


def kernel(x, c, ctx, c_ctx, norm1_g, norm2_g, w_ada, b_ada, w_in, conv_w, gate_b, head_g, w_out,
           w_router, b_router, we_gate, we_up, we_down, ws_gate, ws_up, ws_down, final_g):
    assert w_ada.shape[0] == 1, "single-layer block"
    bsz, seq, d = x.shape
    ctx_len = ctx.shape[1]
    n_tok = bsz * seq
    conv_dim = conv_w.shape[2]
    v_all = head_g.shape[1]
    qk_all = (w_in.shape[2] - 3 * conv_dim - 2 * v_all - N_GATES) // 2
    assert seq % ROW_TILE == 0 and ctx_len % ROW_TILE == 0 and ROW_TILE % GRID_W == 0
    assert bsz + 1 <= SUBLANES

    cc = jnp.zeros((SUBLANES, d), F32).at[:bsz].set(c).at[bsz].set(c_ctx)
    mod = _adaln(cc, w_ada[0], b_ada).reshape(SUBLANES * 6, 1, d)

    n_main = 3 * conv_dim + 2 * qk_all + 2 * v_all
    w_all = w_in[0].astype(BF16)
    k_lo = 3 * conv_dim + qk_all
    w_kt = w_all[:, k_lo:k_lo + qk_all].T
    w_v = w_all[:, k_lo + qk_all:k_lo + qk_all + v_all]
    w_gate = jnp.pad(w_all[:, n_main:], ((0, 0), (0, LANES - N_GATES)))
    gate_bias = jnp.zeros((1, LANES), F32).at[0, :N_GATES].set(gate_b[0].reshape(-1))

    x2d = x.reshape(n_tok, d)
    conv, q, kt, v, og, g, gt = _inproj(x2d, mod, norm1_g, w_all, w_kt, w_gate, gate_bias, conv_w[0],
                                       seq, conv_dim, qk_all, v_all)
    ktc, vc, _, gtc = _inproj_ctx(ctx.reshape(bsz * ctx_len, d), mod, norm1_g, w_v, w_kt, w_gate,
                                  gate_bias, bsz)

    h_bwd = _mlstm(1, q, kt, v, g, gt, ktc, vc, gtc, None, bsz)
    ml = _mlstm(0, q, kt, v, g, gt, ktc, vc, gtc, (h_bwd, og), bsz, head_g)

    assert 2 * N_EXPERTS == LANES
    w_r_hi = w_router[0].astype(BF16)
    w_r = jnp.concatenate([w_r_hi, (w_router[0] - w_r_hi.astype(F32)).astype(BF16)], axis=1)
    b_r = jnp.zeros((1, LANES), F32).at[0, :N_EXPERTS].set(b_router[0])
    x1, idx, gates, rank, cnt, h_rows = _outproj(conv, ml, x2d, mod, norm2_g, w_out[0].astype(BF16),
                                             w_r, b_r, seq)

    tables, dest_chunks, n_slots = _routing_tables(idx[:TOP_K], rank[:TOP_K], cnt[0, :N_EXPERTS], n_tok)
    x_sorted = _sc_dispatch(h_rows, dest_chunks, n_slots)
    y_sorted = _moe(x_sorted, we_gate[0], we_up[0], we_down[0], tables)
    y_tok = _sc_combine(y_sorted, dest_chunks, n_tok)

    ws_gu = jnp.concatenate([ws_gate[0], ws_up[0]], axis=1).astype(BF16)
    out = _final(h_rows, x1, y_tok, gates, mod, ws_gu, ws_down[0].astype(BF16),
                 final_g.reshape(1, d), seq)
    return out.reshape(bsz, seq, d)
```

```python
import functools

import jax
import jax.numpy as jnp
from jax import lax
from jax.experimental import pallas as pl
from jax.experimental.pallas import tpu as pltpu
from jax.experimental.pallas import tpu_sc as plsc

F32 = jnp.float32
BF16 = jnp.bfloat16
I32 = jnp.int32

N_HEADS = 4
GRID_W = 64
CHUNK = 128
TOP_K = 6
N_EXPERTS = 64
ROUTED_SCALE = 2.446
EPS = 1e-6
N_GATES = 4 * N_HEADS
GATE_COLS = 6 * N_HEADS

LANES = 128
SUBLANES = 8
MOE_BLOCK = 256
ROW_TILE = 256
ADALN_TILE = 1024
WEIGHT_PARTS = 8
PART_SHIFT = 3
WEIGHT_RING = 3
SC_CHUNK = 64
HIGH_HALF = -65536
VMEM_LIMIT = 56 * 1024 * 1024
MOE_VMEM_LIMIT = 62 * 1024 * 1024

_HIGHEST = lax.Precision.HIGHEST
_NEG_INF = float("-inf")
assert CHUNK == LANES


def _resident(shape):
    nd = len(shape)
    return pl.BlockSpec(shape, lambda *_: (0,) * nd, pipeline_mode=pl.Buffered(1))


def _params(n_axes):
    return pltpu.CompilerParams(
        dimension_semantics=("arbitrary",) * n_axes, vmem_limit_bytes=VMEM_LIMIT)


def _log_sigmoid(x):
    return jnp.minimum(x, 0.0) - jnp.log1p(jnp.exp(-jnp.abs(x)))


def _silu(x):
    return x * jax.nn.sigmoid(x)


def _pack_words(val):
    half = val.shape[1] // 2
    lo = lax.bitcast_convert_type(val[:, :half].astype(BF16).astype(F32), I32)
    hi = lax.bitcast_convert_type(val[:, half:].astype(BF16).astype(F32), I32)
    return (hi & HIGH_HALF) | lax.shift_right_logical(lo, 16)


def _unpack_words(word):
    lo = lax.bitcast_convert_type(lax.shift_left(word, 16), F32)
    hi = lax.bitcast_convert_type(word & HIGH_HALF, F32)
    return lo, hi


def _row_tile_copies(hbm_rows, row0, tile, sem, to_hbm):
    n = tile.shape[0]
    copies = []
    for c in range(SUBLANES):
        hbm = hbm_rows.at[pl.ds(row0, n), c, :]
        vmem = tile.at[:, pl.ds(c * LANES, LANES)]
        copies.append(pltpu.make_async_copy(vmem, hbm, sem) if to_hbm
                      else pltpu.make_async_copy(hbm, vmem, sem))
    return copies


def _start_all(copies):
    for cp in copies:
        cp.start()


def _wait_all(copies):
    for cp in copies:
        cp.wait()


def _adaln_body(c_ref, w_ref, b_ref, o_ref):
    s = _silu(c_ref[...])
    o_ref[...] = jnp.dot(s.astype(BF16), w_ref[...].astype(BF16),
                         preferred_element_type=F32) + b_ref[...]


def _adaln(cc, w, b):
    d, n6 = w.shape
    return pl.pallas_call(
        _adaln_body,
        grid=(n6 // ADALN_TILE,),
        in_specs=[pl.BlockSpec((SUBLANES, d), lambda j: (0, 0)),
                  pl.BlockSpec((d, ADALN_TILE), lambda j: (0, j)),
                  pl.BlockSpec((1, ADALN_TILE), lambda j: (0, j))],
        out_specs=pl.BlockSpec((SUBLANES, ADALN_TILE), lambda j: (0, j)),
        out_shape=jax.ShapeDtypeStruct((SUBLANES, n6), F32),
        compiler_params=_params(1),
        name="adaln",
    )(cc, w, b)


def _norm_mod(x, g, shift, scale):
    y = x * lax.rsqrt(jnp.mean(x * x, axis=-1, keepdims=True) + EPS) * g
    return y * (1.0 + scale) + shift


def _gate_prep(xb, wg, gb_ref, g_ref, gt_ref):
    tm = xb.shape[0]
    gg = jnp.dot(xb, wg, preferred_element_type=F32) + gb_ref[...]
    lane = lax.broadcasted_iota(I32, (tm, LANES), 1)
    is_f = (lane & N_HEADS) != 0
    is_bwd = (lane & (2 * N_HEADS)) != 0
    lf = jnp.where(is_f, _log_sigmoid(gg), 0.0)
    r = lax.broadcasted_iota(I32, (tm, tm), 0)
    c = lax.broadcasted_iota(I32, (tm, tm), 1)
    same = (r // CHUNK) == (c // CHUNK)
    tri_l = jnp.where(same & (c <= r), 1.0, 0.0).astype(F32)
    tri_u = jnp.where(same & (c >= r), 1.0, 0.0).astype(F32)
    pre = jnp.dot(tri_l, lf, precision=_HIGHEST, preferred_element_type=F32)
    suf = jnp.dot(tri_u, lf, precision=_HIGHEST, preferred_element_type=F32)
    out = jnp.where(is_f, jnp.where(is_bwd, suf, pre), gg)
    diff = out - pltpu.roll(out, LANES - N_HEADS, axis=1)
    pos = lax.broadcasted_iota(I32, (tm, 1), 0) % CHUNK
    run_f, run_b = diff, diff
    k = 1
    while k < CHUNK:
        run_f = jnp.maximum(run_f, jnp.where(pos >= k, pltpu.roll(run_f, k, axis=0), _NEG_INF))
        run_b = jnp.maximum(run_b, jnp.where(pos < CHUNK - k, pltpu.roll(run_b, tm - k, axis=0), _NEG_INF))
        k *= 2
    run = jnp.where(is_bwd, run_b, run_f)
    fwd_lanes = (lane >= N_GATES) & (lane < N_GATES + N_HEADS)
    bwd_lanes = (lane >= N_GATES + N_HEADS) & (lane < GATE_COLS)
    out = jnp.where(fwd_lanes, pltpu.roll(run, N_GATES, axis=1),
                    jnp.where(bwd_lanes, pltpu.roll(run, N_GATES - N_HEADS, axis=1), out))
    g_ref[...] = out[:, :GATE_COLS]
    gt_ref[...] = out.T[:GATE_COLS, :]


def _project_transposed(wt_ref, xb):
    return lax.dot_general(wt_ref[...], xb, (((1,), (1,)), ((), ())),
                           preferred_element_type=F32).astype(BF16)


def _inproj_body(x_ref, sh_ref, sc_ref, g1_ref, w_ref, wkt_ref, wg_ref, gb_ref, cw_ref,
                 conv_ref, q_ref, k_ref, v_ref, o_ref, g_ref, gt_ref, *, conv_dim, qk_all, v_all):
    tm = x_ref.shape[0]
    xb = _norm_mod(x_ref[...], g1_ref[...], sh_ref[0], sc_ref[0]).astype(BF16)

    def proj(lo, width):
        return jnp.dot(xb, w_ref[:, lo:lo + width], preferred_element_type=F32)

    u = proj(conv_dim, conv_dim) * proj(2 * conv_dim, conv_dim)
    pos = lax.broadcasted_iota(I32, (tm, 1), 0) % GRID_W
    um = jnp.where(pos == 0, 0.0, pltpu.roll(u, 1, axis=0))
    up = jnp.where(pos == GRID_W - 1, 0.0, pltpu.roll(u, tm - 1, axis=0))
    y = um * cw_ref[0:1, :] + u * cw_ref[1:2, :] + up * cw_ref[2:3, :]
    conv_ref[...] = (proj(0, conv_dim) * y).astype(BF16)

    off = 3 * conv_dim
    qscale = (qk_all // N_HEADS) ** -0.5
    q_ref[...] = (proj(off, qk_all) * qscale).astype(BF16)
    k_ref[...] = _project_transposed(wkt_ref, xb)
    v_ref[...] = proj(off + 2 * qk_all, v_all).astype(BF16)
    o_ref[...] = jax.nn.sigmoid(proj(off + 2 * qk_all + v_all, v_all)).astype(BF16)
    _gate_prep(xb, wg_ref[...], gb_ref, g_ref, gt_ref)


def _inproj_ctx_body(x_ref, sh_ref, sc_ref, g1_ref, w_ref, wkt_ref, wg_ref, gb_ref,
                     k_ref, v_ref, g_ref, gt_ref):
    xb = _norm_mod(x_ref[...], g1_ref[...], sh_ref[0], sc_ref[0]).astype(BF16)
    k_ref[...] = _project_transposed(wkt_ref, xb)
    v_ref[...] = jnp.dot(xb, w_ref[...], preferred_element_type=F32).astype(BF16)
    _gate_prep(xb, wg_ref[...], gb_ref, g_ref, gt_ref)


def _mod_spec(part, tiles_per_row, fixed_row=None):
    def index(i):
        row = fixed_row if fixed_row is not None else i // tiles_per_row
        return (row * 6 + part, 0, 0)

    return index


def _inproj(x2d, mod, g1, w_all, w_kt, w_gate, gate_b, conv_w, rows_per_batch, conv_dim, qk_all, v_all):
    n, d = x2d.shape
    tm = ROW_TILE
    tiles_per_batch = rows_per_batch // tm
    row = lambda i: (i, 0)
    mod_block = (1, 1, d)
    out_shapes = (
        jax.ShapeDtypeStruct((n, conv_dim), BF16),
        jax.ShapeDtypeStruct((n, qk_all), BF16),
        jax.ShapeDtypeStruct((qk_all, n), BF16),
        jax.ShapeDtypeStruct((n, v_all), BF16),
        jax.ShapeDtypeStruct((n, v_all), BF16),
        jax.ShapeDtypeStruct((n, GATE_COLS), F32),
        jax.ShapeDtypeStruct((GATE_COLS, n), F32),
    )
    out_specs = (
        pl.BlockSpec((tm, conv_dim), row),
        pl.BlockSpec((tm, qk_all), row),
        pl.BlockSpec((qk_all, tm), lambda i: (0, i)),
        pl.BlockSpec((tm, v_all), row),
        pl.BlockSpec((tm, v_all), row),
        pl.BlockSpec((tm, GATE_COLS), row),
        pl.BlockSpec((GATE_COLS, tm), lambda i: (0, i)),
    )
    return pl.pallas_call(
        functools.partial(_inproj_body, conv_dim=conv_dim, qk_all=qk_all, v_all=v_all),
        grid=(n // tm,),
        in_specs=[pl.BlockSpec((tm, d), row),
                  pl.BlockSpec(mod_block, _mod_spec(0, tiles_per_batch)),
                  pl.BlockSpec(mod_block, _mod_spec(1, tiles_per_batch)),
                  _resident(g1.shape), _resident(w_all.shape), _resident(w_kt.shape),
                  _resident(w_gate.shape), _resident(gate_b.shape), _resident(conv_w.shape)],
        out_specs=out_specs,
        out_shape=out_shapes,
        compiler_params=_params(1),
        name="inproj",
    )(x2d, mod, mod, g1, w_all, w_kt, w_gate, gate_b, conv_w)


def _inproj_ctx(c2d, mod, g1, w_v, w_kt, w_gate, gate_b, ctx_mod_row):
    n, d = c2d.shape
    tm = ROW_TILE
    row = lambda i: (i, 0)
    mod_block = (1, 1, d)
    qk_all, v_all = w_kt.shape[0], w_v.shape[1]
    return pl.pallas_call(
        _inproj_ctx_body,
        grid=(n // tm,),
        in_specs=[pl.BlockSpec((tm, d), row),
                  pl.BlockSpec(mod_block, _mod_spec(0, 1, ctx_mod_row)),
                  pl.BlockSpec(mod_block, _mod_spec(1, 1, ctx_mod_row)),
                  _resident(g1.shape), _resident(w_v.shape), _resident(w_kt.shape),
                  _resident(w_gate.shape), _resident(gate_b.shape)],
        out_specs=(pl.BlockSpec((qk_all, tm), lambda i: (0, i)), pl.BlockSpec((tm, v_all), row),
                   pl.BlockSpec((tm, GATE_COLS), row), pl.BlockSpec((GATE_COLS, tm), lambda i: (0, i))),
        out_shape=(jax.ShapeDtypeStruct((qk_all, n), BF16), jax.ShapeDtypeStruct((n, v_all), BF16),
                   jax.ShapeDtypeStruct((n, GATE_COLS), F32), jax.ShapeDtypeStruct((GATE_COLS, n), F32)),
        compiler_params=_params(1),
        name="inproj_ctx",
    )(c2d, mod, mod, g1, w_v, w_kt, w_gate, gate_b)


def _with_ones(v):
    return jnp.concatenate([v, jnp.ones((v.shape[0], LANES), v.dtype)], axis=1)


def _mlstm_state_update(h, direction, kt_ref, v_ref, gt_ref, s_ref, m_ref, qk, vh):
    ci = direction * 2 * N_HEADS + h
    cb = ci + N_HEADS
    last = 0 if direction else CHUNK - 1
    kt = kt_ref[h * qk:(h + 1) * qk, :].astype(F32)
    va = _with_ones(v_ref[:, h * vh:(h + 1) * vh])
    b_last = gt_ref[cb:cb + 1, last:last + 1]
    m_prev = m_ref[h][0:1, 0:1]
    g_r = b_last - gt_ref[cb:cb + 1, :] + gt_ref[ci:ci + 1, :]
    cm = N_GATES + direction * N_HEADS + h
    m_new = b_last + jnp.maximum(m_prev, gt_ref[cm:cm + 1, last:last + 1])
    a = jnp.exp(b_last + m_prev - m_new)
    kw = (kt * jnp.exp(g_r - m_new)).astype(BF16)
    s_ref[h] = a * s_ref[h] + jnp.dot(kw, va, preferred_element_type=F32)
    m_ref[h] = jnp.broadcast_to(m_new, m_ref.shape[1:])


def _mlstm_head_output(h, direction, q_ref, kt_ref, v_ref, g_ref, gt_ref, s_ref, m_ref, qk, vh):
    ci = direction * 2 * N_HEADS + h
    cb = ci + N_HEADS
    q = q_ref[:, h * qk:(h + 1) * qk]
    kt = kt_ref[h * qk:(h + 1) * qk, :]
    va = _with_ones(v_ref[:, h * vh:(h + 1) * vh])
    ig_r = gt_ref[ci:ci + 1, :]
    b_r = gt_ref[cb:cb + 1, :]
    cm = N_GATES + direction * N_HEADS + h
    m_prev = m_ref[h][0:1, :]
    b_rep = jnp.broadcast_to(g_ref[:, cb:cb + 1], (CHUNK, LANES))
    run_rep = jnp.broadcast_to(g_ref[:, cm:cm + 1], (CHUNK, LANES))
    row = lax.broadcasted_iota(I32, (CHUNK, CHUNK), 0)
    col = lax.broadcasted_iota(I32, (CHUNK, CHUNK), 1)
    mask = (col >= row) if direction else (col <= row)
    dm = jnp.where(mask, b_rep + (ig_r - b_r), _NEG_INF)
    inter = b_rep + m_prev
    m_t = jnp.maximum(inter, b_rep + run_rep)
    w_inter = jnp.exp(inter - m_t)
    s = jnp.dot(q, kt, preferred_element_type=F32) * jnp.exp(dm - m_t)
    intra = jnp.dot(s.astype(BF16), va, preferred_element_type=F32)
    carried = jnp.dot(q, s_ref[h].astype(BF16), preferred_element_type=F32)
    den = intra[:, vh:vh + LANES] + w_inter * carried[:, vh:vh + LANES]
    scale = 1.0 / jnp.maximum(jnp.abs(den), jnp.exp(-m_t))
    return jnp.concatenate(
        [(intra[:, j:j + LANES] + w_inter * carried[:, j:j + LANES]) * scale for j in range(0, vh, LANES)],
        axis=1)


def _mlstm_body(*refs, direction, bsz, n_ctx_chunks, qk, vh):
    q_ref, v_ref, g_ref, vc_ref = refs[0:4]
    kt_refs, gt_refs = refs[4:4 + bsz], refs[4 + bsz:4 + 2 * bsz]
    ktc_refs, gtc_refs = refs[4 + 2 * bsz:4 + 3 * bsz], refs[4 + 3 * bsz:4 + 4 * bsz]
    rest = refs[4 + 4 * bsz:]
    if direction:
        out_ref, s_ref, m_ref = rest
    else:
        hb_ref, og_ref, hg_ref, out_ref, s_ref, m_ref = rest
    step = pl.program_id(0)

    @pl.when(step == 0)
    def _():
        s_ref[...] = jnp.zeros_like(s_ref)
        m_ref[...] = jnp.full_like(m_ref, _NEG_INF)

    @pl.when(step < n_ctx_chunks)
    def _():
        for b in range(bsz):
            for h in range(N_HEADS):
                _mlstm_state_update(h, direction, ktc_refs[b], vc_ref.at[b], gtc_refs[b],
                                    s_ref.at[b], m_ref.at[b], qk, vh)

    @pl.when(step >= n_ctx_chunks)
    def _():
        for b in range(bsz):
            for h in range(N_HEADS):
                hh = _mlstm_head_output(h, direction, q_ref.at[b], kt_refs[b], v_ref.at[b], g_ref.at[b],
                                        gt_refs[b], s_ref.at[b], m_ref.at[b], qk, vh)
                cols = slice(h * vh, (h + 1) * vh)
                if direction:
                    out_ref[b, :, cols] = hh
                else:
                    hs = hh + hb_ref[b, :, cols]
                    hs = hs * lax.rsqrt(jnp.mean(hs * hs, axis=-1, keepdims=True) + EPS)
                    out_ref[b, :, cols] = (hs * hg_ref[:, cols]
                                           * og_ref[b, :, cols].astype(F32)).astype(BF16)
                _mlstm_state_update(h, direction, kt_refs[b], v_ref.at[b], gt_refs[b],
                                    s_ref.at[b], m_ref.at[b], qk, vh)


def _mlstm(direction, q, kt, v, g, gt, ktc, vc, gtc, extra, bsz, head_g=None):
    n, qk_all = q.shape
    v_all = v.shape[1]
    qk, vh = qk_all // N_HEADS, v_all // N_HEADS
    seq = n // bsz
    nc = seq // CHUNK
    ncc = vc.shape[0] // bsz // CHUNK

    def lat(s):
        j = jnp.clip(s - ncc, 0, nc - 1)
        return nc - 1 - j if direction else j

    def ctx(s):
        j = jnp.clip(s, 0, ncc - 1)
        return ncc - 1 - j if direction else j

    def per_batch(a):
        return a.reshape(bsz, a.shape[0] // bsz, a.shape[1])

    lat_blk = lambda c: pl.BlockSpec((bsz, CHUNK, c), lambda s: (0, lat(s), 0))
    in_specs = [lat_blk(qk_all), lat_blk(v_all), lat_blk(GATE_COLS),
                pl.BlockSpec((bsz, CHUNK, v_all), lambda s: (0, ctx(s), 0))]
    args = [per_batch(q), per_batch(v), per_batch(g), per_batch(vc)]
    for arr, rows, n_chunks, pos in ((kt, qk_all, nc, lat), (gt, GATE_COLS, nc, lat),
                                     (ktc, qk_all, ncc, ctx), (gtc, GATE_COLS, ncc, ctx)):
        for b in range(bsz):
            in_specs.append(pl.BlockSpec((rows, CHUNK), lambda s, b=b, n_chunks=n_chunks, pos=pos:
                                         (0, b * n_chunks + pos(s))))
            args.append(arr)
    if direction:
        out_dtype = F32
    else:
        hb, og = extra
        in_specs += [lat_blk(v_all), lat_blk(v_all), pl.BlockSpec((1, v_all), lambda s: (0, 0))]
        args += [per_batch(hb), per_batch(og), head_g]
        out_dtype = BF16
    out = pl.pallas_call(
        functools.partial(_mlstm_body, direction=direction, bsz=bsz, n_ctx_chunks=ncc, qk=qk, vh=vh),
        grid=(ncc + nc,),
        in_specs=in_specs,
        out_specs=lat_blk(v_all),
        out_shape=jax.ShapeDtypeStruct((bsz, seq, v_all), out_dtype),
        scratch_shapes=[pltpu.VMEM((bsz, N_HEADS, qk, vh + LANES), F32),
                        pltpu.VMEM((bsz, N_HEADS, SUBLANES, LANES), F32)],
        compiler_params=_params(1),
        name="mlstm_bwd" if direction else "mlstm_fwd",
    )(*args)
    return out.reshape(n, v_all)


def _outproj_body(conv_ref, ml_ref, x_ref, gt1_ref, sh2_ref, sc2_ref, g2_ref, wo_ref, wr_ref, br_ref,
                  x1_ref, idx_ref, gate_ref, rank_ref, cnt_ref, h_hbm, carry_ref, hw, hsem):
    tm = x_ref.shape[0]
    half = conv_ref.shape[1]
    step = pl.program_id(0)
    buf = step % 2

    def h_out(i, s):
        return _row_tile_copies(h_hbm, i * tm, hw.at[s], hsem.at[s], to_hbm=True)

    @pl.when(step == 0)
    def _():
        carry_ref[...] = jnp.zeros_like(carry_ref)

    @pl.when(step >= 2)
    def _():
        _wait_all(h_out(step - 2, buf))

    y = (jnp.dot(conv_ref[...], wo_ref[0:half, :], preferred_element_type=F32)
         + jnp.dot(ml_ref[...], wo_ref[half:2 * half, :], preferred_element_type=F32))
    x1 = x_ref[...] + gt1_ref[0] * y
    x1_ref[...] = x1
    hn = _norm_mod(x1, g2_ref[...], sh2_ref[0], sc2_ref[0])
    hw[buf] = _pack_words(hn)
    _start_all(h_out(step, buf))

    h_hi = hn.astype(BF16)
    h_lo = (hn - h_hi.astype(F32)).astype(BF16)
    parts = (jnp.dot(h_hi, wr_ref[...], preferred_element_type=F32)
             + jnp.dot(h_lo, wr_ref[...], preferred_element_type=F32))
    scores = jax.nn.sigmoid(parts + pltpu.roll(parts, N_EXPERTS, axis=1))
    lane = lax.broadcasted_iota(I32, (tm, LANES), 1).astype(F32)
    biased = jnp.where(lane < N_EXPERTS, scores + br_ref[...], _NEG_INF)
    onehot = jnp.zeros((tm, LANES), F32)
    picks, sels = [], []
    for _ in range(TOP_K):
        mx = jnp.max(biased, axis=1, keepdims=True)
        pick = jnp.min(jnp.where(biased == mx, lane, float(LANES)), axis=1, keepdims=True)
        hit = lane == pick
        sels.append(jnp.sum(jnp.where(hit, scores, 0.0), axis=1, keepdims=True))
        picks.append(pick)
        biased = jnp.where(hit, _NEG_INF, biased)
        onehot = onehot + hit.astype(F32)
    total = sels[0]
    for s in sels[1:]:
        total = total + s

    r = lax.broadcasted_iota(I32, (tm, tm), 0)
    c = lax.broadcasted_iota(I32, (tm, tm), 1)
    strict = jnp.where(c < r, 1.0, 0.0).astype(BF16)
    before = jnp.dot(strict, onehot.astype(BF16), preferred_element_type=F32) + carry_ref[...]
    slot = lax.broadcasted_iota(I32, (tm, SUBLANES), 1)
    idx_out = jnp.zeros((tm, LANES), F32)
    rank_out = jnp.zeros((tm, LANES), F32)
    gate_out = jnp.zeros((tm, SUBLANES), F32)
    for j in range(TOP_K):
        rank = jnp.sum(jnp.where(lane == picks[j], before, 0.0), axis=1, keepdims=True)
        idx_out = jnp.where(lane == float(j), picks[j], idx_out)
        rank_out = jnp.where(lane == float(j), rank, rank_out)
        gate_out = jnp.where(slot == j, sels[j] / total * ROUTED_SCALE, gate_out)
    idx_ref[...] = idx_out.T[:SUBLANES, :].astype(I32)
    rank_ref[...] = rank_out.T[:SUBLANES, :].astype(I32)
    gate_ref[...] = gate_out
    carry_ref[...] = carry_ref[...] + jnp.sum(onehot, axis=0, keepdims=True)
    cnt_ref[...] = jnp.broadcast_to(carry_ref[...], cnt_ref.shape).astype(I32)

    @pl.when(step == pl.num_programs(0) - 1)
    def _():
        @pl.when(step >= 1)
        def _():
            _wait_all(h_out(step - 1, 1 - buf))
        _wait_all(h_out(step, buf))


def _outproj(conv, ml, x2d, mod, g2, w_out, w_router, b_router, rows_per_batch):
    n, d = x2d.shape
    tm = ROW_TILE
    tiles_per_batch = rows_per_batch // tm
    row = lambda i: (i, 0)
    mod_block = (1, 1, d)
    half = conv.shape[1]
    return pl.pallas_call(
        _outproj_body,
        grid=(n // tm,),
        in_specs=[pl.BlockSpec((tm, half), row), pl.BlockSpec((tm, half), row), pl.BlockSpec((tm, d), row),
                  pl.BlockSpec(mod_block, _mod_spec(2, tiles_per_batch)),
                  pl.BlockSpec(mod_block, _mod_spec(3, tiles_per_batch)),
                  pl.BlockSpec(mod_block, _mod_spec(4, tiles_per_batch)),
                  _resident(g2.shape), _resident(w_out.shape), _resident(w_router.shape),
                  _resident(b_router.shape)],
        out_specs=(pl.BlockSpec((tm, d), row),
                   pl.BlockSpec((SUBLANES, tm), lambda i: (0, i)), pl.BlockSpec((tm, SUBLANES), row),
                   pl.BlockSpec((SUBLANES, tm), lambda i: (0, i)),
                   pl.BlockSpec((SUBLANES, LANES), lambda i: (0, 0)),
                   pl.BlockSpec(memory_space=pl.ANY)),
        out_shape=(jax.ShapeDtypeStruct((n, d), F32),
                   jax.ShapeDtypeStruct((SUBLANES, n), I32), jax.ShapeDtypeStruct((n, SUBLANES), F32),
                   jax.ShapeDtypeStruct((SUBLANES, n), I32),
                   jax.ShapeDtypeStruct((SUBLANES, LANES), I32),
                   jax.ShapeDtypeStruct((n, SUBLANES, LANES), I32)),
        scratch_shapes=[pltpu.VMEM((1, LANES), F32), pltpu.VMEM((2, tm, d // 2), I32),
                        pltpu.SemaphoreType.DMA((2,))],
        compiler_params=_params(1),
        name="outproj_router",
    )(conv, ml, x2d, mod, mod, mod, g2, w_out, w_router, b_router)


def _sc_workers():
    info = plsc.get_sparse_core_info()
    return info.num_cores, info.num_cores * info.num_subcores


def _sc_dispatch(h_rows, dest_chunks, n_slots):
    n_tok = h_rows.shape[0]
    n_cores, n_workers = _sc_workers()
    per_worker = n_tok // (n_workers * SC_CHUNK)
    assert per_worker * n_workers * SC_CHUNK == n_tok
    mesh = plsc.VectorSubcoreMesh(core_axis_name="c", subcore_axis_name="s")

    @functools.partial(
        pl.kernel, mesh=mesh,
        out_type=jax.ShapeDtypeStruct((n_slots,) + h_rows.shape[1:], h_rows.dtype),
        scratch_types=[pltpu.VMEM((TOP_K, SC_CHUNK), I32),
                       pltpu.VMEM((SC_CHUNK,) + h_rows.shape[1:], h_rows.dtype)],
    )
    def dispatch(h_hbm, dest_hbm, out_hbm, idx_v, rows_v):
        wid = lax.axis_index("s") * n_cores + lax.axis_index("c")

        @pl.loop(0, per_worker)
        def _(i):
            chunk = wid * per_worker + i
            pltpu.sync_copy(dest_hbm.at[chunk], idx_v)
            pltpu.sync_copy(h_hbm.at[pl.ds(chunk * SC_CHUNK, SC_CHUNK)], rows_v)
            for k in range(TOP_K):
                pltpu.sync_copy(rows_v, out_hbm.at[idx_v.at[k]])

    return dispatch(h_rows, dest_chunks)


def _sc_combine(y_sorted, dest_chunks, n_tok):
    n_cores, n_workers = _sc_workers()
    per_worker = n_tok // (n_workers * SC_CHUNK)
    assert per_worker * n_workers * SC_CHUNK == n_tok
    mesh = plsc.VectorSubcoreMesh(core_axis_name="c", subcore_axis_name="s")

    @functools.partial(
        pl.kernel, mesh=mesh,
        out_type=jax.ShapeDtypeStruct((TOP_K, n_tok) + y_sorted.shape[1:], y_sorted.dtype),
        scratch_types=[pltpu.VMEM((TOP_K, SC_CHUNK), I32),
                       pltpu.VMEM((SC_CHUNK,) + y_sorted.shape[1:], y_sorted.dtype)],
    )
    def combine(y_hbm, dest_hbm, out_hbm, idx_v, rows_v):
        wid = lax.axis_index("s") * n_cores + lax.axis_index("c")

        @pl.loop(0, per_worker)
        def _(i):
            chunk = wid * per_worker + i
            pltpu.sync_copy(dest_hbm.at[chunk], idx_v)
            for k in range(TOP_K):
                pltpu.sync_copy(y_hbm.at[idx_v.at[k]], rows_v)
                pltpu.sync_copy(rows_v, out_hbm.at[k, pl.ds(chunk * SC_CHUNK, SC_CHUNK)])

    return combine(y_sorted, dest_chunks)


def _moe_body(ord_ref, order_ref, glo_ref, ghi_ref, tot_ref, nb_ref,
              x_hbm, wg_hbm, wu_hbm, wd_hbm, y_hbm,
              wgu, wd, stage_a, stage_d, xw, yw, wsem, xsem, ysem, *, d_expert):
    b = pl.program_id(0)
    nb = nb_ref[0]
    total = tot_ref[0]
    d_model = wgu.shape[1]
    rows_a = d_model // WEIGHT_PARTS
    rows_d = d_expert // WEIGHT_PARTS

    def part_copies(g):
        e = order_ref[lax.shift_right_logical(g, PART_SHIFT)]
        i = g & (WEIGHT_PARTS - 1)
        s = lax.rem(g, WEIGHT_RING)
        return (pltpu.make_async_copy(wg_hbm.at[e, pl.ds(i * rows_a, rows_a)], stage_a.at[s, 0],
                                      wsem.at[s, 0]),
                pltpu.make_async_copy(wu_hbm.at[e, pl.ds(i * rows_a, rows_a)], stage_a.at[s, 1],
                                      wsem.at[s, 1]),
                pltpu.make_async_copy(wd_hbm.at[e, pl.ds(i * rows_d, rows_d)], stage_d.at[s],
                                      wsem.at[s, 2]))

    def start_part(g):
        for cp in part_copies(g):
            cp.start()

    def wait_part(g):
        for cp in part_copies(g):
            cp.wait()

    def cast_part(g):
        i = g & (WEIGHT_PARTS - 1)
        s = lax.rem(g, WEIGHT_RING)
        par = lax.shift_right_logical(g, PART_SHIFT) & 1
        ra = pl.multiple_of(i * rows_a, rows_a)
        rd = pl.multiple_of(i * rows_d, rows_d)
        wgu[par, pl.ds(ra, rows_a), 0:d_expert] = stage_a[s, 0].astype(BF16)
        wgu[par, pl.ds(ra, rows_a), d_expert:2 * d_expert] = stage_a[s, 1].astype(BF16)
        wd[par, pl.ds(rd, rows_d), :] = stage_d[s].astype(BF16)

    def refill(g):
        @pl.when(g + WEIGHT_RING < total)
        def _():
            start_part(g + WEIGHT_RING)

    def cast_parts(lo, hi):
        def body(g, carry):
            wait_part(g)
            cast_part(g)
            refill(g)
            return carry
        lax.fori_loop(lo, hi, body, 0)

    slot = b % 2

    def x_in(blk, s):
        return _row_tile_copies(x_hbm, blk * MOE_BLOCK, xw.at[s], xsem.at[s], to_hbm=False)

    def y_out(blk, s):
        return _row_tile_copies(y_hbm, blk * MOE_BLOCK, yw.at[s], ysem.at[s], to_hbm=True)

    @pl.when(b == 0)
    def _():
        _start_all(x_in(0, 0))
        for g in range(WEIGHT_RING):
            start_part(g)
        cast_parts(0, WEIGHT_PARTS)

    @pl.when(b + 1 < nb)
    def _():
        _start_all(x_in(b + 1, 1 - slot))

    @pl.when(b < nb)
    def _():
        par = ord_ref[b] & 1
        _wait_all(x_in(b, slot))

        @pl.when(b >= 2)
        def _():
            _wait_all(y_out(b - 2, slot))

        x = jnp.concatenate(_unpack_words(xw[slot]), axis=1).astype(BF16)
        g = jnp.dot(x, wgu[par, :, 0:d_expert], preferred_element_type=F32)
        u = jnp.dot(x, wgu[par, :, d_expert:2 * d_expert], preferred_element_type=F32)
        hb = (_silu(g) * u).astype(BF16)
        yw[slot] = _pack_words(jnp.dot(hb, wd[par], preferred_element_type=F32))
        _start_all(y_out(b, slot))
        cast_parts(glo_ref[b], ghi_ref[b])

        @pl.when(b == nb - 1)
        def _():
            @pl.when(b >= 1)
            def _():
                _wait_all(y_out(b - 1, 1 - slot))
            _wait_all(y_out(b, slot))


def _moe(x_sorted, we_gate, we_up, we_down, tables):
    d, d_expert = we_gate.shape[1], we_gate.shape[2]
    nb_max = x_sorted.shape[0] // MOE_BLOCK
    any_spec = pl.BlockSpec(memory_space=pl.ANY)
    grid_spec = pltpu.PrefetchScalarGridSpec(
        num_scalar_prefetch=len(tables),
        grid=(nb_max,),
        in_specs=[any_spec, any_spec, any_spec, any_spec],
        out_specs=any_spec,
        scratch_shapes=[pltpu.VMEM((2, d, 2 * d_expert), BF16),
                        pltpu.VMEM((2, d_expert, d), BF16),
                        pltpu.VMEM((WEIGHT_RING, 2, d // WEIGHT_PARTS, d_expert), F32),
                        pltpu.VMEM((WEIGHT_RING, d_expert // WEIGHT_PARTS, d), F32),
                        pltpu.VMEM((2, MOE_BLOCK, d // 2), I32),
                        pltpu.VMEM((2, MOE_BLOCK, d // 2), I32),
                        pltpu.SemaphoreType.DMA((WEIGHT_RING, 3)),
                        pltpu.SemaphoreType.DMA((2,)),
                        pltpu.SemaphoreType.DMA((2,))],
    )
    return pl.pallas_call(
        functools.partial(_moe_body, d_expert=d_expert),
        grid_spec=grid_spec,
        out_shape=jax.ShapeDtypeStruct(x_sorted.shape, x_sorted.dtype),
        compiler_params=pltpu.CompilerParams(
            dimension_semantics=("arbitrary",), vmem_limit_bytes=MOE_VMEM_LIMIT),
        name="moe_routed",
    )(*tables, x_sorted, we_gate, we_up, we_down)


def _final_body(x1_ref, gate_ref, gt2_ref, wsgu_ref, wsd_ref, fg_ref, h_hbm, y_hbm, out_ref,
                hw, yw, sem, *, d_shared):
    tm = x1_ref.shape[0]
    step = pl.program_id(0)
    slot = step % 2

    def rows_in(i, s):
        copies = _row_tile_copies(h_hbm, i * tm, hw.at[s], sem.at[s], to_hbm=False)
        for k in range(TOP_K):
            copies += _row_tile_copies(y_hbm.at[k], i * tm, yw.at[s, k], sem.at[s], to_hbm=False)
        return copies

    @pl.when(step == 0)
    def _():
        _start_all(rows_in(0, 0))

    @pl.when(step + 1 < pl.num_programs(0))
    def _():
        _start_all(rows_in(step + 1, 1 - slot))

    _wait_all(rows_in(step, slot))
    routed = gate_ref[:, 0:1] * jnp.concatenate(_unpack_words(yw[slot, 0]), axis=1)
    for k in range(1, TOP_K):
        routed = routed + gate_ref[:, k:k + 1] * jnp.concatenate(_unpack_words(yw[slot, k]), axis=1)
    h = jnp.concatenate(_unpack_words(hw[slot]), axis=1).astype(BF16)
    gu = jnp.dot(h, wsgu_ref[...], preferred_element_type=F32)
    hb = (_silu(gu[:, 0:d_shared]) * gu[:, d_shared:2 * d_shared]).astype(BF16)
    x2 = x1_ref[...] + gt2_ref[0] * (routed + jnp.dot(hb, wsd_ref[...], preferred_element_type=F32))
    out_ref[...] = x2 * lax.rsqrt(jnp.mean(x2 * x2, axis=-1, keepdims=True) + EPS) * fg_ref[...]


def _final(h_rows, x1, y_tok, gates, mod, ws_gu, ws_d, final_g, rows_per_batch):
    n, d = x1.shape
    tm = ROW_TILE
    tiles_per_batch = rows_per_batch // tm
    row = lambda i: (i, 0)
    any_spec = pl.BlockSpec(memory_space=pl.ANY)
    return pl.pallas_call(
        functools.partial(_final_body, d_shared=ws_d.shape[0]),
        grid=(n // tm,),
        in_specs=[pl.BlockSpec((tm, d), row), pl.BlockSpec((tm, SUBLANES), row),
                  pl.BlockSpec((1, 1, d), _mod_spec(5, tiles_per_batch)),
                  _resident(ws_gu.shape), _resident(ws_d.shape), _resident(final_g.shape),
                  any_spec, any_spec],
        out_specs=pl.BlockSpec((tm, d), row),
        out_shape=jax.ShapeDtypeStruct((n, d), F32),
        scratch_shapes=[pltpu.VMEM((2, tm, d // 2), I32), pltpu.VMEM((2, TOP_K, tm, d // 2), I32),
                        pltpu.SemaphoreType.DMA((2,))],
        compiler_params=_params(1),
        name="shared_combine_final",
    )(x1, gates, mod, ws_gu, ws_d, final_g, h_rows, y_tok)


def _routing_tables(idx, rank, counts, n_tok):
    nb_max = -(-(n_tok * TOP_K) // MOE_BLOCK) + N_EXPERTS
    nblk = (counts + MOE_BLOCK - 1) // MOE_BLOCK
    blk_end = jnp.cumsum(nblk)
    blk_start = blk_end - nblk
    experts = jnp.arange(N_EXPERTS, dtype=I32)[:, None, None]
    first_slot = (blk_start * MOE_BLOCK)[:, None, None]
    dest = jnp.sum(jnp.where(idx[None] == experts, first_slot, 0), axis=0) + rank
    dest_chunks = dest.reshape(TOP_K, n_tok // SC_CHUNK, SC_CHUNK).transpose(1, 0, 2)

    blocks = jnp.arange(nb_max, dtype=I32)[:, None]
    member = (blk_start[None, :] <= blocks) & (blocks < blk_end[None, :])
    lookup = lambda table: jnp.sum(jnp.where(member, table[None, :], 0), axis=1)
    blocks = blocks[:, 0]
    nonempty = nblk > 0
    n_visited = jnp.sum(nonempty.astype(I32))
    ordinal_of = jnp.cumsum(nonempty.astype(I32)) - 1
    slots = jnp.arange(N_EXPERTS, dtype=I32)
    order = jnp.sum(jnp.where(nonempty[None, :] & (ordinal_of[None, :] == slots[:, None]),
                              slots[None, :], 0), axis=1)
    ordinal = lookup(ordinal_of)
    k_in_e = blocks - lookup(blk_start)
    nb_e = jnp.maximum(lookup(nblk), 1)
    live = (ordinal + 1 < n_visited) & (blocks < blk_end[-1])
    first = WEIGHT_PARTS * (ordinal + 1)
    lo = jnp.where(live, first + WEIGHT_PARTS * k_in_e // nb_e, 0)
    hi = jnp.where(live, first + WEIGHT_PARTS * (k_in_e + 1) // nb_e, 0)
    tables = (ordinal.astype(I32), order, lo.astype(I32), hi.astype(I32),
              (WEIGHT_PARTS * n_visited).reshape(1).astype(I32), blk_end[-1:].astype(I32))
    return tables, dest_chunks, nb_max * MOE_BLOCK


def kernel(x, c, ctx, c_ctx, norm1_g, norm2_g, w_ada, b_ada, w_in, conv_w, gate_b, head_g, w_out,
           w_router, b_router, we_gate, we_up, we_down, ws_gate, ws_up, ws_down, final_g):
    assert w_ada.shape[0] == 1, "single-layer block"
    bsz, seq, d = x.shape
    ctx_len = ctx.shape[1]
    n_tok = bsz * seq
    conv_dim = conv_w.shape[2]
    v_all = head_g.shape[1]
    qk_all = (w_in.shape[2] - 3 * conv_dim - 2 * v_all - N_GATES) // 2
    assert seq % ROW_TILE == 0 and ctx_len % ROW_TILE == 0 and ROW_TILE % GRID_W == 0
    assert bsz + 1 <= SUBLANES

    cc = jnp.zeros((SUBLANES, d), F32).at[:bsz].set(c).at[bsz].set(c_ctx)
    mod = _adaln(cc, w_ada[0], b_ada).reshape(SUBLANES * 6, 1, d)

    n_main = 3 * conv_dim + 2 * qk_all + 2 * v_all
    w_all = w_in[0].astype(BF16)
    k_lo = 3 * conv_dim + qk_all
    w_kt = w_all[:, k_lo:k_lo + qk_all].T
    w_v = w_all[:, k_lo + qk_all:k_lo + qk_all + v_all]
    w_gate = jnp.pad(w_all[:, n_main:], ((0, 0), (0, LANES - N_GATES)))
    gate_bias = jnp.zeros((1, LANES), F32).at[0, :N_GATES].set(gate_b[0].reshape(-1))

    x2d = x.reshape(n_tok, d)
    conv, q, kt, v, og, g, gt = _inproj(x2d, mod, norm1_g, w_all, w_kt, w_gate, gate_bias, conv_w[0],
                                       seq, conv_dim, qk_all, v_all)
    ktc, vc, _, gtc = _inproj_ctx(ctx.reshape(bsz * ctx_len, d), mod, norm1_g, w_v, w_kt, w_gate,
                                  gate_bias, bsz)

    h_bwd = _mlstm(1, q, kt, v, g, gt, ktc, vc, gtc, None, bsz)
    ml = _mlstm(0, q, kt, v, g, gt, ktc, vc, gtc, (h_bwd, og), bsz, head_g)

    assert 2 * N_EXPERTS == LANES
    w_r_hi = w_router[0].astype(BF16)
    w_r = jnp.concatenate([w_r_hi, (w_router[0] - w_r_hi.astype(F32)).astype(BF16)], axis=1)
    b_r = jnp.zeros((1, LANES), F32).at[0, :N_EXPERTS].set(b_router[0])
    x1, idx, gates, rank, cnt, h_rows = _outproj(conv, ml, x2d, mod, norm2_g, w_out[0].astype(BF16),
                                             w_r, b_r, seq)

    tables, dest_chunks, n_slots = _routing_tables(idx[:TOP_K], rank[:TOP_K], cnt[0, :N_EXPERTS], n_tok)
    x_sorted = _sc_dispatch(h_rows, dest_chunks, n_slots)
    y_sorted = _moe(x_sorted, we_gate[0], we_up[0], we_down[0], tables)
    y_tok = _sc_combine(y_sorted, dest_chunks, n_tok)

    ws_gu = jnp.concatenate([ws_gate[0], ws_up[0]], axis=1).astype(BF16)
    out = _final(h_rows, x1, y_tok, gates, mod, ws_gu, ws_down[0].astype(BF16),
                 final_g.reshape(1, d), seq)
    return out.reshape(bsz, seq, d)
```
